```python
import math
import jax, jax.numpy as jnp
from jax import lax
import numpy as np

D_MODEL = 1024
BATCH = 4
SEQ = 4096
DEPTH = 1

CHUNK = 64
N_META = 16
D_SSM = D_MODEL // 2
SSM_GROUP = 16
N_SSM_GROUPS = D_SSM // SSM_GROUP
SSM_STATE = 64
D_ATT = D_MODEL - D_SSM
N_HEADS = 8
V_HEAD_DIM = D_ATT // N_HEADS
QK_NOPE_DIM = 64
QK_ROPE_DIM = 32
QK_HEAD_DIM = QK_NOPE_DIM + QK_ROPE_DIM
Q_LORA_RANK = 256
KV_LORA_RANK = 128
D_IN = D_SSM + Q_LORA_RANK + KV_LORA_RANK + QK_ROPE_DIM
D_FF = -(-8 * D_MODEL // (3 * 256)) * 256
ROPE_BASE = 10000.0
Q_BLOCK = 128
EPS = 1e-6

kernel_name = 'hymba_s5_mla_chunk_causal_layer'


def rms_norm(x, g):
    xf = x.astype(jnp.float32)
    y = xf * lax.rsqrt(jnp.mean(xf * xf, axis=-1, keepdims=True) + EPS)
    return (y * g.astype(jnp.float32)).astype(x.dtype)


def chunk_ids(n):
    p = jnp.arange(n)
    return jnp.where(p < N_META, 0, (p - N_META) // CHUNK + 1)


def rope_tables(length):
    pos = jnp.arange(length, dtype=jnp.float32)
    inv_freq = 1.0 / (ROPE_BASE ** (jnp.arange(0, QK_ROPE_DIM, 2, dtype=jnp.float32) / QK_ROPE_DIM))
    ang = pos[:, None] * inv_freq[None, :]
    return jnp.cos(ang), jnp.sin(ang)


def apply_rope(x, cos, sin):
    half = x.shape[-1] // 2
    x1 = x[..., :half].astype(jnp.float32)
    x2 = x[..., half:].astype(jnp.float32)
    c = cos[None, :, None, :]
    s = sin[None, :, None, :]
    return jnp.concatenate([x1 * c - x2 * s, x2 * c + x1 * s], axis=-1).astype(x.dtype)


def _scan_combine(e1, e2):
    a1r, a1i, b1r, b1i = e1
    a2r, a2i, b2r, b2i = e2
    return (a2r * a1r - a2i * a1i,
            a2r * a1i + a2i * a1r,
            a2r * b1r - a2i * b1i + b2r,
            a2r * b1i + a2i * b1r + b2i)


def s5_mixer(u, a_re, a_im, log_dt, b_re, b_im, c_re, c_im, d_skip, w_glu, b_glu):
    bsz, length, _ = u.shape
    f32 = jnp.float32
    uf = u.astype(f32)
    dt = jnp.exp(log_dt.astype(f32))[:, None]
    lr = a_re.astype(f32)
    li = a_im.astype(f32)
    mag = jnp.exp(lr * dt)
    ar = mag * jnp.cos(li * dt)
    ai = mag * jnp.sin(li * dt)
    den = lr * lr + li * li
    fr = ((ar - 1.0) * lr + ai * li) / den
    fi = (ai * lr - (ar - 1.0) * li) / den
    br = b_re.astype(f32)
    bi = b_im.astype(f32)
    bbr = fr[..., None] * br - fi[..., None] * bi
    bbi = fr[..., None] * bi + fi[..., None] * br
    ug = uf.reshape(bsz, length, N_SSM_GROUPS, SSM_GROUP)
    xr = jnp.einsum('blgc,gnc->lbgn', ug, bbr)
    xi = jnp.einsum('blgc,gnc->lbgn', ug, bbi)
    a_shape = (length, 1) + ar.shape
    _, _, hr, hi = lax.associative_scan(
        _scan_combine,
        (jnp.broadcast_to(ar, a_shape), jnp.broadcast_to(ai, a_shape), xr, xi),
        axis=0)
    y = (jnp.einsum('lbgn,gcn->blgc', hr, c_re.astype(f32))
         - jnp.einsum('lbgn,gcn->blgc', hi, c_im.astype(f32)))
    y = y.reshape(bsz, length, D_SSM) + d_skip.astype(f32) * uf
    z = jax.nn.gelu(y)
    out = z * jax.nn.sigmoid(z @ w_glu.astype(f32) + b_glu.astype(f32))
    return out.astype(u.dtype)


def mla_mixer(c_q, c_kv, k_rope, q_lora_norm_g, w_uq, kv_lora_norm_g, w_uk, w_uv,
              q_head_norm_g, k_head_norm_g, cos, sin):
    bsz, length, _ = c_q.shape
    q = (rms_norm(c_q, q_lora_norm_g) @ w_uq).reshape(bsz, length, N_HEADS, QK_HEAD_DIM)
    ckv = rms_norm(c_kv, kv_lora_norm_g)
    k_nope = (ckv @ w_uk).reshape(bsz, length, N_HEADS, QK_NOPE_DIM)
    v = (ckv @ w_uv).reshape(bsz, length, N_HEADS, V_HEAD_DIM)
    k = jnp.concatenate(
        [k_nope, jnp.broadcast_to(k_rope[:, :, None, :], (bsz, length, N_HEADS, QK_ROPE_DIM))], axis=-1)
    q = rms_norm(q, q_head_norm_g)
    k = rms_norm(k, k_head_norm_g)
    q = jnp.concatenate([q[..., :QK_NOPE_DIM], apply_rope(q[..., QK_NOPE_DIM:], cos, sin)], axis=-1)
    k = jnp.concatenate([k[..., :QK_NOPE_DIM], apply_rope(k[..., QK_NOPE_DIM:], cos, sin)], axis=-1)

    n_blk = -(-length // Q_BLOCK)
    pad = n_blk * Q_BLOCK - length
    qp = jnp.pad(q, ((0, 0), (0, pad), (0, 0), (0, 0)))
    qb_all = qp.reshape(bsz, n_blk, Q_BLOCK, N_HEADS, QK_HEAD_DIM).transpose(1, 0, 3, 2, 4)
    kh = k.transpose(0, 2, 1, 3)
    vh = v.transpose(0, 2, 1, 3)
    q_cid = chunk_ids(n_blk * Q_BLOCK).reshape(n_blk, Q_BLOCK)
    k_cid = chunk_ids(length)
    scale = QK_HEAD_DIM ** -0.5

    def attend_block(args):
        qb, cb = args
        s = jnp.einsum('bhqd,bhkd->bhqk', qb, kh, preferred_element_type=jnp.float32) * scale
        mask = k_cid[None, :] <= cb[:, None]
        s = jnp.where(mask[None, None], s, -jnp.inf)
        p = jax.nn.softmax(s, axis=-1).astype(vh.dtype)
        return jnp.einsum('bhqk,bhkd->bhqd', p, vh)

    o = lax.map(attend_block, (qb_all, q_cid))
    o = o.transpose(1, 0, 3, 2, 4).reshape(bsz, n_blk * Q_BLOCK, N_HEADS * V_HEAD_DIM)
    return o[:, :length]


def setup_inputs(seed: int = 0) -> dict:
    key = jax.random.key(seed)
    ks = jax.random.split(key, 32)
    f32 = jnp.float32

    def nrm(k, shape, scale):
        return jax.random.normal(k, shape, f32) * scale

    def gain(k, shape):
        return 1.0 + 0.02 * jax.random.normal(k, shape, f32)

    G, N, C = N_SSM_GROUPS, SSM_STATE, SSM_GROUP
    n_idx = jnp.arange(N, dtype=f32)
    return {
        'x': nrm(ks[0], (BATCH, SEQ, D_MODEL), 1.0),
        'meta_tokens': nrm(ks[1], (N_META, D_MODEL), 1.0),
        'mix_norm_g': gain(ks[2], (DEPTH, D_MODEL)),
        'w_in': nrm(ks[3], (DEPTH, D_MODEL, D_IN), D_MODEL ** -0.5),
        'ssm_a_re': -0.5 + 0.01 * nrm(ks[4], (DEPTH, G, N), 1.0),
        'ssm_a_im': math.pi * n_idx + 0.01 * nrm(ks[5], (DEPTH, G, N), 1.0),
        'ssm_log_dt': jax.random.uniform(ks[6], (DEPTH, G), f32, math.log(1e-3), math.log(1e-1)),
        'ssm_b_re': nrm(ks[7], (DEPTH, G, N, C), (2 * C) ** -0.5),
        'ssm_b_im': nrm(ks[8], (DEPTH, G, N, C), (2 * C) ** -0.5),
        'ssm_c_re': nrm(ks[9], (DEPTH, G, C, N), (2 * N) ** -0.5),
        'ssm_c_im': nrm(ks[10], (DEPTH, G, C, N), (2 * N) ** -0.5),
        'ssm_d': nrm(ks[11], (DEPTH, D_SSM), 1.0),
        'ssm_w_glu': nrm(ks[12], (DEPTH, D_SSM, D_SSM), D_SSM ** -0.5),
        'ssm_b_glu': nrm(ks[13], (DEPTH, D_SSM), 0.01),
        'q_lora_norm_g': gain(ks[14], (DEPTH, Q_LORA_RANK)),
        'w_uq': nrm(ks[15], (DEPTH, Q_LORA_RANK, N_HEADS * QK_HEAD_DIM), Q_LORA_RANK ** -0.5),
        'kv_lora_norm_g': gain(ks[16], (DEPTH, KV_LORA_RANK)),
        'w_uk': nrm(ks[17], (DEPTH, KV_LORA_RANK, N_HEADS * QK_NOPE_DIM), KV_LORA_RANK ** -0.5),
        'w_uv': nrm(ks[18], (DEPTH, KV_LORA_RANK, N_HEADS * V_HEAD_DIM), KV_LORA_RANK ** -0.5),
        'q_head_norm_g': gain(ks[19], (DEPTH, QK_HEAD_DIM)),
        'k_head_norm_g': gain(ks[20], (DEPTH, QK_HEAD_DIM)),
        'ssm_out_norm_g': gain(ks[21], (DEPTH, D_SSM)),
        'att_out_norm_g': gain(ks[22], (DEPTH, D_ATT)),
        'w_out': nrm(ks[23], (DEPTH, D_MODEL, D_MODEL), D_MODEL ** -0.5),
        'ffn_norm_g': gain(ks[24], (DEPTH, D_MODEL)),
        'w_gate': nrm(ks[25], (DEPTH, D_MODEL, D_FF), D_MODEL ** -0.5),
        'w_up': nrm(ks[26], (DEPTH, D_MODEL, D_FF), D_MODEL ** -0.5),
        'w_down': nrm(ks[27], (DEPTH, D_FF, D_MODEL), D_FF ** -0.5),
    }


def reference(x, meta_tokens, mix_norm_g, w_in, ssm_a_re, ssm_a_im, ssm_log_dt, ssm_b_re, ssm_b_im,
              ssm_c_re, ssm_c_im, ssm_d, ssm_w_glu, ssm_b_glu, q_lora_norm_g, w_uq, kv_lora_norm_g,
              w_uk, w_uv, q_head_norm_g, k_head_norm_g, ssm_out_norm_g, att_out_norm_g, w_out,
              ffn_norm_g, w_gate, w_up, w_down):
    bsz = x.shape[0]
    meta = jnp.broadcast_to(meta_tokens[None].astype(x.dtype), (bsz, N_META, D_MODEL))
    h = jnp.concatenate([meta, x], axis=1)
    length = h.shape[1]
    cos, sin = rope_tables(length)
    splits = [D_SSM, D_SSM + Q_LORA_RANK, D_SSM + Q_LORA_RANK + KV_LORA_RANK]
    for l in range(DEPTH):
        xn = rms_norm(h, mix_norm_g[l])
        proj = xn @ w_in[l]
        u, c_q, c_kv, k_rope = jnp.split(proj, splits, axis=-1)
        y_ssm = s5_mixer(u, ssm_a_re[l], ssm_a_im[l], ssm_log_dt[l], ssm_b_re[l], ssm_b_im[l],
                         ssm_c_re[l], ssm_c_im[l], ssm_d[l], ssm_w_glu[l], ssm_b_glu[l])
        y_att = mla_mixer(c_q, c_kv, k_rope, q_lora_norm_g[l], w_uq[l], kv_lora_norm_g[l],
                          w_uk[l], w_uv[l], q_head_norm_g[l], k_head_norm_g[l], cos, sin)
        mixed = jnp.concatenate([rms_norm(y_ssm, ssm_out_norm_g[l]),
                                 rms_norm(y_att, att_out_norm_g[l])], axis=-1)
        h = h + mixed @ w_out[l]
        hn = rms_norm(h, ffn_norm_g[l])
        h = h + (jax.nn.silu(hn @ w_gate[l]) * (hn @ w_up[l])) @ w_down[l]
    return h[:, N_META:]
```

```python
import functools
import math

import jax
import jax.numpy as jnp
from jax import lax
from jax.experimental import pallas as pl
from jax.experimental.pallas import tpu as pltpu

F32 = jnp.float32
BF16 = jnp.bfloat16

D_MODEL = 1024
N_META = 16
CHUNK = 64
D_SSM = 512
SSM_GROUP = 16
N_GROUPS = D_SSM // SSM_GROUP
SSM_STATE = 64
N_HEADS = 8
V_DIM = 64
NOPE = 64
ROPE = 32
HALF_ROPE = ROPE // 2
QK_DIM = NOPE + ROPE
Q_LORA = 256
KV_LORA = 128
D_FF = 2816
ROPE_BASE = 10000.0
EPS = 1e-6

LANES = 128
SUBLANES = 8
HEAD_PAD = LANES
QK_PAD = N_HEADS * HEAD_PAD
N_STATE_COLS = N_GROUPS * SSM_STATE
N_SLABS = N_STATE_COLS // LANES
S5_CHUNK = 128
S5_PITCH = S5_CHUNK + SUBLANES
PROJ_COLS = D_SSM + Q_LORA + KV_LORA + 2 * HEAD_PAD
VMEM_LIMIT = 56 * 1024 * 1024


def _rms(x, g):
    return x * lax.rsqrt(jnp.mean(x * x, axis=-1, keepdims=True) + EPS) * g


def _proj_kernel(x_ref, gmix_ref, win_ref, gq_ref, wq_ref, gkv_ref, wkv_ref,
                 t1q_ref, t2q_ref, t1k_ref, t2k_ref,
                 u_ref, q_ref, k_ref, v_ref):
    x = x_ref[...]
    xn = _rms(x, gmix_ref[...]).astype(BF16)
    p = jnp.dot(xn, win_ref[...], preferred_element_type=F32)
    u_ref[...] = p[:, :D_SSM]

    cq = p[:, D_SSM:D_SSM + Q_LORA]
    cqn = _rms(cq, gq_ref[...]).astype(BF16)
    q12 = jnp.dot(cqn, wq_ref[...], preferred_element_type=F32)
    t1q = t1q_ref[...]
    t2q = t2q_ref[...]
    scale = QK_DIM ** -0.5
    for h in range(N_HEADS):
        q1 = q12[:, h * HEAD_PAD:(h + 1) * HEAD_PAD]
        q2 = q12[:, QK_PAD + h * HEAD_PAD:QK_PAD + (h + 1) * HEAD_PAD]
        r = lax.rsqrt(jnp.sum(q1 * q1, axis=-1, keepdims=True) * (1.0 / QK_DIM) + EPS)
        qh = (q1 * t1q + q2 * t2q) * (r * scale)
        q_ref[:, h * HEAD_PAD:(h + 1) * HEAD_PAD] = qh.astype(BF16)

    c0 = D_SSM + Q_LORA
    ckv = p[:, c0:c0 + KV_LORA]
    ckvn = _rms(ckv, gkv_ref[...]).astype(BF16)
    kv = jnp.dot(ckvn, wkv_ref[...], preferred_element_type=F32)
    v_ref[...] = kv[:, QK_PAD:].astype(BF16)
    kr = p[:, c0 + KV_LORA:c0 + KV_LORA + HEAD_PAD]
    kr_rot = p[:, c0 + KV_LORA + HEAD_PAD:c0 + KV_LORA + 2 * HEAD_PAD]
    ss_r = jnp.sum(kr * kr, axis=-1, keepdims=True)
    t1k = t1k_ref[...]
    kr_part = kr_rot * t2k_ref[...]
    for h in range(N_HEADS):
        kn = kv[:, h * HEAD_PAD:(h + 1) * HEAD_PAD]
        ss = jnp.sum(kn * kn, axis=-1, keepdims=True) + ss_r
        r = lax.rsqrt(ss * (1.0 / QK_DIM) + EPS)
        kh = ((kn + kr) * t1k + kr_part) * r
        k_ref[:, h * HEAD_PAD:(h + 1) * HEAD_PAD] = kh.astype(BF16)


def _const_spec(shape):
    nd = len(shape)
    return pl.BlockSpec(shape, lambda *_: (0,) * nd)


def _proj_call(x2d, tm, tabs, n_tab_blocks, consts):
    n_rows = x2d.shape[0]
    gmix, win, gq, wq, gkv, wkv = consts
    grid = (n_rows // tm,)
    row = lambda i: (i, 0)
    tab = lambda i: (i % n_tab_blocks, 0)
    in_specs = [
        pl.BlockSpec((tm, D_MODEL), row),
        _const_spec(gmix.shape), _const_spec(win.shape), _const_spec(gq.shape),
        _const_spec(wq.shape), _const_spec(gkv.shape), _const_spec(wkv.shape),
    ] + [pl.BlockSpec((tm, HEAD_PAD), tab)] * 4
    out_shape = (
        jax.ShapeDtypeStruct((n_rows, D_SSM), F32),
        jax.ShapeDtypeStruct((n_rows, QK_PAD), BF16),
        jax.ShapeDtypeStruct((n_rows, QK_PAD), BF16),
        jax.ShapeDtypeStruct((n_rows, N_HEADS * V_DIM), BF16),
    )
    out_specs = (
        pl.BlockSpec((tm, D_SSM), row),
        pl.BlockSpec((tm, QK_PAD), row),
        pl.BlockSpec((tm, QK_PAD), row),
        pl.BlockSpec((tm, N_HEADS * V_DIM), row),
    )
    return pl.pallas_call(
        _proj_kernel, out_shape=out_shape, grid=grid, in_specs=in_specs, out_specs=out_specs,
        compiler_params=pltpu.CompilerParams(dimension_semantics=("parallel",),
                                             vmem_limit_bytes=VMEM_LIMIT),
        name="proj_mla",
    )(x2d, gmix, win, gq, wq, gkv, wkv, *tabs)


def _s5_kernel(u_ref, um_ref, bre_ref, bim_ref, cre_ref, cim_ref, ar_ref, ai_ref,
               d_ref, wglu_ref, bglu_ref, g_ref, o_ref, xs_ref, h_ref, *, batch):
    j = pl.program_id(0)
    half_cols = N_STATE_COLS // 2
    slabs_per_half = N_SLABS // 2

    def project_in(ub, rows):
        for kh in range(2):
            lhs = ub[:, kh * 256:(kh + 1) * 256]
            xre = jnp.dot(lhs, bre_ref[kh], preferred_element_type=F32)
            xim = jnp.dot(lhs, bim_ref[kh], preferred_element_type=F32)
            for cl in range(slabs_per_half):
                c = kh * slabs_per_half + cl
                for b in range(batch):
                    xs_ref[c, pl.ds(b * S5_PITCH, rows), :] = (
                        xre[b * rows:(b + 1) * rows, cl * LANES:(cl + 1) * LANES])
                    xs_ref[c, pl.ds((batch + b) * S5_PITCH, rows), :] = (
                        xim[b * rows:(b + 1) * rows, cl * LANES:(cl + 1) * LANES])

    def scan(n_steps):
        def body(t, hs):
            new = []
            for c in range(N_SLABS):
                rows = pl.ds(t, 2 * batch, stride=S5_PITCH)
                x8 = xs_ref[c, rows, :]
                h = hs[c]
                hn = ar_ref[c] * h + ai_ref[c] * pltpu.roll(h, batch, 0) + x8
                xs_ref[c, rows, :] = hn
                new.append(hn)
            return tuple(new)

        hs = tuple(h_ref[c] for c in range(N_SLABS))
        hs = lax.fori_loop(0, n_steps, body, hs)
        for c in range(N_SLABS):
            h_ref[c] = hs[c]

    @pl.when(j == 0)
    def _():
        h_ref[...] = jnp.zeros_like(h_ref)
        um = um_ref[...].astype(BF16)
        project_in(jnp.concatenate([um] * batch, axis=0), N_META)
        scan(N_META)

    uf = u_ref[...].reshape(batch * S5_CHUNK, D_SSM)
    project_in(uf.astype(BF16), S5_CHUNK)
    scan(S5_CHUNK)

    ys = []
    for nh in range(2):
        def gather(plane0):
            return jnp.concatenate(
                [jnp.concatenate(
                    [xs_ref[nh * slabs_per_half + cl, pl.ds((plane0 + b) * S5_PITCH, S5_CHUNK), :]
                     for cl in range(slabs_per_half)], axis=1)
                 for b in range(batch)], axis=0).astype(BF16)
        yre = jnp.dot(gather(0), cre_ref[nh], preferred_element_type=F32)
        yim = jnp.dot(gather(batch), cim_ref[nh], preferred_element_type=F32)
        ys.append(yre - yim)
    y = jnp.concatenate(ys, axis=1) + d_ref[...] * uf
    z = 0.5 * y * (1.0 + jnp.tanh(math.sqrt(2.0 / math.pi) * (y + 0.044715 * (y * y * y))))
    gate = jnp.dot(z.astype(BF16), wglu_ref[...], preferred_element_type=F32) + bglu_ref[...]
    out = z * (1.0 / (1.0 + jnp.exp(-gate)))
    o_ref[...] = _rms(out, g_ref[...]).astype(BF16).reshape(batch, S5_CHUNK, D_SSM)


def _s5_call(u3, u_meta, consts):
    batch, seq, _ = u3.shape
    assert 2 * batch == SUBLANES and seq % S5_CHUNK == 0
    grid = (seq // S5_CHUNK,)
    in_specs = [pl.BlockSpec((batch, S5_CHUNK, D_SSM), lambda j: (0, j, 0)),
                _const_spec(u_meta.shape)] + [_const_spec(c.shape) for c in consts]
    return pl.pallas_call(
        functools.partial(_s5_kernel, batch=batch),
        out_shape=jax.ShapeDtypeStruct((batch, seq, D_SSM), BF16),
        grid=grid, in_specs=in_specs,
        out_specs=pl.BlockSpec((batch, S5_CHUNK, D_SSM), lambda j: (0, j, 0)),
        scratch_shapes=[pltpu.VMEM((N_SLABS, 2 * batch * S5_PITCH, LANES), F32),
                        pltpu.VMEM((N_SLABS, 2 * batch, LANES), F32)],
        compiler_params=pltpu.CompilerParams(dimension_semantics=("arbitrary",),
                                             vmem_limit_bytes=VMEM_LIMIT),
        name="s5_mixer",
    )(u3, u_meta, *consts)


ATT_TQ = 512
ATT_TK = 512


def _attn_kernel(q_ref, k_ref, v_ref, km_ref, vm_ref, o_ref):
    qi = pl.program_id(2)
    nt = (((1,), (1,)), ((), ()))
    outs = []
    for hh in range(2):
        lanes = slice(hh * HEAD_PAD, (hh + 1) * HEAD_PAD)
        q = q_ref[:, lanes]

        def update(carry, s, vblk):
            m, l, acc = carry
            m_new = jnp.maximum(m, jnp.max(s, axis=-1, keepdims=True))
            alpha = jnp.exp(m - m_new)
            p = jnp.exp(s - m_new)
            l = alpha * l + jnp.sum(p, axis=-1, keepdims=True)
            acc = alpha * acc + jnp.dot(p.astype(BF16), vblk, preferred_element_type=F32)
            return m_new, l, acc

        s = lax.dot_general(q, km_ref[:, lanes], nt, preferred_element_type=F32)
        col = lax.broadcasted_iota(jnp.int32, s.shape, 1)
        s = jnp.where(col < N_META, s, -jnp.inf)
        m = jnp.max(s, axis=-1, keepdims=True)
        p = jnp.exp(s - m)
        l = jnp.sum(p, axis=-1, keepdims=True)
        acc = jnp.dot(p.astype(BF16), vm_ref[...], preferred_element_type=F32)

        def body(kb, carry):
            rows = pl.ds(pl.multiple_of(kb * ATT_TK, ATT_TK), ATT_TK)
            s = lax.dot_general(q, k_ref[rows, lanes], nt, preferred_element_type=F32)
            return update(carry, s, v_ref[rows, :])

        carry = lax.fori_loop(0, qi * (ATT_TQ // ATT_TK), body, (m, l, acc))

        rows = pl.ds(pl.multiple_of(qi * ATT_TQ, ATT_TQ), ATT_TQ)
        s = lax.dot_general(q, k_ref[rows, lanes], nt, preferred_element_type=F32)
        rc = lax.broadcasted_iota(jnp.int32, s.shape, 0) // CHUNK
        cc = lax.broadcasted_iota(jnp.int32, s.shape, 1) // CHUNK
        s = jnp.where(cc <= rc, s, -jnp.inf)
        m, l, acc = update(carry, s, v_ref[rows, :])
        outs.append(acc / l)
    lane = lax.broadcasted_iota(jnp.int32, outs[0].shape, 1)
    o_ref[...] = jnp.where(lane < V_DIM, outs[0], outs[1]).astype(BF16)


def _attn_call(q3, k3, v3, k_meta, v_meta):
    batch, seq, _ = q3.shape
    grid = (batch, N_HEADS // 2, seq // ATT_TQ)
    in_specs = [
        pl.BlockSpec((None, ATT_TQ, 2 * HEAD_PAD), lambda b, hp, i: (b, i, hp)),
        pl.BlockSpec((None, seq, 2 * HEAD_PAD), lambda b, hp, i: (b, 0, hp)),
        pl.BlockSpec((None, seq, 2 * V_DIM), lambda b, hp, i: (b, 0, hp)),
        pl.BlockSpec((LANES, 2 * HEAD_PAD), lambda b, hp, i: (0, hp)),
        pl.BlockSpec((LANES, 2 * V_DIM), lambda b, hp, i: (0, hp)),
    ]
    return pl.pallas_call(
        _attn_kernel,
        out_shape=jax.ShapeDtypeStruct((batch, seq, N_HEADS * V_DIM), BF16),
        grid=grid, in_specs=in_specs,
        out_specs=pl.BlockSpec((None, ATT_TQ, 2 * V_DIM), lambda b, hp, i: (b, i, hp)),
        compiler_params=pltpu.CompilerParams(
            dimension_semantics=("parallel", "parallel", "arbitrary"),
            vmem_limit_bytes=VMEM_LIMIT),
        name="mla_attention",
    )(q3, k3, v3, k_meta, v_meta)


FFN_TM = 256


def _ffn_kernel(x_ref, ms_ref, oa_ref, gatt_ref, wout_ref, gffn_ref, wg_ref, wu_ref, wd_ref,
                out_ref):
    ya = _rms(oa_ref[...].astype(F32), gatt_ref[...]).astype(BF16)
    mixed = jnp.concatenate([ms_ref[...], ya], axis=1)
    h1 = x_ref[...] + jnp.dot(mixed, wout_ref[...], preferred_element_type=F32)
    hn = _rms(h1, gffn_ref[...]).astype(BF16)
    g = jnp.dot(hn, wg_ref[...], preferred_element_type=F32)
    u = jnp.dot(hn, wu_ref[...], preferred_element_type=F32)
    a = (g * (1.0 / (1.0 + jnp.exp(-g))) * u).astype(BF16)
    out_ref[...] = h1 + jnp.dot(a, wd_ref[...], preferred_element_type=F32)


def _ffn_call(x2d, ms2d, oa2d, consts):
    n_rows = x2d.shape[0]
    tm = FFN_TM
    row = lambda i: (i, 0)
    once = pl.Buffered(1)
    in_specs = [pl.BlockSpec((tm, D_MODEL), row),
                pl.BlockSpec((tm, D_SSM), row),
                pl.BlockSpec((tm, N_HEADS * V_DIM), row)]
    in_specs += [pl.BlockSpec(c.shape, lambda i: (0, 0), pipeline_mode=once) for c in consts]
    return pl.pallas_call(
        _ffn_kernel,
        out_shape=jax.ShapeDtypeStruct((n_rows, D_MODEL), F32),
        grid=(n_rows // tm,), in_specs=in_specs,
        out_specs=pl.BlockSpec((tm, D_MODEL), row),
        compiler_params=pltpu.CompilerParams(dimension_semantics=("parallel",),
                                             vmem_limit_bytes=VMEM_LIMIT),
        name="outproj_ffn",
    )(x2d, ms2d, oa2d, *consts)


def _rope_tables(length):
    pos = jnp.arange(length, dtype=F32)
    inv_freq = 1.0 / (ROPE_BASE ** (jnp.arange(0, ROPE, 2, dtype=F32) / ROPE))
    ang = pos[:, None] * inv_freq[None, :]
    return jnp.cos(ang), jnp.sin(ang)


def _head_tables(gain, cos, sin):
    length = cos.shape[0]
    g_n, g_r = gain[:NOPE], gain[NOPE:]
    g_r_swapped = jnp.concatenate([g_r[HALF_ROPE:], g_r[:HALF_ROPE]])
    cos2 = jnp.concatenate([cos, cos], axis=1)
    sin2 = jnp.concatenate([sin, sin], axis=1)
    pad = jnp.zeros((length, HEAD_PAD - QK_DIM), F32)
    t1 = jnp.concatenate([jnp.broadcast_to(g_n, (length, NOPE)), g_r * cos2, pad], axis=1)
    t2 = jnp.concatenate([jnp.zeros((length, NOPE), F32), g_r_swapped * sin2, pad], axis=1)
    return t1, t2


def _rot_half_cols(w):
    return jnp.concatenate([-w[..., HALF_ROPE:], w[..., :HALF_ROPE]], axis=-1)


def _pad_cols(w, left, total):
    return jnp.pad(w, ((0, 0), (left, total - left - w.shape[1])))


def kernel(x, meta_tokens, mix_norm_g, w_in, ssm_a_re, ssm_a_im, ssm_log_dt, ssm_b_re, ssm_b_im,
           ssm_c_re, ssm_c_im, ssm_d, ssm_w_glu, ssm_b_glu, q_lora_norm_g, w_uq, kv_lora_norm_g,
           w_uk, w_uv, q_head_norm_g, k_head_norm_g, ssm_out_norm_g, att_out_norm_g, w_out,
           ffn_norm_g, w_gate, w_up, w_down):
    batch, seq, _ = x.shape
    depth = w_in.shape[0]
    assert depth == 1
    l = 0
    length = N_META + seq
    cos, sin = _rope_tables(length)

    wi = w_in[l]
    o_q, o_kv, o_r = D_SSM, D_SSM + Q_LORA, D_SSM + Q_LORA + KV_LORA
    w_r = wi[:, o_r:]
    win = jnp.concatenate([wi[:, :o_r], _pad_cols(w_r, NOPE, HEAD_PAD),
                           _pad_cols(_rot_half_cols(w_r), NOPE, HEAD_PAD)], axis=1).astype(BF16)
    wq3 = w_uq[l].reshape(Q_LORA, N_HEADS, QK_DIM)
    q1 = jnp.pad(wq3, ((0, 0), (0, 0), (0, HEAD_PAD - QK_DIM)))
    q2 = jnp.pad(_rot_half_cols(wq3[..., NOPE:]), ((0, 0), (0, 0), (NOPE, HEAD_PAD - QK_DIM)))
    wq = jnp.concatenate([q1.reshape(Q_LORA, QK_PAD), q2.reshape(Q_LORA, QK_PAD)], axis=1).astype(BF16)
    wk3 = jnp.pad(w_uk[l].reshape(KV_LORA, N_HEADS, NOPE), ((0, 0), (0, 0), (0, HEAD_PAD - NOPE)))
    wkv = jnp.concatenate([wk3.reshape(KV_LORA, QK_PAD), w_uv[l]], axis=1).astype(BF16)
    t1q, t2q = _head_tables(q_head_norm_g[l], cos, sin)
    t1k, t2k = _head_tables(k_head_norm_g[l], cos, sin)
    proj_consts = (mix_norm_g[l][None], win, q_lora_norm_g[l][None], wq,
                   kv_lora_norm_g[l][None], wkv)

    tm = 512
    tabs_f = tuple(t[N_META:] for t in (t1q, t2q, t1k, t2k))
    u2, q2d, k2d, v2d = _proj_call(x.reshape(batch * seq, D_MODEL), tm, tabs_f, seq // tm,
                                   proj_consts)
    tabs_m = tuple(t[:N_META] for t in (t1q, t2q, t1k, t2k))
    u_meta, _, k_meta, v_meta = _proj_call(meta_tokens, N_META, tabs_m, 1, proj_consts)

    dt = jnp.exp(ssm_log_dt[l])[:, None]
    lr, li = ssm_a_re[l], ssm_a_im[l]
    mag = jnp.exp(lr * dt)
    ar = mag * jnp.cos(li * dt)
    ai = mag * jnp.sin(li * dt)
    den = lr * lr + li * li
    fr = ((ar - 1.0) * lr + ai * li) / den
    fi = (ai * lr - (ar - 1.0) * li) / den
    br, bi = ssm_b_re[l], ssm_b_im[l]
    bbr = fr[..., None] * br - fi[..., None] * bi
    bbi = fr[..., None] * bi + fi[..., None] * br
    eye = jnp.eye(N_GROUPS, dtype=F32)

    def in_map(bb):
        full = jnp.einsum('gnc,gh->gchn', bb, eye).reshape(D_SSM, N_STATE_COLS)
        return jnp.stack([full[:256, :1024], full[256:, 1024:]]).astype(BF16)

    def out_map(cc):
        full = jnp.einsum('gcn,gh->gnhc', cc, eye).reshape(N_STATE_COLS, D_SSM)
        return jnp.stack([full[:1024, :256], full[1024:, 256:]]).astype(BF16)

    ar_rows = jnp.broadcast_to(ar.reshape(N_SLABS, 1, LANES), (N_SLABS, 2 * batch, LANES))
    ai_flat = ai.reshape(N_SLABS, 1, LANES)
    ai_rows = jnp.concatenate([jnp.broadcast_to(-ai_flat, (N_SLABS, batch, LANES)),
                               jnp.broadcast_to(ai_flat, (N_SLABS, batch, LANES))], axis=1)
    s5_consts = (in_map(bbr), in_map(bbi), out_map(ssm_c_re[l]), out_map(ssm_c_im[l]),
                 ar_rows, ai_rows, ssm_d[l][None], ssm_w_glu[l].astype(BF16),
                 ssm_b_glu[l][None], ssm_out_norm_g[l][None])
    mixed_ssm = _s5_call(u2.reshape(batch, seq, D_SSM), u_meta, s5_consts)

    pad_rows = ((0, LANES - N_META), (0, 0))
    y_att = _attn_call(q2d.reshape(batch, seq, QK_PAD), k2d.reshape(batch, seq, QK_PAD),
                       v2d.reshape(batch, seq, N_HEADS * V_DIM),
                       jnp.pad(k_meta, pad_rows), jnp.pad(v_meta, pad_rows))

    ffn_consts = (att_out_norm_g[l][None], w_out[l].astype(BF16), ffn_norm_g[l][None],
                  w_gate[l].astype(BF16), w_up[l].astype(BF16), w_down[l].astype(BF16))
    out = _ffn_call(x.reshape(batch * seq, D_MODEL), mixed_ssm.reshape(batch * seq, D_SSM),
                    y_att.reshape(batch * seq, N_HEADS * V_DIM), ffn_consts)
    return out.reshape(batch, seq, D_MODEL)
```

```python
import functools
import math

import jax
import jax.numpy as jnp
from jax import lax
from jax.experimental import pallas as pl
from jax.experimental.pallas import tpu as pltpu

F32 = jnp.float32
BF16 = jnp.bfloat16

D_MODEL = 1024
N_META = 16
CHUNK = 64
D_SSM = 512
SSM_GROUP = 16
N_GROUPS = D_SSM // SSM_GROUP
SSM_STATE = 64
N_HEADS = 8
V_DIM = 64
NOPE = 64
ROPE = 32
HALF_ROPE = ROPE // 2
QK_DIM = NOPE + ROPE
Q_LORA = 256
KV_LORA = 128
D_FF = 2816
ROPE_BASE = 10000.0
EPS = 1e-6
LOG2_E = math.log2(math.e)
MAX_UNSHIFTED_LOG2_SCORE = 40.0

LANES = 128
SUBLANES = 8
HEAD_PAD = LANES
QK_PAD = N_HEADS * HEAD_PAD
N_STATE_COLS = N_GROUPS * SSM_STATE
N_SLABS = N_STATE_COLS // LANES
S5_CHUNK = 128
S5_PITCH = S5_CHUNK + SUBLANES
PROJ_COLS = D_SSM + Q_LORA + KV_LORA + 2 * HEAD_PAD
VMEM_LIMIT = 56 * 1024 * 1024


def _rms(x, g):
    return x * lax.rsqrt(jnp.mean(x * x, axis=-1, keepdims=True) + EPS) * g


def _proj_kernel(x_ref, gmix_ref, win_ref, gq_ref, wq_ref, gkv_ref, wkv_ref, vones_ref,
                 t1q_ref, t2q_ref, t1k_ref, t2k_ref,
                 u_ref, q_ref, k_ref, v_ref):
    x = x_ref[...]
    xn = _rms(x, gmix_ref[...]).astype(BF16)
    p = jnp.dot(xn, win_ref[...], preferred_element_type=F32)
    u_ref[...] = p[:, :D_SSM]

    cq = p[:, D_SSM:D_SSM + Q_LORA]
    cqn = _rms(cq, gq_ref[...]).astype(BF16)
    q12 = jnp.dot(cqn, wq_ref[...], preferred_element_type=F32)
    t1q = t1q_ref[...]
    t2q = t2q_ref[...]
    scale = QK_DIM ** -0.5 * LOG2_E
    for h in range(N_HEADS):
        q1 = q12[:, h * HEAD_PAD:(h + 1) * HEAD_PAD]
        q2 = q12[:, QK_PAD + h * HEAD_PAD:QK_PAD + (h + 1) * HEAD_PAD]
        r = lax.rsqrt(jnp.sum(q1 * q1, axis=-1, keepdims=True) * (1.0 / QK_DIM) + EPS)
        qh = (q1 * t1q + q2 * t2q) * (r * scale)
        q_ref[:, h * HEAD_PAD:(h + 1) * HEAD_PAD] = qh.astype(BF16)

    c0 = D_SSM + Q_LORA
    ckv = p[:, c0:c0 + KV_LORA]
    ckvn = _rms(ckv, gkv_ref[...]).astype(BF16)
    kv = jnp.dot(ckvn, wkv_ref[...], preferred_element_type=F32)
    v_ref[...] = (kv[:, QK_PAD:] + vones_ref[...]).astype(BF16)
    kr = p[:, c0 + KV_LORA:c0 + KV_LORA + HEAD_PAD]
    kr_rot = p[:, c0 + KV_LORA + HEAD_PAD:c0 + KV_LORA + 2 * HEAD_PAD]
    ss_r = jnp.sum(kr * kr, axis=-1, keepdims=True)
    t1k = t1k_ref[...]
    kr_part = kr_rot * t2k_ref[...]
    for h in range(N_HEADS):
        kn = kv[:, h * HEAD_PAD:(h + 1) * HEAD_PAD]
        ss = jnp.sum(kn * kn, axis=-1, keepdims=True) + ss_r
        r = lax.rsqrt(ss * (1.0 / QK_DIM) + EPS)
        kh = ((kn + kr) * t1k + kr_part) * r
        k_ref[:, h * HEAD_PAD:(h + 1) * HEAD_PAD] = kh.astype(BF16)


def _const_spec(shape):
    nd = len(shape)
    return pl.BlockSpec(shape, lambda *_: (0,) * nd)


def _proj_call(x2d, tm, tabs, n_tab_blocks, consts):
    n_rows = x2d.shape[0]
    grid = (n_rows // tm,)
    row = lambda i: (i, 0)
    tab = lambda i: (i % n_tab_blocks, 0)
    in_specs = ([pl.BlockSpec((tm, D_MODEL), row)] + [_const_spec(c.shape) for c in consts]
                + [pl.BlockSpec((tm, HEAD_PAD), tab)] * 4)
    out_shape = (
        jax.ShapeDtypeStruct((n_rows, D_SSM), F32),
        jax.ShapeDtypeStruct((n_rows, QK_PAD), BF16),
        jax.ShapeDtypeStruct((n_rows, QK_PAD), BF16),
        jax.ShapeDtypeStruct((n_rows, QK_PAD), BF16),
    )
    out_specs = (
        pl.BlockSpec((tm, D_SSM), row),
        pl.BlockSpec((tm, QK_PAD), row),
        pl.BlockSpec((tm, QK_PAD), row),
        pl.BlockSpec((tm, QK_PAD), row),
    )
    return pl.pallas_call(
        _proj_kernel, out_shape=out_shape, grid=grid, in_specs=in_specs, out_specs=out_specs,
        compiler_params=pltpu.CompilerParams(dimension_semantics=("parallel",),
                                             vmem_limit_bytes=VMEM_LIMIT),
        name="proj_mla",
    )(x2d, *consts, *tabs)


def _s5_kernel(u_ref, um_ref, bre_ref, bim_ref, cre_ref, cim_ref, ar_ref, ai_ref,
               d_ref, wglu_ref, bglu_ref, g_ref, o_ref, xs_ref, h_ref, *, batch):
    j = pl.program_id(0)
    half_cols = N_STATE_COLS // 2
    slabs_per_half = N_SLABS // 2

    def project_in(ub, rows):
        for kh in range(2):
            lhs = ub[:, kh * 256:(kh + 1) * 256]
            xre = jnp.dot(lhs, bre_ref[kh], preferred_element_type=F32)
            xim = jnp.dot(lhs, bim_ref[kh], preferred_element_type=F32)
            for cl in range(slabs_per_half):
                c = kh * slabs_per_half + cl
                for b in range(batch):
                    xs_ref[c, pl.ds(b * S5_PITCH, rows), :] = (
                        xre[b * rows:(b + 1) * rows, cl * LANES:(cl + 1) * LANES])
                    xs_ref[c, pl.ds((batch + b) * S5_PITCH, rows), :] = (
                        xim[b * rows:(b + 1) * rows, cl * LANES:(cl + 1) * LANES])

    def scan(n_steps):
        def body(t, hs):
            new = []
            for c in range(N_SLABS):
                rows = pl.ds(t, 2 * batch, stride=S5_PITCH)
                x8 = xs_ref[c, rows, :]
                h = hs[c]
                hn = ar_ref[c] * h + ai_ref[c] * pltpu.roll(h, batch, 0) + x8
                xs_ref[c, rows, :] = hn
                new.append(hn)
            return tuple(new)

        hs = tuple(h_ref[c] for c in range(N_SLABS))
        hs = lax.fori_loop(0, n_steps, body, hs)
        for c in range(N_SLABS):
            h_ref[c] = hs[c]

    @pl.when(j == 0)
    def _():
        h_ref[...] = jnp.zeros_like(h_ref)
        um = um_ref[...].astype(BF16)
        project_in(jnp.concatenate([um] * batch, axis=0), N_META)
        scan(N_META)

    uf = u_ref[...].reshape(batch * S5_CHUNK, D_SSM)
    project_in(uf.astype(BF16), S5_CHUNK)
    scan(S5_CHUNK)

    ys = []
    for nh in range(2):
        def gather(plane0):
            return jnp.concatenate(
                [jnp.concatenate(
                    [xs_ref[nh * slabs_per_half + cl, pl.ds((plane0 + b) * S5_PITCH, S5_CHUNK), :]
                     for cl in range(slabs_per_half)], axis=1)
                 for b in range(batch)], axis=0).astype(BF16)
        yre = jnp.dot(gather(0), cre_ref[nh], preferred_element_type=F32)
        yim = jnp.dot(gather(batch), cim_ref[nh], preferred_element_type=F32)
        ys.append(yre - yim)
    y = jnp.concatenate(ys, axis=1) + d_ref[...] * uf
    z = 0.5 * y * (1.0 + jnp.tanh(math.sqrt(2.0 / math.pi) * (y + 0.044715 * (y * y * y))))
    gate = jnp.dot(z.astype(BF16), wglu_ref[...], preferred_element_type=F32) + bglu_ref[...]
    out = z * (1.0 / (1.0 + jnp.exp(-gate)))
    o_ref[...] = _rms(out, g_ref[...]).astype(BF16).reshape(batch, S5_CHUNK, D_SSM)


def _s5_call(u3, u_meta, consts):
    batch, seq, _ = u3.shape
    assert 2 * batch == SUBLANES and seq % S5_CHUNK == 0
    grid = (seq // S5_CHUNK,)
    in_specs = [pl.BlockSpec((batch, S5_CHUNK, D_SSM), lambda j: (0, j, 0)),
                _const_spec(u_meta.shape)] + [_const_spec(c.shape) for c in consts]
    return pl.pallas_call(
        functools.partial(_s5_kernel, batch=batch),
        out_shape=jax.ShapeDtypeStruct((batch, seq, D_SSM), BF16),
        grid=grid, in_specs=in_specs,
        out_specs=pl.BlockSpec((batch, S5_CHUNK, D_SSM), lambda j: (0, j, 0)),
        scratch_shapes=[pltpu.VMEM((N_SLABS, 2 * batch * S5_PITCH, LANES), F32),
                        pltpu.VMEM((N_SLABS, 2 * batch, LANES), F32)],
        compiler_params=pltpu.CompilerParams(dimension_semantics=("arbitrary",),
                                             vmem_limit_bytes=VMEM_LIMIT),
        name="s5_mixer",
    )(u3, u_meta, *consts)


ATT_TQ = 512
ATT_TK = 512


def _attn_kernel(q_ref, k_ref, v_ref, km_ref, vm_ref, mask_ref, o_ref, *, online):
    nt = (((1,), (1,)), ((), ()))
    n_q = q_ref.shape[0] // ATT_TQ
    head_lanes = [slice(h * HEAD_PAD, (h + 1) * HEAD_PAD) for h in range(2)]

    def step(state, q, kblk, vblk, mask):
        s = lax.dot_general(q, kblk, nt, preferred_element_type=F32)
        if online:
            m, acc = state
            if mask is not None:
                s = jnp.where(mask > 0, s, -jnp.inf)
            m_new = jnp.maximum(m, jnp.max(s, axis=-1, keepdims=True))
            p = jnp.exp2(s - m_new).astype(BF16)
            acc = jnp.exp2(m - m_new) * acc + jnp.dot(p, vblk, preferred_element_type=F32)
            return m_new, acc
        (acc,) = state
        p = jnp.exp2(s).astype(BF16)
        if mask is not None:
            p = p * mask
        return (acc + jnp.dot(p, vblk, preferred_element_type=F32),)

    def q_tile(qi, _):
        rows = pl.ds(pl.multiple_of(qi * ATT_TQ, ATT_TQ), ATT_TQ)
        qs = [q_ref[rows, hl] for hl in head_lanes]

        states = []
        for h, hl in enumerate(head_lanes):
            if online:
                init = (jnp.full((ATT_TQ, 1), -1e30, F32), jnp.zeros((ATT_TQ, HEAD_PAD), F32))
            else:
                init = (jnp.zeros((ATT_TQ, HEAD_PAD), F32),)
            states.append(step(init, qs[h], km_ref[:, hl], vm_ref[:, hl], None))

        def body(kb, states):
            krows = pl.ds(pl.multiple_of(kb * ATT_TK, ATT_TK), ATT_TK)
            return tuple(step(states[h], qs[h], k_ref[krows, hl], v_ref[krows, hl], None)
                         for h, hl in enumerate(head_lanes))

        states = lax.fori_loop(0, qi * (ATT_TQ // ATT_TK), body, tuple(states))

        mask = mask_ref[...]
        accs = [step(states[h], qs[h], k_ref[rows, hl], v_ref[rows, hl], mask)[-1]
                for h, hl in enumerate(head_lanes)]
        o_even = accs[0] * (1.0 / accs[0][:, V_DIM:V_DIM + 1])
        o_odd = accs[1] * (1.0 / accs[1][:, 0:1])
        lane = lax.broadcasted_iota(jnp.int32, o_even.shape, 1)
        o_ref[rows, :] = jnp.where(lane < V_DIM, o_even, o_odd).astype(BF16)
        return 0

    lax.fori_loop(0, n_q, q_tile, 0)


def _attn_call(q3, k3, v3, k_meta, v_meta, mask, *, online):
    batch, seq, _ = q3.shape
    grid = (batch, N_HEADS // 2)
    seq_blk = pl.BlockSpec((None, seq, 2 * HEAD_PAD), lambda b, hp: (b, 0, hp))
    meta_blk = pl.BlockSpec((LANES, 2 * HEAD_PAD), lambda b, hp: (0, hp))
    return pl.pallas_call(
        functools.partial(_attn_kernel, online=online),
        out_shape=jax.ShapeDtypeStruct((batch, seq, N_HEADS * V_DIM), BF16),
        grid=grid,
        in_specs=[seq_blk, seq_blk, seq_blk, meta_blk, meta_blk, _const_spec(mask.shape)],
        out_specs=pl.BlockSpec((None, seq, 2 * V_DIM), lambda b, hp: (b, 0, hp)),
        compiler_params=pltpu.CompilerParams(dimension_semantics=("parallel", "parallel"),
                                             vmem_limit_bytes=VMEM_LIMIT),
        name="mla_attention_online" if online else "mla_attention",
    )(q3, k3, v3, k_meta, v_meta, mask)


FFN_TM = 256


def _ffn_kernel(x_ref, ms_ref, oa_ref, gatt_ref, wout_ref, gffn_ref, wg_ref, wu_ref, wd_ref,
                out_ref):
    ya = _rms(oa_ref[...].astype(F32), gatt_ref[...]).astype(BF16)
    mixed = jnp.concatenate([ms_ref[...], ya], axis=1)
    h1 = x_ref[...] + jnp.dot(mixed, wout_ref[...], preferred_element_type=F32)
    hn = _rms(h1, gffn_ref[...]).astype(BF16)
    g = jnp.dot(hn, wg_ref[...], preferred_element_type=F32)
    u = jnp.dot(hn, wu_ref[...], preferred_element_type=F32)
    a = (g * (1.0 / (1.0 + jnp.exp(-g))) * u).astype(BF16)
    out_ref[...] = h1 + jnp.dot(a, wd_ref[...], preferred_element_type=F32)


def _ffn_call(x2d, ms2d, oa2d, consts):
    n_rows = x2d.shape[0]
    tm = FFN_TM
    row = lambda i: (i, 0)
    once = pl.Buffered(1)
    in_specs = [pl.BlockSpec((tm, D_MODEL), row),
                pl.BlockSpec((tm, D_SSM), row),
                pl.BlockSpec((tm, N_HEADS * V_DIM), row)]
    in_specs += [pl.BlockSpec(c.shape, lambda i: (0, 0), pipeline_mode=once) for c in consts]
    return pl.pallas_call(
        _ffn_kernel,
        out_shape=jax.ShapeDtypeStruct((n_rows, D_MODEL), F32),
        grid=(n_rows // tm,), in_specs=in_specs,
        out_specs=pl.BlockSpec((tm, D_MODEL), row),
        compiler_params=pltpu.CompilerParams(dimension_semantics=("parallel",),
                                             vmem_limit_bytes=VMEM_LIMIT),
        name="outproj_ffn",
    )(x2d, ms2d, oa2d, *consts)


def _rope_tables(length):
    pos = jnp.arange(length, dtype=F32)
    inv_freq = 1.0 / (ROPE_BASE ** (jnp.arange(0, ROPE, 2, dtype=F32) / ROPE))
    ang = pos[:, None] * inv_freq[None, :]
    return jnp.cos(ang), jnp.sin(ang)


def _head_tables(gain, cos, sin):
    length = cos.shape[0]
    g_n, g_r = gain[:NOPE], gain[NOPE:]
    g_r_swapped = jnp.concatenate([g_r[HALF_ROPE:], g_r[:HALF_ROPE]])
    cos2 = jnp.concatenate([cos, cos], axis=1)
    sin2 = jnp.concatenate([sin, sin], axis=1)
    pad = jnp.zeros((length, HEAD_PAD - QK_DIM), F32)
    t1 = jnp.concatenate([jnp.broadcast_to(g_n, (length, NOPE)), g_r * cos2, pad], axis=1)
    t2 = jnp.concatenate([jnp.zeros((length, NOPE), F32), g_r_swapped * sin2, pad], axis=1)
    return t1, t2


def _rot_half_cols(w):
    return jnp.concatenate([-w[..., HALF_ROPE:], w[..., :HALF_ROPE]], axis=-1)


def _pad_cols(w, left, total):
    return jnp.pad(w, ((0, 0), (left, total - left - w.shape[1])))


def kernel(x, meta_tokens, mix_norm_g, w_in, ssm_a_re, ssm_a_im, ssm_log_dt, ssm_b_re, ssm_b_im,
           ssm_c_re, ssm_c_im, ssm_d, ssm_w_glu, ssm_b_glu, q_lora_norm_g, w_uq, kv_lora_norm_g,
           w_uk, w_uv, q_head_norm_g, k_head_norm_g, ssm_out_norm_g, att_out_norm_g, w_out,
           ffn_norm_g, w_gate, w_up, w_down):
    batch, seq, _ = x.shape
    depth = w_in.shape[0]
    assert depth == 1
    l = 0
    length = N_META + seq
    cos, sin = _rope_tables(length)

    wi = w_in[l]
    o_q, o_kv, o_r = D_SSM, D_SSM + Q_LORA, D_SSM + Q_LORA + KV_LORA
    w_r = wi[:, o_r:]
    win = jnp.concatenate([wi[:, :o_r], _pad_cols(w_r, NOPE, HEAD_PAD),
                           _pad_cols(_rot_half_cols(w_r), NOPE, HEAD_PAD)], axis=1).astype(BF16)
    wq3 = w_uq[l].reshape(Q_LORA, N_HEADS, QK_DIM)
    q1 = jnp.pad(wq3, ((0, 0), (0, 0), (0, HEAD_PAD - QK_DIM)))
    q2 = jnp.pad(_rot_half_cols(wq3[..., NOPE:]), ((0, 0), (0, 0), (NOPE, HEAD_PAD - QK_DIM)))
    wq = jnp.concatenate([q1.reshape(Q_LORA, QK_PAD), q2.reshape(Q_LORA, QK_PAD)], axis=1).astype(BF16)
    wk3 = jnp.pad(w_uk[l].reshape(KV_LORA, N_HEADS, NOPE), ((0, 0), (0, 0), (0, HEAD_PAD - NOPE)))
    wv4 = w_uv[l].reshape(KV_LORA, N_HEADS // 2, 2, V_DIM)
    zv = jnp.zeros_like(wv4[:, :, 0])
    wv = jnp.stack([jnp.concatenate([wv4[:, :, 0], zv], axis=-1),
                    jnp.concatenate([zv, wv4[:, :, 1]], axis=-1)], axis=2).reshape(KV_LORA, QK_PAD)
    ones_col = jnp.zeros((2, HEAD_PAD), F32).at[0, V_DIM].set(1.0).at[1, 0].set(1.0)
    vones = jnp.tile(ones_col.reshape(1, 2 * HEAD_PAD), (1, N_HEADS // 2))
    wkv = jnp.concatenate([wk3.reshape(KV_LORA, QK_PAD), wv], axis=1).astype(BF16)
    t1q, t2q = _head_tables(q_head_norm_g[l], cos, sin)
    t1k, t2k = _head_tables(k_head_norm_g[l], cos, sin)
    proj_consts = (mix_norm_g[l][None], win, q_lora_norm_g[l][None], wq,
                   kv_lora_norm_g[l][None], wkv, vones)

    tm = 512
    tabs_f = tuple(t[N_META:] for t in (t1q, t2q, t1k, t2k))
    u2, q2d, k2d, v2d = _proj_call(x.reshape(batch * seq, D_MODEL), tm, tabs_f, seq // tm,
                                   proj_consts)
    tabs_m = tuple(t[:N_META] for t in (t1q, t2q, t1k, t2k))
    u_meta, _, k_meta, v_meta = _proj_call(meta_tokens, N_META, tabs_m, 1, proj_consts)

    dt = jnp.exp(ssm_log_dt[l])[:, None]
    lr, li = ssm_a_re[l], ssm_a_im[l]
    mag = jnp.exp(lr * dt)
    ar = mag * jnp.cos(li * dt)
    ai = mag * jnp.sin(li * dt)
    den = lr * lr + li * li
    fr = ((ar - 1.0) * lr + ai * li) / den
    fi = (ai * lr - (ar - 1.0) * li) / den
    br, bi = ssm_b_re[l], ssm_b_im[l]
    bbr = fr[..., None] * br - fi[..., None] * bi
    bbi = fr[..., None] * bi + fi[..., None] * br
    eye = jnp.eye(N_GROUPS, dtype=F32)

    def in_map(bb):
        full = jnp.einsum('gnc,gh->gchn', bb, eye).reshape(D_SSM, N_STATE_COLS)
        return jnp.stack([full[:256, :1024], full[256:, 1024:]]).astype(BF16)

    def out_map(cc):
        full = jnp.einsum('gcn,gh->gnhc', cc, eye).reshape(N_STATE_COLS, D_SSM)
        return jnp.stack([full[:1024, :256], full[1024:, 256:]]).astype(BF16)

    ar_rows = jnp.broadcast_to(ar.reshape(N_SLABS, 1, LANES), (N_SLABS, 2 * batch, LANES))
    ai_flat = ai.reshape(N_SLABS, 1, LANES)
    ai_rows = jnp.concatenate([jnp.broadcast_to(-ai_flat, (N_SLABS, batch, LANES)),
                               jnp.broadcast_to(ai_flat, (N_SLABS, batch, LANES))], axis=1)
    s5_consts = (in_map(bbr), in_map(bbi), out_map(ssm_c_re[l]), out_map(ssm_c_im[l]),
                 ar_rows, ai_rows, ssm_d[l][None], ssm_w_glu[l].astype(BF16),
                 ssm_b_glu[l][None], ssm_out_norm_g[l][None])
    mixed_ssm = _s5_call(u2.reshape(batch, seq, D_SSM), u_meta, s5_consts)

    pad_rows = ((0, LANES - N_META), (0, 0))
    chunk_of = jnp.arange(ATT_TQ) // CHUNK
    mask = (chunk_of[None, :] <= chunk_of[:, None]).astype(BF16)
    score_bound = (LOG2_E * math.sqrt(QK_DIM) * jnp.max(jnp.abs(q_head_norm_g[l]))
                   * jnp.max(jnp.abs(k_head_norm_g[l])))
    attn_args = (q2d.reshape(batch, seq, QK_PAD), k2d.reshape(batch, seq, QK_PAD),
                 v2d.reshape(batch, seq, QK_PAD),
                 jnp.pad(k_meta, pad_rows), jnp.pad(v_meta, pad_rows), mask)
    y_att = lax.cond(score_bound <= MAX_UNSHIFTED_LOG2_SCORE,
                     functools.partial(_attn_call, online=False),
                     functools.partial(_attn_call, online=True), *attn_args)

    ffn_consts = (att_out_norm_g[l][None], w_out[l].astype(BF16), ffn_norm_g[l][None],
                  w_gate[l].astype(BF16), w_up[l].astype(BF16), w_down[l].astype(BF16))
    out = _ffn_call(x.reshape(batch * seq, D_MODEL), mixed_ssm.reshape(batch * seq, D_SSM),
                    y_att.reshape(batch * seq, N_HEADS * V_DIM), ffn_consts)
    return out.reshape(batch, seq, D_MODEL)
```

```python
import functools
import math

import jax
import jax.numpy as jnp
from jax import lax
from jax.experimental import pallas as pl
from jax.experimental.pallas import tpu as pltpu

F32 = jnp.float32
BF16 = jnp.bfloat16

D_MODEL = 1024
N_META = 16
CHUNK = 64
D_SSM = 512
SSM_GROUP = 16
N_GROUPS = D_SSM // SSM_GROUP
SSM_STATE = 64
N_HEADS = 8
V_DIM = 64
NOPE = 64
ROPE = 32
HALF_ROPE = ROPE // 2
QK_DIM = NOPE + ROPE
Q_LORA = 256
KV_LORA = 128
D_FF = 2816
ROPE_BASE = 10000.0
EPS = 1e-6
LOG2_E = math.log2(math.e)
MAX_UNSHIFTED_LOG2_SCORE = 40.0

LANES = 128
SUBLANES = 8
HEAD_PAD = LANES
QK_PAD = N_HEADS * HEAD_PAD
N_STATE_COLS = N_GROUPS * SSM_STATE
N_SLABS = N_STATE_COLS // LANES
S5_CHUNK = 128
S5_PITCH = S5_CHUNK + SUBLANES
PROJ_COLS = D_SSM + Q_LORA + KV_LORA + 2 * HEAD_PAD
VMEM_LIMIT = 56 * 1024 * 1024


def _rms(x, g):
    return x * lax.rsqrt(jnp.mean(x * x, axis=-1, keepdims=True) + EPS) * g


def _proj_kernel(x_ref, gmix_ref, win_ref, gq_ref, wq_ref, gkv_ref, wkv_ref, vones_ref,
                 t1q_ref, t2q_ref, t1k_ref, t2k_ref,
                 u_ref, q_ref, k_ref, v_ref):
    x = x_ref[...]
    xn = _rms(x, gmix_ref[...]).astype(BF16)
    p = jnp.dot(xn, win_ref[...], preferred_element_type=F32)
    u_ref[...] = p[:, :D_SSM]

    cq = p[:, D_SSM:D_SSM + Q_LORA]
    cqn = _rms(cq, gq_ref[...]).astype(BF16)
    q12 = jnp.dot(cqn, wq_ref[...], preferred_element_type=F32)
    t1q = t1q_ref[...]
    t2q = t2q_ref[...]
    scale = QK_DIM ** -0.5 * LOG2_E
    for h in range(N_HEADS):
        q1 = q12[:, h * HEAD_PAD:(h + 1) * HEAD_PAD]
        q2 = q12[:, QK_PAD + h * HEAD_PAD:QK_PAD + (h + 1) * HEAD_PAD]
        r = lax.rsqrt(jnp.sum(q1 * q1, axis=-1, keepdims=True) * (1.0 / QK_DIM) + EPS)
        qh = (q1 * t1q + q2 * t2q) * (r * scale)
        q_ref[:, h * HEAD_PAD:(h + 1) * HEAD_PAD] = qh.astype(BF16)

    c0 = D_SSM + Q_LORA
    ckv = p[:, c0:c0 + KV_LORA]
    ckvn = _rms(ckv, gkv_ref[...]).astype(BF16)
    kv = jnp.dot(ckvn, wkv_ref[...], preferred_element_type=F32)
    v_ref[...] = (kv[:, QK_PAD:] + vones_ref[...]).astype(BF16)
    kr = p[:, c0 + KV_LORA:c0 + KV_LORA + HEAD_PAD]
    kr_rot = p[:, c0 + KV_LORA + HEAD_PAD:c0 + KV_LORA + 2 * HEAD_PAD]
    ss_r = jnp.sum(kr * kr, axis=-1, keepdims=True)
    t1k = t1k_ref[...]
    kr_part = kr_rot * t2k_ref[...]
    for h in range(N_HEADS):
        kn = kv[:, h * HEAD_PAD:(h + 1) * HEAD_PAD]
        ss = jnp.sum(kn * kn, axis=-1, keepdims=True) + ss_r
        r = lax.rsqrt(ss * (1.0 / QK_DIM) + EPS)
        kh = ((kn + kr) * t1k + kr_part) * r
        k_ref[:, h * HEAD_PAD:(h + 1) * HEAD_PAD] = kh.astype(BF16)


def _const_spec(shape):
    nd = len(shape)
    return pl.BlockSpec(shape, lambda *_: (0,) * nd)


def _proj_call(x2d, tm, tabs, n_tab_blocks, consts):
    n_rows = x2d.shape[0]
    grid = (n_rows // tm,)
    row = lambda i: (i, 0)
    tab = lambda i: (i % n_tab_blocks, 0)
    in_specs = ([pl.BlockSpec((tm, D_MODEL), row)] + [_const_spec(c.shape) for c in consts]
                + [pl.BlockSpec((tm, HEAD_PAD), tab)] * 4)
    out_shape = (
        jax.ShapeDtypeStruct((n_rows, D_SSM), F32),
        jax.ShapeDtypeStruct((n_rows, QK_PAD), BF16),
        jax.ShapeDtypeStruct((n_rows, QK_PAD), BF16),
        jax.ShapeDtypeStruct((n_rows, QK_PAD), BF16),
    )
    out_specs = (
        pl.BlockSpec((tm, D_SSM), row),
        pl.BlockSpec((tm, QK_PAD), row),
        pl.BlockSpec((tm, QK_PAD), row),
        pl.BlockSpec((tm, QK_PAD), row),
    )
    return pl.pallas_call(
        _proj_kernel, out_shape=out_shape, grid=grid, in_specs=in_specs, out_specs=out_specs,
        compiler_params=pltpu.CompilerParams(dimension_semantics=("parallel",),
                                             vmem_limit_bytes=VMEM_LIMIT),
        name="proj_mla",
    )(x2d, *consts, *tabs)


def _s5_kernel(u_ref, um_ref, bre_ref, bim_ref, cre_ref, cim_ref, ar_ref, ai_ref,
               d_ref, wglu_ref, bglu_ref, g_ref, o_ref, xs_ref, h_ref, *, batch):
    j = pl.program_id(0)
    half_cols = N_STATE_COLS // 2
    slabs_per_half = N_SLABS // 2

    def project_in(ub, rows):
        for kh in range(2):
            lhs = ub[:, kh * 256:(kh + 1) * 256]
            xre = jnp.dot(lhs, bre_ref[kh], preferred_element_type=F32)
            xim = jnp.dot(lhs, bim_ref[kh], preferred_element_type=F32)
            for cl in range(slabs_per_half):
                c = kh * slabs_per_half + cl
                for b in range(batch):
                    xs_ref[c, pl.ds(b * S5_PITCH, rows), :] = (
                        xre[b * rows:(b + 1) * rows, cl * LANES:(cl + 1) * LANES])
                    xs_ref[c, pl.ds((batch + b) * S5_PITCH, rows), :] = (
                        xim[b * rows:(b + 1) * rows, cl * LANES:(cl + 1) * LANES])

    def scan(n_steps):
        def body(t, hs):
            new = []
            for c in range(N_SLABS):
                rows = pl.ds(t, 2 * batch, stride=S5_PITCH)
                x8 = xs_ref[c, rows, :]
                h = hs[c]
                hn = ar_ref[c] * h + ai_ref[c] * pltpu.roll(h, batch, 0) + x8
                xs_ref[c, rows, :] = hn
                new.append(hn)
            return tuple(new)

        hs = tuple(h_ref[c] for c in range(N_SLABS))
        hs = lax.fori_loop(0, n_steps, body, hs, unroll=4)
        for c in range(N_SLABS):
            h_ref[c] = hs[c]

    @pl.when(j == 0)
    def _():
        h_ref[...] = jnp.zeros_like(h_ref)
        um = um_ref[...].astype(BF16)
        project_in(jnp.concatenate([um] * batch, axis=0), N_META)
        scan(N_META)

    uf = u_ref[...].reshape(batch * S5_CHUNK, D_SSM)
    project_in(uf.astype(BF16), S5_CHUNK)
    scan(S5_CHUNK)

    ys = []
    for nh in range(2):
        def gather(plane0):
            return jnp.concatenate(
                [jnp.concatenate(
                    [xs_ref[nh * slabs_per_half + cl, pl.ds((plane0 + b) * S5_PITCH, S5_CHUNK), :]
                     for cl in range(slabs_per_half)], axis=1)
                 for b in range(batch)], axis=0).astype(BF16)
        yre = jnp.dot(gather(0), cre_ref[nh], preferred_element_type=F32)
        yim = jnp.dot(gather(batch), cim_ref[nh], preferred_element_type=F32)
        ys.append(yre - yim)
    y = jnp.concatenate(ys, axis=1) + d_ref[...] * uf
    z = 0.5 * y * (1.0 + jnp.tanh(math.sqrt(2.0 / math.pi) * (y + 0.044715 * (y * y * y))))
    gate = jnp.dot(z.astype(BF16), wglu_ref[...], preferred_element_type=F32) + bglu_ref[...]
    out = z * (1.0 / (1.0 + jnp.exp(-gate)))
    o_ref[...] = _rms(out, g_ref[...]).astype(BF16).reshape(batch, S5_CHUNK, D_SSM)


def _s5_call(u3, u_meta, consts):
    batch, seq, _ = u3.shape
    assert 2 * batch == SUBLANES and seq % S5_CHUNK == 0
    grid = (seq // S5_CHUNK,)
    in_specs = [pl.BlockSpec((batch, S5_CHUNK, D_SSM), lambda j: (0, j, 0)),
                _const_spec(u_meta.shape)] + [_const_spec(c.shape) for c in consts]
    return pl.pallas_call(
        functools.partial(_s5_kernel, batch=batch),
        out_shape=jax.ShapeDtypeStruct((batch, seq, D_SSM), BF16),
        grid=grid, in_specs=in_specs,
        out_specs=pl.BlockSpec((batch, S5_CHUNK, D_SSM), lambda j: (0, j, 0)),
        scratch_shapes=[pltpu.VMEM((N_SLABS, 2 * batch * S5_PITCH, LANES), F32),
                        pltpu.VMEM((N_SLABS, 2 * batch, LANES), F32)],
        compiler_params=pltpu.CompilerParams(dimension_semantics=("arbitrary",),
                                             vmem_limit_bytes=VMEM_LIMIT),
        name="s5_mixer",
    )(u3, u_meta, *consts)


ATT_TQ = 512
ATT_TK = 512
ATT_HEADS = 4


def _attn_kernel(q_ref, k_ref, v_ref, km_ref, vm_ref, mask_ref, o_ref, *, online):
    nt = (((1,), (1,)), ((), ()))
    n_q = q_ref.shape[0] // ATT_TQ
    head_lanes = [slice(h * HEAD_PAD, (h + 1) * HEAD_PAD) for h in range(ATT_HEADS)]

    def step(state, q, kblk, vblk, mask):
        s = lax.dot_general(q, kblk, nt, preferred_element_type=F32)
        if online:
            m, acc = state
            if mask is not None:
                s = jnp.where(mask > 0, s, -jnp.inf)
            m_new = jnp.maximum(m, jnp.max(s, axis=-1, keepdims=True))
            p = jnp.exp2(s - m_new).astype(BF16)
            acc = jnp.exp2(m - m_new) * acc + jnp.dot(p, vblk, preferred_element_type=F32)
            return m_new, acc
        (acc,) = state
        p = jnp.exp2(s).astype(BF16)
        if mask is not None:
            p = p * mask
        return (acc + jnp.dot(p, vblk, preferred_element_type=F32),)

    def key_rows(kb):
        return pl.ds(pl.multiple_of(kb * ATT_TK, ATT_TK), ATT_TK)

    def q_tile(qi, _):
        rows = key_rows(qi)
        qs = [q_ref[rows, hl] for hl in head_lanes]
        zeros = jnp.zeros((ATT_TQ, HEAD_PAD), F32)
        init = (jnp.full((ATT_TQ, 1), -1e30, F32), zeros) if online else (zeros,)

        def body(kb, states):
            return tuple(step(states[h], qs[h], k_ref[key_rows(kb), hl], v_ref[key_rows(kb), hl], None)
                         for h, hl in enumerate(head_lanes))

        states = lax.fori_loop(0, qi, body, (init,) * ATT_HEADS)

        mask = mask_ref[...]
        accs = []
        for h, hl in enumerate(head_lanes):
            kblk = jnp.concatenate([km_ref[:, hl], k_ref[rows, hl]], axis=0)
            vblk = jnp.concatenate([vm_ref[:, hl], v_ref[rows, hl]], axis=0)
            accs.append(step(states[h], qs[h], kblk, vblk, mask)[-1])
        lane = lax.broadcasted_iota(jnp.int32, (ATT_TQ, HEAD_PAD), 1)
        for hp in range(ATT_HEADS // 2):
            even, odd = accs[2 * hp], accs[2 * hp + 1]
            o_even = even * (1.0 / even[:, V_DIM:V_DIM + 1])
            o_odd = odd * (1.0 / odd[:, 0:1])
            o_ref[rows, hp * HEAD_PAD:(hp + 1) * HEAD_PAD] = (
                jnp.where(lane < V_DIM, o_even, o_odd).astype(BF16))
        return 0

    lax.fori_loop(0, n_q, q_tile, 0)


def _attn_call(q3, k3, v3, k_meta, v_meta, mask, *, online):
    batch, seq, _ = q3.shape
    grid = (batch, N_HEADS // ATT_HEADS)
    seq_blk = pl.BlockSpec((None, seq, ATT_HEADS * HEAD_PAD), lambda b, hg: (b, 0, hg))
    meta_blk = pl.BlockSpec((LANES, ATT_HEADS * HEAD_PAD), lambda b, hg: (0, hg))
    return pl.pallas_call(
        functools.partial(_attn_kernel, online=online),
        out_shape=jax.ShapeDtypeStruct((batch, seq, N_HEADS * V_DIM), BF16),
        grid=grid,
        in_specs=[seq_blk, seq_blk, seq_blk, meta_blk, meta_blk, _const_spec(mask.shape)],
        out_specs=pl.BlockSpec((None, seq, ATT_HEADS * V_DIM), lambda b, hg: (b, 0, hg)),
        compiler_params=pltpu.CompilerParams(dimension_semantics=("parallel", "parallel"),
                                             vmem_limit_bytes=VMEM_LIMIT),
        name="mla_attention_online" if online else "mla_attention",
    )(q3, k3, v3, k_meta, v_meta, mask)


FFN_TM = 512


def _ffn_kernel(x_ref, ms_ref, oa_ref, gatt_ref, wout_ref, gffn_ref, wg_ref, wu_ref, wd_ref,
                out_ref):
    ya = _rms(oa_ref[...].astype(F32), gatt_ref[...]).astype(BF16)
    mixed = jnp.concatenate([ms_ref[...], ya], axis=1)
    h1 = x_ref[...] + jnp.dot(mixed, wout_ref[...], preferred_element_type=F32)
    hn = _rms(h1, gffn_ref[...]).astype(BF16)
    g = jnp.dot(hn, wg_ref[...], preferred_element_type=F32)
    u = jnp.dot(hn, wu_ref[...], preferred_element_type=F32)
    a = (g * (1.0 / (1.0 + jnp.exp(-g))) * u).astype(BF16)
    out_ref[...] = h1 + jnp.dot(a, wd_ref[...], preferred_element_type=F32)


def _ffn_call(x2d, ms2d, oa2d, consts):
    n_rows = x2d.shape[0]
    tm = FFN_TM
    row = lambda i: (i, 0)
    once = pl.Buffered(1)
    in_specs = [pl.BlockSpec((tm, D_MODEL), row),
                pl.BlockSpec((tm, D_SSM), row),
                pl.BlockSpec((tm, N_HEADS * V_DIM), row)]
    in_specs += [pl.BlockSpec(c.shape, lambda i: (0, 0), pipeline_mode=once) for c in consts]
    return pl.pallas_call(
        _ffn_kernel,
        out_shape=jax.ShapeDtypeStruct((n_rows, D_MODEL), F32),
        grid=(n_rows // tm,), in_specs=in_specs,
        out_specs=pl.BlockSpec((tm, D_MODEL), row),
        compiler_params=pltpu.CompilerParams(dimension_semantics=("parallel",),
                                             vmem_limit_bytes=VMEM_LIMIT),
        name="outproj_ffn",
    )(x2d, ms2d, oa2d, *consts)


def _rope_tables(length):
    pos = jnp.arange(length, dtype=F32)
    inv_freq = 1.0 / (ROPE_BASE ** (jnp.arange(0, ROPE, 2, dtype=F32) / ROPE))
    ang = pos[:, None] * inv_freq[None, :]
    return jnp.cos(ang), jnp.sin(ang)


def _head_tables(gain, cos, sin):
    length = cos.shape[0]
    g_n, g_r = gain[:NOPE], gain[NOPE:]
    g_r_swapped = jnp.concatenate([g_r[HALF_ROPE:], g_r[:HALF_ROPE]])
    cos2 = jnp.concatenate([cos, cos], axis=1)
    sin2 = jnp.concatenate([sin, sin], axis=1)
    pad = jnp.zeros((length, HEAD_PAD - QK_DIM), F32)
    t1 = jnp.concatenate([jnp.broadcast_to(g_n, (length, NOPE)), g_r * cos2, pad], axis=1)
    t2 = jnp.concatenate([jnp.zeros((length, NOPE), F32), g_r_swapped * sin2, pad], axis=1)
    return t1, t2


def _rot_half_cols(w):
    return jnp.concatenate([-w[..., HALF_ROPE:], w[..., :HALF_ROPE]], axis=-1)


def _pad_cols(w, left, total):
    return jnp.pad(w, ((0, 0), (left, total - left - w.shape[1])))


def kernel(x, meta_tokens, mix_norm_g, w_in, ssm_a_re, ssm_a_im, ssm_log_dt, ssm_b_re, ssm_b_im,
           ssm_c_re, ssm_c_im, ssm_d, ssm_w_glu, ssm_b_glu, q_lora_norm_g, w_uq, kv_lora_norm_g,
           w_uk, w_uv, q_head_norm_g, k_head_norm_g, ssm_out_norm_g, att_out_norm_g, w_out,
           ffn_norm_g, w_gate, w_up, w_down):
    batch, seq, _ = x.shape
    depth = w_in.shape[0]
    assert depth == 1
    l = 0
    length = N_META + seq
    cos, sin = _rope_tables(length)

    wi = w_in[l]
    o_q, o_kv, o_r = D_SSM, D_SSM + Q_LORA, D_SSM + Q_LORA + KV_LORA
    w_r = wi[:, o_r:]
    win = jnp.concatenate([wi[:, :o_r], _pad_cols(w_r, NOPE, HEAD_PAD),
                           _pad_cols(_rot_half_cols(w_r), NOPE, HEAD_PAD)], axis=1).astype(BF16)
    wq3 = w_uq[l].reshape(Q_LORA, N_HEADS, QK_DIM)
    q1 = jnp.pad(wq3, ((0, 0), (0, 0), (0, HEAD_PAD - QK_DIM)))
    q2 = jnp.pad(_rot_half_cols(wq3[..., NOPE:]), ((0, 0), (0, 0), (NOPE, HEAD_PAD - QK_DIM)))
    wq = jnp.concatenate([q1.reshape(Q_LORA, QK_PAD), q2.reshape(Q_LORA, QK_PAD)], axis=1).astype(BF16)
    wk3 = jnp.pad(w_uk[l].reshape(KV_LORA, N_HEADS, NOPE), ((0, 0), (0, 0), (0, HEAD_PAD - NOPE)))
    wv4 = w_uv[l].reshape(KV_LORA, N_HEADS // 2, 2, V_DIM)
    zv = jnp.zeros_like(wv4[:, :, 0])
    wv = jnp.stack([jnp.concatenate([wv4[:, :, 0], zv], axis=-1),
                    jnp.concatenate([zv, wv4[:, :, 1]], axis=-1)], axis=2).reshape(KV_LORA, QK_PAD)
    ones_col = jnp.zeros((2, HEAD_PAD), F32).at[0, V_DIM].set(1.0).at[1, 0].set(1.0)
    vones = jnp.tile(ones_col.reshape(1, 2 * HEAD_PAD), (1, N_HEADS // 2))
    wkv = jnp.concatenate([wk3.reshape(KV_LORA, QK_PAD), wv], axis=1).astype(BF16)
    t1q, t2q = _head_tables(q_head_norm_g[l], cos, sin)
    t1k, t2k = _head_tables(k_head_norm_g[l], cos, sin)
    proj_consts = (mix_norm_g[l][None], win, q_lora_norm_g[l][None], wq,
                   kv_lora_norm_g[l][None], wkv, vones)

    tm = 512
    tabs_f = tuple(t[N_META:] for t in (t1q, t2q, t1k, t2k))
    u2, q2d, k2d, v2d = _proj_call(x.reshape(batch * seq, D_MODEL), tm, tabs_f, seq // tm,
                                   proj_consts)
    tabs_m = tuple(t[:N_META] for t in (t1q, t2q, t1k, t2k))
    u_meta, _, k_meta, v_meta = _proj_call(meta_tokens, N_META, tabs_m, 1, proj_consts)

    dt = jnp.exp(ssm_log_dt[l])[:, None]
    lr, li = ssm_a_re[l], ssm_a_im[l]
    mag = jnp.exp(lr * dt)
    ar = mag * jnp.cos(li * dt)
    ai = mag * jnp.sin(li * dt)
    den = lr * lr + li * li
    fr = ((ar - 1.0) * lr + ai * li) / den
    fi = (ai * lr - (ar - 1.0) * li) / den
    br, bi = ssm_b_re[l], ssm_b_im[l]
    bbr = fr[..., None] * br - fi[..., None] * bi
    bbi = fr[..., None] * bi + fi[..., None] * br
    half_groups = N_GROUPS // 2

    def block_diag_halves(blocks):
        _, r, c = blocks.shape
        rows = blocks.reshape(2, half_groups * r, c)
        tiled = jnp.tile(rows, (1, 1, half_groups))
        row_g = jnp.arange(half_groups * r) // r
        col_g = jnp.arange(half_groups * c) // c
        return jnp.where(row_g[:, None] == col_g[None, :], tiled, 0.0).astype(BF16)

    def in_map(bb):
        return block_diag_halves(jnp.swapaxes(bb, 1, 2))

    def out_map(cc):
        return block_diag_halves(jnp.swapaxes(cc, 1, 2))

    ar_rows = jnp.broadcast_to(ar.reshape(N_SLABS, 1, LANES), (N_SLABS, 2 * batch, LANES))
    ai_flat = ai.reshape(N_SLABS, 1, LANES)
    ai_rows = jnp.concatenate([jnp.broadcast_to(-ai_flat, (N_SLABS, batch, LANES)),
                               jnp.broadcast_to(ai_flat, (N_SLABS, batch, LANES))], axis=1)
    s5_consts = (in_map(bbr), in_map(bbi), out_map(ssm_c_re[l]), out_map(ssm_c_im[l]),
                 ar_rows, ai_rows, ssm_d[l][None], ssm_w_glu[l].astype(BF16),
                 ssm_b_glu[l][None], ssm_out_norm_g[l][None])
    mixed_ssm = _s5_call(u2.reshape(batch, seq, D_SSM), u_meta, s5_consts)

    pad_rows = ((0, LANES - N_META), (0, 0))
    chunk_of = jnp.arange(ATT_TQ) // CHUNK
    causal = (chunk_of[None, :] <= chunk_of[:, None]).astype(BF16)
    mask = jnp.concatenate([jnp.ones((ATT_TQ, LANES), BF16), causal], axis=1)
    score_bound = (LOG2_E * math.sqrt(QK_DIM) * jnp.max(jnp.abs(q_head_norm_g[l]))
                   * jnp.max(jnp.abs(k_head_norm_g[l])))
    attn_args = (q2d.reshape(batch, seq, QK_PAD), k2d.reshape(batch, seq, QK_PAD),
                 v2d.reshape(batch, seq, QK_PAD),
                 jnp.pad(k_meta, pad_rows), jnp.pad(v_meta, pad_rows), mask)
    y_att = lax.cond(score_bound <= MAX_UNSHIFTED_LOG2_SCORE,
                     functools.partial(_attn_call, online=False),
                     functools.partial(_attn_call, online=True), *attn_args)

    ffn_consts = (att_out_norm_g[l][None], w_out[l].astype(BF16), ffn_norm_g[l][None],
                  w_gate[l].astype(BF16), w_up[l].astype(BF16), w_down[l].astype(BF16))
    out = _ffn_call(x.reshape(batch * seq, D_MODEL), mixed_ssm.reshape(batch * seq, D_SSM),
                    y_att.reshape(batch * seq, N_HEADS * V_DIM), ffn_consts)
    return out.reshape(batch, seq, D_MODEL)
```

```python
import functools
import math

import jax
import jax.numpy as jnp
from jax import lax
from jax.experimental import pallas as pl
from jax.experimental.pallas import tpu as pltpu

F32 = jnp.float32
BF16 = jnp.bfloat16

D_MODEL = 1024
N_META = 16
CHUNK = 64
D_SSM = 512
SSM_GROUP = 16
N_GROUPS = D_SSM // SSM_GROUP
SSM_STATE = 64
N_HEADS = 8
V_DIM = 64
NOPE = 64
ROPE = 32
HALF_ROPE = ROPE // 2
QK_DIM = NOPE + ROPE
Q_LORA = 256
KV_LORA = 128
D_FF = 2816
ROPE_BASE = 10000.0
EPS = 1e-6
LOG2_E = math.log2(math.e)
MAX_UNSHIFTED_LOG2_SCORE = 40.0

LANES = 128
SUBLANES = 8
HEAD_PAD = LANES
QK_PAD = N_HEADS * HEAD_PAD
N_STATE_COLS = N_GROUPS * SSM_STATE
N_SLABS = N_STATE_COLS // LANES
S5_CHUNK = 128
S5_PITCH = S5_CHUNK + SUBLANES
PROJ_COLS = D_SSM + Q_LORA + KV_LORA + 2 * HEAD_PAD
VMEM_LIMIT = 56 * 1024 * 1024


def _rms(x, g):
    return x * lax.rsqrt(jnp.mean(x * x, axis=-1, keepdims=True) + EPS) * g


def _proj_kernel(x_ref, gmix_ref, win_ref, gq_ref, wq_ref, gkv_ref, wkv_ref, vones_ref,
                 t1q_ref, t2q_ref, t1k_ref, t2k_ref,
                 u_ref, q_ref, k_ref, v_ref):
    x = x_ref[...]
    xn = _rms(x, gmix_ref[...]).astype(BF16)
    p = jnp.dot(xn, win_ref[...], preferred_element_type=F32)
    u_ref[...] = p[:, :D_SSM]

    cq = p[:, D_SSM:D_SSM + Q_LORA]
    cqn = _rms(cq, gq_ref[...]).astype(BF16)
    q12 = jnp.dot(cqn, wq_ref[...], preferred_element_type=F32)
    t1q = t1q_ref[...]
    t2q = t2q_ref[...]
    scale = QK_DIM ** -0.5 * LOG2_E
    for h in range(N_HEADS):
        q1 = q12[:, h * HEAD_PAD:(h + 1) * HEAD_PAD]
        q2 = q12[:, QK_PAD + h * HEAD_PAD:QK_PAD + (h + 1) * HEAD_PAD]
        r = lax.rsqrt(jnp.sum(q1 * q1, axis=-1, keepdims=True) * (1.0 / QK_DIM) + EPS)
        qh = (q1 * t1q + q2 * t2q) * (r * scale)
        q_ref[:, h * HEAD_PAD:(h + 1) * HEAD_PAD] = qh.astype(BF16)

    c0 = D_SSM + Q_LORA
    ckv = p[:, c0:c0 + KV_LORA]
    ckvn = _rms(ckv, gkv_ref[...]).astype(BF16)
    kv = jnp.dot(ckvn, wkv_ref[...], preferred_element_type=F32)
    v_ref[...] = (kv[:, QK_PAD:] + vones_ref[...]).astype(BF16)
    kr = p[:, c0 + KV_LORA:c0 + KV_LORA + HEAD_PAD]
    kr_rot = p[:, c0 + KV_LORA + HEAD_PAD:c0 + KV_LORA + 2 * HEAD_PAD]
    ss_r = jnp.sum(kr * kr, axis=-1, keepdims=True)
    t1k = t1k_ref[...]
    kr_part = kr_rot * t2k_ref[...]
    for h in range(N_HEADS):
        kn = kv[:, h * HEAD_PAD:(h + 1) * HEAD_PAD]
        ss = jnp.sum(kn * kn, axis=-1, keepdims=True) + ss_r
        r = lax.rsqrt(ss * (1.0 / QK_DIM) + EPS)
        kh = ((kn + kr) * t1k + kr_part) * r
        k_ref[:, h * HEAD_PAD:(h + 1) * HEAD_PAD] = kh.astype(BF16)


def _const_spec(shape):
    nd = len(shape)
    return pl.BlockSpec(shape, lambda *_: (0,) * nd)


def _proj_call(x2d, tm, tabs, n_tab_blocks, consts):
    n_rows = x2d.shape[0]
    grid = (n_rows // tm,)
    row = lambda i: (i, 0)
    tab = lambda i: (i % n_tab_blocks, 0)
    in_specs = ([pl.BlockSpec((tm, D_MODEL), row)] + [_const_spec(c.shape) for c in consts]
                + [pl.BlockSpec((tm, HEAD_PAD), tab)] * 4)
    out_shape = (
        jax.ShapeDtypeStruct((n_rows, D_SSM), F32),
        jax.ShapeDtypeStruct((n_rows, QK_PAD), BF16),
        jax.ShapeDtypeStruct((n_rows, QK_PAD), BF16),
        jax.ShapeDtypeStruct((n_rows, QK_PAD), BF16),
    )
    out_specs = (
        pl.BlockSpec((tm, D_SSM), row),
        pl.BlockSpec((tm, QK_PAD), row),
        pl.BlockSpec((tm, QK_PAD), row),
        pl.BlockSpec((tm, QK_PAD), row),
    )
    return pl.pallas_call(
        _proj_kernel, out_shape=out_shape, grid=grid, in_specs=in_specs, out_specs=out_specs,
        compiler_params=pltpu.CompilerParams(dimension_semantics=("parallel",),
                                             vmem_limit_bytes=VMEM_LIMIT),
        name="proj_mla",
    )(x2d, *consts, *tabs)


def _s5_kernel(up_ref, ue_ref, um_ref, bre_ref, bim_ref, cre_ref, cim_ref, ar_ref, ai_ref,
               d_ref, wglu_ref, bglu_ref, g_ref, o_ref, x0_ref, x1_ref, h0_ref, h1_ref, hc_ref,
               *, batch, n_chunks):
    s = pl.program_id(0)
    slabs_per_half = N_SLABS // 2

    def input_map_pieces(ub, rows, x_ref):
        def piece(kh, w_ref, plane0):
            x = jnp.dot(ub[:, kh * 256:(kh + 1) * 256], w_ref[kh],
                        preferred_element_type=F32)
            for cl in range(slabs_per_half):
                for b in range(batch):
                    x_ref[kh * slabs_per_half + cl, pl.ds((plane0 + b) * S5_PITCH, rows), :] = (
                        x[b * rows:(b + 1) * rows, cl * LANES:(cl + 1) * LANES])
        return [functools.partial(piece, kh, w_ref, plane0)
                for kh in range(2) for w_ref, plane0 in ((bre_ref, 0), (bim_ref, batch))]

    def scan(t0, t1, x_ref, h_ref, hs):
        hs = list(hs)
        for t in range(t0, t1):
            rows = pl.ds(t, 2 * batch, stride=S5_PITCH)
            for c in range(N_SLABS):
                h = hs[c]
                hn = ar_ref[c] * h + ai_ref[c] * pltpu.roll(h, batch, 0) + x_ref[c, rows, :]
                h_ref[c, rows, :] = hn
                hs[c] = hn
        return hs

    def output_map_pieces(h_ref):
        ys = {}

        def piece(nh, w_ref, plane0):
            hb = jnp.concatenate(
                [jnp.concatenate(
                    [h_ref[nh * slabs_per_half + cl, pl.ds((plane0 + b) * S5_PITCH, S5_CHUNK), :]
                     for cl in range(slabs_per_half)], axis=1)
                 for b in range(batch)], axis=0).astype(BF16)
            ys[nh, plane0] = jnp.dot(hb, w_ref[nh], preferred_element_type=F32)

        def tail():
            uf = ue_ref[...].reshape(batch * S5_CHUNK, D_SSM)
            y = jnp.concatenate([ys[nh, 0] - ys[nh, batch] for nh in range(2)], axis=1)
            y = y + d_ref[...] * uf
            z = 0.5 * y * (1.0 + jnp.tanh(math.sqrt(2.0 / math.pi) * (y + 0.044715 * (y * y * y))))
            gate = (jnp.dot(z.astype(BF16), wglu_ref[...], preferred_element_type=F32)
                    + bglu_ref[...])
            out = z * (1.0 / (1.0 + jnp.exp(-gate)))
            o_ref[...] = _rms(out, g_ref[...]).astype(BF16).reshape(batch, S5_CHUNK, D_SSM)

        return [functools.partial(piece, nh, w_ref, plane0)
                for nh in range(2) for w_ref, plane0 in ((cre_ref, 0), (cim_ref, batch))] + [tail]

    @pl.when(s == 0)
    def _():
        x1_ref[...] = jnp.zeros_like(x1_ref)
        h0_ref[...] = jnp.zeros_like(h0_ref)
        um = um_ref[...].astype(BF16)
        for piece in input_map_pieces(jnp.concatenate([um] * batch, axis=0), N_META, x1_ref):
            piece()
        hs = scan(0, N_META, x1_ref, h1_ref, [jnp.zeros((2 * batch, LANES), F32)] * N_SLABS)
        for c in range(N_SLABS):
            hc_ref[c] = hs[c]

    def step(x_new, x_prev, h_prev, h_old):
        ub = up_ref[...].reshape(batch * S5_CHUNK, D_SSM).astype(BF16)
        pieces = output_map_pieces(h_old) + input_map_pieces(ub, S5_CHUNK, x_new)
        per_piece = -(-S5_CHUNK // len(pieces))
        h_in = [hc_ref[c] for c in range(N_SLABS)]
        hs = h_in
        for i, piece in enumerate(pieces):
            piece()
            hs = scan(min(i * per_piece, S5_CHUNK), min((i + 1) * per_piece, S5_CHUNK),
                      x_prev, h_prev, hs)
        real_scan = jnp.logical_and(s >= 1, s <= n_chunks)
        for c in range(N_SLABS):
            hc_ref[c] = jnp.where(real_scan, hs[c], h_in[c])

    parity = lax.rem(s, 2)

    @pl.when(parity == 0)
    def _():
        step(x0_ref, x1_ref, h1_ref, h0_ref)

    @pl.when(parity == 1)
    def _():
        step(x1_ref, x0_ref, h0_ref, h1_ref)


def _s5_call(u3, u_meta, consts):
    batch, seq, _ = u3.shape
    assert 2 * batch == SUBLANES and seq % S5_CHUNK == 0
    n_chunks = seq // S5_CHUNK
    chunk_blk = (batch, S5_CHUNK, D_SSM)
    once = pl.Buffered(1)
    in_specs = [pl.BlockSpec(chunk_blk, lambda s: (0, jnp.minimum(s, n_chunks - 1), 0)),
                pl.BlockSpec(chunk_blk, lambda s: (0, jnp.maximum(s - 2, 0), 0))]
    in_specs += [pl.BlockSpec(c.shape, functools.partial(lambda nd, s: (0,) * nd, c.ndim),
                              pipeline_mode=once) for c in (u_meta,) + tuple(consts)]
    planes = pltpu.VMEM((N_SLABS, 2 * batch * S5_PITCH, LANES), F32)
    return pl.pallas_call(
        functools.partial(_s5_kernel, batch=batch, n_chunks=n_chunks),
        out_shape=jax.ShapeDtypeStruct((batch, seq, D_SSM), BF16),
        grid=(n_chunks + 2,), in_specs=in_specs,
        out_specs=pl.BlockSpec(chunk_blk, lambda s: (0, jnp.maximum(s - 2, 0), 0)),
        scratch_shapes=[planes, planes, planes, planes,
                        pltpu.VMEM((N_SLABS, 2 * batch, LANES), F32)],
        compiler_params=pltpu.CompilerParams(dimension_semantics=("arbitrary",),
                                             vmem_limit_bytes=VMEM_LIMIT),
        name="s5_mixer",
    )(u3, u3, u_meta, *consts)


ATT_TQ = 512
ATT_TK = 512
ATT_HEADS = 4


def _attn_kernel(q_ref, k_ref, v_ref, km_ref, vm_ref, mask_ref, o_ref, *, online):
    nt = (((1,), (1,)), ((), ()))
    n_q = q_ref.shape[0] // ATT_TQ
    head_lanes = [slice(h * HEAD_PAD, (h + 1) * HEAD_PAD) for h in range(ATT_HEADS)]

    def step(state, q, kblk, vblk, mask):
        s = lax.dot_general(q, kblk, nt, preferred_element_type=F32)
        if online:
            m, acc = state
            if mask is not None:
                s = jnp.where(mask > 0, s, -jnp.inf)
            m_new = jnp.maximum(m, jnp.max(s, axis=-1, keepdims=True))
            p = jnp.exp2(s - m_new).astype(BF16)
            acc = jnp.exp2(m - m_new) * acc + jnp.dot(p, vblk, preferred_element_type=F32)
            return m_new, acc
        (acc,) = state
        p = jnp.exp2(s).astype(BF16)
        if mask is not None:
            p = p * mask
        return (acc + jnp.dot(p, vblk, preferred_element_type=F32),)

    def key_rows(kb):
        return pl.ds(pl.multiple_of(kb * ATT_TK, ATT_TK), ATT_TK)

    def q_tile(qi, _):
        rows = key_rows(qi)
        qs = [q_ref[rows, hl] for hl in head_lanes]
        zeros = jnp.zeros((ATT_TQ, HEAD_PAD), F32)
        init = (jnp.full((ATT_TQ, 1), -1e30, F32), zeros) if online else (zeros,)

        def body(kb, states):
            return tuple(step(states[h], qs[h], k_ref[key_rows(kb), hl], v_ref[key_rows(kb), hl], None)
                         for h, hl in enumerate(head_lanes))

        states = lax.fori_loop(0, qi, body, (init,) * ATT_HEADS)

        mask = mask_ref[...]
        accs = []
        for h, hl in enumerate(head_lanes):
            kblk = jnp.concatenate([km_ref[:, hl], k_ref[rows, hl]], axis=0)
            vblk = jnp.concatenate([vm_ref[:, hl], v_ref[rows, hl]], axis=0)
            accs.append(step(states[h], qs[h], kblk, vblk, mask)[-1])
        lane = lax.broadcasted_iota(jnp.int32, (ATT_TQ, HEAD_PAD), 1)
        for hp in range(ATT_HEADS // 2):
            even, odd = accs[2 * hp], accs[2 * hp + 1]
            o_even = even * (1.0 / even[:, V_DIM:V_DIM + 1])
            o_odd = odd * (1.0 / odd[:, 0:1])
            o_ref[rows, hp * HEAD_PAD:(hp + 1) * HEAD_PAD] = (
                jnp.where(lane < V_DIM, o_even, o_odd).astype(BF16))
        return 0

    lax.fori_loop(0, n_q, q_tile, 0)


def _attn_call(q3, k3, v3, k_meta, v_meta, mask, *, online):
    batch, seq, _ = q3.shape
    grid = (batch, N_HEADS // ATT_HEADS)
    seq_blk = pl.BlockSpec((None, seq, ATT_HEADS * HEAD_PAD), lambda b, hg: (b, 0, hg))
    meta_blk = pl.BlockSpec((LANES, ATT_HEADS * HEAD_PAD), lambda b, hg: (0, hg))
    return pl.pallas_call(
        functools.partial(_attn_kernel, online=online),
        out_shape=jax.ShapeDtypeStruct((batch, seq, N_HEADS * V_DIM), BF16),
        grid=grid,
        in_specs=[seq_blk, seq_blk, seq_blk, meta_blk, meta_blk, _const_spec(mask.shape)],
        out_specs=pl.BlockSpec((None, seq, ATT_HEADS * V_DIM), lambda b, hg: (b, 0, hg)),
        compiler_params=pltpu.CompilerParams(dimension_semantics=("parallel", "parallel"),
                                             vmem_limit_bytes=VMEM_LIMIT),
        name="mla_attention_online" if online else "mla_attention",
    )(q3, k3, v3, k_meta, v_meta, mask)


FFN_TM = 512


def _ffn_kernel(x_ref, ms_ref, oa_ref, gatt_ref, wout_ref, gffn_ref, wg_ref, wu_ref, wd_ref,
                out_ref):
    ya = _rms(oa_ref[...].astype(F32), gatt_ref[...]).astype(BF16)
    mixed = jnp.concatenate([ms_ref[...], ya], axis=1)
    h1 = x_ref[...] + jnp.dot(mixed, wout_ref[...], preferred_element_type=F32)
    hn = _rms(h1, gffn_ref[...]).astype(BF16)
    g = jnp.dot(hn, wg_ref[...], preferred_element_type=F32)
    u = jnp.dot(hn, wu_ref[...], preferred_element_type=F32)
    a = (g * (1.0 / (1.0 + jnp.exp(-g))) * u).astype(BF16)
    out_ref[...] = h1 + jnp.dot(a, wd_ref[...], preferred_element_type=F32)


def _ffn_call(x2d, ms2d, oa2d, consts):
    n_rows = x2d.shape[0]
    tm = FFN_TM
    row = lambda i: (i, 0)
    once = pl.Buffered(1)
    in_specs = [pl.BlockSpec((tm, D_MODEL), row),
                pl.BlockSpec((tm, D_SSM), row),
                pl.BlockSpec((tm, N_HEADS * V_DIM), row)]
    in_specs += [pl.BlockSpec(c.shape, lambda i: (0, 0), pipeline_mode=once) for c in consts]
    return pl.pallas_call(
        _ffn_kernel,
        out_shape=jax.ShapeDtypeStruct((n_rows, D_MODEL), F32),
        grid=(n_rows // tm,), in_specs=in_specs,
        out_specs=pl.BlockSpec((tm, D_MODEL), row),
        compiler_params=pltpu.CompilerParams(dimension_semantics=("parallel",),
                                             vmem_limit_bytes=VMEM_LIMIT),
        name="outproj_ffn",
    )(x2d, ms2d, oa2d, *consts)


def _rope_tables(length):
    pos = jnp.arange(length, dtype=F32)
    inv_freq = 1.0 / (ROPE_BASE ** (jnp.arange(0, ROPE, 2, dtype=F32) / ROPE))
    ang = pos[:, None] * inv_freq[None, :]
    return jnp.cos(ang), jnp.sin(ang)


def _head_tables(gain, cos, sin):
    length = cos.shape[0]
    g_n, g_r = gain[:NOPE], gain[NOPE:]
    g_r_swapped = jnp.concatenate([g_r[HALF_ROPE:], g_r[:HALF_ROPE]])
    cos2 = jnp.concatenate([cos, cos], axis=1)
    sin2 = jnp.concatenate([sin, sin], axis=1)
    pad = jnp.zeros((length, HEAD_PAD - QK_DIM), F32)
    t1 = jnp.concatenate([jnp.broadcast_to(g_n, (length, NOPE)), g_r * cos2, pad], axis=1)
    t2 = jnp.concatenate([jnp.zeros((length, NOPE), F32), g_r_swapped * sin2, pad], axis=1)
    return t1, t2


def _rot_half_cols(w):
    return jnp.concatenate([-w[..., HALF_ROPE:], w[..., :HALF_ROPE]], axis=-1)


def _pad_cols(w, left, total):
    return jnp.pad(w, ((0, 0), (left, total - left - w.shape[1])))


def kernel(x, meta_tokens, mix_norm_g, w_in, ssm_a_re, ssm_a_im, ssm_log_dt, ssm_b_re, ssm_b_im,
           ssm_c_re, ssm_c_im, ssm_d, ssm_w_glu, ssm_b_glu, q_lora_norm_g, w_uq, kv_lora_norm_g,
           w_uk, w_uv, q_head_norm_g, k_head_norm_g, ssm_out_norm_g, att_out_norm_g, w_out,
           ffn_norm_g, w_gate, w_up, w_down):
    batch, seq, _ = x.shape
    depth = w_in.shape[0]
    assert depth == 1
    l = 0
    length = N_META + seq
    cos, sin = _rope_tables(length)

    wi = w_in[l]
    o_q, o_kv, o_r = D_SSM, D_SSM + Q_LORA, D_SSM + Q_LORA + KV_LORA
    w_r = wi[:, o_r:]
    win = jnp.concatenate([wi[:, :o_r], _pad_cols(w_r, NOPE, HEAD_PAD),
                           _pad_cols(_rot_half_cols(w_r), NOPE, HEAD_PAD)], axis=1).astype(BF16)
    wq3 = w_uq[l].reshape(Q_LORA, N_HEADS, QK_DIM)
    q1 = jnp.pad(wq3, ((0, 0), (0, 0), (0, HEAD_PAD - QK_DIM)))
    q2 = jnp.pad(_rot_half_cols(wq3[..., NOPE:]), ((0, 0), (0, 0), (NOPE, HEAD_PAD - QK_DIM)))
    wq = jnp.concatenate([q1.reshape(Q_LORA, QK_PAD), q2.reshape(Q_LORA, QK_PAD)], axis=1).astype(BF16)
    wk3 = jnp.pad(w_uk[l].reshape(KV_LORA, N_HEADS, NOPE), ((0, 0), (0, 0), (0, HEAD_PAD - NOPE)))
    wv4 = w_uv[l].reshape(KV_LORA, N_HEADS // 2, 2, V_DIM)
    zv = jnp.zeros_like(wv4[:, :, 0])
    wv = jnp.stack([jnp.concatenate([wv4[:, :, 0], zv], axis=-1),
                    jnp.concatenate([zv, wv4[:, :, 1]], axis=-1)], axis=2).reshape(KV_LORA, QK_PAD)
    ones_col = jnp.zeros((2, HEAD_PAD), F32).at[0, V_DIM].set(1.0).at[1, 0].set(1.0)
    vones = jnp.tile(ones_col.reshape(1, 2 * HEAD_PAD), (1, N_HEADS // 2))
    wkv = jnp.concatenate([wk3.reshape(KV_LORA, QK_PAD), wv], axis=1).astype(BF16)
    t1q, t2q = _head_tables(q_head_norm_g[l], cos, sin)
    t1k, t2k = _head_tables(k_head_norm_g[l], cos, sin)
    proj_consts = (mix_norm_g[l][None], win, q_lora_norm_g[l][None], wq,
                   kv_lora_norm_g[l][None], wkv, vones)

    tm = 512
    tabs_f = tuple(t[N_META:] for t in (t1q, t2q, t1k, t2k))
    u2, q2d, k2d, v2d = _proj_call(x.reshape(batch * seq, D_MODEL), tm, tabs_f, seq // tm,
                                   proj_consts)
    tabs_m = tuple(t[:N_META] for t in (t1q, t2q, t1k, t2k))
    u_meta, _, k_meta, v_meta = _proj_call(meta_tokens, N_META, tabs_m, 1, proj_consts)

    dt = jnp.exp(ssm_log_dt[l])[:, None]
    lr, li = ssm_a_re[l], ssm_a_im[l]
    mag = jnp.exp(lr * dt)
    ar = mag * jnp.cos(li * dt)
    ai = mag * jnp.sin(li * dt)
    den = lr * lr + li * li
    fr = ((ar - 1.0) * lr + ai * li) / den
    fi = (ai * lr - (ar - 1.0) * li) / den
    br, bi = ssm_b_re[l], ssm_b_im[l]
    bbr = fr[..., None] * br - fi[..., None] * bi
    bbi = fr[..., None] * bi + fi[..., None] * br
    half_groups = N_GROUPS // 2

    def block_diag_halves(blocks):
        _, r, c = blocks.shape
        rows = blocks.reshape(2, half_groups * r, c)
        tiled = jnp.tile(rows, (1, 1, half_groups))
        row_g = jnp.arange(half_groups * r) // r
        col_g = jnp.arange(half_groups * c) // c
        return jnp.where(row_g[:, None] == col_g[None, :], tiled, 0.0).astype(BF16)

    def in_map(bb):
        return block_diag_halves(jnp.swapaxes(bb, 1, 2))

    def out_map(cc):
        return block_diag_halves(jnp.swapaxes(cc, 1, 2))

    ar_rows = jnp.broadcast_to(ar.reshape(N_SLABS, 1, LANES), (N_SLABS, 2 * batch, LANES))
    ai_flat = ai.reshape(N_SLABS, 1, LANES)
    ai_rows = jnp.concatenate([jnp.broadcast_to(-ai_flat, (N_SLABS, batch, LANES)),
                               jnp.broadcast_to(ai_flat, (N_SLABS, batch, LANES))], axis=1)
    s5_consts = (in_map(bbr), in_map(bbi), out_map(ssm_c_re[l]), out_map(ssm_c_im[l]),
                 ar_rows, ai_rows, ssm_d[l][None], ssm_w_glu[l].astype(BF16),
                 ssm_b_glu[l][None], ssm_out_norm_g[l][None])
    mixed_ssm = _s5_call(u2.reshape(batch, seq, D_SSM), u_meta, s5_consts)

    pad_rows = ((0, LANES - N_META), (0, 0))
    chunk_of = jnp.arange(ATT_TQ) // CHUNK
    causal = (chunk_of[None, :] <= chunk_of[:, None]).astype(BF16)
    mask = jnp.concatenate([jnp.ones((ATT_TQ, LANES), BF16), causal], axis=1)
    score_bound = (LOG2_E * math.sqrt(QK_DIM) * jnp.max(jnp.abs(q_head_norm_g[l]))
                   * jnp.max(jnp.abs(k_head_norm_g[l])))
    attn_args = (q2d.reshape(batch, seq, QK_PAD), k2d.reshape(batch, seq, QK_PAD),
                 v2d.reshape(batch, seq, QK_PAD),
                 jnp.pad(k_meta, pad_rows), jnp.pad(v_meta, pad_rows), mask)
    y_att = lax.cond(score_bound <= MAX_UNSHIFTED_LOG2_SCORE,
                     functools.partial(_attn_call, online=False),
                     functools.partial(_attn_call, online=True), *attn_args)

    ffn_consts = (att_out_norm_g[l][None], w_out[l].astype(BF16), ffn_norm_g[l][None],
                  w_gate[l].astype(BF16), w_up[l].astype(BF16), w_down[l].astype(BF16))
    out = _ffn_call(x.reshape(batch * seq, D_MODEL), mixed_ssm.reshape(batch * seq, D_SSM),
                    y_att.reshape(batch * seq, N_HEADS * V_DIM), ffn_consts)
    return out.reshape(batch, seq, D_MODEL)
```

```python
import functools
import math

import jax
import jax.numpy as jnp
from jax import lax
from jax.experimental import pallas as pl
from jax.experimental.pallas import tpu as pltpu

F32 = jnp.float32
BF16 = jnp.bfloat16

D_MODEL = 1024
N_META = 16
CHUNK = 64
D_SSM = 512
SSM_GROUP = 16
N_GROUPS = D_SSM // SSM_GROUP
SSM_STATE = 64
N_HEADS = 8
V_DIM = 64
NOPE = 64
ROPE = 32
HALF_ROPE = ROPE // 2
QK_DIM = NOPE + ROPE
Q_LORA = 256
KV_LORA = 128
D_FF = 2816
ROPE_BASE = 10000.0
EPS = 1e-6
LOG2_E = math.log2(math.e)
MAX_UNSHIFTED_LOG2_SCORE = 40.0

LANES = 128
SUBLANES = 8
HEAD_PAD = LANES
QK_PAD = N_HEADS * HEAD_PAD
N_STATE_COLS = N_GROUPS * SSM_STATE
N_SLABS = N_STATE_COLS // LANES
S5_CHUNK = 128
S5_PITCH = S5_CHUNK + SUBLANES
VMEM_LIMIT = 56 * 1024 * 1024


def _rms(x, g):
    return x * lax.rsqrt(jnp.mean(x * x, axis=-1, keepdims=True) + EPS) * g


def _proj_kernel(x_ref, gmix_ref, win_ref, gq_ref, wq_ref, gkv_ref, wkv_ref, vones_ref,
                 t1q_ref, t2q_ref, t1k_ref, t2k_ref,
                 u_ref, q_ref, k_ref, v_ref):
    x = x_ref[...]
    xn = _rms(x, gmix_ref[...]).astype(BF16)
    p = jnp.dot(xn, win_ref[...], preferred_element_type=F32)
    u_ref[...] = p[:, :D_SSM]

    cq = p[:, D_SSM:D_SSM + Q_LORA]
    cqn = _rms(cq, gq_ref[...]).astype(BF16)
    q12 = jnp.dot(cqn, wq_ref[...], preferred_element_type=F32)
    t1q = t1q_ref[...]
    t2q = t2q_ref[...]
    scale = QK_DIM ** -0.5 * LOG2_E
    for h in range(N_HEADS):
        q1 = q12[:, h * HEAD_PAD:(h + 1) * HEAD_PAD]
        q2 = q12[:, QK_PAD + h * HEAD_PAD:QK_PAD + (h + 1) * HEAD_PAD]
        r = lax.rsqrt(jnp.sum(q1 * q1, axis=-1, keepdims=True) * (1.0 / QK_DIM) + EPS)
        qh = (q1 * t1q + q2 * t2q) * (r * scale)
        q_ref[:, h * HEAD_PAD:(h + 1) * HEAD_PAD] = qh.astype(BF16)

    c0 = D_SSM + Q_LORA
    ckv = p[:, c0:c0 + KV_LORA]
    ckvn = _rms(ckv, gkv_ref[...]).astype(BF16)
    kv = jnp.dot(ckvn, wkv_ref[...], preferred_element_type=F32)
    v_ref[...] = (kv[:, QK_PAD:] + vones_ref[...]).astype(BF16)
    kr = p[:, c0 + KV_LORA:c0 + KV_LORA + HEAD_PAD]
    kr_rot = p[:, c0 + KV_LORA + HEAD_PAD:c0 + KV_LORA + 2 * HEAD_PAD]
    ss_r = jnp.sum(kr * kr, axis=-1, keepdims=True)
    t1k = t1k_ref[...]
    kr_part = kr_rot * t2k_ref[...]
    for h in range(N_HEADS):
        kn = kv[:, h * HEAD_PAD:(h + 1) * HEAD_PAD]
        ss = jnp.sum(kn * kn, axis=-1, keepdims=True) + ss_r
        r = lax.rsqrt(ss * (1.0 / QK_DIM) + EPS)
        kh = ((kn + kr) * t1k + kr_part) * r
        k_ref[:, h * HEAD_PAD:(h + 1) * HEAD_PAD] = kh.astype(BF16)


def _const_spec(shape):
    nd = len(shape)
    return pl.BlockSpec(shape, lambda *_: (0,) * nd)


def _proj_call(x2d, tm, tabs, n_tab_blocks, consts):
    n_rows = x2d.shape[0]
    grid = (n_rows // tm,)
    row = lambda i: (i, 0)
    tab = lambda i: (i % n_tab_blocks, 0)
    in_specs = ([pl.BlockSpec((tm, D_MODEL), row)] + [_const_spec(c.shape) for c in consts]
                + [pl.BlockSpec((tm, HEAD_PAD), tab)] * 4)
    out_shape = (
        jax.ShapeDtypeStruct((n_rows, D_SSM), F32),
        jax.ShapeDtypeStruct((n_rows, QK_PAD), BF16),
        jax.ShapeDtypeStruct((n_rows, QK_PAD), BF16),
        jax.ShapeDtypeStruct((n_rows, QK_PAD), BF16),
    )
    out_specs = (
        pl.BlockSpec((tm, D_SSM), row),
        pl.BlockSpec((tm, QK_PAD), row),
        pl.BlockSpec((tm, QK_PAD), row),
        pl.BlockSpec((tm, QK_PAD), row),
    )
    return pl.pallas_call(
        _proj_kernel, out_shape=out_shape, grid=grid, in_specs=in_specs, out_specs=out_specs,
        compiler_params=pltpu.CompilerParams(dimension_semantics=("parallel",),
                                             vmem_limit_bytes=VMEM_LIMIT),
        name="proj_mla",
    )(x2d, *consts, *tabs)


def _s5_kernel(u_ref, um_ref, bre_ref, bim_ref, cre_ref, cim_ref, ar_ref, ai_ref,
               d_ref, wglu_ref, bglu_ref, g_ref, o_ref, xs_ref, h_ref, *, batch):
    j = pl.program_id(0)
    slabs_per_half = N_SLABS // 2

    def project_in(ub, rows):
        for kh in range(2):
            lhs = ub[:, kh * 256:(kh + 1) * 256]
            xre = jnp.dot(lhs, bre_ref[kh], preferred_element_type=F32)
            xim = jnp.dot(lhs, bim_ref[kh], preferred_element_type=F32)
            for cl in range(slabs_per_half):
                c = kh * slabs_per_half + cl
                for b in range(batch):
                    xs_ref[c, pl.ds(b * S5_PITCH, rows), :] = (
                        xre[b * rows:(b + 1) * rows, cl * LANES:(cl + 1) * LANES])
                    xs_ref[c, pl.ds((batch + b) * S5_PITCH, rows), :] = (
                        xim[b * rows:(b + 1) * rows, cl * LANES:(cl + 1) * LANES])

    def scan(n_steps):
        def body(t, hs):
            new = []
            for c in range(N_SLABS):
                rows = pl.ds(t, 2 * batch, stride=S5_PITCH)
                x8 = xs_ref[c, rows, :]
                h = hs[c]
                hn = ar_ref[c] * h + ai_ref[c] * pltpu.roll(h, batch, 0) + x8
                xs_ref[c, rows, :] = hn
                new.append(hn)
            return tuple(new)

        hs = tuple(h_ref[c] for c in range(N_SLABS))
        hs = lax.fori_loop(0, n_steps, body, hs, unroll=4)
        for c in range(N_SLABS):
            h_ref[c] = hs[c]

    @pl.when(j == 0)
    def _():
        h_ref[...] = jnp.zeros_like(h_ref)
        um = um_ref[...].astype(BF16)
        project_in(jnp.concatenate([um] * batch, axis=0), N_META)
        scan(N_META)

    uf = u_ref[...].reshape(batch * S5_CHUNK, D_SSM)
    project_in(uf.astype(BF16), S5_CHUNK)
    scan(S5_CHUNK)

    ys = []
    for nh in range(2):
        def gather(plane0):
            return jnp.concatenate(
                [jnp.concatenate(
                    [xs_ref[nh * slabs_per_half + cl, pl.ds((plane0 + b) * S5_PITCH, S5_CHUNK), :]
                     for cl in range(slabs_per_half)], axis=1)
                 for b in range(batch)], axis=0).astype(BF16)
        yre = jnp.dot(gather(0), cre_ref[nh], preferred_element_type=F32)
        yim = jnp.dot(gather(batch), cim_ref[nh], preferred_element_type=F32)
        ys.append(yre - yim)
    y = jnp.concatenate(ys, axis=1) + d_ref[...] * uf
    z = 0.5 * y * (1.0 + jnp.tanh(math.sqrt(2.0 / math.pi) * (y + 0.044715 * (y * y * y))))
    gate = jnp.dot(z.astype(BF16), wglu_ref[...], preferred_element_type=F32) + bglu_ref[...]
    out = z * (1.0 / (1.0 + jnp.exp(-gate)))
    o_ref[...] = _rms(out, g_ref[...]).astype(BF16).reshape(batch, S5_CHUNK, D_SSM)


def _s5_call(u3, u_meta, consts):
    batch, seq, _ = u3.shape
    assert 2 * batch == SUBLANES and seq % S5_CHUNK == 0
    grid = (seq // S5_CHUNK,)
    in_specs = [pl.BlockSpec((batch, S5_CHUNK, D_SSM), lambda j: (0, j, 0)),
                _const_spec(u_meta.shape)] + [_const_spec(c.shape) for c in consts]
    return pl.pallas_call(
        functools.partial(_s5_kernel, batch=batch),
        out_shape=jax.ShapeDtypeStruct((batch, seq, D_SSM), BF16),
        grid=grid, in_specs=in_specs,
        out_specs=pl.BlockSpec((batch, S5_CHUNK, D_SSM), lambda j: (0, j, 0)),
        scratch_shapes=[pltpu.VMEM((N_SLABS, 2 * batch * S5_PITCH, LANES), F32),
                        pltpu.VMEM((N_SLABS, 2 * batch, LANES), F32)],
        compiler_params=pltpu.CompilerParams(dimension_semantics=("arbitrary",),
                                             vmem_limit_bytes=VMEM_LIMIT),
        name="s5_mixer",
    )(u3, u_meta, *consts)


ATT_TQ = 1024
ATT_TK = 1024
ATT_SUB = 256
ATT_HEADS = 4


def _attn_kernel(q_ref, k_ref, v_ref, km_ref, vm_ref, mask_ref, o_ref, acc_ref, *m_refs, online):
    nt = (((1,), (1,)), ((), ()))
    (m_ref,) = m_refs if online else (None,)
    n_q = q_ref.shape[0] // ATT_TQ
    head_lanes = [slice(h * HEAD_PAD, (h + 1) * HEAD_PAD) for h in range(ATT_HEADS)]

    def step(h, sub, q, kblk, vblk, mask):
        s = lax.dot_general(q, kblk, nt, preferred_element_type=F32)
        if online:
            if mask is not None:
                s = jnp.where(mask > 0, s, -jnp.inf)
            m = m_ref[h, sub]
            m_new = jnp.maximum(m, jnp.max(s, axis=-1, keepdims=True))
            p = jnp.exp2(s - m_new).astype(BF16)
            acc_ref[h, sub] = (jnp.exp2(m - m_new) * acc_ref[h, sub]
                               + jnp.dot(p, vblk, preferred_element_type=F32))
            m_ref[h, sub] = m_new
        else:
            p = jnp.exp2(s).astype(BF16)
            if mask is not None:
                p = p * mask
            acc_ref[h, sub] += jnp.dot(p, vblk, preferred_element_type=F32)

    def q_tile(qi, _):
        q0 = pl.multiple_of(qi * ATT_TQ, ATT_TQ)
        rows = pl.ds(q0, ATT_TQ)
        qs = [q_ref[rows, hl] for hl in head_lanes]
        acc_ref[...] = jnp.zeros_like(acc_ref)
        if online:
            m_ref[...] = jnp.full(m_ref.shape, -1e30, F32)

        def body(kb, _):
            krows = pl.ds(pl.multiple_of(kb * ATT_TK, ATT_TK), ATT_TK)
            for h, hl in enumerate(head_lanes):
                step(h, slice(None), qs[h], k_ref[krows, hl], v_ref[krows, hl], None)
            return 0

        lax.fori_loop(0, qi * (ATT_TQ // ATT_TK), body, 0)

        for i in range(ATT_TQ // ATT_SUB):
            sub = slice(i * ATT_SUB, (i + 1) * ATT_SUB)
            n_keys = (i + 1) * ATT_SUB
            mask = mask_ref[sub, :LANES + n_keys]
            for h, hl in enumerate(head_lanes):
                kblk = jnp.concatenate([km_ref[:, hl], k_ref[pl.ds(q0, n_keys), hl]], axis=0)
                vblk = jnp.concatenate([vm_ref[:, hl], v_ref[pl.ds(q0, n_keys), hl]], axis=0)
                step(h, sub, qs[h][sub], kblk, vblk, mask)
        lane = lax.broadcasted_iota(jnp.int32, (ATT_TQ, HEAD_PAD), 1)
        for hp in range(ATT_HEADS // 2):
            even, odd = acc_ref[2 * hp], acc_ref[2 * hp + 1]
            o_even = even * (1.0 / even[:, V_DIM:V_DIM + 1])
            o_odd = odd * (1.0 / odd[:, 0:1])
            o_ref[rows, hp * HEAD_PAD:(hp + 1) * HEAD_PAD] = (
                jnp.where(lane < V_DIM, o_even, o_odd).astype(BF16))
        return 0

    lax.fori_loop(0, n_q, q_tile, 0)


def _attn_call(q3, k3, v3, k_meta, v_meta, mask, *, online):
    batch, seq, _ = q3.shape
    grid = (batch, N_HEADS // ATT_HEADS)
    seq_blk = pl.BlockSpec((None, seq, ATT_HEADS * HEAD_PAD), lambda b, hg: (b, 0, hg))
    meta_blk = pl.BlockSpec((LANES, ATT_HEADS * HEAD_PAD), lambda b, hg: (0, hg))
    scratch = [pltpu.VMEM((ATT_HEADS, ATT_TQ, HEAD_PAD), F32)]
    if online:
        scratch.append(pltpu.VMEM((ATT_HEADS, ATT_TQ, 1), F32))
    return pl.pallas_call(
        functools.partial(_attn_kernel, online=online),
        out_shape=jax.ShapeDtypeStruct((batch, seq, N_HEADS * V_DIM), BF16),
        grid=grid,
        in_specs=[seq_blk, seq_blk, seq_blk, meta_blk, meta_blk, _const_spec(mask.shape)],
        out_specs=pl.BlockSpec((None, seq, ATT_HEADS * V_DIM), lambda b, hg: (b, 0, hg)),
        scratch_shapes=scratch,
        compiler_params=pltpu.CompilerParams(dimension_semantics=("parallel", "parallel"),
                                             vmem_limit_bytes=VMEM_LIMIT),
        name="mla_attention_online" if online else "mla_attention",
    )(q3, k3, v3, k_meta, v_meta, mask)


FFN_TM = 512


def _ffn_kernel(x_ref, ms_ref, oa_ref, gatt_ref, wout_ref, gffn_ref, wg_ref, wu_ref, wd_ref,
                out_ref):
    ya = _rms(oa_ref[...].astype(F32), gatt_ref[...]).astype(BF16)
    mixed = jnp.concatenate([ms_ref[...], ya], axis=1)
    h1 = x_ref[...] + jnp.dot(mixed, wout_ref[...], preferred_element_type=F32)
    hn = _rms(h1, gffn_ref[...]).astype(BF16)
    g = jnp.dot(hn, wg_ref[...], preferred_element_type=F32)
    u = jnp.dot(hn, wu_ref[...], preferred_element_type=F32)
    a = (g * (1.0 / (1.0 + jnp.exp(-g))) * u).astype(BF16)
    out_ref[...] = h1 + jnp.dot(a, wd_ref[...], preferred_element_type=F32)


def _ffn_call(x2d, ms2d, oa2d, consts):
    n_rows = x2d.shape[0]
    tm = FFN_TM
    row = lambda i: (i, 0)
    once = pl.Buffered(1)
    in_specs = [pl.BlockSpec((tm, D_MODEL), row),
                pl.BlockSpec((tm, D_SSM), row),
                pl.BlockSpec((tm, N_HEADS * V_DIM), row)]
    in_specs += [pl.BlockSpec(c.shape, lambda i: (0, 0), pipeline_mode=once) for c in consts]
    return pl.pallas_call(
        _ffn_kernel,
        out_shape=jax.ShapeDtypeStruct((n_rows, D_MODEL), F32),
        grid=(n_rows // tm,), in_specs=in_specs,
        out_specs=pl.BlockSpec((tm, D_MODEL), row),
        compiler_params=pltpu.CompilerParams(dimension_semantics=("parallel",),
                                             vmem_limit_bytes=VMEM_LIMIT),
        name="outproj_ffn",
    )(x2d, ms2d, oa2d, *consts)


def _rope_tables(length):
    pos = jnp.arange(length, dtype=F32)
    inv_freq = 1.0 / (ROPE_BASE ** (jnp.arange(0, ROPE, 2, dtype=F32) / ROPE))
    ang = pos[:, None] * inv_freq[None, :]
    return jnp.cos(ang), jnp.sin(ang)


def _head_tables(gain, cos, sin):
    length = cos.shape[0]
    g_n, g_r = gain[:NOPE], gain[NOPE:]
    g_r_swapped = jnp.concatenate([g_r[HALF_ROPE:], g_r[:HALF_ROPE]])
    cos2 = jnp.concatenate([cos, cos], axis=1)
    sin2 = jnp.concatenate([sin, sin], axis=1)
    pad = jnp.zeros((length, HEAD_PAD - QK_DIM), F32)
    t1 = jnp.concatenate([jnp.broadcast_to(g_n, (length, NOPE)), g_r * cos2, pad], axis=1)
    t2 = jnp.concatenate([jnp.zeros((length, NOPE), F32), g_r_swapped * sin2, pad], axis=1)
    return t1, t2


def _rot_half_cols(w):
    return jnp.concatenate([-w[..., HALF_ROPE:], w[..., :HALF_ROPE]], axis=-1)


def _pad_cols(w, left, total):
    return jnp.pad(w, ((0, 0), (left, total - left - w.shape[1])))


def kernel(x, meta_tokens, mix_norm_g, w_in, ssm_a_re, ssm_a_im, ssm_log_dt, ssm_b_re, ssm_b_im,
           ssm_c_re, ssm_c_im, ssm_d, ssm_w_glu, ssm_b_glu, q_lora_norm_g, w_uq, kv_lora_norm_g,
           w_uk, w_uv, q_head_norm_g, k_head_norm_g, ssm_out_norm_g, att_out_norm_g, w_out,
           ffn_norm_g, w_gate, w_up, w_down):
    batch, seq, _ = x.shape
    depth = w_in.shape[0]
    assert depth == 1
    l = 0
    length = N_META + seq
    cos, sin = _rope_tables(length)

    wi = w_in[l]
    o_r = D_SSM + Q_LORA + KV_LORA
    w_r = wi[:, o_r:]
    win = jnp.concatenate([wi[:, :o_r], _pad_cols(w_r, NOPE, HEAD_PAD),
                           _pad_cols(_rot_half_cols(w_r), NOPE, HEAD_PAD)], axis=1).astype(BF16)
    wq3 = w_uq[l].reshape(Q_LORA, N_HEADS, QK_DIM)
    q1 = jnp.pad(wq3, ((0, 0), (0, 0), (0, HEAD_PAD - QK_DIM)))
    q2 = jnp.pad(_rot_half_cols(wq3[..., NOPE:]), ((0, 0), (0, 0), (NOPE, HEAD_PAD - QK_DIM)))
    wq = jnp.concatenate([q1.reshape(Q_LORA, QK_PAD), q2.reshape(Q_LORA, QK_PAD)], axis=1).astype(BF16)
    wk3 = jnp.pad(w_uk[l].reshape(KV_LORA, N_HEADS, NOPE), ((0, 0), (0, 0), (0, HEAD_PAD - NOPE)))
    wv4 = w_uv[l].reshape(KV_LORA, N_HEADS // 2, 2, V_DIM)
    zv = jnp.zeros_like(wv4[:, :, 0])
    wv = jnp.stack([jnp.concatenate([wv4[:, :, 0], zv], axis=-1),
                    jnp.concatenate([zv, wv4[:, :, 1]], axis=-1)], axis=2).reshape(KV_LORA, QK_PAD)
    ones_col = jnp.zeros((2, HEAD_PAD), F32).at[0, V_DIM].set(1.0).at[1, 0].set(1.0)
    vones = jnp.tile(ones_col.reshape(1, 2 * HEAD_PAD), (1, N_HEADS // 2))
    wkv = jnp.concatenate([wk3.reshape(KV_LORA, QK_PAD), wv], axis=1).astype(BF16)
    t1q, t2q = _head_tables(q_head_norm_g[l], cos, sin)
    t1k, t2k = _head_tables(k_head_norm_g[l], cos, sin)
    proj_consts = (mix_norm_g[l][None], win, q_lora_norm_g[l][None], wq,
                   kv_lora_norm_g[l][None], wkv, vones)

    tm = 512
    tabs_f = tuple(t[N_META:] for t in (t1q, t2q, t1k, t2k))
    u2, q2d, k2d, v2d = _proj_call(x.reshape(batch * seq, D_MODEL), tm, tabs_f, seq // tm,
                                   proj_consts)
    tabs_m = tuple(t[:N_META] for t in (t1q, t2q, t1k, t2k))
    u_meta, _, k_meta, v_meta = _proj_call(meta_tokens, N_META, tabs_m, 1, proj_consts)

    dt = jnp.exp(ssm_log_dt[l])[:, None]
    lr, li = ssm_a_re[l], ssm_a_im[l]
    mag = jnp.exp(lr * dt)
    ar = mag * jnp.cos(li * dt)
    ai = mag * jnp.sin(li * dt)
    den = lr * lr + li * li
    fr = ((ar - 1.0) * lr + ai * li) / den
    fi = (ai * lr - (ar - 1.0) * li) / den
    br, bi = ssm_b_re[l], ssm_b_im[l]
    bbr = fr[..., None] * br - fi[..., None] * bi
    bbi = fr[..., None] * bi + fi[..., None] * br
    half_groups = N_GROUPS // 2

    def block_diag_halves(blocks):
        _, r, c = blocks.shape
        rows = blocks.reshape(2, half_groups * r, c)
        tiled = jnp.tile(rows, (1, 1, half_groups))
        row_g = jnp.arange(half_groups * r) // r
        col_g = jnp.arange(half_groups * c) // c
        return jnp.where(row_g[:, None] == col_g[None, :], tiled, 0.0).astype(BF16)

    def in_map(bb):
        return block_diag_halves(jnp.swapaxes(bb, 1, 2))

    def out_map(cc):
        return block_diag_halves(jnp.swapaxes(cc, 1, 2))

    ar_rows = jnp.broadcast_to(ar.reshape(N_SLABS, 1, LANES), (N_SLABS, 2 * batch, LANES))
    ai_flat = ai.reshape(N_SLABS, 1, LANES)
    ai_rows = jnp.concatenate([jnp.broadcast_to(-ai_flat, (N_SLABS, batch, LANES)),
                               jnp.broadcast_to(ai_flat, (N_SLABS, batch, LANES))], axis=1)
    s5_consts = (in_map(bbr), in_map(bbi), out_map(ssm_c_re[l]), out_map(ssm_c_im[l]),
                 ar_rows, ai_rows, ssm_d[l][None], ssm_w_glu[l].astype(BF16),
                 ssm_b_glu[l][None], ssm_out_norm_g[l][None])
    mixed_ssm = _s5_call(u2.reshape(batch, seq, D_SSM), u_meta, s5_consts)

    pad_rows = ((0, LANES - N_META), (0, 0))
    chunk_of = jnp.arange(ATT_TQ) // CHUNK
    causal = (chunk_of[None, :] <= chunk_of[:, None]).astype(BF16)
    mask = jnp.concatenate([jnp.ones((ATT_TQ, LANES), BF16), causal], axis=1)
    score_bound = (LOG2_E * math.sqrt(QK_DIM) * jnp.max(jnp.abs(q_head_norm_g[l]))
                   * jnp.max(jnp.abs(k_head_norm_g[l])))
    attn_args = (q2d.reshape(batch, seq, QK_PAD), k2d.reshape(batch, seq, QK_PAD),
                 v2d.reshape(batch, seq, QK_PAD),
                 jnp.pad(k_meta, pad_rows), jnp.pad(v_meta, pad_rows), mask)
    y_att = lax.cond(score_bound <= MAX_UNSHIFTED_LOG2_SCORE,
                     functools.partial(_attn_call, online=False),
                     functools.partial(_attn_call, online=True), *attn_args)

    ffn_consts = (att_out_norm_g[l][None], w_out[l].astype(BF16), ffn_norm_g[l][None],
                  w_gate[l].astype(BF16), w_up[l].astype(BF16), w_down[l].astype(BF16))
    out = _ffn_call(x.reshape(batch * seq, D_MODEL), mixed_ssm.reshape(batch * seq, D_SSM),
                    y_att.reshape(batch * seq, N_HEADS * V_DIM), ffn_consts)
    return out.reshape(batch, seq, D_MODEL)
```

```python
import functools
import math

import jax
import jax.numpy as jnp
from jax import lax
from jax.experimental import pallas as pl
from jax.experimental.pallas import tpu as pltpu

F32 = jnp.float32
BF16 = jnp.bfloat16

D_MODEL = 1024
N_META = 16
CHUNK = 64
D_SSM = 512
SSM_GROUP = 16
N_GROUPS = D_SSM // SSM_GROUP
SSM_STATE = 64
N_HEADS = 8
V_DIM = 64
NOPE = 64
ROPE = 32
HALF_ROPE = ROPE // 2
QK_DIM = NOPE + ROPE
Q_LORA = 256
KV_LORA = 128
D_FF = 2816
ROPE_BASE = 10000.0
EPS = 1e-6
LOG2_E = math.log2(math.e)
MAX_UNSHIFTED_LOG2_SCORE = 40.0

LANES = 128
SUBLANES = 8
HEAD_PAD = LANES
QK_PAD = N_HEADS * HEAD_PAD
N_STATE_COLS = N_GROUPS * SSM_STATE
N_SLABS = N_STATE_COLS // LANES
S5_CHUNK = 128
S5_PITCH = S5_CHUNK + SUBLANES
VMEM_LIMIT = 56 * 1024 * 1024


def _rms(x, g):
    return x * lax.rsqrt(jnp.mean(x * x, axis=-1, keepdims=True) + EPS) * g


def _proj_kernel(x_ref, gmix_ref, win_ref, gq_ref, wq_ref, gkv_ref, wkv_ref, vones_ref,
                 t1q_ref, t2q_ref, t1k_ref, t2k_ref,
                 u_ref, q_ref, k_ref, v_ref):
    x = x_ref[...]
    xn = _rms(x, gmix_ref[...]).astype(BF16)
    p = jnp.dot(xn, win_ref[...], preferred_element_type=F32)
    u_ref[...] = p[:, :D_SSM]

    cq = p[:, D_SSM:D_SSM + Q_LORA]
    cqn = _rms(cq, gq_ref[...]).astype(BF16)
    q12 = jnp.dot(cqn, wq_ref[...], preferred_element_type=F32)
    t1q = t1q_ref[...]
    t2q = t2q_ref[...]
    scale = QK_DIM ** -0.5 * LOG2_E
    for h in range(N_HEADS):
        q1 = q12[:, h * HEAD_PAD:(h + 1) * HEAD_PAD]
        q2 = q12[:, QK_PAD + h * HEAD_PAD:QK_PAD + (h + 1) * HEAD_PAD]
        r = lax.rsqrt(jnp.sum(q1 * q1, axis=-1, keepdims=True) * (1.0 / QK_DIM) + EPS)
        qh = (q1 * t1q + q2 * t2q) * (r * scale)
        q_ref[:, h * HEAD_PAD:(h + 1) * HEAD_PAD] = qh.astype(BF16)

    c0 = D_SSM + Q_LORA
    ckv = p[:, c0:c0 + KV_LORA]
    ckvn = _rms(ckv, gkv_ref[...]).astype(BF16)
    kv = jnp.dot(ckvn, wkv_ref[...], preferred_element_type=F32)
    v_ref[...] = (kv[:, QK_PAD:] + vones_ref[...]).astype(BF16)
    kr = p[:, c0 + KV_LORA:c0 + KV_LORA + HEAD_PAD]
    kr_rot = p[:, c0 + KV_LORA + HEAD_PAD:c0 + KV_LORA + 2 * HEAD_PAD]
    ss_r = jnp.sum(kr * kr, axis=-1, keepdims=True)
    t1k = t1k_ref[...]
    kr_part = kr_rot * t2k_ref[...]
    for h in range(N_HEADS):
        kn = kv[:, h * HEAD_PAD:(h + 1) * HEAD_PAD]
        ss = jnp.sum(kn * kn, axis=-1, keepdims=True) + ss_r
        r = lax.rsqrt(ss * (1.0 / QK_DIM) + EPS)
        kh = ((kn + kr) * t1k + kr_part) * r
        k_ref[:, h * HEAD_PAD:(h + 1) * HEAD_PAD] = kh.astype(BF16)


def _const_spec(shape):
    nd = len(shape)
    return pl.BlockSpec(shape, lambda *_: (0,) * nd)


def _proj_call(x2d, tm, tabs, n_tab_blocks, consts):
    n_rows = x2d.shape[0]
    grid = (n_rows // tm,)
    row = lambda i: (i, 0)
    tab = lambda i: (i % n_tab_blocks, 0)
    in_specs = ([pl.BlockSpec((tm, D_MODEL), row)] + [_const_spec(c.shape) for c in consts]
                + [pl.BlockSpec((tm, HEAD_PAD), tab)] * 4)
    out_shape = (
        jax.ShapeDtypeStruct((n_rows, D_SSM), F32),
        jax.ShapeDtypeStruct((n_rows, QK_PAD), BF16),
        jax.ShapeDtypeStruct((n_rows, QK_PAD), BF16),
        jax.ShapeDtypeStruct((n_rows, QK_PAD), BF16),
    )
    out_specs = (
        pl.BlockSpec((tm, D_SSM), row),
        pl.BlockSpec((tm, QK_PAD), row),
        pl.BlockSpec((tm, QK_PAD), row),
        pl.BlockSpec((tm, QK_PAD), row),
    )
    return pl.pallas_call(
        _proj_kernel, out_shape=out_shape, grid=grid, in_specs=in_specs, out_specs=out_specs,
        compiler_params=pltpu.CompilerParams(dimension_semantics=("parallel",),
                                             vmem_limit_bytes=VMEM_LIMIT),
        name="proj_mla",
    )(x2d, *consts, *tabs)


def _s5_kernel(u_ref, um_ref, bre_ref, bim_ref, cre_ref, cim_ref, ar_ref, ai_ref,
               d_ref, wglu_ref, bglu_ref, g_ref, o_ref, xs_ref, h_ref, *, batch):
    j = pl.program_id(0)
    slabs_per_half = N_SLABS // 2

    def project_in(ub, rows):
        for kh in range(2):
            lhs = ub[:, kh * 256:(kh + 1) * 256]
            xre = jnp.dot(lhs, bre_ref[kh], preferred_element_type=F32)
            xim = jnp.dot(lhs, bim_ref[kh], preferred_element_type=F32)
            for cl in range(slabs_per_half):
                c = kh * slabs_per_half + cl
                for b in range(batch):
                    xs_ref[c, pl.ds(b * S5_PITCH, rows), :] = (
                        xre[b * rows:(b + 1) * rows, cl * LANES:(cl + 1) * LANES])
                    xs_ref[c, pl.ds((batch + b) * S5_PITCH, rows), :] = (
                        xim[b * rows:(b + 1) * rows, cl * LANES:(cl + 1) * LANES])

    def scan(n_steps):
        def body(t, hs):
            new = []
            for c in range(N_SLABS):
                rows = pl.ds(t, 2 * batch, stride=S5_PITCH)
                x8 = xs_ref[c, rows, :]
                h = hs[c]
                hn = ar_ref[c] * h + ai_ref[c] * pltpu.roll(h, batch, 0) + x8
                xs_ref[c, rows, :] = hn
                new.append(hn)
            return tuple(new)

        hs = tuple(h_ref[c] for c in range(N_SLABS))
        hs = lax.fori_loop(0, n_steps, body, hs, unroll=4)
        for c in range(N_SLABS):
            h_ref[c] = hs[c]

    @pl.when(j == 0)
    def _():
        h_ref[...] = jnp.zeros_like(h_ref)
        um = um_ref[...].astype(BF16)
        project_in(jnp.concatenate([um] * batch, axis=0), N_META)
        scan(N_META)

    uf = u_ref[...].reshape(batch * S5_CHUNK, D_SSM)
    project_in(uf.astype(BF16), S5_CHUNK)
    scan(S5_CHUNK)

    ys = []
    for nh in range(2):
        def gather(plane0):
            return jnp.concatenate(
                [jnp.concatenate(
                    [xs_ref[nh * slabs_per_half + cl, pl.ds((plane0 + b) * S5_PITCH, S5_CHUNK), :]
                     for cl in range(slabs_per_half)], axis=1)
                 for b in range(batch)], axis=0).astype(BF16)
        yre = jnp.dot(gather(0), cre_ref[nh], preferred_element_type=F32)
        yim = jnp.dot(gather(batch), cim_ref[nh], preferred_element_type=F32)
        ys.append(yre - yim)
    y = jnp.concatenate(ys, axis=1) + d_ref[...] * uf
    z = 0.5 * y * (1.0 + jnp.tanh(math.sqrt(2.0 / math.pi) * (y + 0.044715 * (y * y * y))))
    gate = jnp.dot(z.astype(BF16), wglu_ref[...], preferred_element_type=F32) + bglu_ref[...]
    out = z * (1.0 / (1.0 + jnp.exp(-gate)))
    o_ref[...] = _rms(out, g_ref[...]).astype(BF16).reshape(batch, S5_CHUNK, D_SSM)


def _s5_call(u3, u_meta, consts):
    batch, seq, _ = u3.shape
    assert 2 * batch == SUBLANES and seq % S5_CHUNK == 0
    grid = (seq // S5_CHUNK,)
    in_specs = [pl.BlockSpec((batch, S5_CHUNK, D_SSM), lambda j: (0, j, 0)),
                _const_spec(u_meta.shape)] + [_const_spec(c.shape) for c in consts]
    return pl.pallas_call(
        functools.partial(_s5_kernel, batch=batch),
        out_shape=jax.ShapeDtypeStruct((batch, seq, D_SSM), BF16),
        grid=grid, in_specs=in_specs,
        out_specs=pl.BlockSpec((batch, S5_CHUNK, D_SSM), lambda j: (0, j, 0)),
        scratch_shapes=[pltpu.VMEM((N_SLABS, 2 * batch * S5_PITCH, LANES), F32),
                        pltpu.VMEM((N_SLABS, 2 * batch, LANES), F32)],
        compiler_params=pltpu.CompilerParams(dimension_semantics=("arbitrary",),
                                             vmem_limit_bytes=VMEM_LIMIT),
        name="s5_mixer",
    )(u3, u_meta, *consts)


ATT_TQ = 1024
ATT_TK = 1024
ATT_SUB = 256
ATT_HEADS = 4


def _attn_kernel(unshifted_ref, *refs):
    @pl.when(unshifted_ref[0] != 0)
    def _():
        _attn_body(*refs, online=False)

    @pl.when(unshifted_ref[0] == 0)
    def _():
        _attn_body(*refs, online=True)


def _attn_body(q_ref, k_ref, v_ref, km_ref, vm_ref, mask_ref, o_ref, acc_ref, m_ref, *, online):
    nt = (((1,), (1,)), ((), ()))
    n_q = q_ref.shape[0] // ATT_TQ
    head_lanes = [slice(h * HEAD_PAD, (h + 1) * HEAD_PAD) for h in range(ATT_HEADS)]

    def step(h, sub, q, kblk, vblk, mask):
        s = lax.dot_general(q, kblk, nt, preferred_element_type=F32)
        if online:
            if mask is not None:
                s = jnp.where(mask > 0, s, -jnp.inf)
            m = m_ref[h, sub]
            m_new = jnp.maximum(m, jnp.max(s, axis=-1, keepdims=True))
            p = jnp.exp2(s - m_new).astype(BF16)
            acc_ref[h, sub] = (jnp.exp2(m - m_new) * acc_ref[h, sub]
                               + jnp.dot(p, vblk, preferred_element_type=F32))
            m_ref[h, sub] = m_new
        else:
            p = jnp.exp2(s).astype(BF16)
            if mask is not None:
                p = p * mask
            acc_ref[h, sub] += jnp.dot(p, vblk, preferred_element_type=F32)

    def q_tile(qi, _):
        q0 = pl.multiple_of(qi * ATT_TQ, ATT_TQ)
        rows = pl.ds(q0, ATT_TQ)
        qs = [q_ref[rows, hl] for hl in head_lanes]
        acc_ref[...] = jnp.zeros_like(acc_ref)
        if online:
            m_ref[...] = jnp.full(m_ref.shape, -1e30, F32)

        def body(kb, _):
            krows = pl.ds(pl.multiple_of(kb * ATT_TK, ATT_TK), ATT_TK)
            for h, hl in enumerate(head_lanes):
                step(h, slice(None), qs[h], k_ref[krows, hl], v_ref[krows, hl], None)
            return 0

        lax.fori_loop(0, qi * (ATT_TQ // ATT_TK), body, 0)

        for i in range(ATT_TQ // ATT_SUB):
            sub = slice(i * ATT_SUB, ATT_TQ)
            krows = pl.ds(q0 + i * ATT_SUB, ATT_SUB)
            for h, hl in enumerate(head_lanes):
                kblk, vblk = k_ref[krows, hl], v_ref[krows, hl]
                mask = mask_ref[sub, LANES + i * ATT_SUB:LANES + (i + 1) * ATT_SUB]
                if i == 0:
                    kblk = jnp.concatenate([km_ref[:, hl], kblk], axis=0)
                    vblk = jnp.concatenate([vm_ref[:, hl], vblk], axis=0)
                    mask = mask_ref[sub, :LANES + ATT_SUB]
                step(h, sub, qs[h][sub], kblk, vblk, mask)
        lane = lax.broadcasted_iota(jnp.int32, (ATT_TQ, HEAD_PAD), 1)
        for hp in range(ATT_HEADS // 2):
            even, odd = acc_ref[2 * hp], acc_ref[2 * hp + 1]
            o_even = even * (1.0 / even[:, V_DIM:V_DIM + 1])
            o_odd = odd * (1.0 / odd[:, 0:1])
            o_ref[rows, hp * HEAD_PAD:(hp + 1) * HEAD_PAD] = (
                jnp.where(lane < V_DIM, o_even, o_odd).astype(BF16))
        return 0

    lax.fori_loop(0, n_q, q_tile, 0)


def _attn_call(unshifted, q3, k3, v3, k_meta, v_meta, mask):
    batch, seq, _ = q3.shape
    grid = (batch, N_HEADS // ATT_HEADS)
    seq_blk = pl.BlockSpec((None, seq, ATT_HEADS * HEAD_PAD), lambda b, hg: (b, 0, hg))
    meta_blk = pl.BlockSpec((LANES, ATT_HEADS * HEAD_PAD), lambda b, hg: (0, hg))
    return pl.pallas_call(
        _attn_kernel,
        out_shape=jax.ShapeDtypeStruct((batch, seq, N_HEADS * V_DIM), BF16),
        grid=grid,
        in_specs=[pl.BlockSpec(memory_space=pltpu.SMEM),
                  seq_blk, seq_blk, seq_blk, meta_blk, meta_blk, _const_spec(mask.shape)],
        out_specs=pl.BlockSpec((None, seq, ATT_HEADS * V_DIM), lambda b, hg: (b, 0, hg)),
        scratch_shapes=[pltpu.VMEM((ATT_HEADS, ATT_TQ, HEAD_PAD), F32),
                        pltpu.VMEM((ATT_HEADS, ATT_TQ, 1), F32)],
        compiler_params=pltpu.CompilerParams(dimension_semantics=("parallel", "parallel"),
                                             vmem_limit_bytes=VMEM_LIMIT),
        name="mla_attention",
    )(unshifted, q3, k3, v3, k_meta, v_meta, mask)


FFN_TM = 512


def _ffn_kernel(x_ref, ms_ref, oa_ref, gatt_ref, wout_ref, gffn_ref, wg_ref, wu_ref, wd_ref,
                out_ref):
    ya = _rms(oa_ref[...].astype(F32), gatt_ref[...]).astype(BF16)
    mixed = jnp.concatenate([ms_ref[...], ya], axis=1)
    h1 = x_ref[...] + jnp.dot(mixed, wout_ref[...], preferred_element_type=F32)
    hn = _rms(h1, gffn_ref[...]).astype(BF16)
    g = jnp.dot(hn, wg_ref[...], preferred_element_type=F32)
    u = jnp.dot(hn, wu_ref[...], preferred_element_type=F32)
    a = (g * (1.0 / (1.0 + jnp.exp(-g))) * u).astype(BF16)
    out_ref[...] = h1 + jnp.dot(a, wd_ref[...], preferred_element_type=F32)


def _ffn_call(x2d, ms2d, oa2d, consts):
    n_rows = x2d.shape[0]
    tm = FFN_TM
    row = lambda i: (i, 0)
    once = pl.Buffered(1)
    in_specs = [pl.BlockSpec((tm, D_MODEL), row),
                pl.BlockSpec((tm, D_SSM), row),
                pl.BlockSpec((tm, N_HEADS * V_DIM), row)]
    in_specs += [pl.BlockSpec(c.shape, lambda i: (0, 0), pipeline_mode=once) for c in consts]
    return pl.pallas_call(
        _ffn_kernel,
        out_shape=jax.ShapeDtypeStruct((n_rows, D_MODEL), F32),
        grid=(n_rows // tm,), in_specs=in_specs,
        out_specs=pl.BlockSpec((tm, D_MODEL), row),
        compiler_params=pltpu.CompilerParams(dimension_semantics=("parallel",),
                                             vmem_limit_bytes=VMEM_LIMIT),
        name="outproj_ffn",
    )(x2d, ms2d, oa2d, *consts)


def _rope_tables(length):
    pos = jnp.arange(length, dtype=F32)
    inv_freq = 1.0 / (ROPE_BASE ** (jnp.arange(0, ROPE, 2, dtype=F32) / ROPE))
    ang = pos[:, None] * inv_freq[None, :]
    return jnp.cos(ang), jnp.sin(ang)


def _head_tables(gain, cos, sin):
    length = cos.shape[0]
    g_n, g_r = gain[:NOPE], gain[NOPE:]
    g_r_swapped = jnp.concatenate([g_r[HALF_ROPE:], g_r[:HALF_ROPE]])
    cos2 = jnp.concatenate([cos, cos], axis=1)
    sin2 = jnp.concatenate([sin, sin], axis=1)
    pad = jnp.zeros((length, HEAD_PAD - QK_DIM), F32)
    t1 = jnp.concatenate([jnp.broadcast_to(g_n, (length, NOPE)), g_r * cos2, pad], axis=1)
    t2 = jnp.concatenate([jnp.zeros((length, NOPE), F32), g_r_swapped * sin2, pad], axis=1)
    return t1, t2


def _rot_half_cols(w):
    return jnp.concatenate([-w[..., HALF_ROPE:], w[..., :HALF_ROPE]], axis=-1)


def _pad_cols(w, left, total):
    return jnp.pad(w, ((0, 0), (left, total - left - w.shape[1])))


def kernel(x, meta_tokens, mix_norm_g, w_in, ssm_a_re, ssm_a_im, ssm_log_dt, ssm_b_re, ssm_b_im,
           ssm_c_re, ssm_c_im, ssm_d, ssm_w_glu, ssm_b_glu, q_lora_norm_g, w_uq, kv_lora_norm_g,
           w_uk, w_uv, q_head_norm_g, k_head_norm_g, ssm_out_norm_g, att_out_norm_g, w_out,
           ffn_norm_g, w_gate, w_up, w_down):
    batch, seq, _ = x.shape
    depth = w_in.shape[0]
    assert depth == 1
    l = 0
    length = N_META + seq
    cos, sin = _rope_tables(length)

    wi = w_in[l]
    o_r = D_SSM + Q_LORA + KV_LORA
    w_r = wi[:, o_r:]
    win = jnp.concatenate([wi[:, :o_r], _pad_cols(w_r, NOPE, HEAD_PAD),
                           _pad_cols(_rot_half_cols(w_r), NOPE, HEAD_PAD)], axis=1).astype(BF16)
    wq3 = w_uq[l].reshape(Q_LORA, N_HEADS, QK_DIM)
    q1 = jnp.pad(wq3, ((0, 0), (0, 0), (0, HEAD_PAD - QK_DIM)))
    q2 = jnp.pad(_rot_half_cols(wq3[..., NOPE:]), ((0, 0), (0, 0), (NOPE, HEAD_PAD - QK_DIM)))
    wq = jnp.concatenate([q1.reshape(Q_LORA, QK_PAD), q2.reshape(Q_LORA, QK_PAD)], axis=1).astype(BF16)
    wk3 = jnp.pad(w_uk[l].reshape(KV_LORA, N_HEADS, NOPE), ((0, 0), (0, 0), (0, HEAD_PAD - NOPE)))
    wv4 = w_uv[l].reshape(KV_LORA, N_HEADS // 2, 2, V_DIM)
    zv = jnp.zeros_like(wv4[:, :, 0])
    wv = jnp.stack([jnp.concatenate([wv4[:, :, 0], zv], axis=-1),
                    jnp.concatenate([zv, wv4[:, :, 1]], axis=-1)], axis=2).reshape(KV_LORA, QK_PAD)
    ones_col = jnp.zeros((2, HEAD_PAD), F32).at[0, V_DIM].set(1.0).at[1, 0].set(1.0)
    vones = jnp.tile(ones_col.reshape(1, 2 * HEAD_PAD), (1, N_HEADS // 2))
    wkv = jnp.concatenate([wk3.reshape(KV_LORA, QK_PAD), wv], axis=1).astype(BF16)
    t1q, t2q = _head_tables(q_head_norm_g[l], cos, sin)
    t1k, t2k = _head_tables(k_head_norm_g[l], cos, sin)
    proj_consts = (mix_norm_g[l][None], win, q_lora_norm_g[l][None], wq,
                   kv_lora_norm_g[l][None], wkv, vones)

    tm = 512
    tabs_f = tuple(t[N_META:] for t in (t1q, t2q, t1k, t2k))
    u2, q2d, k2d, v2d = _proj_call(x.reshape(batch * seq, D_MODEL), tm, tabs_f, seq // tm,
                                   proj_consts)
    tabs_m = tuple(t[:N_META] for t in (t1q, t2q, t1k, t2k))
    u_meta, _, k_meta, v_meta = _proj_call(meta_tokens, N_META, tabs_m, 1, proj_consts)

    dt = jnp.exp(ssm_log_dt[l])[:, None]
    lr, li = ssm_a_re[l], ssm_a_im[l]
    mag = jnp.exp(lr * dt)
    ar = mag * jnp.cos(li * dt)
    ai = mag * jnp.sin(li * dt)
    den = lr * lr + li * li
    fr = ((ar - 1.0) * lr + ai * li) / den
    fi = (ai * lr - (ar - 1.0) * li) / den
    br, bi = ssm_b_re[l], ssm_b_im[l]
    bbr = fr[..., None] * br - fi[..., None] * bi
    bbi = fr[..., None] * bi + fi[..., None] * br
    half_groups = N_GROUPS // 2

    def block_diag_halves(blocks):
        _, r, c = blocks.shape
        rows = blocks.reshape(2, half_groups * r, c)
        tiled = jnp.tile(rows, (1, 1, half_groups))
        row_g = jnp.arange(half_groups * r) // r
        col_g = jnp.arange(half_groups * c) // c
        return jnp.where(row_g[:, None] == col_g[None, :], tiled, 0.0).astype(BF16)

    def in_map(bb):
        return block_diag_halves(jnp.swapaxes(bb, 1, 2))

    def out_map(cc):
        return block_diag_halves(jnp.swapaxes(cc, 1, 2))

    ar_rows = jnp.broadcast_to(ar.reshape(N_SLABS, 1, LANES), (N_SLABS, 2 * batch, LANES))
    ai_flat = ai.reshape(N_SLABS, 1, LANES)
    ai_rows = jnp.concatenate([jnp.broadcast_to(-ai_flat, (N_SLABS, batch, LANES)),
                               jnp.broadcast_to(ai_flat, (N_SLABS, batch, LANES))], axis=1)
    s5_consts = (in_map(bbr), in_map(bbi), out_map(ssm_c_re[l]), out_map(ssm_c_im[l]),
                 ar_rows, ai_rows, ssm_d[l][None], ssm_w_glu[l].astype(BF16),
                 ssm_b_glu[l][None], ssm_out_norm_g[l][None])
    mixed_ssm = _s5_call(u2.reshape(batch, seq, D_SSM), u_meta, s5_consts)

    pad_rows = ((0, LANES - N_META), (0, 0))
    chunk_of = jnp.arange(ATT_TQ) // CHUNK
    causal = (chunk_of[None, :] <= chunk_of[:, None]).astype(BF16)
    mask = jnp.concatenate([jnp.ones((ATT_TQ, LANES), BF16), causal], axis=1)
    score_bound = (LOG2_E * math.sqrt(QK_DIM) * jnp.max(jnp.abs(q_head_norm_g[l]))
                   * jnp.max(jnp.abs(k_head_norm_g[l])))
    unshifted = (score_bound <= MAX_UNSHIFTED_LOG2_SCORE).astype(jnp.int32).reshape(1)
    y_att = _attn_call(unshifted, q2d.reshape(batch, seq, QK_PAD), k2d.reshape(batch, seq, QK_PAD),
                       v2d.reshape(batch, seq, QK_PAD),
                       jnp.pad(k_meta, pad_rows), jnp.pad(v_meta, pad_rows), mask)

    ffn_consts = (att_out_norm_g[l][None], w_out[l].astype(BF16), ffn_norm_g[l][None],
                  w_gate[l].astype(BF16), w_up[l].astype(BF16), w_down[l].astype(BF16))
    out = _ffn_call(x.reshape(batch * seq, D_MODEL), mixed_ssm.reshape(batch * seq, D_SSM),
                    y_att.reshape(batch * seq, N_HEADS * V_DIM), ffn_consts)
    return out.reshape(batch, seq, D_MODEL)
```

```python
import functools
import math

import jax
import jax.numpy as jnp
from jax import lax
from jax.experimental import pallas as pl
from jax.experimental.pallas import tpu as pltpu

F32 = jnp.float32
BF16 = jnp.bfloat16

D_MODEL = 1024
N_META = 16
CHUNK = 64
D_SSM = 512
SSM_GROUP = 16
N_GROUPS = D_SSM // SSM_GROUP
SSM_STATE = 64
N_HEADS = 8
V_DIM = 64
NOPE = 64
ROPE = 32
HALF_ROPE = ROPE // 2
QK_DIM = NOPE + ROPE
Q_LORA = 256
KV_LORA = 128
D_FF = 2816
ROPE_BASE = 10000.0
EPS = 1e-6
LOG2_E = math.log2(math.e)
MAX_UNSHIFTED_LOG2_SCORE = 40.0

LANES = 128
SUBLANES = 8
HEAD_PAD = LANES
QK_PAD = N_HEADS * HEAD_PAD
N_STATE_COLS = N_GROUPS * SSM_STATE
N_SLABS = N_STATE_COLS // LANES
S5_CHUNK = 128
S5_PITCH = S5_CHUNK + SUBLANES
VMEM_LIMIT = 56 * 1024 * 1024


def _rms(x, g):
    return x * lax.rsqrt(jnp.mean(x * x, axis=-1, keepdims=True) + EPS) * g


def _proj_kernel(x_ref, gmix_ref, win_ref, gq_ref, wq_ref, gkv_ref, wkv_ref, vones_ref,
                 t1q_ref, t2q_ref, t1k_ref, t2k_ref,
                 u_ref, q_ref, k_ref, v_ref):
    x = x_ref[...]
    xn = _rms(x, gmix_ref[...]).astype(BF16)
    p = jnp.dot(xn, win_ref[...], preferred_element_type=F32)
    u_ref[...] = p[:, :D_SSM]

    cq = p[:, D_SSM:D_SSM + Q_LORA]
    cqn = _rms(cq, gq_ref[...]).astype(BF16)
    q12 = jnp.dot(cqn, wq_ref[...], preferred_element_type=F32)
    t1q = t1q_ref[...]
    t2q = t2q_ref[...]
    scale = QK_DIM ** -0.5 * LOG2_E
    for h in range(N_HEADS):
        q1 = q12[:, h * HEAD_PAD:(h + 1) * HEAD_PAD]
        q2 = q12[:, QK_PAD + h * HEAD_PAD:QK_PAD + (h + 1) * HEAD_PAD]
        r = lax.rsqrt(jnp.sum(q1 * q1, axis=-1, keepdims=True) * (1.0 / QK_DIM) + EPS)
        qh = (q1 * t1q + q2 * t2q) * (r * scale)
        q_ref[:, h * HEAD_PAD:(h + 1) * HEAD_PAD] = qh.astype(BF16)

    c0 = D_SSM + Q_LORA
    ckv = p[:, c0:c0 + KV_LORA]
    ckvn = _rms(ckv, gkv_ref[...]).astype(BF16)
    kv = jnp.dot(ckvn, wkv_ref[...], preferred_element_type=F32)
    v_ref[...] = (kv[:, QK_PAD:] + vones_ref[...]).astype(BF16)
    kr = p[:, c0 + KV_LORA:c0 + KV_LORA + HEAD_PAD]
    kr_rot = p[:, c0 + KV_LORA + HEAD_PAD:c0 + KV_LORA + 2 * HEAD_PAD]
    ss_r = jnp.sum(kr * kr, axis=-1, keepdims=True)
    t1k = t1k_ref[...]
    kr_part = kr_rot * t2k_ref[...]
    for h in range(N_HEADS):
        kn = kv[:, h * HEAD_PAD:(h + 1) * HEAD_PAD]
        ss = jnp.sum(kn * kn, axis=-1, keepdims=True) + ss_r
        r = lax.rsqrt(ss * (1.0 / QK_DIM) + EPS)
        kh = ((kn + kr) * t1k + kr_part) * r
        k_ref[:, h * HEAD_PAD:(h + 1) * HEAD_PAD] = kh.astype(BF16)


def _const_spec(shape):
    nd = len(shape)
    return pl.BlockSpec(shape, lambda *_: (0,) * nd)


def _proj_call(x2d, tm, tabs, n_tab_blocks, consts):
    n_rows = x2d.shape[0]
    grid = (n_rows // tm,)
    row = lambda i: (i, 0)
    tab = lambda i: (i % n_tab_blocks, 0)
    in_specs = ([pl.BlockSpec((tm, D_MODEL), row)] + [_const_spec(c.shape) for c in consts]
                + [pl.BlockSpec((tm, HEAD_PAD), tab)] * 4)
    out_shape = (
        jax.ShapeDtypeStruct((n_rows, D_SSM), F32),
        jax.ShapeDtypeStruct((n_rows, QK_PAD), BF16),
        jax.ShapeDtypeStruct((n_rows, QK_PAD), BF16),
        jax.ShapeDtypeStruct((n_rows, QK_PAD), BF16),
    )
    out_specs = (
        pl.BlockSpec((tm, D_SSM), row),
        pl.BlockSpec((tm, QK_PAD), row),
        pl.BlockSpec((tm, QK_PAD), row),
        pl.BlockSpec((tm, QK_PAD), row),
    )
    return pl.pallas_call(
        _proj_kernel, out_shape=out_shape, grid=grid, in_specs=in_specs, out_specs=out_specs,
        compiler_params=pltpu.CompilerParams(dimension_semantics=("parallel",),
                                             vmem_limit_bytes=VMEM_LIMIT),
        name="proj_mla",
    )(x2d, *consts, *tabs)


def _s5_kernel(u_ref, um_ref, bre_ref, bim_ref, cre_ref, cim_ref, ar_ref, ai_ref,
               d_ref, wglu_ref, bglu_ref, g_ref, o_ref, xs_ref, h_ref, *, batch):
    j = pl.program_id(0)
    slabs_per_half = N_SLABS // 2

    def project_in(ub, rows):
        for kh in range(2):
            lhs = ub[:, kh * 256:(kh + 1) * 256]
            xre = jnp.dot(lhs, bre_ref[kh], preferred_element_type=F32)
            xim = jnp.dot(lhs, bim_ref[kh], preferred_element_type=F32)
            for cl in range(slabs_per_half):
                c = kh * slabs_per_half + cl
                for b in range(batch):
                    xs_ref[c, pl.ds(b * S5_PITCH, rows), :] = (
                        xre[b * rows:(b + 1) * rows, cl * LANES:(cl + 1) * LANES])
                    xs_ref[c, pl.ds((batch + b) * S5_PITCH, rows), :] = (
                        xim[b * rows:(b + 1) * rows, cl * LANES:(cl + 1) * LANES])

    def scan(n_steps):
        def body(t, hs):
            new = []
            for c in range(N_SLABS):
                rows = pl.ds(t, 2 * batch, stride=S5_PITCH)
                x8 = xs_ref[c, rows, :]
                h = hs[c]
                hn = ar_ref[c] * h + ai_ref[c] * pltpu.roll(h, batch, 0) + x8
                xs_ref[c, rows, :] = hn
                new.append(hn)
            return tuple(new)

        hs = tuple(h_ref[c] for c in range(N_SLABS))
        hs = lax.fori_loop(0, n_steps, body, hs, unroll=4)
        for c in range(N_SLABS):
            h_ref[c] = hs[c]

    @pl.when(j == 0)
    def _():
        h_ref[...] = jnp.zeros_like(h_ref)
        um = um_ref[...].astype(BF16)
        project_in(jnp.concatenate([um] * batch, axis=0), N_META)
        scan(N_META)

    uf = u_ref[...].reshape(batch * S5_CHUNK, D_SSM)
    project_in(uf.astype(BF16), S5_CHUNK)
    scan(S5_CHUNK)

    ys = []
    for nh in range(2):
        def gather(plane0):
            return jnp.concatenate(
                [jnp.concatenate(
                    [xs_ref[nh * slabs_per_half + cl, pl.ds((plane0 + b) * S5_PITCH, S5_CHUNK), :]
                     for cl in range(slabs_per_half)], axis=1)
                 for b in range(batch)], axis=0).astype(BF16)
        yre = jnp.dot(gather(0), cre_ref[nh], preferred_element_type=F32)
        yim = jnp.dot(gather(batch), cim_ref[nh], preferred_element_type=F32)
        ys.append(yre - yim)
    y = jnp.concatenate(ys, axis=1) + d_ref[...] * uf
    z = 0.5 * y * (1.0 + jnp.tanh(math.sqrt(2.0 / math.pi) * (y + 0.044715 * (y * y * y))))
    gate = jnp.dot(z.astype(BF16), wglu_ref[...], preferred_element_type=F32) + bglu_ref[...]
    out = z * (1.0 / (1.0 + jnp.exp(-gate)))
    o_ref[...] = _rms(out, g_ref[...]).astype(BF16).reshape(batch, S5_CHUNK, D_SSM)


def _s5_call(u3, u_meta, consts):
    batch, seq, _ = u3.shape
    assert 2 * batch == SUBLANES and seq % S5_CHUNK == 0
    grid = (seq // S5_CHUNK,)
    in_specs = [pl.BlockSpec((batch, S5_CHUNK, D_SSM), lambda j: (0, j, 0)),
                _const_spec(u_meta.shape)] + [_const_spec(c.shape) for c in consts]
    return pl.pallas_call(
        functools.partial(_s5_kernel, batch=batch),
        out_shape=jax.ShapeDtypeStruct((batch, seq, D_SSM), BF16),
        grid=grid, in_specs=in_specs,
        out_specs=pl.BlockSpec((batch, S5_CHUNK, D_SSM), lambda j: (0, j, 0)),
        scratch_shapes=[pltpu.VMEM((N_SLABS, 2 * batch * S5_PITCH, LANES), F32),
                        pltpu.VMEM((N_SLABS, 2 * batch, LANES), F32)],
        compiler_params=pltpu.CompilerParams(dimension_semantics=("arbitrary",),
                                             vmem_limit_bytes=VMEM_LIMIT),
        name="s5_mixer",
    )(u3, u_meta, *consts)


ATT_TQ = 1024
ATT_TK = 1024
ATT_SUB = 256
ATT_HEADS = 4


def _attn_kernel(q_ref, k_ref, v_ref, km_ref, vm_ref, mask_ref, o_ref, acc_ref, m_ref, *, online):
    nt = (((1,), (1,)), ((), ()))
    n_q = q_ref.shape[0] // ATT_TQ
    head_lanes = [slice(h * HEAD_PAD, (h + 1) * HEAD_PAD) for h in range(ATT_HEADS)]

    def step(h, sub, q, kblk, vblk, mask):
        s = lax.dot_general(q, kblk, nt, preferred_element_type=F32)
        if online:
            if mask is not None:
                s = jnp.where(mask > 0, s, -jnp.inf)
            m = m_ref[h, sub]
            m_new = jnp.maximum(m, jnp.max(s, axis=-1, keepdims=True))
            p = jnp.exp2(s - m_new).astype(BF16)
            acc_ref[h, sub] = (jnp.exp2(m - m_new) * acc_ref[h, sub]
                               + jnp.dot(p, vblk, preferred_element_type=F32))
            m_ref[h, sub] = m_new
        else:
            p = jnp.exp2(s).astype(BF16)
            if mask is not None:
                p = p * mask
            acc_ref[h, sub] += jnp.dot(p, vblk, preferred_element_type=F32)

    def q_tile(qi, _):
        q0 = pl.multiple_of(qi * ATT_TQ, ATT_TQ)
        rows = pl.ds(q0, ATT_TQ)
        qs = [q_ref[rows, hl] for hl in head_lanes]
        acc_ref[...] = jnp.zeros_like(acc_ref)
        if online:
            m_ref[...] = jnp.full(m_ref.shape, -1e30, F32)

        def body(kb, _):
            krows = pl.ds(pl.multiple_of(kb * ATT_TK, ATT_TK), ATT_TK)
            for h, hl in enumerate(head_lanes):
                step(h, slice(None), qs[h], k_ref[krows, hl], v_ref[krows, hl], None)
            return 0

        lax.fori_loop(0, qi * (ATT_TQ // ATT_TK), body, 0)

        for i in range(ATT_TQ // ATT_SUB):
            sub = slice(i * ATT_SUB, ATT_TQ)
            krows = pl.ds(q0 + i * ATT_SUB, ATT_SUB)
            for h, hl in enumerate(head_lanes):
                kblk, vblk = k_ref[krows, hl], v_ref[krows, hl]
                mask = mask_ref[sub, LANES + i * ATT_SUB:LANES + (i + 1) * ATT_SUB]
                if i == 0:
                    kblk = jnp.concatenate([km_ref[:, hl], kblk], axis=0)
                    vblk = jnp.concatenate([vm_ref[:, hl], vblk], axis=0)
                    mask = mask_ref[sub, :LANES + ATT_SUB]
                step(h, sub, qs[h][sub], kblk, vblk, mask)
        lane = lax.broadcasted_iota(jnp.int32, (ATT_TQ, HEAD_PAD), 1)
        for hp in range(ATT_HEADS // 2):
            even, odd = acc_ref[2 * hp], acc_ref[2 * hp + 1]
            o_even = even * (1.0 / even[:, V_DIM:V_DIM + 1])
            o_odd = odd * (1.0 / odd[:, 0:1])
            o_ref[rows, hp * HEAD_PAD:(hp + 1) * HEAD_PAD] = (
                jnp.where(lane < V_DIM, o_even, o_odd).astype(BF16))
        return 0

    lax.fori_loop(0, n_q, q_tile, 0)


def _attn_call(q3, k3, v3, k_meta, v_meta, mask, *, online):
    batch, seq, _ = q3.shape
    grid = (batch, N_HEADS // ATT_HEADS)
    seq_blk = pl.BlockSpec((None, seq, ATT_HEADS * HEAD_PAD), lambda b, hg: (b, 0, hg))
    meta_blk = pl.BlockSpec((LANES, ATT_HEADS * HEAD_PAD), lambda b, hg: (0, hg))
    return pl.pallas_call(
        functools.partial(_attn_kernel, online=online),
        out_shape=jax.ShapeDtypeStruct((batch, seq, N_HEADS * V_DIM), BF16),
        grid=grid,
        in_specs=[seq_blk, seq_blk, seq_blk, meta_blk, meta_blk, _const_spec(mask.shape)],
        out_specs=pl.BlockSpec((None, seq, ATT_HEADS * V_DIM), lambda b, hg: (b, 0, hg)),
        scratch_shapes=[pltpu.VMEM((ATT_HEADS, ATT_TQ, HEAD_PAD), F32),
                        pltpu.VMEM((ATT_HEADS, ATT_TQ, 1), F32)],
        compiler_params=pltpu.CompilerParams(dimension_semantics=("parallel", "parallel"),
                                             vmem_limit_bytes=VMEM_LIMIT),
        name="mla_attention_online" if online else "mla_attention",
    )(q3, k3, v3, k_meta, v_meta, mask)


FFN_TM = 512


def _ffn_kernel(x_ref, ms_ref, oa_ref, gatt_ref, wout_ref, gffn_ref, wg_ref, wu_ref, wd_ref,
                out_ref):
    ya = _rms(oa_ref[...].astype(F32), gatt_ref[...]).astype(BF16)
    mixed = jnp.concatenate([ms_ref[...], ya], axis=1)
    h1 = x_ref[...] + jnp.dot(mixed, wout_ref[...], preferred_element_type=F32)
    hn = _rms(h1, gffn_ref[...]).astype(BF16)
    g = jnp.dot(hn, wg_ref[...], preferred_element_type=F32)
    u = jnp.dot(hn, wu_ref[...], preferred_element_type=F32)
    a = (g * (1.0 / (1.0 + jnp.exp(-g))) * u).astype(BF16)
    out_ref[...] = h1 + jnp.dot(a, wd_ref[...], preferred_element_type=F32)


def _ffn_call(x2d, ms2d, oa2d, consts):
    n_rows = x2d.shape[0]
    tm = FFN_TM
    row = lambda i: (i, 0)
    once = pl.Buffered(1)
    in_specs = [pl.BlockSpec((tm, D_MODEL), row),
                pl.BlockSpec((tm, D_SSM), row),
                pl.BlockSpec((tm, N_HEADS * V_DIM), row)]
    in_specs += [pl.BlockSpec(c.shape, lambda i: (0, 0), pipeline_mode=once) for c in consts]
    return pl.pallas_call(
        _ffn_kernel,
        out_shape=jax.ShapeDtypeStruct((n_rows, D_MODEL), F32),
        grid=(n_rows // tm,), in_specs=in_specs,
        out_specs=pl.BlockSpec((tm, D_MODEL), row),
        compiler_params=pltpu.CompilerParams(dimension_semantics=("parallel",),
                                             vmem_limit_bytes=VMEM_LIMIT),
        name="outproj_ffn",
    )(x2d, ms2d, oa2d, *consts)


def _rope_tables(length):
    pos = jnp.arange(length, dtype=F32)
    inv_freq = 1.0 / (ROPE_BASE ** (jnp.arange(0, ROPE, 2, dtype=F32) / ROPE))
    ang = pos[:, None] * inv_freq[None, :]
    return jnp.cos(ang), jnp.sin(ang)


def _head_tables(gain, cos, sin):
    length = cos.shape[0]
    g_n, g_r = gain[:NOPE], gain[NOPE:]
    g_r_swapped = jnp.concatenate([g_r[HALF_ROPE:], g_r[:HALF_ROPE]])
    cos2 = jnp.concatenate([cos, cos], axis=1)
    sin2 = jnp.concatenate([sin, sin], axis=1)
    pad = jnp.zeros((length, HEAD_PAD - QK_DIM), F32)
    t1 = jnp.concatenate([jnp.broadcast_to(g_n, (length, NOPE)), g_r * cos2, pad], axis=1)
    t2 = jnp.concatenate([jnp.zeros((length, NOPE), F32), g_r_swapped * sin2, pad], axis=1)
    return t1, t2


def _rot_half_cols(w):
    return jnp.concatenate([-w[..., HALF_ROPE:], w[..., :HALF_ROPE]], axis=-1)


def _pad_cols(w, left, total):
    return jnp.pad(w, ((0, 0), (left, total - left - w.shape[1])))


def kernel(x, meta_tokens, mix_norm_g, w_in, ssm_a_re, ssm_a_im, ssm_log_dt, ssm_b_re, ssm_b_im,
           ssm_c_re, ssm_c_im, ssm_d, ssm_w_glu, ssm_b_glu, q_lora_norm_g, w_uq, kv_lora_norm_g,
           w_uk, w_uv, q_head_norm_g, k_head_norm_g, ssm_out_norm_g, att_out_norm_g, w_out,
           ffn_norm_g, w_gate, w_up, w_down):
    batch, seq, _ = x.shape
    depth = w_in.shape[0]
    assert depth == 1
    l = 0
    length = N_META + seq
    cos, sin = _rope_tables(length)

    wi = w_in[l]
    o_r = D_SSM + Q_LORA + KV_LORA
    w_r = wi[:, o_r:]
    win = jnp.concatenate([wi[:, :o_r], _pad_cols(w_r, NOPE, HEAD_PAD),
                           _pad_cols(_rot_half_cols(w_r), NOPE, HEAD_PAD)], axis=1).astype(BF16)
    wq3 = w_uq[l].reshape(Q_LORA, N_HEADS, QK_DIM)
    q1 = jnp.pad(wq3, ((0, 0), (0, 0), (0, HEAD_PAD - QK_DIM)))
    q2 = jnp.pad(_rot_half_cols(wq3[..., NOPE:]), ((0, 0), (0, 0), (NOPE, HEAD_PAD - QK_DIM)))
    wq = jnp.concatenate([q1.reshape(Q_LORA, QK_PAD), q2.reshape(Q_LORA, QK_PAD)], axis=1).astype(BF16)
    wk3 = jnp.pad(w_uk[l].reshape(KV_LORA, N_HEADS, NOPE), ((0, 0), (0, 0), (0, HEAD_PAD - NOPE)))
    wv4 = w_uv[l].reshape(KV_LORA, N_HEADS // 2, 2, V_DIM)
    zv = jnp.zeros_like(wv4[:, :, 0])
    wv = jnp.stack([jnp.concatenate([wv4[:, :, 0], zv], axis=-1),
                    jnp.concatenate([zv, wv4[:, :, 1]], axis=-1)], axis=2).reshape(KV_LORA, QK_PAD)
    ones_col = jnp.zeros((2, HEAD_PAD), F32).at[0, V_DIM].set(1.0).at[1, 0].set(1.0)
    vones = jnp.tile(ones_col.reshape(1, 2 * HEAD_PAD), (1, N_HEADS // 2))
    wkv = jnp.concatenate([wk3.reshape(KV_LORA, QK_PAD), wv], axis=1).astype(BF16)
    t1q, t2q = _head_tables(q_head_norm_g[l], cos, sin)
    t1k, t2k = _head_tables(k_head_norm_g[l], cos, sin)
    proj_consts = (mix_norm_g[l][None], win, q_lora_norm_g[l][None], wq,
                   kv_lora_norm_g[l][None], wkv, vones)

    tm = 512
    tabs_f = tuple(t[N_META:] for t in (t1q, t2q, t1k, t2k))
    u2, q2d, k2d, v2d = _proj_call(x.reshape(batch * seq, D_MODEL), tm, tabs_f, seq // tm,
                                   proj_consts)
    tabs_m = tuple(t[:N_META] for t in (t1q, t2q, t1k, t2k))
    u_meta, _, k_meta, v_meta = _proj_call(meta_tokens, N_META, tabs_m, 1, proj_consts)

    dt = jnp.exp(ssm_log_dt[l])[:, None]
    lr, li = ssm_a_re[l], ssm_a_im[l]
    mag = jnp.exp(lr * dt)
    ar = mag * jnp.cos(li * dt)
    ai = mag * jnp.sin(li * dt)
    den = lr * lr + li * li
    fr = ((ar - 1.0) * lr + ai * li) / den
    fi = (ai * lr - (ar - 1.0) * li) / den
    br, bi = ssm_b_re[l], ssm_b_im[l]
    bbr = fr[..., None] * br - fi[..., None] * bi
    bbi = fr[..., None] * bi + fi[..., None] * br
    half_groups = N_GROUPS // 2

    def block_diag_halves(blocks):
        _, r, c = blocks.shape
        rows = blocks.reshape(2, half_groups * r, c)
        tiled = jnp.tile(rows, (1, 1, half_groups))
        row_g = jnp.arange(half_groups * r) // r
        col_g = jnp.arange(half_groups * c) // c
        return jnp.where(row_g[:, None] == col_g[None, :], tiled, 0.0).astype(BF16)

    def in_map(bb):
        return block_diag_halves(jnp.swapaxes(bb, 1, 2))

    def out_map(cc):
        return block_diag_halves(jnp.swapaxes(cc, 1, 2))

    ar_rows = jnp.broadcast_to(ar.reshape(N_SLABS, 1, LANES), (N_SLABS, 2 * batch, LANES))
    ai_flat = ai.reshape(N_SLABS, 1, LANES)
    ai_rows = jnp.concatenate([jnp.broadcast_to(-ai_flat, (N_SLABS, batch, LANES)),
                               jnp.broadcast_to(ai_flat, (N_SLABS, batch, LANES))], axis=1)
    s5_consts = (in_map(bbr), in_map(bbi), out_map(ssm_c_re[l]), out_map(ssm_c_im[l]),
                 ar_rows, ai_rows, ssm_d[l][None], ssm_w_glu[l].astype(BF16),
                 ssm_b_glu[l][None], ssm_out_norm_g[l][None])
    mixed_ssm = _s5_call(u2.reshape(batch, seq, D_SSM), u_meta, s5_consts)

    pad_rows = ((0, LANES - N_META), (0, 0))
    chunk_of = jnp.arange(ATT_TQ) // CHUNK
    causal = (chunk_of[None, :] <= chunk_of[:, None]).astype(BF16)
    mask = jnp.concatenate([jnp.ones((ATT_TQ, LANES), BF16), causal], axis=1)
    score_bound = (LOG2_E * math.sqrt(QK_DIM) * jnp.max(jnp.abs(q_head_norm_g[l]))
                   * jnp.max(jnp.abs(k_head_norm_g[l])))
    attn_args = (q2d.reshape(batch, seq, QK_PAD), k2d.reshape(batch, seq, QK_PAD),
                 v2d.reshape(batch, seq, QK_PAD),
                 jnp.pad(k_meta, pad_rows), jnp.pad(v_meta, pad_rows), mask)
    y_att = lax.cond(score_bound <= MAX_UNSHIFTED_LOG2_SCORE,
                     functools.partial(_attn_call, online=False),
                     functools.partial(_attn_call, online=True), *attn_args)

    ffn_consts = (att_out_norm_g[l][None], w_out[l].astype(BF16), ffn_norm_g[l][None],
                  w_gate[l].astype(BF16), w_up[l].astype(BF16), w_down[l].astype(BF16))
    out = _ffn_call(x.reshape(batch * seq, D_MODEL), mixed_ssm.reshape(batch * seq, D_SSM),
                    y_att.reshape(batch * seq, N_HEADS * V_DIM), ffn_consts)
    return out.reshape(batch, seq, D_MODEL)
```

```python
import functools
import math

import jax
import jax.numpy as jnp
from jax import lax
from jax.experimental import pallas as pl
from jax.experimental.pallas import tpu as pltpu

F32 = jnp.float32
BF16 = jnp.bfloat16

D_MODEL = 1024
N_META = 16
CHUNK = 64
D_SSM = 512
SSM_GROUP = 16
N_GROUPS = D_SSM // SSM_GROUP
SSM_STATE = 64
N_HEADS = 8
V_DIM = 64
NOPE = 64
ROPE = 32
HALF_ROPE = ROPE // 2
QK_DIM = NOPE + ROPE
Q_LORA = 256
KV_LORA = 128
D_FF = 2816
ROPE_BASE = 10000.0
EPS = 1e-6
LOG2_E = math.log2(math.e)
MAX_UNSHIFTED_LOG2_SCORE = 40.0

LANES = 128
SUBLANES = 8
HEAD_PAD = LANES
QK_PAD = N_HEADS * HEAD_PAD
N_STATE_COLS = N_GROUPS * SSM_STATE
N_SLABS = N_STATE_COLS // LANES
S5_CHUNK = 128
S5_PITCH = S5_CHUNK + SUBLANES
PROJ_TM = 1024
PROJ_SUB = 256
VMEM_LIMIT = 56 * 1024 * 1024


def _rms(x, g):
    return x * lax.rsqrt(jnp.mean(x * x, axis=-1, keepdims=True) + EPS) * g


def _proj_kernel(x_ref, gmix_ref, win_ref, gq_ref, wq_ref, gkv_ref, wkv_ref, vones_ref,
                 t1q_ref, t2q_ref, t1k_ref, t2k_ref,
                 u_ref, q_ref, k_ref, v_ref):
    tm = x_ref.shape[0]
    sub_rows = min(PROJ_SUB, tm)
    scale = QK_DIM ** -0.5 * LOG2_E
    c0 = D_SSM + Q_LORA

    def in_proj(rows):
        xn = _rms(x_ref[rows, :], gmix_ref[...]).astype(BF16)
        p = jnp.dot(xn, win_ref[...], preferred_element_type=F32)
        u_ref[rows, :] = p[:, :D_SSM]
        return p

    def heads(rows, p):
        cqn = _rms(p[:, D_SSM:c0], gq_ref[...]).astype(BF16)
        q12 = jnp.dot(cqn, wq_ref[...], preferred_element_type=F32)
        t1q = t1q_ref[rows, :]
        t2q = t2q_ref[rows, :]
        for h in range(N_HEADS):
            q1 = q12[:, h * HEAD_PAD:(h + 1) * HEAD_PAD]
            q2 = q12[:, QK_PAD + h * HEAD_PAD:QK_PAD + (h + 1) * HEAD_PAD]
            r = lax.rsqrt(jnp.sum(q1 * q1, axis=-1, keepdims=True) * (1.0 / QK_DIM) + EPS)
            qh = (q1 * t1q + q2 * t2q) * (r * scale)
            q_ref[rows, h * HEAD_PAD:(h + 1) * HEAD_PAD] = qh.astype(BF16)

        ckvn = _rms(p[:, c0:c0 + KV_LORA], gkv_ref[...]).astype(BF16)
        kv = jnp.dot(ckvn, wkv_ref[...], preferred_element_type=F32)
        v_ref[rows, :] = (kv[:, QK_PAD:] + vones_ref[...]).astype(BF16)
        kr = p[:, c0 + KV_LORA:c0 + KV_LORA + HEAD_PAD]
        kr_rot = p[:, c0 + KV_LORA + HEAD_PAD:c0 + KV_LORA + 2 * HEAD_PAD]
        ss_r = jnp.sum(kr * kr, axis=-1, keepdims=True)
        t1k = t1k_ref[rows, :]
        kr_part = kr_rot * t2k_ref[rows, :]
        for h in range(N_HEADS):
            kn = kv[:, h * HEAD_PAD:(h + 1) * HEAD_PAD]
            ss = jnp.sum(kn * kn, axis=-1, keepdims=True) + ss_r
            r = lax.rsqrt(ss * (1.0 / QK_DIM) + EPS)
            kh = ((kn + kr) * t1k + kr_part) * r
            k_ref[rows, h * HEAD_PAD:(h + 1) * HEAD_PAD] = kh.astype(BF16)

    groups = [slice(i * sub_rows, (i + 1) * sub_rows) for i in range(tm // sub_rows)]
    pending = None
    for rows in groups:
        p = in_proj(rows)
        if pending is not None:
            heads(*pending)
        pending = (rows, p)
    heads(*pending)


def _const_spec(shape):
    nd = len(shape)
    return pl.BlockSpec(shape, lambda *_: (0,) * nd)


def _proj_call(x2d, tm, tabs, n_tab_blocks, consts):
    n_rows = x2d.shape[0]
    grid = (n_rows // tm,)
    row = lambda i: (i, 0)
    tab = lambda i: (i % n_tab_blocks, 0)
    in_specs = ([pl.BlockSpec((tm, D_MODEL), row)] + [_const_spec(c.shape) for c in consts]
                + [pl.BlockSpec((tm, HEAD_PAD), tab)] * 4)
    out_shape = (
        jax.ShapeDtypeStruct((n_rows, D_SSM), F32),
        jax.ShapeDtypeStruct((n_rows, QK_PAD), BF16),
        jax.ShapeDtypeStruct((n_rows, QK_PAD), BF16),
        jax.ShapeDtypeStruct((n_rows, QK_PAD), BF16),
    )
    out_specs = (
        pl.BlockSpec((tm, D_SSM), row),
        pl.BlockSpec((tm, QK_PAD), row),
        pl.BlockSpec((tm, QK_PAD), row),
        pl.BlockSpec((tm, QK_PAD), row),
    )
    return pl.pallas_call(
        _proj_kernel, out_shape=out_shape, grid=grid, in_specs=in_specs, out_specs=out_specs,
        compiler_params=pltpu.CompilerParams(dimension_semantics=("parallel",),
                                             vmem_limit_bytes=VMEM_LIMIT),
        name="proj_mla",
    )(x2d, *consts, *tabs)


def _s5_kernel(u_ref, um_ref, bre_ref, bim_ref, cre_ref, cim_ref, ar_ref, ai_ref,
               d_ref, wglu_ref, bglu_ref, g_ref, o_ref, xs_ref, h_ref, *, batch):
    j = pl.program_id(0)
    slabs_per_half = N_SLABS // 2

    def project_in(ub, rows):
        for kh in range(2):
            lhs = ub[:, kh * 256:(kh + 1) * 256]
            xre = jnp.dot(lhs, bre_ref[kh], preferred_element_type=F32)
            xim = jnp.dot(lhs, bim_ref[kh], preferred_element_type=F32)
            for cl in range(slabs_per_half):
                c = kh * slabs_per_half + cl
                for b in range(batch):
                    xs_ref[c, pl.ds(b * S5_PITCH, rows), :] = (
                        xre[b * rows:(b + 1) * rows, cl * LANES:(cl + 1) * LANES])
                    xs_ref[c, pl.ds((batch + b) * S5_PITCH, rows), :] = (
                        xim[b * rows:(b + 1) * rows, cl * LANES:(cl + 1) * LANES])

    def scan(n_steps):
        def body(t, hs):
            new = []
            for c in range(N_SLABS):
                rows = pl.ds(t, 2 * batch, stride=S5_PITCH)
                x8 = xs_ref[c, rows, :]
                h = hs[c]
                hn = ar_ref[c] * h + ai_ref[c] * pltpu.roll(h, batch, 0) + x8
                xs_ref[c, rows, :] = hn
                new.append(hn)
            return tuple(new)

        hs = tuple(h_ref[c] for c in range(N_SLABS))
        hs = lax.fori_loop(0, n_steps, body, hs, unroll=4)
        for c in range(N_SLABS):
            h_ref[c] = hs[c]

    @pl.when(j == 0)
    def _():
        h_ref[...] = jnp.zeros_like(h_ref)
        um = um_ref[...].astype(BF16)
        project_in(jnp.concatenate([um] * batch, axis=0), N_META)
        scan(N_META)

    uf = u_ref[...].reshape(batch * S5_CHUNK, D_SSM)
    project_in(uf.astype(BF16), S5_CHUNK)
    scan(S5_CHUNK)

    ys = []
    for nh in range(2):
        def gather(plane0):
            return jnp.concatenate(
                [jnp.concatenate(
                    [xs_ref[nh * slabs_per_half + cl, pl.ds((plane0 + b) * S5_PITCH, S5_CHUNK), :]
                     for cl in range(slabs_per_half)], axis=1)
                 for b in range(batch)], axis=0).astype(BF16)
        yre = jnp.dot(gather(0), cre_ref[nh], preferred_element_type=F32)
        yim = jnp.dot(gather(batch), cim_ref[nh], preferred_element_type=F32)
        ys.append(yre - yim)
    y = jnp.concatenate(ys, axis=1) + d_ref[...] * uf
    z = 0.5 * y * (1.0 + jnp.tanh(math.sqrt(2.0 / math.pi) * (y + 0.044715 * (y * y * y))))
    gate = jnp.dot(z.astype(BF16), wglu_ref[...], preferred_element_type=F32) + bglu_ref[...]
    out = z * (1.0 / (1.0 + jnp.exp(-gate)))
    o_ref[...] = _rms(out, g_ref[...]).astype(BF16).reshape(batch, S5_CHUNK, D_SSM)


def _s5_call(u3, u_meta, consts):
    batch, seq, _ = u3.shape
    assert 2 * batch == SUBLANES and seq % S5_CHUNK == 0
    grid = (seq // S5_CHUNK,)
    in_specs = [pl.BlockSpec((batch, S5_CHUNK, D_SSM), lambda j: (0, j, 0)),
                _const_spec(u_meta.shape)] + [_const_spec(c.shape) for c in consts]
    return pl.pallas_call(
        functools.partial(_s5_kernel, batch=batch),
        out_shape=jax.ShapeDtypeStruct((batch, seq, D_SSM), BF16),
        grid=grid, in_specs=in_specs,
        out_specs=pl.BlockSpec((batch, S5_CHUNK, D_SSM), lambda j: (0, j, 0)),
        scratch_shapes=[pltpu.VMEM((N_SLABS, 2 * batch * S5_PITCH, LANES), F32),
                        pltpu.VMEM((N_SLABS, 2 * batch, LANES), F32)],
        compiler_params=pltpu.CompilerParams(dimension_semantics=("arbitrary",),
                                             vmem_limit_bytes=VMEM_LIMIT),
        name="s5_mixer",
    )(u3, u_meta, *consts)


ATT_TQ = 1024
ATT_TK = 1024
ATT_SUB = 256
ATT_HEADS = 4


def _attn_kernel(q_ref, k_ref, v_ref, km_ref, vm_ref, mask_ref, o_ref, acc_ref, m_ref, *, online):
    nt = (((1,), (1,)), ((), ()))
    n_q = q_ref.shape[0] // ATT_TQ
    head_lanes = [slice(h * HEAD_PAD, (h + 1) * HEAD_PAD) for h in range(ATT_HEADS)]

    def step(h, sub, q, kblk, vblk, mask):
        s = lax.dot_general(q, kblk, nt, preferred_element_type=F32)
        if online:
            if mask is not None:
                s = jnp.where(mask > 0, s, -jnp.inf)
            m = m_ref[h, sub]
            m_new = jnp.maximum(m, jnp.max(s, axis=-1, keepdims=True))
            p = jnp.exp2(s - m_new).astype(BF16)
            acc_ref[h, sub] = (jnp.exp2(m - m_new) * acc_ref[h, sub]
                               + jnp.dot(p, vblk, preferred_element_type=F32))
            m_ref[h, sub] = m_new
        else:
            p = jnp.exp2(s).astype(BF16)
            if mask is not None:
                p = p * mask
            acc_ref[h, sub] += jnp.dot(p, vblk, preferred_element_type=F32)

    def q_tile(qi, _):
        q0 = pl.multiple_of(qi * ATT_TQ, ATT_TQ)
        rows = pl.ds(q0, ATT_TQ)
        qs = [q_ref[rows, hl] for hl in head_lanes]
        acc_ref[...] = jnp.zeros_like(acc_ref)
        if online:
            m_ref[...] = jnp.full(m_ref.shape, -1e30, F32)

        def body(kb, _):
            krows = pl.ds(pl.multiple_of(kb * ATT_TK, ATT_TK), ATT_TK)
            for h, hl in enumerate(head_lanes):
                step(h, slice(None), qs[h], k_ref[krows, hl], v_ref[krows, hl], None)
            return 0

        lax.fori_loop(0, qi * (ATT_TQ // ATT_TK), body, 0)

        for i in range(ATT_TQ // ATT_SUB):
            sub = slice(i * ATT_SUB, ATT_TQ)
            krows = pl.ds(q0 + i * ATT_SUB, ATT_SUB)
            for h, hl in enumerate(head_lanes):
                kblk, vblk = k_ref[krows, hl], v_ref[krows, hl]
                mask = mask_ref[sub, LANES + i * ATT_SUB:LANES + (i + 1) * ATT_SUB]
                if i == 0:
                    kblk = jnp.concatenate([km_ref[:, hl], kblk], axis=0)
                    vblk = jnp.concatenate([vm_ref[:, hl], vblk], axis=0)
                    mask = mask_ref[sub, :LANES + ATT_SUB]
                step(h, sub, qs[h][sub], kblk, vblk, mask)
        lane = lax.broadcasted_iota(jnp.int32, (ATT_TQ, HEAD_PAD), 1)
        for hp in range(ATT_HEADS // 2):
            even, odd = acc_ref[2 * hp], acc_ref[2 * hp + 1]
            o_even = even * (1.0 / even[:, V_DIM:V_DIM + 1])
            o_odd = odd * (1.0 / odd[:, 0:1])
            o_ref[rows, hp * HEAD_PAD:(hp + 1) * HEAD_PAD] = (
                jnp.where(lane < V_DIM, o_even, o_odd).astype(BF16))
        return 0

    lax.fori_loop(0, n_q, q_tile, 0)


def _attn_call(q3, k3, v3, k_meta, v_meta, mask, *, online):
    batch, seq, _ = q3.shape
    grid = (batch, N_HEADS // ATT_HEADS)
    seq_blk = pl.BlockSpec((None, seq, ATT_HEADS * HEAD_PAD), lambda b, hg: (b, 0, hg))
    meta_blk = pl.BlockSpec((LANES, ATT_HEADS * HEAD_PAD), lambda b, hg: (0, hg))
    return pl.pallas_call(
        functools.partial(_attn_kernel, online=online),
        out_shape=jax.ShapeDtypeStruct((batch, seq, N_HEADS * V_DIM), BF16),
        grid=grid,
        in_specs=[seq_blk, seq_blk, seq_blk, meta_blk, meta_blk, _const_spec(mask.shape)],
        out_specs=pl.BlockSpec((None, seq, ATT_HEADS * V_DIM), lambda b, hg: (b, 0, hg)),
        scratch_shapes=[pltpu.VMEM((ATT_HEADS, ATT_TQ, HEAD_PAD), F32),
                        pltpu.VMEM((ATT_HEADS, ATT_TQ, 1), F32)],
        compiler_params=pltpu.CompilerParams(dimension_semantics=("parallel", "parallel"),
                                             vmem_limit_bytes=VMEM_LIMIT),
        name="mla_attention_online" if online else "mla_attention",
    )(q3, k3, v3, k_meta, v_meta, mask)


FFN_TM = 512


def _ffn_kernel(x_ref, ms_ref, oa_ref, gatt_ref, wout_ref, gffn_ref, wg_ref, wu_ref, wd_ref,
                out_ref):
    ya = _rms(oa_ref[...].astype(F32), gatt_ref[...]).astype(BF16)
    mixed = jnp.concatenate([ms_ref[...], ya], axis=1)
    h1 = x_ref[...] + jnp.dot(mixed, wout_ref[...], preferred_element_type=F32)
    hn = _rms(h1, gffn_ref[...]).astype(BF16)
    g = jnp.dot(hn, wg_ref[...], preferred_element_type=F32)
    u = jnp.dot(hn, wu_ref[...], preferred_element_type=F32)
    a = (g * (1.0 / (1.0 + jnp.exp(-g))) * u).astype(BF16)
    out_ref[...] = h1 + jnp.dot(a, wd_ref[...], preferred_element_type=F32)


def _ffn_call(x2d, ms2d, oa2d, consts):
    n_rows = x2d.shape[0]
    tm = FFN_TM
    row = lambda i: (i, 0)
    once = pl.Buffered(1)
    in_specs = [pl.BlockSpec((tm, D_MODEL), row),
                pl.BlockSpec((tm, D_SSM), row),
                pl.BlockSpec((tm, N_HEADS * V_DIM), row)]
    in_specs += [pl.BlockSpec(c.shape, lambda i: (0, 0), pipeline_mode=once) for c in consts]
    return pl.pallas_call(
        _ffn_kernel,
        out_shape=jax.ShapeDtypeStruct((n_rows, D_MODEL), F32),
        grid=(n_rows // tm,), in_specs=in_specs,
        out_specs=pl.BlockSpec((tm, D_MODEL), row),
        compiler_params=pltpu.CompilerParams(dimension_semantics=("parallel",),
                                             vmem_limit_bytes=VMEM_LIMIT),
        name="outproj_ffn",
    )(x2d, ms2d, oa2d, *consts)


def _rope_tables(length):
    pos = jnp.arange(length, dtype=F32)
    inv_freq = 1.0 / (ROPE_BASE ** (jnp.arange(0, ROPE, 2, dtype=F32) / ROPE))
    ang = pos[:, None] * inv_freq[None, :]
    return jnp.cos(ang), jnp.sin(ang)


def _head_tables(gain, cos, sin):
    length = cos.shape[0]
    g_n, g_r = gain[:NOPE], gain[NOPE:]
    g_r_swapped = jnp.concatenate([g_r[HALF_ROPE:], g_r[:HALF_ROPE]])
    cos2 = jnp.concatenate([cos, cos], axis=1)
    sin2 = jnp.concatenate([sin, sin], axis=1)
    pad = jnp.zeros((length, HEAD_PAD - QK_DIM), F32)
    t1 = jnp.concatenate([jnp.broadcast_to(g_n, (length, NOPE)), g_r * cos2, pad], axis=1)
    t2 = jnp.concatenate([jnp.zeros((length, NOPE), F32), g_r_swapped * sin2, pad], axis=1)
    return t1, t2


def _rot_half_cols(w):
    return jnp.concatenate([-w[..., HALF_ROPE:], w[..., :HALF_ROPE]], axis=-1)


def _pad_cols(w, left, total):
    return jnp.pad(w, ((0, 0), (left, total - left - w.shape[1])))


def kernel(x, meta_tokens, mix_norm_g, w_in, ssm_a_re, ssm_a_im, ssm_log_dt, ssm_b_re, ssm_b_im,
           ssm_c_re, ssm_c_im, ssm_d, ssm_w_glu, ssm_b_glu, q_lora_norm_g, w_uq, kv_lora_norm_g,
           w_uk, w_uv, q_head_norm_g, k_head_norm_g, ssm_out_norm_g, att_out_norm_g, w_out,
           ffn_norm_g, w_gate, w_up, w_down):
    batch, seq, _ = x.shape
    depth = w_in.shape[0]
    assert depth == 1
    l = 0
    length = N_META + seq
    cos, sin = _rope_tables(length)

    wi = w_in[l]
    o_r = D_SSM + Q_LORA + KV_LORA
    w_r = wi[:, o_r:]
    win = jnp.concatenate([wi[:, :o_r], _pad_cols(w_r, NOPE, HEAD_PAD),
                           _pad_cols(_rot_half_cols(w_r), NOPE, HEAD_PAD)], axis=1).astype(BF16)
    wq3 = w_uq[l].reshape(Q_LORA, N_HEADS, QK_DIM)
    q1 = jnp.pad(wq3, ((0, 0), (0, 0), (0, HEAD_PAD - QK_DIM)))
    q2 = jnp.pad(_rot_half_cols(wq3[..., NOPE:]), ((0, 0), (0, 0), (NOPE, HEAD_PAD - QK_DIM)))
    wq = jnp.concatenate([q1.reshape(Q_LORA, QK_PAD), q2.reshape(Q_LORA, QK_PAD)], axis=1).astype(BF16)
    wk3 = jnp.pad(w_uk[l].reshape(KV_LORA, N_HEADS, NOPE), ((0, 0), (0, 0), (0, HEAD_PAD - NOPE)))
    wv4 = w_uv[l].reshape(KV_LORA, N_HEADS // 2, 2, V_DIM)
    zv = jnp.zeros_like(wv4[:, :, 0])
    wv = jnp.stack([jnp.concatenate([wv4[:, :, 0], zv], axis=-1),
                    jnp.concatenate([zv, wv4[:, :, 1]], axis=-1)], axis=2).reshape(KV_LORA, QK_PAD)
    ones_col = jnp.zeros((2, HEAD_PAD), F32).at[0, V_DIM].set(1.0).at[1, 0].set(1.0)
    vones = jnp.tile(ones_col.reshape(1, 2 * HEAD_PAD), (1, N_HEADS // 2))
    wkv = jnp.concatenate([wk3.reshape(KV_LORA, QK_PAD), wv], axis=1).astype(BF16)
    t1q, t2q = _head_tables(q_head_norm_g[l], cos, sin)
    t1k, t2k = _head_tables(k_head_norm_g[l], cos, sin)
    proj_consts = (mix_norm_g[l][None], win, q_lora_norm_g[l][None], wq,
                   kv_lora_norm_g[l][None], wkv, vones)

    tm = PROJ_TM
    tabs_f = tuple(t[N_META:] for t in (t1q, t2q, t1k, t2k))
    u2, q2d, k2d, v2d = _proj_call(x.reshape(batch * seq, D_MODEL), tm, tabs_f, seq // tm,
                                   proj_consts)
    tabs_m = tuple(t[:N_META] for t in (t1q, t2q, t1k, t2k))
    u_meta, _, k_meta, v_meta = _proj_call(meta_tokens, N_META, tabs_m, 1, proj_consts)

    dt = jnp.exp(ssm_log_dt[l])[:, None]
    lr, li = ssm_a_re[l], ssm_a_im[l]
    mag = jnp.exp(lr * dt)
    ar = mag * jnp.cos(li * dt)
    ai = mag * jnp.sin(li * dt)
    den = lr * lr + li * li
    fr = ((ar - 1.0) * lr + ai * li) / den
    fi = (ai * lr - (ar - 1.0) * li) / den
    br, bi = ssm_b_re[l], ssm_b_im[l]
    bbr = fr[..., None] * br - fi[..., None] * bi
    bbi = fr[..., None] * bi + fi[..., None] * br
    half_groups = N_GROUPS // 2

    def block_diag_halves(blocks):
        _, r, c = blocks.shape
        rows = blocks.reshape(2, half_groups * r, c)
        tiled = jnp.tile(rows, (1, 1, half_groups))
        row_g = jnp.arange(half_groups * r) // r
        col_g = jnp.arange(half_groups * c) // c
        return jnp.where(row_g[:, None] == col_g[None, :], tiled, 0.0).astype(BF16)

    def in_map(bb):
        return block_diag_halves(jnp.swapaxes(bb, 1, 2))

    def out_map(cc):
        return block_diag_halves(jnp.swapaxes(cc, 1, 2))

    ar_rows = jnp.broadcast_to(ar.reshape(N_SLABS, 1, LANES), (N_SLABS, 2 * batch, LANES))
    ai_flat = ai.reshape(N_SLABS, 1, LANES)
    ai_rows = jnp.concatenate([jnp.broadcast_to(-ai_flat, (N_SLABS, batch, LANES)),
                               jnp.broadcast_to(ai_flat, (N_SLABS, batch, LANES))], axis=1)
    s5_consts = (in_map(bbr), in_map(bbi), out_map(ssm_c_re[l]), out_map(ssm_c_im[l]),
                 ar_rows, ai_rows, ssm_d[l][None], ssm_w_glu[l].astype(BF16),
                 ssm_b_glu[l][None], ssm_out_norm_g[l][None])
    mixed_ssm = _s5_call(u2.reshape(batch, seq, D_SSM), u_meta, s5_consts)

    pad_rows = ((0, LANES - N_META), (0, 0))
    chunk_of = jnp.arange(ATT_TQ) // CHUNK
    causal = (chunk_of[None, :] <= chunk_of[:, None]).astype(BF16)
    mask = jnp.concatenate([jnp.ones((ATT_TQ, LANES), BF16), causal], axis=1)
    score_bound = (LOG2_E * math.sqrt(QK_DIM) * jnp.max(jnp.abs(q_head_norm_g[l]))
                   * jnp.max(jnp.abs(k_head_norm_g[l])))
    attn_args = (q2d.reshape(batch, seq, QK_PAD), k2d.reshape(batch, seq, QK_PAD),
                 v2d.reshape(batch, seq, QK_PAD),
                 jnp.pad(k_meta, pad_rows), jnp.pad(v_meta, pad_rows), mask)
    y_att = lax.cond(score_bound <= MAX_UNSHIFTED_LOG2_SCORE,
                     functools.partial(_attn_call, online=False),
                     functools.partial(_attn_call, online=True), *attn_args)

    ffn_consts = (att_out_norm_g[l][None], w_out[l].astype(BF16), ffn_norm_g[l][None],
                  w_gate[l].astype(BF16), w_up[l].astype(BF16), w_down[l].astype(BF16))
    out = _ffn_call(x.reshape(batch * seq, D_MODEL), mixed_ssm.reshape(batch * seq, D_SSM),
                    y_att.reshape(batch * seq, N_HEADS * V_DIM), ffn_consts)
    return out.reshape(batch, seq, D_MODEL)
```

```python
import functools
import math

import jax
import jax.numpy as jnp
from jax import lax
from jax.experimental import pallas as pl
from jax.experimental.pallas import tpu as pltpu

F32 = jnp.float32
BF16 = jnp.bfloat16

D_MODEL = 1024
N_META = 16
CHUNK = 64
D_SSM = 512
SSM_GROUP = 16
N_GROUPS = D_SSM // SSM_GROUP
SSM_STATE = 64
N_HEADS = 8
V_DIM = 64
NOPE = 64
ROPE = 32
HALF_ROPE = ROPE // 2
QK_DIM = NOPE + ROPE
Q_LORA = 256
KV_LORA = 128
D_FF = 2816
ROPE_BASE = 10000.0
EPS = 1e-6
LOG2_E = math.log2(math.e)
MAX_UNSHIFTED_LOG2_SCORE = 40.0

LANES = 128
SUBLANES = 8
HEAD_PAD = LANES
QK_PAD = N_HEADS * HEAD_PAD
N_STATE_COLS = N_GROUPS * SSM_STATE
N_SLABS = N_STATE_COLS // LANES
S5_CHUNK = 128
S5_PITCH = S5_CHUNK + SUBLANES
VMEM_LIMIT = 56 * 1024 * 1024


def _rms(x, g):
    return x * lax.rsqrt(jnp.mean(x * x, axis=-1, keepdims=True) + EPS) * g


def _proj_kernel(x_ref, gmix_ref, win_ref, gq_ref, wq_ref, gkv_ref, wkv_ref, vones_ref,
                 t1q_ref, t2q_ref, t1k_ref, t2k_ref, *rest):
    n_cast = (len(rest) - 4) // 2
    u_ref, q_ref, k_ref, v_ref = rest[n_cast:n_cast + 4]
    for src_ref, dst_ref in zip(rest[:n_cast], rest[n_cast + 4:]):
        dst_ref[...] = src_ref[...].astype(BF16)

    x = x_ref[...]
    xn = _rms(x, gmix_ref[...]).astype(BF16)
    p = jnp.dot(xn, win_ref[...], preferred_element_type=F32)
    u_ref[...] = p[:, :D_SSM]

    cq = p[:, D_SSM:D_SSM + Q_LORA]
    cqn = _rms(cq, gq_ref[...]).astype(BF16)
    q12 = jnp.dot(cqn, wq_ref[...], preferred_element_type=F32)
    t1q = t1q_ref[...]
    t2q = t2q_ref[...]
    scale = QK_DIM ** -0.5 * LOG2_E
    for h in range(N_HEADS):
        q1 = q12[:, h * HEAD_PAD:(h + 1) * HEAD_PAD]
        q2 = q12[:, QK_PAD + h * HEAD_PAD:QK_PAD + (h + 1) * HEAD_PAD]
        r = lax.rsqrt(jnp.sum(q1 * q1, axis=-1, keepdims=True) * (1.0 / QK_DIM) + EPS)
        qh = (q1 * t1q + q2 * t2q) * (r * scale)
        q_ref[:, h * HEAD_PAD:(h + 1) * HEAD_PAD] = qh.astype(BF16)

    c0 = D_SSM + Q_LORA
    ckv = p[:, c0:c0 + KV_LORA]
    ckvn = _rms(ckv, gkv_ref[...]).astype(BF16)
    kv = jnp.dot(ckvn, wkv_ref[...], preferred_element_type=F32)
    v_ref[...] = (kv[:, QK_PAD:] + vones_ref[...]).astype(BF16)
    kr = p[:, c0 + KV_LORA:c0 + KV_LORA + HEAD_PAD]
    kr_rot = p[:, c0 + KV_LORA + HEAD_PAD:c0 + KV_LORA + 2 * HEAD_PAD]
    ss_r = jnp.sum(kr * kr, axis=-1, keepdims=True)
    t1k = t1k_ref[...]
    kr_part = kr_rot * t2k_ref[...]
    for h in range(N_HEADS):
        kn = kv[:, h * HEAD_PAD:(h + 1) * HEAD_PAD]
        ss = jnp.sum(kn * kn, axis=-1, keepdims=True) + ss_r
        r = lax.rsqrt(ss * (1.0 / QK_DIM) + EPS)
        kh = ((kn + kr) * t1k + kr_part) * r
        k_ref[:, h * HEAD_PAD:(h + 1) * HEAD_PAD] = kh.astype(BF16)


def _const_spec(shape):
    nd = len(shape)
    return pl.BlockSpec(shape, lambda *_: (0,) * nd)


def _proj_call(x2d, tm, tabs, n_tab_blocks, consts, to_bf16=()):
    n_rows = x2d.shape[0]
    n_steps = n_rows // tm
    row = lambda i: (i, 0)
    tab = lambda i: (i % n_tab_blocks, 0)
    cast_specs = [pl.BlockSpec((w.shape[0] // n_steps, w.shape[1]), row) for w in to_bf16]
    in_specs = ([pl.BlockSpec((tm, D_MODEL), row)] + [_const_spec(c.shape) for c in consts]
                + [pl.BlockSpec((tm, HEAD_PAD), tab)] * 4 + cast_specs)
    out_shape = (
        jax.ShapeDtypeStruct((n_rows, D_SSM), F32),
        jax.ShapeDtypeStruct((n_rows, QK_PAD), BF16),
        jax.ShapeDtypeStruct((n_rows, QK_PAD), BF16),
        jax.ShapeDtypeStruct((n_rows, QK_PAD), BF16),
    ) + tuple(jax.ShapeDtypeStruct(w.shape, BF16) for w in to_bf16)
    out_specs = (
        pl.BlockSpec((tm, D_SSM), row),
        pl.BlockSpec((tm, QK_PAD), row),
        pl.BlockSpec((tm, QK_PAD), row),
        pl.BlockSpec((tm, QK_PAD), row),
    ) + tuple(cast_specs)
    return pl.pallas_call(
        _proj_kernel, out_shape=out_shape, grid=(n_steps,), in_specs=in_specs,
        out_specs=out_specs,
        compiler_params=pltpu.CompilerParams(dimension_semantics=("parallel",),
                                             vmem_limit_bytes=VMEM_LIMIT),
        name="proj_mla",
    )(x2d, *consts, *tabs, *to_bf16)


def _s5_kernel(u_ref, um_ref, bre_ref, bim_ref, cre_ref, cim_ref, ar_ref, ai_ref,
               d_ref, wglu_ref, bglu_ref, g_ref, o_ref, xs_ref, h_ref, *, batch):
    j = pl.program_id(0)
    slabs_per_half = N_SLABS // 2

    def project_in(ub, rows):
        for kh in range(2):
            lhs = ub[:, kh * 256:(kh + 1) * 256]
            xre = jnp.dot(lhs, bre_ref[kh], preferred_element_type=F32)
            xim = jnp.dot(lhs, bim_ref[kh], preferred_element_type=F32)
            for cl in range(slabs_per_half):
                c = kh * slabs_per_half + cl
                for b in range(batch):
                    xs_ref[c, pl.ds(b * S5_PITCH, rows), :] = (
                        xre[b * rows:(b + 1) * rows, cl * LANES:(cl + 1) * LANES])
                    xs_ref[c, pl.ds((batch + b) * S5_PITCH, rows), :] = (
                        xim[b * rows:(b + 1) * rows, cl * LANES:(cl + 1) * LANES])

    def scan(n_steps):
        def body(t, hs):
            new = []
            for c in range(N_SLABS):
                rows = pl.ds(t, 2 * batch, stride=S5_PITCH)
                x8 = xs_ref[c, rows, :]
                h = hs[c]
                hn = ar_ref[c] * h + ai_ref[c] * pltpu.roll(h, batch, 0) + x8
                xs_ref[c, rows, :] = hn
                new.append(hn)
            return tuple(new)

        hs = tuple(h_ref[c] for c in range(N_SLABS))
        hs = lax.fori_loop(0, n_steps, body, hs, unroll=4)
        for c in range(N_SLABS):
            h_ref[c] = hs[c]

    @pl.when(j == 0)
    def _():
        h_ref[...] = jnp.zeros_like(h_ref)
        um = um_ref[...].astype(BF16)
        project_in(jnp.concatenate([um] * batch, axis=0), N_META)
        scan(N_META)

    uf = u_ref[...].reshape(batch * S5_CHUNK, D_SSM)
    project_in(uf.astype(BF16), S5_CHUNK)
    scan(S5_CHUNK)

    ys = []
    for nh in range(2):
        def gather(plane0):
            return jnp.concatenate(
                [jnp.concatenate(
                    [xs_ref[nh * slabs_per_half + cl, pl.ds((plane0 + b) * S5_PITCH, S5_CHUNK), :]
                     for cl in range(slabs_per_half)], axis=1)
                 for b in range(batch)], axis=0).astype(BF16)
        yre = jnp.dot(gather(0), cre_ref[nh], preferred_element_type=F32)
        yim = jnp.dot(gather(batch), cim_ref[nh], preferred_element_type=F32)
        ys.append(yre - yim)
    y = jnp.concatenate(ys, axis=1) + d_ref[...] * uf
    z = 0.5 * y * (1.0 + jnp.tanh(math.sqrt(2.0 / math.pi) * (y + 0.044715 * (y * y * y))))
    gate = jnp.dot(z.astype(BF16), wglu_ref[...], preferred_element_type=F32) + bglu_ref[...]
    out = z * (1.0 / (1.0 + jnp.exp(-gate)))
    o_ref[...] = _rms(out, g_ref[...]).astype(BF16).reshape(batch, S5_CHUNK, D_SSM)


def _s5_call(u3, u_meta, consts):
    batch, seq, _ = u3.shape
    assert 2 * batch == SUBLANES and seq % S5_CHUNK == 0
    grid = (seq // S5_CHUNK,)
    in_specs = [pl.BlockSpec((batch, S5_CHUNK, D_SSM), lambda j: (0, j, 0)),
                _const_spec(u_meta.shape)] + [_const_spec(c.shape) for c in consts]
    return pl.pallas_call(
        functools.partial(_s5_kernel, batch=batch),
        out_shape=jax.ShapeDtypeStruct((batch, seq, D_SSM), BF16),
        grid=grid, in_specs=in_specs,
        out_specs=pl.BlockSpec((batch, S5_CHUNK, D_SSM), lambda j: (0, j, 0)),
        scratch_shapes=[pltpu.VMEM((N_SLABS, 2 * batch * S5_PITCH, LANES), F32),
                        pltpu.VMEM((N_SLABS, 2 * batch, LANES), F32)],
        compiler_params=pltpu.CompilerParams(dimension_semantics=("arbitrary",),
                                             vmem_limit_bytes=VMEM_LIMIT),
        name="s5_mixer",
    )(u3, u_meta, *consts)


ATT_TQ = 1024
ATT_TK = 1024
ATT_SUB = 256
ATT_HEADS = 4


def _attn_kernel(q_ref, k_ref, v_ref, km_ref, vm_ref, mask_ref, o_ref, acc_ref, m_ref, *, online):
    nt = (((1,), (1,)), ((), ()))
    n_q = q_ref.shape[0] // ATT_TQ
    head_lanes = [slice(h * HEAD_PAD, (h + 1) * HEAD_PAD) for h in range(ATT_HEADS)]

    def step(h, sub, q, kblk, vblk, mask):
        s = lax.dot_general(q, kblk, nt, preferred_element_type=F32)
        if online:
            if mask is not None:
                s = jnp.where(mask > 0, s, -jnp.inf)
            m = m_ref[h, sub]
            m_new = jnp.maximum(m, jnp.max(s, axis=-1, keepdims=True))
            p = jnp.exp2(s - m_new).astype(BF16)
            acc_ref[h, sub] = (jnp.exp2(m - m_new) * acc_ref[h, sub]
                               + jnp.dot(p, vblk, preferred_element_type=F32))
            m_ref[h, sub] = m_new
        else:
            p = jnp.exp2(s).astype(BF16)
            if mask is not None:
                p = p * mask
            acc_ref[h, sub] += jnp.dot(p, vblk, preferred_element_type=F32)

    def q_tile(qi, _):
        q0 = pl.multiple_of(qi * ATT_TQ, ATT_TQ)
        rows = pl.ds(q0, ATT_TQ)
        qs = [q_ref[rows, hl] for hl in head_lanes]
        acc_ref[...] = jnp.zeros_like(acc_ref)
        if online:
            m_ref[...] = jnp.full(m_ref.shape, -1e30, F32)

        def body(kb, _):
            krows = pl.ds(pl.multiple_of(kb * ATT_TK, ATT_TK), ATT_TK)
            for h, hl in enumerate(head_lanes):
                step(h, slice(None), qs[h], k_ref[krows, hl], v_ref[krows, hl], None)
            return 0

        lax.fori_loop(0, qi * (ATT_TQ // ATT_TK), body, 0)

        for i in range(ATT_TQ // ATT_SUB):
            sub = slice(i * ATT_SUB, ATT_TQ)
            krows = pl.ds(q0 + i * ATT_SUB, ATT_SUB)
            for h, hl in enumerate(head_lanes):
                kblk, vblk = k_ref[krows, hl], v_ref[krows, hl]
                mask = mask_ref[sub, LANES + i * ATT_SUB:LANES + (i + 1) * ATT_SUB]
                if i == 0:
                    kblk = jnp.concatenate([km_ref[:, hl], kblk], axis=0)
                    vblk = jnp.concatenate([vm_ref[:, hl], vblk], axis=0)
                    mask = mask_ref[sub, :LANES + ATT_SUB]
                step(h, sub, qs[h][sub], kblk, vblk, mask)
        lane = lax.broadcasted_iota(jnp.int32, (ATT_TQ, HEAD_PAD), 1)
        for hp in range(ATT_HEADS // 2):
            even, odd = acc_ref[2 * hp], acc_ref[2 * hp + 1]
            o_even = even * (1.0 / even[:, V_DIM:V_DIM + 1])
            o_odd = odd * (1.0 / odd[:, 0:1])
            o_ref[rows, hp * HEAD_PAD:(hp + 1) * HEAD_PAD] = (
                jnp.where(lane < V_DIM, o_even, o_odd).astype(BF16))
        return 0

    lax.fori_loop(0, n_q, q_tile, 0)


def _attn_call(q3, k3, v3, k_meta, v_meta, mask, *, online):
    batch, seq, _ = q3.shape
    grid = (batch, N_HEADS // ATT_HEADS)
    seq_blk = pl.BlockSpec((None, seq, ATT_HEADS * HEAD_PAD), lambda b, hg: (b, 0, hg))
    meta_blk = pl.BlockSpec((LANES, ATT_HEADS * HEAD_PAD), lambda b, hg: (0, hg))
    return pl.pallas_call(
        functools.partial(_attn_kernel, online=online),
        out_shape=jax.ShapeDtypeStruct((batch, seq, N_HEADS * V_DIM), BF16),
        grid=grid,
        in_specs=[seq_blk, seq_blk, seq_blk, meta_blk, meta_blk, _const_spec(mask.shape)],
        out_specs=pl.BlockSpec((None, seq, ATT_HEADS * V_DIM), lambda b, hg: (b, 0, hg)),
        scratch_shapes=[pltpu.VMEM((ATT_HEADS, ATT_TQ, HEAD_PAD), F32),
                        pltpu.VMEM((ATT_HEADS, ATT_TQ, 1), F32)],
        compiler_params=pltpu.CompilerParams(dimension_semantics=("parallel", "parallel"),
                                             vmem_limit_bytes=VMEM_LIMIT),
        name="mla_attention_online" if online else "mla_attention",
    )(q3, k3, v3, k_meta, v_meta, mask)


FFN_TM = 512


def _ffn_kernel(x_ref, ms_ref, oa_ref, gatt_ref, wout_ref, gffn_ref, wg_ref, wu_ref, wd_ref,
                out_ref):
    ya = _rms(oa_ref[...].astype(F32), gatt_ref[...]).astype(BF16)
    mixed = jnp.concatenate([ms_ref[...], ya], axis=1)
    h1 = x_ref[...] + jnp.dot(mixed, wout_ref[...], preferred_element_type=F32)
    hn = _rms(h1, gffn_ref[...]).astype(BF16)
    g = jnp.dot(hn, wg_ref[...], preferred_element_type=F32)
    u = jnp.dot(hn, wu_ref[...], preferred_element_type=F32)
    a = (g * (1.0 / (1.0 + jnp.exp(-g))) * u).astype(BF16)
    out_ref[...] = h1 + jnp.dot(a, wd_ref[...], preferred_element_type=F32)


def _ffn_call(x2d, ms2d, oa2d, consts):
    n_rows = x2d.shape[0]
    tm = FFN_TM
    row = lambda i: (i, 0)
    once = pl.Buffered(1)
    in_specs = [pl.BlockSpec((tm, D_MODEL), row),
                pl.BlockSpec((tm, D_SSM), row),
                pl.BlockSpec((tm, N_HEADS * V_DIM), row)]
    in_specs += [pl.BlockSpec(c.shape, lambda i: (0, 0), pipeline_mode=once) for c in consts]
    return pl.pallas_call(
        _ffn_kernel,
        out_shape=jax.ShapeDtypeStruct((n_rows, D_MODEL), F32),
        grid=(n_rows // tm,), in_specs=in_specs,
        out_specs=pl.BlockSpec((tm, D_MODEL), row),
        compiler_params=pltpu.CompilerParams(dimension_semantics=("parallel",),
                                             vmem_limit_bytes=VMEM_LIMIT),
        name="outproj_ffn",
    )(x2d, ms2d, oa2d, *consts)


def _rope_tables(length):
    pos = jnp.arange(length, dtype=F32)
    inv_freq = 1.0 / (ROPE_BASE ** (jnp.arange(0, ROPE, 2, dtype=F32) / ROPE))
    ang = pos[:, None] * inv_freq[None, :]
    return jnp.cos(ang), jnp.sin(ang)


def _head_tables(gain, cos, sin):
    length = cos.shape[0]
    g_n, g_r = gain[:NOPE], gain[NOPE:]
    g_r_swapped = jnp.concatenate([g_r[HALF_ROPE:], g_r[:HALF_ROPE]])
    cos2 = jnp.concatenate([cos, cos], axis=1)
    sin2 = jnp.concatenate([sin, sin], axis=1)
    pad = jnp.zeros((length, HEAD_PAD - QK_DIM), F32)
    t1 = jnp.concatenate([jnp.broadcast_to(g_n, (length, NOPE)), g_r * cos2, pad], axis=1)
    t2 = jnp.concatenate([jnp.zeros((length, NOPE), F32), g_r_swapped * sin2, pad], axis=1)
    return t1, t2


def _rot_half_cols(w):
    return jnp.concatenate([-w[..., HALF_ROPE:], w[..., :HALF_ROPE]], axis=-1)


def _pad_cols(w, left, total):
    return jnp.pad(w, ((0, 0), (left, total - left - w.shape[1])))


def kernel(x, meta_tokens, mix_norm_g, w_in, ssm_a_re, ssm_a_im, ssm_log_dt, ssm_b_re, ssm_b_im,
           ssm_c_re, ssm_c_im, ssm_d, ssm_w_glu, ssm_b_glu, q_lora_norm_g, w_uq, kv_lora_norm_g,
           w_uk, w_uv, q_head_norm_g, k_head_norm_g, ssm_out_norm_g, att_out_norm_g, w_out,
           ffn_norm_g, w_gate, w_up, w_down):
    batch, seq, _ = x.shape
    depth = w_in.shape[0]
    assert depth == 1
    l = 0
    length = N_META + seq
    cos, sin = _rope_tables(length)

    wi = w_in[l]
    o_r = D_SSM + Q_LORA + KV_LORA
    w_r = wi[:, o_r:]
    win = jnp.concatenate([wi[:, :o_r], _pad_cols(w_r, NOPE, HEAD_PAD),
                           _pad_cols(_rot_half_cols(w_r), NOPE, HEAD_PAD)], axis=1).astype(BF16)
    wq3 = w_uq[l].reshape(Q_LORA, N_HEADS, QK_DIM)
    q1 = jnp.pad(wq3, ((0, 0), (0, 0), (0, HEAD_PAD - QK_DIM)))
    q2 = jnp.pad(_rot_half_cols(wq3[..., NOPE:]), ((0, 0), (0, 0), (NOPE, HEAD_PAD - QK_DIM)))
    wq = jnp.concatenate([q1.reshape(Q_LORA, QK_PAD), q2.reshape(Q_LORA, QK_PAD)], axis=1).astype(BF16)
    wk3 = jnp.pad(w_uk[l].reshape(KV_LORA, N_HEADS, NOPE), ((0, 0), (0, 0), (0, HEAD_PAD - NOPE)))
    wv4 = w_uv[l].reshape(KV_LORA, N_HEADS // 2, 2, V_DIM)
    zv = jnp.zeros_like(wv4[:, :, 0])
    wv = jnp.stack([jnp.concatenate([wv4[:, :, 0], zv], axis=-1),
                    jnp.concatenate([zv, wv4[:, :, 1]], axis=-1)], axis=2).reshape(KV_LORA, QK_PAD)
    ones_col = jnp.zeros((2, HEAD_PAD), F32).at[0, V_DIM].set(1.0).at[1, 0].set(1.0)
    vones = jnp.tile(ones_col.reshape(1, 2 * HEAD_PAD), (1, N_HEADS // 2))
    wkv = jnp.concatenate([wk3.reshape(KV_LORA, QK_PAD), wv], axis=1).astype(BF16)
    tabs_m = (_head_tables(q_head_norm_g[l], cos[:N_META], sin[:N_META])
              + _head_tables(k_head_norm_g[l], cos[:N_META], sin[:N_META]))
    tabs_f = (_head_tables(q_head_norm_g[l], cos[N_META:], sin[N_META:])
              + _head_tables(k_head_norm_g[l], cos[N_META:], sin[N_META:]))
    proj_consts = (mix_norm_g[l][None], win, q_lora_norm_g[l][None], wq,
                   kv_lora_norm_g[l][None], wkv, vones)

    tm = 512
    ffn_f32 = (w_out[l], w_gate[l], w_up[l], w_down[l].reshape(D_MODEL, D_FF))
    u2, q2d, k2d, v2d, wout_b, wg_b, wu_b, wd_b = _proj_call(
        x.reshape(batch * seq, D_MODEL), tm, tabs_f, seq // tm, proj_consts, ffn_f32)
    u_meta, _, k_meta, v_meta = _proj_call(meta_tokens, N_META, tabs_m, 1, proj_consts)

    dt = jnp.exp(ssm_log_dt[l])[:, None]
    lr, li = ssm_a_re[l], ssm_a_im[l]
    mag = jnp.exp(lr * dt)
    ar = mag * jnp.cos(li * dt)
    ai = mag * jnp.sin(li * dt)
    den = lr * lr + li * li
    fr = ((ar - 1.0) * lr + ai * li) / den
    fi = (ai * lr - (ar - 1.0) * li) / den
    br, bi = ssm_b_re[l], ssm_b_im[l]
    bbr = fr[..., None] * br - fi[..., None] * bi
    bbi = fr[..., None] * bi + fi[..., None] * br
    half_groups = N_GROUPS // 2

    def block_diag_halves(blocks):
        _, r, c = blocks.shape
        rows = blocks.reshape(2, half_groups * r, c)
        tiled = jnp.tile(rows, (1, 1, half_groups))
        row_g = jnp.arange(half_groups * r) // r
        col_g = jnp.arange(half_groups * c) // c
        return jnp.where(row_g[:, None] == col_g[None, :], tiled, 0.0).astype(BF16)

    def in_map(bb):
        return block_diag_halves(jnp.swapaxes(bb, 1, 2))

    def out_map(cc):
        return block_diag_halves(jnp.swapaxes(cc, 1, 2))

    ar_rows = jnp.broadcast_to(ar.reshape(N_SLABS, 1, LANES), (N_SLABS, 2 * batch, LANES))
    ai_flat = ai.reshape(N_SLABS, 1, LANES)
    ai_rows = jnp.concatenate([jnp.broadcast_to(-ai_flat, (N_SLABS, batch, LANES)),
                               jnp.broadcast_to(ai_flat, (N_SLABS, batch, LANES))], axis=1)
    s5_consts = (in_map(bbr), in_map(bbi), out_map(ssm_c_re[l]), out_map(ssm_c_im[l]),
                 ar_rows, ai_rows, ssm_d[l][None], ssm_w_glu[l].astype(BF16),
                 ssm_b_glu[l][None], ssm_out_norm_g[l][None])
    mixed_ssm = _s5_call(u2.reshape(batch, seq, D_SSM), u_meta, s5_consts)

    pad_rows = ((0, LANES - N_META), (0, 0))
    chunk_of = jnp.arange(ATT_TQ) // CHUNK
    causal = (chunk_of[None, :] <= chunk_of[:, None]).astype(BF16)
    mask = jnp.concatenate([jnp.ones((ATT_TQ, LANES), BF16), causal], axis=1)
    score_bound = (LOG2_E * math.sqrt(QK_DIM) * jnp.max(jnp.abs(q_head_norm_g[l]))
                   * jnp.max(jnp.abs(k_head_norm_g[l])))
    attn_args = (q2d.reshape(batch, seq, QK_PAD), k2d.reshape(batch, seq, QK_PAD),
                 v2d.reshape(batch, seq, QK_PAD),
                 jnp.pad(k_meta, pad_rows), jnp.pad(v_meta, pad_rows), mask)
    y_att = lax.cond(score_bound <= MAX_UNSHIFTED_LOG2_SCORE,
                     functools.partial(_attn_call, online=False),
                     functools.partial(_attn_call, online=True), *attn_args)

    ffn_consts = (att_out_norm_g[l][None], wout_b, ffn_norm_g[l][None],
                  wg_b, wu_b, wd_b.reshape(D_FF, D_MODEL))
    out = _ffn_call(x.reshape(batch * seq, D_MODEL), mixed_ssm.reshape(batch * seq, D_SSM),
                    y_att.reshape(batch * seq, N_HEADS * V_DIM), ffn_consts)
    return out.reshape(batch, seq, D_MODEL)
```

```python
import functools
import math

import jax
import jax.numpy as jnp
from jax import lax
from jax.experimental import pallas as pl
from jax.experimental.pallas import tpu as pltpu

F32 = jnp.float32
BF16 = jnp.bfloat16

D_MODEL = 1024
N_META = 16
CHUNK = 64
D_SSM = 512
SSM_GROUP = 16
N_GROUPS = D_SSM // SSM_GROUP
SSM_STATE = 64
N_HEADS = 8
V_DIM = 64
NOPE = 64
ROPE = 32
HALF_ROPE = ROPE // 2
QK_DIM = NOPE + ROPE
Q_LORA = 256
KV_LORA = 128
D_FF = 2816
ROPE_BASE = 10000.0
EPS = 1e-6
LOG2_E = math.log2(math.e)
MAX_UNSHIFTED_LOG2_SCORE = 40.0

LANES = 128
SUBLANES = 8
HEAD_PAD = LANES
QK_PAD = N_HEADS * HEAD_PAD
N_STATE_COLS = N_GROUPS * SSM_STATE
N_SLABS = N_STATE_COLS // LANES
S5_CHUNK = 128
S5_PITCH = S5_CHUNK + SUBLANES
VMEM_LIMIT = 56 * 1024 * 1024


def _rms(x, g):
    return x * lax.rsqrt(jnp.mean(x * x, axis=-1, keepdims=True) + EPS) * g


def _proj_kernel(x_ref, gmix_ref, win_ref, gq_ref, wq_ref, gkv_ref, wkv_ref, vones_ref,
                 t1q_ref, t2q_ref, t1k_ref, t2k_ref, *rest):
    n_cast = (len(rest) - 4) // 2
    u_ref, q_ref, k_ref, v_ref = rest[n_cast:n_cast + 4]
    for src_ref, dst_ref in zip(rest[:n_cast], rest[n_cast + 4:]):
        dst_ref[...] = src_ref[...].astype(BF16)

    x = x_ref[...]
    xn = _rms(x, gmix_ref[...]).astype(BF16)
    p = jnp.dot(xn, win_ref[...], preferred_element_type=F32)
    u_ref[...] = p[:, :D_SSM]

    cq = p[:, D_SSM:D_SSM + Q_LORA]
    cqn = _rms(cq, gq_ref[...]).astype(BF16)
    q12 = jnp.dot(cqn, wq_ref[...], preferred_element_type=F32)
    t1q = t1q_ref[...]
    t2q = t2q_ref[...]
    scale = QK_DIM ** -0.5 * LOG2_E
    for h in range(N_HEADS):
        q1 = q12[:, h * HEAD_PAD:(h + 1) * HEAD_PAD]
        q2 = q12[:, QK_PAD + h * HEAD_PAD:QK_PAD + (h + 1) * HEAD_PAD]
        r = lax.rsqrt(jnp.sum(q1 * q1, axis=-1, keepdims=True) * (1.0 / QK_DIM) + EPS)
        qh = (q1 * t1q + q2 * t2q) * (r * scale)
        q_ref[:, h * HEAD_PAD:(h + 1) * HEAD_PAD] = qh.astype(BF16)

    c0 = D_SSM + Q_LORA
    ckv = p[:, c0:c0 + KV_LORA]
    ckvn = _rms(ckv, gkv_ref[...]).astype(BF16)
    kv = jnp.dot(ckvn, wkv_ref[...], preferred_element_type=F32)
    v_ref[...] = (kv[:, QK_PAD:] + vones_ref[...]).astype(BF16)
    kr = p[:, c0 + KV_LORA:c0 + KV_LORA + HEAD_PAD]
    kr_rot = p[:, c0 + KV_LORA + HEAD_PAD:c0 + KV_LORA + 2 * HEAD_PAD]
    ss_r = jnp.sum(kr * kr, axis=-1, keepdims=True)
    t1k = t1k_ref[...]
    kr_part = kr_rot * t2k_ref[...]
    for h in range(N_HEADS):
        kn = kv[:, h * HEAD_PAD:(h + 1) * HEAD_PAD]
        ss = jnp.sum(kn * kn, axis=-1, keepdims=True) + ss_r
        r = lax.rsqrt(ss * (1.0 / QK_DIM) + EPS)
        kh = ((kn + kr) * t1k + kr_part) * r
        k_ref[:, h * HEAD_PAD:(h + 1) * HEAD_PAD] = kh.astype(BF16)


def _const_spec(shape):
    nd = len(shape)
    return pl.BlockSpec(shape, lambda *_: (0,) * nd)


def _proj_call(x2d, tm, tabs, n_tab_blocks, consts, to_bf16=()):
    n_rows = x2d.shape[0]
    n_steps = n_rows // tm
    row = lambda i: (i, 0)
    tab = lambda i: (i % n_tab_blocks, 0)
    cast_in, cast_out = [], []
    for w in to_bf16:
        _, w_rows, w_cols = w.shape
        rep = next(r for r in (1, 2, 4, 8) if (w_rows * r) % (16 * n_steps) == 0)
        blk_rows = w_rows * rep // n_steps
        cast_in.append(pl.BlockSpec((None, blk_rows, w_cols),
                                    functools.partial(lambda rep, i: (0, i // rep, 0), rep)))
        cast_out.append(pl.BlockSpec((blk_rows, w_cols),
                                     functools.partial(lambda rep, i: (i // rep, 0), rep)))
    in_specs = ([pl.BlockSpec((tm, D_MODEL), row)] + [_const_spec(c.shape) for c in consts]
                + [pl.BlockSpec((tm, HEAD_PAD), tab)] * 4 + cast_in)
    out_shape = (
        jax.ShapeDtypeStruct((n_rows, D_SSM), F32),
        jax.ShapeDtypeStruct((n_rows, QK_PAD), BF16),
        jax.ShapeDtypeStruct((n_rows, QK_PAD), BF16),
        jax.ShapeDtypeStruct((n_rows, QK_PAD), BF16),
    ) + tuple(jax.ShapeDtypeStruct(w.shape[1:], BF16) for w in to_bf16)
    out_specs = (
        pl.BlockSpec((tm, D_SSM), row),
        pl.BlockSpec((tm, QK_PAD), row),
        pl.BlockSpec((tm, QK_PAD), row),
        pl.BlockSpec((tm, QK_PAD), row),
    ) + tuple(cast_out)
    return pl.pallas_call(
        _proj_kernel, out_shape=out_shape, grid=(n_steps,), in_specs=in_specs,
        out_specs=out_specs,
        compiler_params=pltpu.CompilerParams(dimension_semantics=("arbitrary",),
                                             vmem_limit_bytes=VMEM_LIMIT),
        name="proj_mla",
    )(x2d, *consts, *tabs, *to_bf16)


def _s5_kernel(u_ref, um_ref, bre_ref, bim_ref, cre_ref, cim_ref, ar_ref, ai_ref,
               d_ref, wglu_ref, bglu_ref, g_ref, o_ref, xs_ref, h_ref, *, batch):
    j = pl.program_id(0)
    slabs_per_half = N_SLABS // 2

    def project_in(ub, rows):
        for kh in range(2):
            lhs = ub[:, kh * 256:(kh + 1) * 256]
            xre = jnp.dot(lhs, bre_ref[kh], preferred_element_type=F32)
            xim = jnp.dot(lhs, bim_ref[kh], preferred_element_type=F32)
            for cl in range(slabs_per_half):
                c = kh * slabs_per_half + cl
                for b in range(batch):
                    xs_ref[c, pl.ds(b * S5_PITCH, rows), :] = (
                        xre[b * rows:(b + 1) * rows, cl * LANES:(cl + 1) * LANES])
                    xs_ref[c, pl.ds((batch + b) * S5_PITCH, rows), :] = (
                        xim[b * rows:(b + 1) * rows, cl * LANES:(cl + 1) * LANES])

    def scan(n_steps):
        def body(t, hs):
            new = []
            for c in range(N_SLABS):
                rows = pl.ds(t, 2 * batch, stride=S5_PITCH)
                x8 = xs_ref[c, rows, :]
                h = hs[c]
                hn = ar_ref[c] * h + ai_ref[c] * pltpu.roll(h, batch, 0) + x8
                xs_ref[c, rows, :] = hn
                new.append(hn)
            return tuple(new)

        hs = tuple(h_ref[c] for c in range(N_SLABS))
        hs = lax.fori_loop(0, n_steps, body, hs, unroll=4)
        for c in range(N_SLABS):
            h_ref[c] = hs[c]

    @pl.when(j == 0)
    def _():
        h_ref[...] = jnp.zeros_like(h_ref)
        um = um_ref[...].astype(BF16)
        project_in(jnp.concatenate([um] * batch, axis=0), N_META)
        scan(N_META)

    uf = u_ref[...].reshape(batch * S5_CHUNK, D_SSM)
    project_in(uf.astype(BF16), S5_CHUNK)
    scan(S5_CHUNK)

    ys = []
    for nh in range(2):
        def gather(plane0):
            return jnp.concatenate(
                [jnp.concatenate(
                    [xs_ref[nh * slabs_per_half + cl, pl.ds((plane0 + b) * S5_PITCH, S5_CHUNK), :]
                     for cl in range(slabs_per_half)], axis=1)
                 for b in range(batch)], axis=0).astype(BF16)
        yre = jnp.dot(gather(0), cre_ref[nh], preferred_element_type=F32)
        yim = jnp.dot(gather(batch), cim_ref[nh], preferred_element_type=F32)
        ys.append(yre - yim)
    y = jnp.concatenate(ys, axis=1) + d_ref[...] * uf
    z = 0.5 * y * (1.0 + jnp.tanh(math.sqrt(2.0 / math.pi) * (y + 0.044715 * (y * y * y))))
    gate = jnp.dot(z.astype(BF16), wglu_ref[...], preferred_element_type=F32) + bglu_ref[...]
    out = z * (1.0 / (1.0 + jnp.exp(-gate)))
    o_ref[...] = _rms(out, g_ref[...]).astype(BF16).reshape(batch, S5_CHUNK, D_SSM)


def _s5_call(u3, u_meta, consts):
    batch, seq, _ = u3.shape
    assert 2 * batch == SUBLANES and seq % S5_CHUNK == 0
    grid = (seq // S5_CHUNK,)
    in_specs = [pl.BlockSpec((batch, S5_CHUNK, D_SSM), lambda j: (0, j, 0)),
                _const_spec(u_meta.shape)] + [_const_spec(c.shape) for c in consts]
    return pl.pallas_call(
        functools.partial(_s5_kernel, batch=batch),
        out_shape=jax.ShapeDtypeStruct((batch, seq, D_SSM), BF16),
        grid=grid, in_specs=in_specs,
        out_specs=pl.BlockSpec((batch, S5_CHUNK, D_SSM), lambda j: (0, j, 0)),
        scratch_shapes=[pltpu.VMEM((N_SLABS, 2 * batch * S5_PITCH, LANES), F32),
                        pltpu.VMEM((N_SLABS, 2 * batch, LANES), F32)],
        compiler_params=pltpu.CompilerParams(dimension_semantics=("arbitrary",),
                                             vmem_limit_bytes=VMEM_LIMIT),
        name="s5_mixer",
    )(u3, u_meta, *consts)


ATT_TQ = 1024
ATT_TK = 1024
ATT_SUB = 256
ATT_HEADS = 4


def _attn_kernel(q_ref, k_ref, v_ref, km_ref, vm_ref, mask_ref, o_ref, acc_ref, m_ref, *, online):
    nt = (((1,), (1,)), ((), ()))
    n_q = q_ref.shape[0] // ATT_TQ
    head_lanes = [slice(h * HEAD_PAD, (h + 1) * HEAD_PAD) for h in range(ATT_HEADS)]

    def step(h, sub, q, kblk, vblk, mask):
        s = lax.dot_general(q, kblk, nt, preferred_element_type=F32)
        if online:
            if mask is not None:
                s = jnp.where(mask > 0, s, -jnp.inf)
            m = m_ref[h, sub]
            m_new = jnp.maximum(m, jnp.max(s, axis=-1, keepdims=True))
            p = jnp.exp2(s - m_new).astype(BF16)
            acc_ref[h, sub] = (jnp.exp2(m - m_new) * acc_ref[h, sub]
                               + jnp.dot(p, vblk, preferred_element_type=F32))
            m_ref[h, sub] = m_new
        else:
            p = jnp.exp2(s).astype(BF16)
            if mask is not None:
                p = p * mask
            acc_ref[h, sub] += jnp.dot(p, vblk, preferred_element_type=F32)

    def q_tile(qi, _):
        q0 = pl.multiple_of(qi * ATT_TQ, ATT_TQ)
        rows = pl.ds(q0, ATT_TQ)
        qs = [q_ref[rows, hl] for hl in head_lanes]
        acc_ref[...] = jnp.zeros_like(acc_ref)
        if online:
            m_ref[...] = jnp.full(m_ref.shape, -1e30, F32)

        def body(kb, _):
            krows = pl.ds(pl.multiple_of(kb * ATT_TK, ATT_TK), ATT_TK)
            for h, hl in enumerate(head_lanes):
                step(h, slice(None), qs[h], k_ref[krows, hl], v_ref[krows, hl], None)
            return 0

        lax.fori_loop(0, qi * (ATT_TQ // ATT_TK), body, 0)

        for i in range(ATT_TQ // ATT_SUB):
            sub = slice(i * ATT_SUB, ATT_TQ)
            krows = pl.ds(q0 + i * ATT_SUB, ATT_SUB)
            for h, hl in enumerate(head_lanes):
                kblk, vblk = k_ref[krows, hl], v_ref[krows, hl]
                mask = mask_ref[sub, LANES + i * ATT_SUB:LANES + (i + 1) * ATT_SUB]
                if i == 0:
                    kblk = jnp.concatenate([km_ref[:, hl], kblk], axis=0)
                    vblk = jnp.concatenate([vm_ref[:, hl], vblk], axis=0)
                    mask = mask_ref[sub, :LANES + ATT_SUB]
                step(h, sub, qs[h][sub], kblk, vblk, mask)
        lane = lax.broadcasted_iota(jnp.int32, (ATT_TQ, HEAD_PAD), 1)
        for hp in range(ATT_HEADS // 2):
            even, odd = acc_ref[2 * hp], acc_ref[2 * hp + 1]
            o_even = even * (1.0 / even[:, V_DIM:V_DIM + 1])
            o_odd = odd * (1.0 / odd[:, 0:1])
            o_ref[rows, hp * HEAD_PAD:(hp + 1) * HEAD_PAD] = (
                jnp.where(lane < V_DIM, o_even, o_odd).astype(BF16))
        return 0

    lax.fori_loop(0, n_q, q_tile, 0)


def _attn_call(q3, k3, v3, k_meta, v_meta, mask, *, online):
    batch, seq, _ = q3.shape
    grid = (batch, N_HEADS // ATT_HEADS)
    seq_blk = pl.BlockSpec((None, seq, ATT_HEADS * HEAD_PAD), lambda b, hg: (b, 0, hg))
    meta_blk = pl.BlockSpec((LANES, ATT_HEADS * HEAD_PAD), lambda b, hg: (0, hg))
    return pl.pallas_call(
        functools.partial(_attn_kernel, online=online),
        out_shape=jax.ShapeDtypeStruct((batch, seq, N_HEADS * V_DIM), BF16),
        grid=grid,
        in_specs=[seq_blk, seq_blk, seq_blk, meta_blk, meta_blk, _const_spec(mask.shape)],
        out_specs=pl.BlockSpec((None, seq, ATT_HEADS * V_DIM), lambda b, hg: (b, 0, hg)),
        scratch_shapes=[pltpu.VMEM((ATT_HEADS, ATT_TQ, HEAD_PAD), F32),
                        pltpu.VMEM((ATT_HEADS, ATT_TQ, 1), F32)],
        compiler_params=pltpu.CompilerParams(dimension_semantics=("parallel", "parallel"),
                                             vmem_limit_bytes=VMEM_LIMIT),
        name="mla_attention_online" if online else "mla_attention",
    )(q3, k3, v3, k_meta, v_meta, mask)


FFN_TM = 512


def _ffn_kernel(x_ref, ms_ref, oa_ref, gatt_ref, wout_ref, gffn_ref, wg_ref, wu_ref, wd_ref,
                out_ref):
    ya = _rms(oa_ref[...].astype(F32), gatt_ref[...]).astype(BF16)
    mixed = jnp.concatenate([ms_ref[...], ya], axis=1)
    h1 = x_ref[...] + jnp.dot(mixed, wout_ref[...], preferred_element_type=F32)
    hn = _rms(h1, gffn_ref[...]).astype(BF16)
    g = jnp.dot(hn, wg_ref[...], preferred_element_type=F32)
    u = jnp.dot(hn, wu_ref[...], preferred_element_type=F32)
    a = (g * (1.0 / (1.0 + jnp.exp(-g))) * u).astype(BF16)
    out_ref[...] = h1 + jnp.dot(a, wd_ref[...], preferred_element_type=F32)


def _ffn_call(x2d, ms2d, oa2d, consts):
    n_rows = x2d.shape[0]
    tm = FFN_TM
    row = lambda i: (i, 0)
    once = pl.Buffered(1)
    in_specs = [pl.BlockSpec((tm, D_MODEL), row),
                pl.BlockSpec((tm, D_SSM), row),
                pl.BlockSpec((tm, N_HEADS * V_DIM), row)]
    in_specs += [pl.BlockSpec(c.shape, lambda i: (0, 0), pipeline_mode=once) for c in consts]
    return pl.pallas_call(
        _ffn_kernel,
        out_shape=jax.ShapeDtypeStruct((n_rows, D_MODEL), F32),
        grid=(n_rows // tm,), in_specs=in_specs,
        out_specs=pl.BlockSpec((tm, D_MODEL), row),
        compiler_params=pltpu.CompilerParams(dimension_semantics=("parallel",),
                                             vmem_limit_bytes=VMEM_LIMIT),
        name="outproj_ffn",
    )(x2d, ms2d, oa2d, *consts)


def _rope_tables(length):
    pos = jnp.arange(length, dtype=F32)
    inv_freq = 1.0 / (ROPE_BASE ** (jnp.arange(0, ROPE, 2, dtype=F32) / ROPE))
    ang = pos[:, None] * inv_freq[None, :]
    return jnp.cos(ang), jnp.sin(ang)


def _head_tables(gain, cos, sin):
    length = cos.shape[0]
    g_n, g_r = gain[:NOPE], gain[NOPE:]
    g_r_swapped = jnp.concatenate([g_r[HALF_ROPE:], g_r[:HALF_ROPE]])
    cos2 = jnp.concatenate([cos, cos], axis=1)
    sin2 = jnp.concatenate([sin, sin], axis=1)
    pad = jnp.zeros((length, HEAD_PAD - QK_DIM), F32)
    t1 = jnp.concatenate([jnp.broadcast_to(g_n, (length, NOPE)), g_r * cos2, pad], axis=1)
    t2 = jnp.concatenate([jnp.zeros((length, NOPE), F32), g_r_swapped * sin2, pad], axis=1)
    return t1, t2


def _rot_half_cols(w):
    return jnp.concatenate([-w[..., HALF_ROPE:], w[..., :HALF_ROPE]], axis=-1)


def _pad_cols(w, left, total):
    return jnp.pad(w, ((0, 0), (left, total - left - w.shape[1])))


def kernel(x, meta_tokens, mix_norm_g, w_in, ssm_a_re, ssm_a_im, ssm_log_dt, ssm_b_re, ssm_b_im,
           ssm_c_re, ssm_c_im, ssm_d, ssm_w_glu, ssm_b_glu, q_lora_norm_g, w_uq, kv_lora_norm_g,
           w_uk, w_uv, q_head_norm_g, k_head_norm_g, ssm_out_norm_g, att_out_norm_g, w_out,
           ffn_norm_g, w_gate, w_up, w_down):
    batch, seq, _ = x.shape
    depth = w_in.shape[0]
    assert depth == 1
    l = 0
    length = N_META + seq
    cos, sin = _rope_tables(length)

    wi = w_in[l]
    o_r = D_SSM + Q_LORA + KV_LORA
    w_r = wi[:, o_r:]
    win = jnp.concatenate([wi[:, :o_r], _pad_cols(w_r, NOPE, HEAD_PAD),
                           _pad_cols(_rot_half_cols(w_r), NOPE, HEAD_PAD)], axis=1).astype(BF16)
    wq3 = w_uq[l].reshape(Q_LORA, N_HEADS, QK_DIM)
    q1 = jnp.pad(wq3, ((0, 0), (0, 0), (0, HEAD_PAD - QK_DIM)))
    q2 = jnp.pad(_rot_half_cols(wq3[..., NOPE:]), ((0, 0), (0, 0), (NOPE, HEAD_PAD - QK_DIM)))
    wq = jnp.concatenate([q1.reshape(Q_LORA, QK_PAD), q2.reshape(Q_LORA, QK_PAD)], axis=1).astype(BF16)
    wk3 = jnp.pad(w_uk[l].reshape(KV_LORA, N_HEADS, NOPE), ((0, 0), (0, 0), (0, HEAD_PAD - NOPE)))
    wv4 = w_uv[l].reshape(KV_LORA, N_HEADS // 2, 2, V_DIM)
    zv = jnp.zeros_like(wv4[:, :, 0])
    wv = jnp.stack([jnp.concatenate([wv4[:, :, 0], zv], axis=-1),
                    jnp.concatenate([zv, wv4[:, :, 1]], axis=-1)], axis=2).reshape(KV_LORA, QK_PAD)
    ones_col = jnp.zeros((2, HEAD_PAD), F32).at[0, V_DIM].set(1.0).at[1, 0].set(1.0)
    vones = jnp.tile(ones_col.reshape(1, 2 * HEAD_PAD), (1, N_HEADS // 2))
    wkv = jnp.concatenate([wk3.reshape(KV_LORA, QK_PAD), wv], axis=1).astype(BF16)
    tabs_m = (_head_tables(q_head_norm_g[l], cos[:N_META], sin[:N_META])
              + _head_tables(k_head_norm_g[l], cos[:N_META], sin[:N_META]))
    tabs_f = (_head_tables(q_head_norm_g[l], cos[N_META:], sin[N_META:])
              + _head_tables(k_head_norm_g[l], cos[N_META:], sin[N_META:]))
    proj_consts = (mix_norm_g[l][None], win, q_lora_norm_g[l][None], wq,
                   kv_lora_norm_g[l][None], wkv, vones)

    tm = 512
    ffn_f32 = (w_out, w_gate, w_up, w_down)
    u2, q2d, k2d, v2d, wout_b, wg_b, wu_b, wd_b = _proj_call(
        x.reshape(batch * seq, D_MODEL), tm, tabs_f, seq // tm, proj_consts, ffn_f32)
    u_meta, _, k_meta, v_meta = _proj_call(meta_tokens, N_META, tabs_m, 1, proj_consts)

    dt = jnp.exp(ssm_log_dt[l])[:, None]
    lr, li = ssm_a_re[l], ssm_a_im[l]
    mag = jnp.exp(lr * dt)
    ar = mag * jnp.cos(li * dt)
    ai = mag * jnp.sin(li * dt)
    den = lr * lr + li * li
    fr = ((ar - 1.0) * lr + ai * li) / den
    fi = (ai * lr - (ar - 1.0) * li) / den
    br, bi = ssm_b_re[l], ssm_b_im[l]
    bbr = fr[..., None] * br - fi[..., None] * bi
    bbi = fr[..., None] * bi + fi[..., None] * br
    half_groups = N_GROUPS // 2

    def block_diag_halves(blocks):
        _, r, c = blocks.shape
        rows = blocks.reshape(2, half_groups * r, c)
        tiled = jnp.tile(rows, (1, 1, half_groups))
        row_g = jnp.arange(half_groups * r) // r
        col_g = jnp.arange(half_groups * c) // c
        return jnp.where(row_g[:, None] == col_g[None, :], tiled, 0.0).astype(BF16)

    def in_map(bb):
        return block_diag_halves(jnp.swapaxes(bb, 1, 2))

    def out_map(cc):
        return block_diag_halves(jnp.swapaxes(cc, 1, 2))

    ar_rows = jnp.broadcast_to(ar.reshape(N_SLABS, 1, LANES), (N_SLABS, 2 * batch, LANES))
    ai_flat = ai.reshape(N_SLABS, 1, LANES)
    ai_rows = jnp.concatenate([jnp.broadcast_to(-ai_flat, (N_SLABS, batch, LANES)),
                               jnp.broadcast_to(ai_flat, (N_SLABS, batch, LANES))], axis=1)
    s5_consts = (in_map(bbr), in_map(bbi), out_map(ssm_c_re[l]), out_map(ssm_c_im[l]),
                 ar_rows, ai_rows, ssm_d[l][None], ssm_w_glu[l].astype(BF16),
                 ssm_b_glu[l][None], ssm_out_norm_g[l][None])
    mixed_ssm = _s5_call(u2.reshape(batch, seq, D_SSM), u_meta, s5_consts)

    pad_rows = ((0, LANES - N_META), (0, 0))
    chunk_of = jnp.arange(ATT_TQ) // CHUNK
    causal = (chunk_of[None, :] <= chunk_of[:, None]).astype(BF16)
    mask = jnp.concatenate([jnp.ones((ATT_TQ, LANES), BF16), causal], axis=1)
    score_bound = (LOG2_E * math.sqrt(QK_DIM) * jnp.max(jnp.abs(q_head_norm_g[l]))
                   * jnp.max(jnp.abs(k_head_norm_g[l])))
    attn_args = (q2d.reshape(batch, seq, QK_PAD), k2d.reshape(batch, seq, QK_PAD),
                 v2d.reshape(batch, seq, QK_PAD),
                 jnp.pad(k_meta, pad_rows), jnp.pad(v_meta, pad_rows), mask)
    y_att = lax.cond(score_bound <= MAX_UNSHIFTED_LOG2_SCORE,
                     functools.partial(_attn_call, online=False),
                     functools.partial(_attn_call, online=True), *attn_args)

    ffn_consts = (att_out_norm_g[l][None], wout_b, ffn_norm_g[l][None],
                  wg_b, wu_b, wd_b)
    out = _ffn_call(x.reshape(batch * seq, D_MODEL), mixed_ssm.reshape(batch * seq, D_SSM),
                    y_att.reshape(batch * seq, N_HEADS * V_DIM), ffn_consts)
    return out.reshape(batch, seq, D_MODEL)
```

```python
import functools
import math

import jax
import jax.numpy as jnp
import numpy as np
from jax import lax
from jax.experimental import pallas as pl
from jax.experimental.pallas import tpu as pltpu

F32 = jnp.float32
BF16 = jnp.bfloat16

D_MODEL = 1024
N_META = 16
CHUNK = 64
D_SSM = 512
SSM_GROUP = 16
N_GROUPS = D_SSM // SSM_GROUP
SSM_STATE = 64
N_HEADS = 8
V_DIM = 64
NOPE = 64
ROPE = 32
HALF_ROPE = ROPE // 2
QK_DIM = NOPE + ROPE
Q_LORA = 256
KV_LORA = 128
D_FF = 2816
ROPE_BASE = 10000.0
EPS = 1e-6
LOG2_E = math.log2(math.e)
MAX_UNSHIFTED_LOG2_SCORE = 40.0

LANES = 128
SUBLANES = 8
HEAD_PAD = LANES
QK_PAD = N_HEADS * HEAD_PAD
N_STATE_COLS = N_GROUPS * SSM_STATE
N_SLABS = N_STATE_COLS // LANES
S5_CHUNK = 128
S5_PITCH = S5_CHUNK + SUBLANES
VMEM_LIMIT = 56 * 1024 * 1024


def _rms(x, g):
    return x * lax.rsqrt(jnp.mean(x * x, axis=-1, keepdims=True) + EPS) * g


def _proj_kernel(x_ref, gmix_ref, win_ref, gq_ref, wq_ref, gkv_ref, wkv_ref, vones_ref,
                 hgq_ref, hgk_ref, ctab_ref, stab_ref, *rest):
    n_cast = (len(rest) - 4) // 2
    u_ref, q_ref, k_ref, v_ref = rest[n_cast:n_cast + 4]
    for src_ref, dst_ref in zip(rest[:n_cast], rest[n_cast + 4:]):
        dst_ref[...] = src_ref[...].astype(BF16)

    x = x_ref[...]
    xn = _rms(x, gmix_ref[...]).astype(BF16)
    p = jnp.dot(xn, win_ref[...], preferred_element_type=F32)
    u_ref[...] = p[:, :D_SSM]

    cq = p[:, D_SSM:D_SSM + Q_LORA]
    cqn = _rms(cq, gq_ref[...]).astype(BF16)
    q12 = jnp.dot(cqn, wq_ref[...], preferred_element_type=F32)
    ctab, stab = ctab_ref[...], stab_ref[...]
    t1q, t2q = ctab * hgq_ref[0:1, :], stab * hgq_ref[1:2, :]
    scale = QK_DIM ** -0.5 * LOG2_E
    for h in range(N_HEADS):
        q1 = q12[:, h * HEAD_PAD:(h + 1) * HEAD_PAD]
        q2 = q12[:, QK_PAD + h * HEAD_PAD:QK_PAD + (h + 1) * HEAD_PAD]
        r = lax.rsqrt(jnp.sum(q1 * q1, axis=-1, keepdims=True) * (1.0 / QK_DIM) + EPS)
        qh = (q1 * t1q + q2 * t2q) * (r * scale)
        q_ref[:, h * HEAD_PAD:(h + 1) * HEAD_PAD] = qh.astype(BF16)

    c0 = D_SSM + Q_LORA
    ckv = p[:, c0:c0 + KV_LORA]
    ckvn = _rms(ckv, gkv_ref[...]).astype(BF16)
    kv = jnp.dot(ckvn, wkv_ref[...], preferred_element_type=F32)
    v_ref[...] = (kv[:, QK_PAD:] + vones_ref[...]).astype(BF16)
    kr = p[:, c0 + KV_LORA:c0 + KV_LORA + HEAD_PAD]
    kr_rot = p[:, c0 + KV_LORA + HEAD_PAD:c0 + KV_LORA + 2 * HEAD_PAD]
    ss_r = jnp.sum(kr * kr, axis=-1, keepdims=True)
    t1k = ctab * hgk_ref[0:1, :]
    kr_part = kr_rot * (stab * hgk_ref[1:2, :])
    for h in range(N_HEADS):
        kn = kv[:, h * HEAD_PAD:(h + 1) * HEAD_PAD]
        ss = jnp.sum(kn * kn, axis=-1, keepdims=True) + ss_r
        r = lax.rsqrt(ss * (1.0 / QK_DIM) + EPS)
        kh = ((kn + kr) * t1k + kr_part) * r
        k_ref[:, h * HEAD_PAD:(h + 1) * HEAD_PAD] = kh.astype(BF16)


def _const_spec(shape):
    nd = len(shape)
    return pl.BlockSpec(shape, lambda *_: (0,) * nd)


def _proj_call(x2d, tm, tabs, n_tab_blocks, consts, to_bf16=()):
    n_rows = x2d.shape[0]
    n_steps = n_rows // tm
    row = lambda i: (i, 0)
    tab = lambda i: (i % n_tab_blocks, 0)
    cast_in, cast_out = [], []
    for w in to_bf16:
        _, w_rows, w_cols = w.shape
        rep = next(r for r in (1, 2, 4, 8) if (w_rows * r) % (16 * n_steps) == 0)
        blk_rows = w_rows * rep // n_steps
        cast_in.append(pl.BlockSpec((None, blk_rows, w_cols),
                                    functools.partial(lambda rep, i: (0, i // rep, 0), rep)))
        cast_out.append(pl.BlockSpec((blk_rows, w_cols),
                                     functools.partial(lambda rep, i: (i // rep, 0), rep)))
    in_specs = ([pl.BlockSpec((tm, D_MODEL), row)] + [_const_spec(c.shape) for c in consts]
                + [pl.BlockSpec((tm, HEAD_PAD), tab)] * len(tabs) + cast_in)
    out_shape = (
        jax.ShapeDtypeStruct((n_rows, D_SSM), F32),
        jax.ShapeDtypeStruct((n_rows, QK_PAD), BF16),
        jax.ShapeDtypeStruct((n_rows, QK_PAD), BF16),
        jax.ShapeDtypeStruct((n_rows, QK_PAD), BF16),
    ) + tuple(jax.ShapeDtypeStruct(w.shape[1:], BF16) for w in to_bf16)
    out_specs = (
        pl.BlockSpec((tm, D_SSM), row),
        pl.BlockSpec((tm, QK_PAD), row),
        pl.BlockSpec((tm, QK_PAD), row),
        pl.BlockSpec((tm, QK_PAD), row),
    ) + tuple(cast_out)
    return pl.pallas_call(
        _proj_kernel, out_shape=out_shape, grid=(n_steps,), in_specs=in_specs,
        out_specs=out_specs,
        compiler_params=pltpu.CompilerParams(dimension_semantics=("arbitrary",),
                                             vmem_limit_bytes=VMEM_LIMIT),
        name="proj_mla",
    )(x2d, *consts, *tabs, *to_bf16)


def _s5_kernel(u_ref, um_ref, bre_ref, bim_ref, cre_ref, cim_ref, ar_ref, ai_ref,
               d_ref, wglu_ref, bglu_ref, g_ref, o_ref, xs_ref, h_ref, *, batch):
    j = pl.program_id(0)
    slabs_per_half = N_SLABS // 2

    def project_in(ub, rows):
        for kh in range(2):
            lhs = ub[:, kh * 256:(kh + 1) * 256]
            xre = jnp.dot(lhs, bre_ref[kh], preferred_element_type=F32)
            xim = jnp.dot(lhs, bim_ref[kh], preferred_element_type=F32)
            for cl in range(slabs_per_half):
                c = kh * slabs_per_half + cl
                for b in range(batch):
                    xs_ref[c, pl.ds(b * S5_PITCH, rows), :] = (
                        xre[b * rows:(b + 1) * rows, cl * LANES:(cl + 1) * LANES])
                    xs_ref[c, pl.ds((batch + b) * S5_PITCH, rows), :] = (
                        xim[b * rows:(b + 1) * rows, cl * LANES:(cl + 1) * LANES])

    def scan(n_steps):
        def body(t, hs):
            new = []
            for c in range(N_SLABS):
                rows = pl.ds(t, 2 * batch, stride=S5_PITCH)
                x8 = xs_ref[c, rows, :]
                h = hs[c]
                hn = ar_ref[c] * h + ai_ref[c] * pltpu.roll(h, batch, 0) + x8
                xs_ref[c, rows, :] = hn
                new.append(hn)
            return tuple(new)

        hs = tuple(h_ref[c] for c in range(N_SLABS))
        hs = lax.fori_loop(0, n_steps, body, hs, unroll=4)
        for c in range(N_SLABS):
            h_ref[c] = hs[c]

    @pl.when(j == 0)
    def _():
        h_ref[...] = jnp.zeros_like(h_ref)
        um = um_ref[...].astype(BF16)
        project_in(jnp.concatenate([um] * batch, axis=0), N_META)
        scan(N_META)

    uf = u_ref[...].reshape(batch * S5_CHUNK, D_SSM)
    project_in(uf.astype(BF16), S5_CHUNK)
    scan(S5_CHUNK)

    ys = []
    for nh in range(2):
        def gather(plane0):
            return jnp.concatenate(
                [jnp.concatenate(
                    [xs_ref[nh * slabs_per_half + cl, pl.ds((plane0 + b) * S5_PITCH, S5_CHUNK), :]
                     for cl in range(slabs_per_half)], axis=1)
                 for b in range(batch)], axis=0).astype(BF16)
        yre = jnp.dot(gather(0), cre_ref[nh], preferred_element_type=F32)
        yim = jnp.dot(gather(batch), cim_ref[nh], preferred_element_type=F32)
        ys.append(yre - yim)
    y = jnp.concatenate(ys, axis=1) + d_ref[...] * uf
    z = 0.5 * y * (1.0 + jnp.tanh(math.sqrt(2.0 / math.pi) * (y + 0.044715 * (y * y * y))))
    gate = jnp.dot(z.astype(BF16), wglu_ref[...], preferred_element_type=F32) + bglu_ref[...]
    out = z * (1.0 / (1.0 + jnp.exp(-gate)))
    o_ref[...] = _rms(out, g_ref[...]).astype(BF16).reshape(batch, S5_CHUNK, D_SSM)


def _s5_call(u3, u_meta, consts):
    batch, seq, _ = u3.shape
    assert 2 * batch == SUBLANES and seq % S5_CHUNK == 0
    grid = (seq // S5_CHUNK,)
    in_specs = [pl.BlockSpec((batch, S5_CHUNK, D_SSM), lambda j: (0, j, 0)),
                _const_spec(u_meta.shape)] + [_const_spec(c.shape) for c in consts]
    return pl.pallas_call(
        functools.partial(_s5_kernel, batch=batch),
        out_shape=jax.ShapeDtypeStruct((batch, seq, D_SSM), BF16),
        grid=grid, in_specs=in_specs,
        out_specs=pl.BlockSpec((batch, S5_CHUNK, D_SSM), lambda j: (0, j, 0)),
        scratch_shapes=[pltpu.VMEM((N_SLABS, 2 * batch * S5_PITCH, LANES), F32),
                        pltpu.VMEM((N_SLABS, 2 * batch, LANES), F32)],
        compiler_params=pltpu.CompilerParams(dimension_semantics=("arbitrary",),
                                             vmem_limit_bytes=VMEM_LIMIT),
        name="s5_mixer",
    )(u3, u_meta, *consts)


ATT_TQ = 1024
ATT_TK = 1024
ATT_SUB = 256
ATT_HEADS = 4


def _attn_kernel(q_ref, k_ref, v_ref, km_ref, vm_ref, mask_ref, o_ref, acc_ref, m_ref, *, online):
    nt = (((1,), (1,)), ((), ()))
    n_q = q_ref.shape[0] // ATT_TQ
    head_lanes = [slice(h * HEAD_PAD, (h + 1) * HEAD_PAD) for h in range(ATT_HEADS)]

    def step(h, sub, q, kblk, vblk, mask):
        s = lax.dot_general(q, kblk, nt, preferred_element_type=F32)
        if online:
            if mask is not None:
                s = jnp.where(mask > 0, s, -jnp.inf)
            m = m_ref[h, sub]
            m_new = jnp.maximum(m, jnp.max(s, axis=-1, keepdims=True))
            p = jnp.exp2(s - m_new).astype(BF16)
            acc_ref[h, sub] = (jnp.exp2(m - m_new) * acc_ref[h, sub]
                               + jnp.dot(p, vblk, preferred_element_type=F32))
            m_ref[h, sub] = m_new
        else:
            p = jnp.exp2(s).astype(BF16)
            if mask is not None:
                p = p * mask
            acc_ref[h, sub] += jnp.dot(p, vblk, preferred_element_type=F32)

    def q_tile(qi, _):
        q0 = pl.multiple_of(qi * ATT_TQ, ATT_TQ)
        rows = pl.ds(q0, ATT_TQ)
        qs = [q_ref[rows, hl] for hl in head_lanes]
        acc_ref[...] = jnp.zeros_like(acc_ref)
        if online:
            m_ref[...] = jnp.full(m_ref.shape, -1e30, F32)

        def body(kb, _):
            krows = pl.ds(pl.multiple_of(kb * ATT_TK, ATT_TK), ATT_TK)
            for h, hl in enumerate(head_lanes):
                step(h, slice(None), qs[h], k_ref[krows, hl], v_ref[krows, hl], None)
            return 0

        lax.fori_loop(0, qi * (ATT_TQ // ATT_TK), body, 0)

        for i in range(ATT_TQ // ATT_SUB):
            sub = slice(i * ATT_SUB, ATT_TQ)
            krows = pl.ds(q0 + i * ATT_SUB, ATT_SUB)
            for h, hl in enumerate(head_lanes):
                kblk, vblk = k_ref[krows, hl], v_ref[krows, hl]
                mask = mask_ref[sub, LANES + i * ATT_SUB:LANES + (i + 1) * ATT_SUB]
                if i == 0:
                    kblk = jnp.concatenate([km_ref[:, hl], kblk], axis=0)
                    vblk = jnp.concatenate([vm_ref[:, hl], vblk], axis=0)
                    mask = mask_ref[sub, :LANES + ATT_SUB]
                step(h, sub, qs[h][sub], kblk, vblk, mask)
        lane = lax.broadcasted_iota(jnp.int32, (ATT_TQ, HEAD_PAD), 1)
        for hp in range(ATT_HEADS // 2):
            even, odd = acc_ref[2 * hp], acc_ref[2 * hp + 1]
            o_even = even * (1.0 / even[:, V_DIM:V_DIM + 1])
            o_odd = odd * (1.0 / odd[:, 0:1])
            o_ref[rows, hp * HEAD_PAD:(hp + 1) * HEAD_PAD] = (
                jnp.where(lane < V_DIM, o_even, o_odd).astype(BF16))
        return 0

    lax.fori_loop(0, n_q, q_tile, 0)


def _attn_call(q3, k3, v3, k_meta, v_meta, mask, *, online):
    batch, seq, _ = q3.shape
    grid = (batch, N_HEADS // ATT_HEADS)
    seq_blk = pl.BlockSpec((None, seq, ATT_HEADS * HEAD_PAD), lambda b, hg: (b, 0, hg))
    meta_blk = pl.BlockSpec((LANES, ATT_HEADS * HEAD_PAD), lambda b, hg: (0, hg))
    return pl.pallas_call(
        functools.partial(_attn_kernel, online=online),
        out_shape=jax.ShapeDtypeStruct((batch, seq, N_HEADS * V_DIM), BF16),
        grid=grid,
        in_specs=[seq_blk, seq_blk, seq_blk, meta_blk, meta_blk, _const_spec(mask.shape)],
        out_specs=pl.BlockSpec((None, seq, ATT_HEADS * V_DIM), lambda b, hg: (b, 0, hg)),
        scratch_shapes=[pltpu.VMEM((ATT_HEADS, ATT_TQ, HEAD_PAD), F32),
                        pltpu.VMEM((ATT_HEADS, ATT_TQ, 1), F32)],
        compiler_params=pltpu.CompilerParams(dimension_semantics=("parallel", "parallel"),
                                             vmem_limit_bytes=VMEM_LIMIT),
        name="mla_attention_online" if online else "mla_attention",
    )(q3, k3, v3, k_meta, v_meta, mask)


FFN_TM = 512


def _ffn_kernel(x_ref, ms_ref, oa_ref, gatt_ref, wout_ref, gffn_ref, wg_ref, wu_ref, wd_ref,
                out_ref):
    ya = _rms(oa_ref[...].astype(F32), gatt_ref[...]).astype(BF16)
    mixed = jnp.concatenate([ms_ref[...], ya], axis=1)
    h1 = x_ref[...] + jnp.dot(mixed, wout_ref[...], preferred_element_type=F32)
    hn = _rms(h1, gffn_ref[...]).astype(BF16)
    g = jnp.dot(hn, wg_ref[...], preferred_element_type=F32)
    u = jnp.dot(hn, wu_ref[...], preferred_element_type=F32)
    a = (g * (1.0 / (1.0 + jnp.exp(-g))) * u).astype(BF16)
    out_ref[...] = h1 + jnp.dot(a, wd_ref[...], preferred_element_type=F32)


def _ffn_call(x2d, ms2d, oa2d, consts):
    n_rows = x2d.shape[0]
    tm = FFN_TM
    row = lambda i: (i, 0)
    once = pl.Buffered(1)
    in_specs = [pl.BlockSpec((tm, D_MODEL), row),
                pl.BlockSpec((tm, D_SSM), row),
                pl.BlockSpec((tm, N_HEADS * V_DIM), row)]
    in_specs += [pl.BlockSpec(c.shape, lambda i: (0, 0), pipeline_mode=once) for c in consts]
    return pl.pallas_call(
        _ffn_kernel,
        out_shape=jax.ShapeDtypeStruct((n_rows, D_MODEL), F32),
        grid=(n_rows // tm,), in_specs=in_specs,
        out_specs=pl.BlockSpec((tm, D_MODEL), row),
        compiler_params=pltpu.CompilerParams(dimension_semantics=("parallel",),
                                             vmem_limit_bytes=VMEM_LIMIT),
        name="outproj_ffn",
    )(x2d, ms2d, oa2d, *consts)


def _rope_tables(first, count):
    pos = np.arange(first, first + count, dtype=np.float64)
    inv_freq = 1.0 / (ROPE_BASE ** (np.arange(0, ROPE, 2, dtype=np.float64) / ROPE))
    ang = pos[:, None] * inv_freq[None, :]
    cos, sin = np.cos(ang), np.sin(ang)
    ctab = np.zeros((count, HEAD_PAD), np.float32)
    stab = np.zeros((count, HEAD_PAD), np.float32)
    ctab[:, :NOPE] = 1.0
    ctab[:, NOPE:QK_DIM] = np.concatenate([cos, cos], axis=1)
    stab[:, NOPE:QK_DIM] = np.concatenate([sin, sin], axis=1)
    return jnp.asarray(ctab), jnp.asarray(stab)


def _head_gains(gain):
    g_r = gain[NOPE:]
    g_r_swapped = jnp.concatenate([g_r[HALF_ROPE:], g_r[:HALF_ROPE]])
    pad = jnp.zeros((HEAD_PAD - QK_DIM,), F32)
    return jnp.stack([jnp.concatenate([gain, pad]),
                      jnp.concatenate([jnp.zeros((NOPE,), F32), g_r_swapped, pad])])


def _rot_half_cols(w):
    return jnp.concatenate([-w[..., HALF_ROPE:], w[..., :HALF_ROPE]], axis=-1)


def _pad_cols(w, left, total):
    return jnp.pad(w, ((0, 0), (left, total - left - w.shape[1])))


def kernel(x, meta_tokens, mix_norm_g, w_in, ssm_a_re, ssm_a_im, ssm_log_dt, ssm_b_re, ssm_b_im,
           ssm_c_re, ssm_c_im, ssm_d, ssm_w_glu, ssm_b_glu, q_lora_norm_g, w_uq, kv_lora_norm_g,
           w_uk, w_uv, q_head_norm_g, k_head_norm_g, ssm_out_norm_g, att_out_norm_g, w_out,
           ffn_norm_g, w_gate, w_up, w_down):
    batch, seq, _ = x.shape
    depth = w_in.shape[0]
    assert depth == 1
    l = 0

    wi = w_in[l]
    o_r = D_SSM + Q_LORA + KV_LORA
    w_r = wi[:, o_r:]
    win = jnp.concatenate([wi[:, :o_r], _pad_cols(w_r, NOPE, HEAD_PAD),
                           _pad_cols(_rot_half_cols(w_r), NOPE, HEAD_PAD)], axis=1).astype(BF16)
    wq3 = w_uq[l].reshape(Q_LORA, N_HEADS, QK_DIM)
    q1 = jnp.pad(wq3, ((0, 0), (0, 0), (0, HEAD_PAD - QK_DIM)))
    q2 = jnp.pad(_rot_half_cols(wq3[..., NOPE:]), ((0, 0), (0, 0), (NOPE, HEAD_PAD - QK_DIM)))
    wq = jnp.concatenate([q1.reshape(Q_LORA, QK_PAD), q2.reshape(Q_LORA, QK_PAD)], axis=1).astype(BF16)
    wk3 = jnp.pad(w_uk[l].reshape(KV_LORA, N_HEADS, NOPE), ((0, 0), (0, 0), (0, HEAD_PAD - NOPE)))
    wv4 = w_uv[l].reshape(KV_LORA, N_HEADS // 2, 2, V_DIM)
    zv = jnp.zeros_like(wv4[:, :, 0])
    wv = jnp.stack([jnp.concatenate([wv4[:, :, 0], zv], axis=-1),
                    jnp.concatenate([zv, wv4[:, :, 1]], axis=-1)], axis=2).reshape(KV_LORA, QK_PAD)
    ones_col = jnp.zeros((2, HEAD_PAD), F32).at[0, V_DIM].set(1.0).at[1, 0].set(1.0)
    vones = jnp.tile(ones_col.reshape(1, 2 * HEAD_PAD), (1, N_HEADS // 2))
    wkv = jnp.concatenate([wk3.reshape(KV_LORA, QK_PAD), wv], axis=1).astype(BF16)
    tabs_m = _rope_tables(0, N_META)
    tabs_f = _rope_tables(N_META, seq)
    proj_consts = (mix_norm_g[l][None], win, q_lora_norm_g[l][None], wq,
                   kv_lora_norm_g[l][None], wkv, vones,
                   _head_gains(q_head_norm_g[l]), _head_gains(k_head_norm_g[l]))

    tm = 512
    ffn_f32 = (w_out, w_gate, w_up, w_down)
    u2, q2d, k2d, v2d, wout_b, wg_b, wu_b, wd_b = _proj_call(
        x.reshape(batch * seq, D_MODEL), tm, tabs_f, seq // tm, proj_consts, ffn_f32)
    u_meta, _, k_meta, v_meta = _proj_call(meta_tokens, N_META, tabs_m, 1, proj_consts)

    dt = jnp.exp(ssm_log_dt[l])[:, None]
    lr, li = ssm_a_re[l], ssm_a_im[l]
    mag = jnp.exp(lr * dt)
    ar = mag * jnp.cos(li * dt)
    ai = mag * jnp.sin(li * dt)
    den = lr * lr + li * li
    fr = ((ar - 1.0) * lr + ai * li) / den
    fi = (ai * lr - (ar - 1.0) * li) / den
    br, bi = ssm_b_re[l], ssm_b_im[l]
    bbr = fr[..., None] * br - fi[..., None] * bi
    bbi = fr[..., None] * bi + fi[..., None] * br
    half_groups = N_GROUPS // 2

    def block_diag_halves(blocks):
        _, r, c = blocks.shape
        rows = blocks.reshape(2, half_groups * r, c)
        tiled = jnp.tile(rows, (1, 1, half_groups))
        row_g = jnp.arange(half_groups * r) // r
        col_g = jnp.arange(half_groups * c) // c
        return jnp.where(row_g[:, None] == col_g[None, :], tiled, 0.0).astype(BF16)

    def in_map(bb):
        return block_diag_halves(jnp.swapaxes(bb, 1, 2))

    def out_map(cc):
        return block_diag_halves(jnp.swapaxes(cc, 1, 2))

    ar_rows = jnp.broadcast_to(ar.reshape(N_SLABS, 1, LANES), (N_SLABS, 2 * batch, LANES))
    ai_flat = ai.reshape(N_SLABS, 1, LANES)
    ai_rows = jnp.concatenate([jnp.broadcast_to(-ai_flat, (N_SLABS, batch, LANES)),
                               jnp.broadcast_to(ai_flat, (N_SLABS, batch, LANES))], axis=1)
    s5_consts = (in_map(bbr), in_map(bbi), out_map(ssm_c_re[l]), out_map(ssm_c_im[l]),
                 ar_rows, ai_rows, ssm_d[l][None], ssm_w_glu[l].astype(BF16),
                 ssm_b_glu[l][None], ssm_out_norm_g[l][None])
    mixed_ssm = _s5_call(u2.reshape(batch, seq, D_SSM), u_meta, s5_consts)

    pad_rows = ((0, LANES - N_META), (0, 0))
    chunk_of = jnp.arange(ATT_TQ) // CHUNK
    causal = (chunk_of[None, :] <= chunk_of[:, None]).astype(BF16)
    mask = jnp.concatenate([jnp.ones((ATT_TQ, LANES), BF16), causal], axis=1)
    score_bound = (LOG2_E * math.sqrt(QK_DIM) * jnp.max(jnp.abs(q_head_norm_g[l]))
                   * jnp.max(jnp.abs(k_head_norm_g[l])))
    attn_args = (q2d.reshape(batch, seq, QK_PAD), k2d.reshape(batch, seq, QK_PAD),
                 v2d.reshape(batch, seq, QK_PAD),
                 jnp.pad(k_meta, pad_rows), jnp.pad(v_meta, pad_rows), mask)
    y_att = lax.cond(score_bound <= MAX_UNSHIFTED_LOG2_SCORE,
                     functools.partial(_attn_call, online=False),
                     functools.partial(_attn_call, online=True), *attn_args)

    ffn_consts = (att_out_norm_g[l][None], wout_b, ffn_norm_g[l][None],
                  wg_b, wu_b, wd_b)
    out = _ffn_call(x.reshape(batch * seq, D_MODEL), mixed_ssm.reshape(batch * seq, D_SSM),
                    y_att.reshape(batch * seq, N_HEADS * V_DIM), ffn_consts)
    return out.reshape(batch, seq, D_MODEL)
```

```python
import functools
import math

import jax
import jax.numpy as jnp
import numpy as np
from jax import lax
from jax.experimental import pallas as pl
from jax.experimental.pallas import tpu as pltpu

F32 = jnp.float32
BF16 = jnp.bfloat16

D_MODEL = 1024
N_META = 16
CHUNK = 64
D_SSM = 512
SSM_GROUP = 16
N_GROUPS = D_SSM // SSM_GROUP
SSM_STATE = 64
N_HEADS = 8
V_DIM = 64
NOPE = 64
ROPE = 32
HALF_ROPE = ROPE // 2
QK_DIM = NOPE + ROPE
Q_LORA = 256
KV_LORA = 128
D_FF = 2816
ROPE_BASE = 10000.0
EPS = 1e-6
LOG2_E = math.log2(math.e)
MAX_UNSHIFTED_LOG2_SCORE = 40.0

LANES = 128
SUBLANES = 8
HEAD_PAD = LANES
QK_PAD = N_HEADS * HEAD_PAD
N_STATE_COLS = N_GROUPS * SSM_STATE
N_SLABS = N_STATE_COLS // LANES
S5_CHUNK = 256
S5_PITCH = S5_CHUNK + SUBLANES
VMEM_LIMIT = 56 * 1024 * 1024


def _rms(x, g):
    return x * lax.rsqrt(jnp.mean(x * x, axis=-1, keepdims=True) + EPS) * g


def _proj_kernel(x_ref, gmix_ref, win_ref, gq_ref, wq_ref, gkv_ref, wkv_ref, vones_ref,
                 hgq_ref, hgk_ref, ctab_ref, stab_ref, *rest):
    n_cast = (len(rest) - 4) // 2
    u_ref, q_ref, k_ref, v_ref = rest[n_cast:n_cast + 4]
    for src_ref, dst_ref in zip(rest[:n_cast], rest[n_cast + 4:]):
        dst_ref[...] = src_ref[...].astype(BF16)

    x = x_ref[...]
    xn = _rms(x, gmix_ref[...]).astype(BF16)
    p = jnp.dot(xn, win_ref[...], preferred_element_type=F32)
    u_ref[...] = p[:, :D_SSM]

    cq = p[:, D_SSM:D_SSM + Q_LORA]
    cqn = _rms(cq, gq_ref[...]).astype(BF16)
    q12 = jnp.dot(cqn, wq_ref[...], preferred_element_type=F32)
    ctab, stab = ctab_ref[...], stab_ref[...]
    t1q, t2q = ctab * hgq_ref[0:1, :], stab * hgq_ref[1:2, :]
    scale = QK_DIM ** -0.5 * LOG2_E
    for h in range(N_HEADS):
        q1 = q12[:, h * HEAD_PAD:(h + 1) * HEAD_PAD]
        q2 = q12[:, QK_PAD + h * HEAD_PAD:QK_PAD + (h + 1) * HEAD_PAD]
        r = lax.rsqrt(jnp.sum(q1 * q1, axis=-1, keepdims=True) * (1.0 / QK_DIM) + EPS)
        qh = (q1 * t1q + q2 * t2q) * (r * scale)
        q_ref[:, h * HEAD_PAD:(h + 1) * HEAD_PAD] = qh.astype(BF16)

    c0 = D_SSM + Q_LORA
    ckv = p[:, c0:c0 + KV_LORA]
    ckvn = _rms(ckv, gkv_ref[...]).astype(BF16)
    kv = jnp.dot(ckvn, wkv_ref[...], preferred_element_type=F32)
    v_ref[...] = (kv[:, QK_PAD:] + vones_ref[...]).astype(BF16)
    kr = p[:, c0 + KV_LORA:c0 + KV_LORA + HEAD_PAD]
    kr_rot = p[:, c0 + KV_LORA + HEAD_PAD:c0 + KV_LORA + 2 * HEAD_PAD]
    ss_r = jnp.sum(kr * kr, axis=-1, keepdims=True)
    t1k = ctab * hgk_ref[0:1, :]
    kr_part = kr_rot * (stab * hgk_ref[1:2, :])
    for h in range(N_HEADS):
        kn = kv[:, h * HEAD_PAD:(h + 1) * HEAD_PAD]
        ss = jnp.sum(kn * kn, axis=-1, keepdims=True) + ss_r
        r = lax.rsqrt(ss * (1.0 / QK_DIM) + EPS)
        kh = ((kn + kr) * t1k + kr_part) * r
        k_ref[:, h * HEAD_PAD:(h + 1) * HEAD_PAD] = kh.astype(BF16)


def _const_spec(shape):
    nd = len(shape)
    return pl.BlockSpec(shape, lambda *_: (0,) * nd)


def _proj_call(x2d, tm, tabs, n_tab_blocks, consts, to_bf16=()):
    n_rows = x2d.shape[0]
    n_steps = n_rows // tm
    row = lambda i: (i, 0)
    tab = lambda i: (i % n_tab_blocks, 0)
    cast_in, cast_out = [], []
    for w in to_bf16:
        _, w_rows, w_cols = w.shape
        rep = next(r for r in (1, 2, 4, 8) if (w_rows * r) % (16 * n_steps) == 0)
        blk_rows = w_rows * rep // n_steps
        cast_in.append(pl.BlockSpec((None, blk_rows, w_cols),
                                    functools.partial(lambda rep, i: (0, i // rep, 0), rep)))
        cast_out.append(pl.BlockSpec((blk_rows, w_cols),
                                     functools.partial(lambda rep, i: (i // rep, 0), rep)))
    in_specs = ([pl.BlockSpec((tm, D_MODEL), row)] + [_const_spec(c.shape) for c in consts]
                + [pl.BlockSpec((tm, HEAD_PAD), tab)] * len(tabs) + cast_in)
    out_shape = (
        jax.ShapeDtypeStruct((n_rows, D_SSM), F32),
        jax.ShapeDtypeStruct((n_rows, QK_PAD), BF16),
        jax.ShapeDtypeStruct((n_rows, QK_PAD), BF16),
        jax.ShapeDtypeStruct((n_rows, QK_PAD), BF16),
    ) + tuple(jax.ShapeDtypeStruct(w.shape[1:], BF16) for w in to_bf16)
    out_specs = (
        pl.BlockSpec((tm, D_SSM), row),
        pl.BlockSpec((tm, QK_PAD), row),
        pl.BlockSpec((tm, QK_PAD), row),
        pl.BlockSpec((tm, QK_PAD), row),
    ) + tuple(cast_out)
    return pl.pallas_call(
        _proj_kernel, out_shape=out_shape, grid=(n_steps,), in_specs=in_specs,
        out_specs=out_specs,
        compiler_params=pltpu.CompilerParams(dimension_semantics=("arbitrary",),
                                             vmem_limit_bytes=VMEM_LIMIT),
        name="proj_mla",
    )(x2d, *consts, *tabs, *to_bf16)


def _s5_kernel(u_ref, um_ref, bre_ref, bim_ref, cre_ref, cim_ref, ar_ref, ai_ref,
               d_ref, wglu_ref, bglu_ref, g_ref, o_ref, xs_ref, h_ref, *, batch):
    j = pl.program_id(0)
    slabs_per_half = N_SLABS // 2

    def project_in(ub, rows):
        for kh in range(2):
            lhs = ub[:, kh * 256:(kh + 1) * 256]
            xre = jnp.dot(lhs, bre_ref[kh], preferred_element_type=F32)
            xim = jnp.dot(lhs, bim_ref[kh], preferred_element_type=F32)
            for cl in range(slabs_per_half):
                c = kh * slabs_per_half + cl
                for b in range(batch):
                    xs_ref[c, pl.ds(b * S5_PITCH, rows), :] = (
                        xre[b * rows:(b + 1) * rows, cl * LANES:(cl + 1) * LANES])
                    xs_ref[c, pl.ds((batch + b) * S5_PITCH, rows), :] = (
                        xim[b * rows:(b + 1) * rows, cl * LANES:(cl + 1) * LANES])

    def scan(n_steps):
        def body(t, hs):
            new = []
            for c in range(N_SLABS):
                rows = pl.ds(t, 2 * batch, stride=S5_PITCH)
                x8 = xs_ref[c, rows, :]
                h = hs[c]
                hn = ar_ref[c] * h + ai_ref[c] * pltpu.roll(h, batch, 0) + x8
                xs_ref[c, rows, :] = hn
                new.append(hn)
            return tuple(new)

        hs = tuple(h_ref[c] for c in range(N_SLABS))
        hs = lax.fori_loop(0, n_steps, body, hs, unroll=4)
        for c in range(N_SLABS):
            h_ref[c] = hs[c]

    @pl.when(j == 0)
    def _():
        h_ref[...] = jnp.zeros_like(h_ref)
        um = um_ref[...].astype(BF16)
        project_in(jnp.concatenate([um] * batch, axis=0), N_META)
        scan(N_META)

    uf = u_ref[...].reshape(batch * S5_CHUNK, D_SSM)
    project_in(uf.astype(BF16), S5_CHUNK)
    scan(S5_CHUNK)

    ys = []
    for nh in range(2):
        def gather(plane0):
            return jnp.concatenate(
                [jnp.concatenate(
                    [xs_ref[nh * slabs_per_half + cl, pl.ds((plane0 + b) * S5_PITCH, S5_CHUNK), :]
                     for cl in range(slabs_per_half)], axis=1)
                 for b in range(batch)], axis=0).astype(BF16)
        yre = jnp.dot(gather(0), cre_ref[nh], preferred_element_type=F32)
        yim = jnp.dot(gather(batch), cim_ref[nh], preferred_element_type=F32)
        ys.append(yre - yim)
    y = jnp.concatenate(ys, axis=1) + d_ref[...] * uf
    z = 0.5 * y * (1.0 + jnp.tanh(math.sqrt(2.0 / math.pi) * (y + 0.044715 * (y * y * y))))
    gate = jnp.dot(z.astype(BF16), wglu_ref[...], preferred_element_type=F32) + bglu_ref[...]
    out = z * (1.0 / (1.0 + jnp.exp(-gate)))
    o_ref[...] = _rms(out, g_ref[...]).astype(BF16).reshape(batch, S5_CHUNK, D_SSM)


def _s5_call(u3, u_meta, consts):
    batch, seq, _ = u3.shape
    assert 2 * batch == SUBLANES and seq % S5_CHUNK == 0
    grid = (seq // S5_CHUNK,)
    in_specs = [pl.BlockSpec((batch, S5_CHUNK, D_SSM), lambda j: (0, j, 0)),
                _const_spec(u_meta.shape)] + [_const_spec(c.shape) for c in consts]
    return pl.pallas_call(
        functools.partial(_s5_kernel, batch=batch),
        out_shape=jax.ShapeDtypeStruct((batch, seq, D_SSM), BF16),
        grid=grid, in_specs=in_specs,
        out_specs=pl.BlockSpec((batch, S5_CHUNK, D_SSM), lambda j: (0, j, 0)),
        scratch_shapes=[pltpu.VMEM((N_SLABS, 2 * batch * S5_PITCH, LANES), F32),
                        pltpu.VMEM((N_SLABS, 2 * batch, LANES), F32)],
        compiler_params=pltpu.CompilerParams(dimension_semantics=("arbitrary",),
                                             vmem_limit_bytes=VMEM_LIMIT),
        name="s5_mixer",
    )(u3, u_meta, *consts)


ATT_TQ = 1024
ATT_TK = 1024
ATT_SUB = 256
ATT_HEADS = 4


def _attn_kernel(q_ref, k_ref, v_ref, km_ref, vm_ref, mask_ref, o_ref, acc_ref, m_ref, *, online):
    nt = (((1,), (1,)), ((), ()))
    n_q = q_ref.shape[0] // ATT_TQ
    head_lanes = [slice(h * HEAD_PAD, (h + 1) * HEAD_PAD) for h in range(ATT_HEADS)]

    def step(h, sub, q, kblk, vblk, mask):
        s = lax.dot_general(q, kblk, nt, preferred_element_type=F32)
        if online:
            if mask is not None:
                s = jnp.where(mask > 0, s, -jnp.inf)
            m = m_ref[h, sub]
            m_new = jnp.maximum(m, jnp.max(s, axis=-1, keepdims=True))
            p = jnp.exp2(s - m_new).astype(BF16)
            acc_ref[h, sub] = (jnp.exp2(m - m_new) * acc_ref[h, sub]
                               + jnp.dot(p, vblk, preferred_element_type=F32))
            m_ref[h, sub] = m_new
        else:
            p = jnp.exp2(s).astype(BF16)
            if mask is not None:
                p = p * mask
            acc_ref[h, sub] += jnp.dot(p, vblk, preferred_element_type=F32)

    def q_tile(qi, _):
        q0 = pl.multiple_of(qi * ATT_TQ, ATT_TQ)
        rows = pl.ds(q0, ATT_TQ)
        qs = [q_ref[rows, hl] for hl in head_lanes]
        acc_ref[...] = jnp.zeros_like(acc_ref)
        if online:
            m_ref[...] = jnp.full(m_ref.shape, -1e30, F32)

        def body(kb, _):
            krows = pl.ds(pl.multiple_of(kb * ATT_TK, ATT_TK), ATT_TK)
            for h, hl in enumerate(head_lanes):
                step(h, slice(None), qs[h], k_ref[krows, hl], v_ref[krows, hl], None)
            return 0

        lax.fori_loop(0, qi * (ATT_TQ // ATT_TK), body, 0)

        for i in range(ATT_TQ // ATT_SUB):
            sub = slice(i * ATT_SUB, ATT_TQ)
            krows = pl.ds(q0 + i * ATT_SUB, ATT_SUB)
            for h, hl in enumerate(head_lanes):
                kblk, vblk = k_ref[krows, hl], v_ref[krows, hl]
                mask = mask_ref[sub, LANES + i * ATT_SUB:LANES + (i + 1) * ATT_SUB]
                if i == 0:
                    kblk = jnp.concatenate([km_ref[:, hl], kblk], axis=0)
                    vblk = jnp.concatenate([vm_ref[:, hl], vblk], axis=0)
                    mask = mask_ref[sub, :LANES + ATT_SUB]
                step(h, sub, qs[h][sub], kblk, vblk, mask)
        lane = lax.broadcasted_iota(jnp.int32, (ATT_TQ, HEAD_PAD), 1)
        for hp in range(ATT_HEADS // 2):
            even, odd = acc_ref[2 * hp], acc_ref[2 * hp + 1]
            o_even = even * (1.0 / even[:, V_DIM:V_DIM + 1])
            o_odd = odd * (1.0 / odd[:, 0:1])
            o_ref[rows, hp * HEAD_PAD:(hp + 1) * HEAD_PAD] = (
                jnp.where(lane < V_DIM, o_even, o_odd).astype(BF16))
        return 0

    lax.fori_loop(0, n_q, q_tile, 0)


def _attn_call(q3, k3, v3, k_meta, v_meta, mask, *, online):
    batch, seq, _ = q3.shape
    grid = (batch, N_HEADS // ATT_HEADS)
    seq_blk = pl.BlockSpec((None, seq, ATT_HEADS * HEAD_PAD), lambda b, hg: (b, 0, hg))
    meta_blk = pl.BlockSpec((LANES, ATT_HEADS * HEAD_PAD), lambda b, hg: (0, hg))
    return pl.pallas_call(
        functools.partial(_attn_kernel, online=online),
        out_shape=jax.ShapeDtypeStruct((batch, seq, N_HEADS * V_DIM), BF16),
        grid=grid,
        in_specs=[seq_blk, seq_blk, seq_blk, meta_blk, meta_blk, _const_spec(mask.shape)],
        out_specs=pl.BlockSpec((None, seq, ATT_HEADS * V_DIM), lambda b, hg: (b, 0, hg)),
        scratch_shapes=[pltpu.VMEM((ATT_HEADS, ATT_TQ, HEAD_PAD), F32),
                        pltpu.VMEM((ATT_HEADS, ATT_TQ, 1), F32)],
        compiler_params=pltpu.CompilerParams(dimension_semantics=("parallel", "parallel"),
                                             vmem_limit_bytes=VMEM_LIMIT),
        name="mla_attention_online" if online else "mla_attention",
    )(q3, k3, v3, k_meta, v_meta, mask)


FFN_TM = 512


def _ffn_kernel(x_ref, ms_ref, oa_ref, gatt_ref, wout_ref, gffn_ref, wg_ref, wu_ref, wd_ref,
                out_ref):
    ya = _rms(oa_ref[...].astype(F32), gatt_ref[...]).astype(BF16)
    mixed = jnp.concatenate([ms_ref[...], ya], axis=1)
    h1 = x_ref[...] + jnp.dot(mixed, wout_ref[...], preferred_element_type=F32)
    hn = _rms(h1, gffn_ref[...]).astype(BF16)
    g = jnp.dot(hn, wg_ref[...], preferred_element_type=F32)
    u = jnp.dot(hn, wu_ref[...], preferred_element_type=F32)
    a = (g * (1.0 / (1.0 + jnp.exp(-g))) * u).astype(BF16)
    out_ref[...] = h1 + jnp.dot(a, wd_ref[...], preferred_element_type=F32)


def _ffn_call(x2d, ms2d, oa2d, consts):
    n_rows = x2d.shape[0]
    tm = FFN_TM
    row = lambda i: (i, 0)
    once = pl.Buffered(1)
    in_specs = [pl.BlockSpec((tm, D_MODEL), row),
                pl.BlockSpec((tm, D_SSM), row),
                pl.BlockSpec((tm, N_HEADS * V_DIM), row)]
    in_specs += [pl.BlockSpec(c.shape, lambda i: (0, 0), pipeline_mode=once) for c in consts]
    return pl.pallas_call(
        _ffn_kernel,
        out_shape=jax.ShapeDtypeStruct((n_rows, D_MODEL), F32),
        grid=(n_rows // tm,), in_specs=in_specs,
        out_specs=pl.BlockSpec((tm, D_MODEL), row),
        compiler_params=pltpu.CompilerParams(dimension_semantics=("parallel",),
                                             vmem_limit_bytes=VMEM_LIMIT),
        name="outproj_ffn",
    )(x2d, ms2d, oa2d, *consts)


def _rope_tables(first, count):
    pos = np.arange(first, first + count, dtype=np.float64)
    inv_freq = 1.0 / (ROPE_BASE ** (np.arange(0, ROPE, 2, dtype=np.float64) / ROPE))
    ang = pos[:, None] * inv_freq[None, :]
    cos, sin = np.cos(ang), np.sin(ang)
    ctab = np.zeros((count, HEAD_PAD), np.float32)
    stab = np.zeros((count, HEAD_PAD), np.float32)
    ctab[:, :NOPE] = 1.0
    ctab[:, NOPE:QK_DIM] = np.concatenate([cos, cos], axis=1)
    stab[:, NOPE:QK_DIM] = np.concatenate([sin, sin], axis=1)
    return jnp.asarray(ctab), jnp.asarray(stab)


def _head_gains(gain):
    g_r = gain[NOPE:]
    g_r_swapped = jnp.concatenate([g_r[HALF_ROPE:], g_r[:HALF_ROPE]])
    pad = jnp.zeros((HEAD_PAD - QK_DIM,), F32)
    return jnp.stack([jnp.concatenate([gain, pad]),
                      jnp.concatenate([jnp.zeros((NOPE,), F32), g_r_swapped, pad])])


def _rot_half_cols(w):
    return jnp.concatenate([-w[..., HALF_ROPE:], w[..., :HALF_ROPE]], axis=-1)


def _pad_cols(w, left, total):
    return jnp.pad(w, ((0, 0), (left, total - left - w.shape[1])))


def kernel(x, meta_tokens, mix_norm_g, w_in, ssm_a_re, ssm_a_im, ssm_log_dt, ssm_b_re, ssm_b_im,
           ssm_c_re, ssm_c_im, ssm_d, ssm_w_glu, ssm_b_glu, q_lora_norm_g, w_uq, kv_lora_norm_g,
           w_uk, w_uv, q_head_norm_g, k_head_norm_g, ssm_out_norm_g, att_out_norm_g, w_out,
           ffn_norm_g, w_gate, w_up, w_down):
    batch, seq, _ = x.shape
    depth = w_in.shape[0]
    assert depth == 1
    l = 0

    wi = w_in[l]
    o_r = D_SSM + Q_LORA + KV_LORA
    w_r = wi[:, o_r:]
    win = jnp.concatenate([wi[:, :o_r], _pad_cols(w_r, NOPE, HEAD_PAD),
                           _pad_cols(_rot_half_cols(w_r), NOPE, HEAD_PAD)], axis=1).astype(BF16)
    wq3 = w_uq[l].reshape(Q_LORA, N_HEADS, QK_DIM)
    q1 = jnp.pad(wq3, ((0, 0), (0, 0), (0, HEAD_PAD - QK_DIM)))
    q2 = jnp.pad(_rot_half_cols(wq3[..., NOPE:]), ((0, 0), (0, 0), (NOPE, HEAD_PAD - QK_DIM)))
    wq = jnp.concatenate([q1.reshape(Q_LORA, QK_PAD), q2.reshape(Q_LORA, QK_PAD)], axis=1).astype(BF16)
    wk3 = jnp.pad(w_uk[l].reshape(KV_LORA, N_HEADS, NOPE), ((0, 0), (0, 0), (0, HEAD_PAD - NOPE)))
    wv4 = w_uv[l].reshape(KV_LORA, N_HEADS // 2, 2, V_DIM)
    zv = jnp.zeros_like(wv4[:, :, 0])
    wv = jnp.stack([jnp.concatenate([wv4[:, :, 0], zv], axis=-1),
                    jnp.concatenate([zv, wv4[:, :, 1]], axis=-1)], axis=2).reshape(KV_LORA, QK_PAD)
    ones_col = jnp.zeros((2, HEAD_PAD), F32).at[0, V_DIM].set(1.0).at[1, 0].set(1.0)
    vones = jnp.tile(ones_col.reshape(1, 2 * HEAD_PAD), (1, N_HEADS // 2))
    wkv = jnp.concatenate([wk3.reshape(KV_LORA, QK_PAD), wv], axis=1).astype(BF16)
    tabs_m = _rope_tables(0, N_META)
    tabs_f = _rope_tables(N_META, seq)
    proj_consts = (mix_norm_g[l][None], win, q_lora_norm_g[l][None], wq,
                   kv_lora_norm_g[l][None], wkv, vones,
                   _head_gains(q_head_norm_g[l]), _head_gains(k_head_norm_g[l]))

    tm = 512
    ffn_f32 = (w_out, w_gate, w_up, w_down)
    u2, q2d, k2d, v2d, wout_b, wg_b, wu_b, wd_b = _proj_call(
        x.reshape(batch * seq, D_MODEL), tm, tabs_f, seq // tm, proj_consts, ffn_f32)
    u_meta, _, k_meta, v_meta = _proj_call(meta_tokens, N_META, tabs_m, 1, proj_consts)

    dt = jnp.exp(ssm_log_dt[l])[:, None]
    lr, li = ssm_a_re[l], ssm_a_im[l]
    mag = jnp.exp(lr * dt)
    ar = mag * jnp.cos(li * dt)
    ai = mag * jnp.sin(li * dt)
    den = lr * lr + li * li
    fr = ((ar - 1.0) * lr + ai * li) / den
    fi = (ai * lr - (ar - 1.0) * li) / den
    br, bi = ssm_b_re[l], ssm_b_im[l]
    bbr = fr[..., None] * br - fi[..., None] * bi
    bbi = fr[..., None] * bi + fi[..., None] * br
    half_groups = N_GROUPS // 2

    def block_diag_halves(blocks):
        _, r, c = blocks.shape
        rows = blocks.reshape(2, half_groups * r, c)
        tiled = jnp.tile(rows, (1, 1, half_groups))
        row_g = jnp.arange(half_groups * r) // r
        col_g = jnp.arange(half_groups * c) // c
        return jnp.where(row_g[:, None] == col_g[None, :], tiled, 0.0).astype(BF16)

    def in_map(bb):
        return block_diag_halves(jnp.swapaxes(bb, 1, 2))

    def out_map(cc):
        return block_diag_halves(jnp.swapaxes(cc, 1, 2))

    ar_rows = jnp.broadcast_to(ar.reshape(N_SLABS, 1, LANES), (N_SLABS, 2 * batch, LANES))
    ai_flat = ai.reshape(N_SLABS, 1, LANES)
    ai_rows = jnp.concatenate([jnp.broadcast_to(-ai_flat, (N_SLABS, batch, LANES)),
                               jnp.broadcast_to(ai_flat, (N_SLABS, batch, LANES))], axis=1)
    s5_consts = (in_map(bbr), in_map(bbi), out_map(ssm_c_re[l]), out_map(ssm_c_im[l]),
                 ar_rows, ai_rows, ssm_d[l][None], ssm_w_glu[l].astype(BF16),
                 ssm_b_glu[l][None], ssm_out_norm_g[l][None])
    mixed_ssm = _s5_call(u2.reshape(batch, seq, D_SSM), u_meta, s5_consts)

    pad_rows = ((0, LANES - N_META), (0, 0))
    chunk_of = jnp.arange(ATT_TQ) // CHUNK
    causal = (chunk_of[None, :] <= chunk_of[:, None]).astype(BF16)
    mask = jnp.concatenate([jnp.ones((ATT_TQ, LANES), BF16), causal], axis=1)
    score_bound = (LOG2_E * math.sqrt(QK_DIM) * jnp.max(jnp.abs(q_head_norm_g[l]))
                   * jnp.max(jnp.abs(k_head_norm_g[l])))
    attn_args = (q2d.reshape(batch, seq, QK_PAD), k2d.reshape(batch, seq, QK_PAD),
                 v2d.reshape(batch, seq, QK_PAD),
                 jnp.pad(k_meta, pad_rows), jnp.pad(v_meta, pad_rows), mask)
    y_att = lax.cond(score_bound <= MAX_UNSHIFTED_LOG2_SCORE,
                     functools.partial(_attn_call, online=False),
                     functools.partial(_attn_call, online=True), *attn_args)

    ffn_consts = (att_out_norm_g[l][None], wout_b, ffn_norm_g[l][None],
                  wg_b, wu_b, wd_b)
    out = _ffn_call(x.reshape(batch * seq, D_MODEL), mixed_ssm.reshape(batch * seq, D_SSM),
                    y_att.reshape(batch * seq, N_HEADS * V_DIM), ffn_consts)
    return out.reshape(batch, seq, D_MODEL)
```

```python
import functools
import math

import jax
import jax.numpy as jnp
import numpy as np
from jax import lax
from jax.experimental import pallas as pl
from jax.experimental.pallas import tpu as pltpu

F32 = jnp.float32
BF16 = jnp.bfloat16

D_MODEL = 1024
N_META = 16
CHUNK = 64
D_SSM = 512
SSM_GROUP = 16
N_GROUPS = D_SSM // SSM_GROUP
SSM_STATE = 64
N_HEADS = 8
V_DIM = 64
NOPE = 64
ROPE = 32
HALF_ROPE = ROPE // 2
QK_DIM = NOPE + ROPE
Q_LORA = 256
KV_LORA = 128
D_FF = 2816
ROPE_BASE = 10000.0
EPS = 1e-6
LOG2_E = math.log2(math.e)
MAX_UNSHIFTED_LOG2_SCORE = 40.0

LANES = 128
SUBLANES = 8
HEAD_PAD = LANES
QK_PAD = N_HEADS * HEAD_PAD
N_STATE_COLS = N_GROUPS * SSM_STATE
N_SLABS = N_STATE_COLS // LANES
S5_CHUNK = 128
S5_PITCH = S5_CHUNK + SUBLANES
VMEM_LIMIT = 56 * 1024 * 1024


def _rms(x, g):
    return x * lax.rsqrt(jnp.mean(x * x, axis=-1, keepdims=True) + EPS) * g


def _proj_kernel(x_ref, gmix_ref, win_ref, gq_ref, wq_ref, gkv_ref, wkv_ref, vones_ref,
                 hgq_ref, hgk_ref, ctab_ref, stab_ref, *rest):
    n_cast = (len(rest) - 4) // 2
    u_ref, q_ref, k_ref, v_ref = rest[n_cast:n_cast + 4]
    for src_ref, dst_ref in zip(rest[:n_cast], rest[n_cast + 4:]):
        dst_ref[...] = src_ref[...].astype(BF16)

    x = x_ref[...]
    xn = _rms(x, gmix_ref[...]).astype(BF16)
    p = jnp.dot(xn, win_ref[...], preferred_element_type=F32)
    u_ref[...] = p[:, :D_SSM]

    cq = p[:, D_SSM:D_SSM + Q_LORA]
    cqn = _rms(cq, gq_ref[...]).astype(BF16)
    q12 = jnp.dot(cqn, wq_ref[...], preferred_element_type=F32)
    ctab, stab = ctab_ref[...], stab_ref[...]
    t1q, t2q = ctab * hgq_ref[0:1, :], stab * hgq_ref[1:2, :]
    scale = QK_DIM ** -0.5 * LOG2_E
    for h in range(N_HEADS):
        q1 = q12[:, h * HEAD_PAD:(h + 1) * HEAD_PAD]
        q2 = q12[:, QK_PAD + h * HEAD_PAD:QK_PAD + (h + 1) * HEAD_PAD]
        r = lax.rsqrt(jnp.sum(q1 * q1, axis=-1, keepdims=True) * (1.0 / QK_DIM) + EPS)
        qh = (q1 * t1q + q2 * t2q) * (r * scale)
        q_ref[:, h * HEAD_PAD:(h + 1) * HEAD_PAD] = qh.astype(BF16)

    c0 = D_SSM + Q_LORA
    ckv = p[:, c0:c0 + KV_LORA]
    ckvn = _rms(ckv, gkv_ref[...]).astype(BF16)
    kv = jnp.dot(ckvn, wkv_ref[...], preferred_element_type=F32)
    v_ref[...] = (kv[:, QK_PAD:] + vones_ref[...]).astype(BF16)
    kr = p[:, c0 + KV_LORA:c0 + KV_LORA + HEAD_PAD]
    kr_rot = p[:, c0 + KV_LORA + HEAD_PAD:c0 + KV_LORA + 2 * HEAD_PAD]
    ss_r = jnp.sum(kr * kr, axis=-1, keepdims=True)
    t1k = ctab * hgk_ref[0:1, :]
    kr_part = kr_rot * (stab * hgk_ref[1:2, :])
    for h in range(N_HEADS):
        kn = kv[:, h * HEAD_PAD:(h + 1) * HEAD_PAD]
        ss = jnp.sum(kn * kn, axis=-1, keepdims=True) + ss_r
        r = lax.rsqrt(ss * (1.0 / QK_DIM) + EPS)
        kh = ((kn + kr) * t1k + kr_part) * r
        k_ref[:, h * HEAD_PAD:(h + 1) * HEAD_PAD] = kh.astype(BF16)


def _const_spec(shape):
    nd = len(shape)
    return pl.BlockSpec(shape, lambda *_: (0,) * nd)


def _proj_call(x2d, tm, tabs, n_tab_blocks, consts, to_bf16=()):
    n_rows = x2d.shape[0]
    n_steps = n_rows // tm
    row = lambda i: (i, 0)
    tab = lambda i: (i % n_tab_blocks, 0)
    cast_in, cast_out = [], []
    for w in to_bf16:
        _, w_rows, w_cols = w.shape
        rep = next(r for r in (1, 2, 4, 8) if (w_rows * r) % (16 * n_steps) == 0)
        blk_rows = w_rows * rep // n_steps
        cast_in.append(pl.BlockSpec((None, blk_rows, w_cols),
                                    functools.partial(lambda rep, i: (0, i // rep, 0), rep)))
        cast_out.append(pl.BlockSpec((blk_rows, w_cols),
                                     functools.partial(lambda rep, i: (i // rep, 0), rep)))
    in_specs = ([pl.BlockSpec((tm, D_MODEL), row)] + [_const_spec(c.shape) for c in consts]
                + [pl.BlockSpec((tm, HEAD_PAD), tab)] * len(tabs) + cast_in)
    out_shape = (
        jax.ShapeDtypeStruct((n_rows, D_SSM), F32),
        jax.ShapeDtypeStruct((n_rows, QK_PAD), BF16),
        jax.ShapeDtypeStruct((n_rows, QK_PAD), BF16),
        jax.ShapeDtypeStruct((n_rows, QK_PAD), BF16),
    ) + tuple(jax.ShapeDtypeStruct(w.shape[1:], BF16) for w in to_bf16)
    out_specs = (
        pl.BlockSpec((tm, D_SSM), row),
        pl.BlockSpec((tm, QK_PAD), row),
        pl.BlockSpec((tm, QK_PAD), row),
        pl.BlockSpec((tm, QK_PAD), row),
    ) + tuple(cast_out)
    return pl.pallas_call(
        _proj_kernel, out_shape=out_shape, grid=(n_steps,), in_specs=in_specs,
        out_specs=out_specs,
        compiler_params=pltpu.CompilerParams(dimension_semantics=("arbitrary",),
                                             vmem_limit_bytes=VMEM_LIMIT),
        name="proj_mla",
    )(x2d, *consts, *tabs, *to_bf16)


def _s5_kernel(u_ref, um_ref, bin_ref, cout_ref, tin_ref, tout_ref, ar_ref, ai_ref,
               d_ref, wglu_ref, bglu_ref, g_ref, o_ref, xs_ref, h_ref, bmap_ref, cmap_ref, *, batch):
    j = pl.program_id(0)
    slabs_per_half = N_SLABS // 2
    half_ch, half_st = D_SSM // 2, N_STATE_COLS // 2

    def build_maps():
        def diag(shape, row_block, col_block):
            return (lax.broadcasted_iota(jnp.int32, shape, 0) // row_block
                    == lax.broadcasted_iota(jnp.int32, shape, 1) // col_block)
        diag_in = diag((half_ch, half_st), SSM_GROUP, SSM_STATE)
        diag_out = diag((half_st, half_ch), SSM_STATE, SSM_GROUP)
        for part in range(2):
            for half in range(2):
                rows_in = bin_ref[part, half * half_ch:(half + 1) * half_ch, :].astype(BF16)
                tiled = jnp.dot(rows_in, tin_ref[...], preferred_element_type=F32)
                bmap_ref[part, half] = jnp.where(diag_in, tiled, 0.0).astype(BF16)
                rows_out = cout_ref[part, half * half_st:(half + 1) * half_st, :].astype(BF16)
                tiled = jnp.dot(rows_out, tout_ref[...], preferred_element_type=F32)
                cmap_ref[part, half] = jnp.where(diag_out, tiled, 0.0).astype(BF16)

    def project_in(ub, rows):
        for kh in range(2):
            lhs = ub[:, kh * 256:(kh + 1) * 256]
            xre = jnp.dot(lhs, bmap_ref[0, kh], preferred_element_type=F32)
            xim = jnp.dot(lhs, bmap_ref[1, kh], preferred_element_type=F32)
            for cl in range(slabs_per_half):
                c = kh * slabs_per_half + cl
                for b in range(batch):
                    xs_ref[c, pl.ds(b * S5_PITCH, rows), :] = (
                        xre[b * rows:(b + 1) * rows, cl * LANES:(cl + 1) * LANES])
                    xs_ref[c, pl.ds((batch + b) * S5_PITCH, rows), :] = (
                        xim[b * rows:(b + 1) * rows, cl * LANES:(cl + 1) * LANES])

    def scan(n_steps):
        def body(t, hs):
            new = []
            for c in range(N_SLABS):
                rows = pl.ds(t, 2 * batch, stride=S5_PITCH)
                x8 = xs_ref[c, rows, :]
                h = hs[c]
                hn = ar_ref[c] * h + ai_ref[c] * pltpu.roll(h, batch, 0) + x8
                xs_ref[c, rows, :] = hn
                new.append(hn)
            return tuple(new)

        hs = tuple(h_ref[c] for c in range(N_SLABS))
        hs = lax.fori_loop(0, n_steps, body, hs, unroll=4)
        for c in range(N_SLABS):
            h_ref[c] = hs[c]

    @pl.when(j == 0)
    def _():
        build_maps()
        h_ref[...] = jnp.zeros_like(h_ref)
        um = um_ref[...].astype(BF16)
        project_in(jnp.concatenate([um] * batch, axis=0), N_META)
        scan(N_META)

    uf = u_ref[...].reshape(batch * S5_CHUNK, D_SSM)
    project_in(uf.astype(BF16), S5_CHUNK)
    scan(S5_CHUNK)

    ys = []
    for nh in range(2):
        def gather(plane0):
            return jnp.concatenate(
                [jnp.concatenate(
                    [xs_ref[nh * slabs_per_half + cl, pl.ds((plane0 + b) * S5_PITCH, S5_CHUNK), :]
                     for cl in range(slabs_per_half)], axis=1)
                 for b in range(batch)], axis=0).astype(BF16)
        yre = jnp.dot(gather(0), cmap_ref[0, nh], preferred_element_type=F32)
        yim = jnp.dot(gather(batch), cmap_ref[1, nh], preferred_element_type=F32)
        ys.append(yre - yim)
    y = jnp.concatenate(ys, axis=1) + d_ref[...] * uf
    z = 0.5 * y * (1.0 + jnp.tanh(math.sqrt(2.0 / math.pi) * (y + 0.044715 * (y * y * y))))
    gate = jnp.dot(z.astype(BF16), wglu_ref[...], preferred_element_type=F32) + bglu_ref[...]
    out = z * (1.0 / (1.0 + jnp.exp(-gate)))
    o_ref[...] = _rms(out, g_ref[...]).astype(BF16).reshape(batch, S5_CHUNK, D_SSM)


def _s5_call(u3, u_meta, consts):
    batch, seq, _ = u3.shape
    assert 2 * batch == SUBLANES and seq % S5_CHUNK == 0
    grid = (seq // S5_CHUNK,)
    in_specs = [pl.BlockSpec((batch, S5_CHUNK, D_SSM), lambda j: (0, j, 0)),
                _const_spec(u_meta.shape)] + [_const_spec(c.shape) for c in consts]
    return pl.pallas_call(
        functools.partial(_s5_kernel, batch=batch),
        out_shape=jax.ShapeDtypeStruct((batch, seq, D_SSM), BF16),
        grid=grid, in_specs=in_specs,
        out_specs=pl.BlockSpec((batch, S5_CHUNK, D_SSM), lambda j: (0, j, 0)),
        scratch_shapes=[pltpu.VMEM((N_SLABS, 2 * batch * S5_PITCH, LANES), F32),
                        pltpu.VMEM((N_SLABS, 2 * batch, LANES), F32),
                        pltpu.VMEM((2, 2, D_SSM // 2, N_STATE_COLS // 2), BF16),
                        pltpu.VMEM((2, 2, N_STATE_COLS // 2, D_SSM // 2), BF16)],
        compiler_params=pltpu.CompilerParams(dimension_semantics=("arbitrary",),
                                             vmem_limit_bytes=VMEM_LIMIT),
        name="s5_mixer",
    )(u3, u_meta, *consts)


ATT_TQ = 1024
ATT_TK = 1024
ATT_SUB = 256
ATT_HEADS = 4


def _attn_kernel(q_ref, k_ref, v_ref, km_ref, vm_ref, mask_ref, o_ref, acc_ref, m_ref, *, online):
    nt = (((1,), (1,)), ((), ()))
    n_q = q_ref.shape[0] // ATT_TQ
    head_lanes = [slice(h * HEAD_PAD, (h + 1) * HEAD_PAD) for h in range(ATT_HEADS)]

    def step(h, sub, q, kblk, vblk, mask):
        s = lax.dot_general(q, kblk, nt, preferred_element_type=F32)
        if online:
            if mask is not None:
                s = jnp.where(mask > 0, s, -jnp.inf)
            m = m_ref[h, sub]
            m_new = jnp.maximum(m, jnp.max(s, axis=-1, keepdims=True))
            p = jnp.exp2(s - m_new).astype(BF16)
            acc_ref[h, sub] = (jnp.exp2(m - m_new) * acc_ref[h, sub]
                               + jnp.dot(p, vblk, preferred_element_type=F32))
            m_ref[h, sub] = m_new
        else:
            p = jnp.exp2(s).astype(BF16)
            if mask is not None:
                p = p * mask
            acc_ref[h, sub] += jnp.dot(p, vblk, preferred_element_type=F32)

    def q_tile(qi, _):
        q0 = pl.multiple_of(qi * ATT_TQ, ATT_TQ)
        rows = pl.ds(q0, ATT_TQ)
        qs = [q_ref[rows, hl] for hl in head_lanes]
        acc_ref[...] = jnp.zeros_like(acc_ref)
        if online:
            m_ref[...] = jnp.full(m_ref.shape, -1e30, F32)

        def body(kb, _):
            krows = pl.ds(pl.multiple_of(kb * ATT_TK, ATT_TK), ATT_TK)
            for h, hl in enumerate(head_lanes):
                step(h, slice(None), qs[h], k_ref[krows, hl], v_ref[krows, hl], None)
            return 0

        lax.fori_loop(0, qi * (ATT_TQ // ATT_TK), body, 0)

        for i in range(ATT_TQ // ATT_SUB):
            sub = slice(i * ATT_SUB, ATT_TQ)
            krows = pl.ds(q0 + i * ATT_SUB, ATT_SUB)
            for h, hl in enumerate(head_lanes):
                kblk, vblk = k_ref[krows, hl], v_ref[krows, hl]
                mask = mask_ref[sub, LANES + i * ATT_SUB:LANES + (i + 1) * ATT_SUB]
                if i == 0:
                    kblk = jnp.concatenate([km_ref[:, hl], kblk], axis=0)
                    vblk = jnp.concatenate([vm_ref[:, hl], vblk], axis=0)
                    mask = mask_ref[sub, :LANES + ATT_SUB]
                step(h, sub, qs[h][sub], kblk, vblk, mask)
        lane = lax.broadcasted_iota(jnp.int32, (ATT_TQ, HEAD_PAD), 1)
        for hp in range(ATT_HEADS // 2):
            even, odd = acc_ref[2 * hp], acc_ref[2 * hp + 1]
            o_even = even * (1.0 / even[:, V_DIM:V_DIM + 1])
            o_odd = odd * (1.0 / odd[:, 0:1])
            o_ref[rows, hp * HEAD_PAD:(hp + 1) * HEAD_PAD] = (
                jnp.where(lane < V_DIM, o_even, o_odd).astype(BF16))
        return 0

    lax.fori_loop(0, n_q, q_tile, 0)


def _attn_call(q3, k3, v3, k_meta, v_meta, mask, *, online):
    batch, seq, _ = q3.shape
    grid = (batch, N_HEADS // ATT_HEADS)
    seq_blk = pl.BlockSpec((None, seq, ATT_HEADS * HEAD_PAD), lambda b, hg: (b, 0, hg))
    meta_blk = pl.BlockSpec((LANES, ATT_HEADS * HEAD_PAD), lambda b, hg: (0, hg))
    return pl.pallas_call(
        functools.partial(_attn_kernel, online=online),
        out_shape=jax.ShapeDtypeStruct((batch, seq, N_HEADS * V_DIM), BF16),
        grid=grid,
        in_specs=[seq_blk, seq_blk, seq_blk, meta_blk, meta_blk, _const_spec(mask.shape)],
        out_specs=pl.BlockSpec((None, seq, ATT_HEADS * V_DIM), lambda b, hg: (b, 0, hg)),
        scratch_shapes=[pltpu.VMEM((ATT_HEADS, ATT_TQ, HEAD_PAD), F32),
                        pltpu.VMEM((ATT_HEADS, ATT_TQ, 1), F32)],
        compiler_params=pltpu.CompilerParams(dimension_semantics=("parallel", "parallel"),
                                             vmem_limit_bytes=VMEM_LIMIT),
        name="mla_attention_online" if online else "mla_attention",
    )(q3, k3, v3, k_meta, v_meta, mask)


FFN_TM = 512


def _ffn_kernel(x_ref, ms_ref, oa_ref, gatt_ref, wout_ref, gffn_ref, wg_ref, wu_ref, wd_ref,
                out_ref):
    ya = _rms(oa_ref[...].astype(F32), gatt_ref[...]).astype(BF16)
    mixed = jnp.concatenate([ms_ref[...], ya], axis=1)
    h1 = x_ref[...] + jnp.dot(mixed, wout_ref[...], preferred_element_type=F32)
    hn = _rms(h1, gffn_ref[...]).astype(BF16)
    g = jnp.dot(hn, wg_ref[...], preferred_element_type=F32)
    u = jnp.dot(hn, wu_ref[...], preferred_element_type=F32)
    a = (g * (1.0 / (1.0 + jnp.exp(-g))) * u).astype(BF16)
    out_ref[...] = h1 + jnp.dot(a, wd_ref[...], preferred_element_type=F32)


def _ffn_call(x2d, ms2d, oa2d, consts):
    n_rows = x2d.shape[0]
    tm = FFN_TM
    row = lambda i: (i, 0)
    once = pl.Buffered(1)
    in_specs = [pl.BlockSpec((tm, D_MODEL), row),
                pl.BlockSpec((tm, D_SSM), row),
                pl.BlockSpec((tm, N_HEADS * V_DIM), row)]
    in_specs += [pl.BlockSpec(c.shape, lambda i: (0, 0), pipeline_mode=once) for c in consts]
    return pl.pallas_call(
        _ffn_kernel,
        out_shape=jax.ShapeDtypeStruct((n_rows, D_MODEL), F32),
        grid=(n_rows // tm,), in_specs=in_specs,
        out_specs=pl.BlockSpec((tm, D_MODEL), row),
        compiler_params=pltpu.CompilerParams(dimension_semantics=("parallel",),
                                             vmem_limit_bytes=VMEM_LIMIT),
        name="outproj_ffn",
    )(x2d, ms2d, oa2d, *consts)


def _rope_tables(first, count):
    pos = np.arange(first, first + count, dtype=np.float64)
    inv_freq = 1.0 / (ROPE_BASE ** (np.arange(0, ROPE, 2, dtype=np.float64) / ROPE))
    ang = pos[:, None] * inv_freq[None, :]
    cos, sin = np.cos(ang), np.sin(ang)
    ctab = np.zeros((count, HEAD_PAD), np.float32)
    stab = np.zeros((count, HEAD_PAD), np.float32)
    ctab[:, :NOPE] = 1.0
    ctab[:, NOPE:QK_DIM] = np.concatenate([cos, cos], axis=1)
    stab[:, NOPE:QK_DIM] = np.concatenate([sin, sin], axis=1)
    return jnp.asarray(ctab), jnp.asarray(stab)


def _head_gains(gain):
    g_r = gain[NOPE:]
    g_r_swapped = jnp.concatenate([g_r[HALF_ROPE:], g_r[:HALF_ROPE]])
    pad = jnp.zeros((HEAD_PAD - QK_DIM,), F32)
    return jnp.stack([jnp.concatenate([gain, pad]),
                      jnp.concatenate([jnp.zeros((NOPE,), F32), g_r_swapped, pad])])


def _rot_half_cols(w):
    return jnp.concatenate([-w[..., HALF_ROPE:], w[..., :HALF_ROPE]], axis=-1)


def _pad_cols(w, left, total):
    return jnp.pad(w, ((0, 0), (left, total - left - w.shape[1])))


def kernel(x, meta_tokens, mix_norm_g, w_in, ssm_a_re, ssm_a_im, ssm_log_dt, ssm_b_re, ssm_b_im,
           ssm_c_re, ssm_c_im, ssm_d, ssm_w_glu, ssm_b_glu, q_lora_norm_g, w_uq, kv_lora_norm_g,
           w_uk, w_uv, q_head_norm_g, k_head_norm_g, ssm_out_norm_g, att_out_norm_g, w_out,
           ffn_norm_g, w_gate, w_up, w_down):
    batch, seq, _ = x.shape
    depth = w_in.shape[0]
    assert depth == 1
    l = 0

    wi = w_in[l]
    o_r = D_SSM + Q_LORA + KV_LORA
    w_r = wi[:, o_r:]
    win = jnp.concatenate([wi[:, :o_r], _pad_cols(w_r, NOPE, HEAD_PAD),
                           _pad_cols(_rot_half_cols(w_r), NOPE, HEAD_PAD)], axis=1).astype(BF16)
    wq3 = w_uq[l].reshape(Q_LORA, N_HEADS, QK_DIM)
    q1 = jnp.pad(wq3, ((0, 0), (0, 0), (0, HEAD_PAD - QK_DIM)))
    q2 = jnp.pad(_rot_half_cols(wq3[..., NOPE:]), ((0, 0), (0, 0), (NOPE, HEAD_PAD - QK_DIM)))
    wq = jnp.concatenate([q1.reshape(Q_LORA, QK_PAD), q2.reshape(Q_LORA, QK_PAD)], axis=1).astype(BF16)
    wk3 = jnp.pad(w_uk[l].reshape(KV_LORA, N_HEADS, NOPE), ((0, 0), (0, 0), (0, HEAD_PAD - NOPE)))
    wv4 = w_uv[l].reshape(KV_LORA, N_HEADS // 2, 2, V_DIM)
    zv = jnp.zeros_like(wv4[:, :, 0])
    wv = jnp.stack([jnp.concatenate([wv4[:, :, 0], zv], axis=-1),
                    jnp.concatenate([zv, wv4[:, :, 1]], axis=-1)], axis=2).reshape(KV_LORA, QK_PAD)
    ones_col = np.zeros((2, HEAD_PAD), np.float32)
    ones_col[0, V_DIM] = ones_col[1, 0] = 1.0
    vones = jnp.asarray(np.tile(ones_col.reshape(1, 2 * HEAD_PAD), (1, N_HEADS // 2)))
    wkv = jnp.concatenate([wk3.reshape(KV_LORA, QK_PAD), wv], axis=1).astype(BF16)
    tabs_m = _rope_tables(0, N_META)
    tabs_f = _rope_tables(N_META, seq)
    proj_consts = (mix_norm_g[l][None], win, q_lora_norm_g[l][None], wq,
                   kv_lora_norm_g[l][None], wkv, vones,
                   _head_gains(q_head_norm_g[l]), _head_gains(k_head_norm_g[l]))

    tm = 512
    ffn_f32 = (w_out, w_gate, w_up, w_down)
    u2, q2d, k2d, v2d, wout_b, wg_b, wu_b, wd_b = _proj_call(
        x.reshape(batch * seq, D_MODEL), tm, tabs_f, seq // tm, proj_consts, ffn_f32)
    u_meta, _, k_meta, v_meta = _proj_call(meta_tokens, N_META, tabs_m, 1, proj_consts)

    dt = jnp.exp(ssm_log_dt[l])[:, None]
    lr, li = ssm_a_re[l], ssm_a_im[l]
    mag = jnp.exp(lr * dt)
    ar = mag * jnp.cos(li * dt)
    ai = mag * jnp.sin(li * dt)
    den = lr * lr + li * li
    fr = ((ar - 1.0) * lr + ai * li) / den
    fi = (ai * lr - (ar - 1.0) * li) / den
    br, bi = ssm_b_re[l], ssm_b_im[l]
    bbr = fr[..., None] * br - fi[..., None] * bi
    bbi = fr[..., None] * bi + fi[..., None] * br
    b_in = jnp.swapaxes(jnp.stack([bbr, bbi]), 2, 3).reshape(2, D_SSM, SSM_STATE)
    c_out = jnp.swapaxes(jnp.stack([ssm_c_re[l], ssm_c_im[l]]), 2, 3)
    c_out = jnp.pad(c_out.reshape(2, N_STATE_COLS, SSM_GROUP),
                    ((0, 0), (0, 0), (0, LANES - SSM_GROUP)))
    half_groups = N_GROUPS // 2
    tile_in = np.tile(np.eye(SSM_STATE, dtype=np.float32), (1, half_groups))
    tile_out = np.zeros((LANES, half_groups * SSM_GROUP), np.float32)
    tile_out[:SSM_GROUP] = np.tile(np.eye(SSM_GROUP, dtype=np.float32), (1, half_groups))

    ar_rows = jnp.broadcast_to(ar.reshape(N_SLABS, 1, LANES), (N_SLABS, 2 * batch, LANES))
    ai_flat = ai.reshape(N_SLABS, 1, LANES)
    ai_rows = jnp.concatenate([jnp.broadcast_to(-ai_flat, (N_SLABS, batch, LANES)),
                               jnp.broadcast_to(ai_flat, (N_SLABS, batch, LANES))], axis=1)
    s5_consts = (b_in, c_out, jnp.asarray(tile_in, BF16), jnp.asarray(tile_out, BF16),
                 ar_rows, ai_rows, ssm_d[l][None], ssm_w_glu[l].astype(BF16),
                 ssm_b_glu[l][None], ssm_out_norm_g[l][None])
    mixed_ssm = _s5_call(u2.reshape(batch, seq, D_SSM), u_meta, s5_consts)

    pad_rows = ((0, LANES - N_META), (0, 0))
    chunk_of = np.arange(ATT_TQ) // CHUNK
    causal = (chunk_of[None, :] <= chunk_of[:, None]).astype(np.float32)
    mask = jnp.asarray(np.concatenate([np.ones((ATT_TQ, LANES), np.float32), causal], axis=1),
                       dtype=BF16)
    score_bound = (LOG2_E * math.sqrt(QK_DIM) * jnp.max(jnp.abs(q_head_norm_g[l]))
                   * jnp.max(jnp.abs(k_head_norm_g[l])))
    attn_args = (q2d.reshape(batch, seq, QK_PAD), k2d.reshape(batch, seq, QK_PAD),
                 v2d.reshape(batch, seq, QK_PAD),
                 jnp.pad(k_meta, pad_rows), jnp.pad(v_meta, pad_rows), mask)
    y_att = lax.cond(score_bound <= MAX_UNSHIFTED_LOG2_SCORE,
                     functools.partial(_attn_call, online=False),
                     functools.partial(_attn_call, online=True), *attn_args)

    ffn_consts = (att_out_norm_g[l][None], wout_b, ffn_norm_g[l][None],
                  wg_b, wu_b, wd_b)
    out = _ffn_call(x.reshape(batch * seq, D_MODEL), mixed_ssm.reshape(batch * seq, D_SSM),
                    y_att.reshape(batch * seq, N_HEADS * V_DIM), ffn_consts)
    return out.reshape(batch, seq, D_MODEL)
```

```python
import functools
import math

import jax
import jax.numpy as jnp
import numpy as np
from jax import lax
from jax.experimental import pallas as pl
from jax.experimental.pallas import tpu as pltpu

F32 = jnp.float32
BF16 = jnp.bfloat16

D_MODEL = 1024
N_META = 16
CHUNK = 64
D_SSM = 512
SSM_GROUP = 16
N_GROUPS = D_SSM // SSM_GROUP
SSM_STATE = 64
N_HEADS = 8
V_DIM = 64
NOPE = 64
ROPE = 32
HALF_ROPE = ROPE // 2
QK_DIM = NOPE + ROPE
Q_LORA = 256
KV_LORA = 128
D_FF = 2816
ROPE_BASE = 10000.0
EPS = 1e-6
LOG2_E = math.log2(math.e)
MAX_UNSHIFTED_LOG2_SCORE = 40.0

LANES = 128
SUBLANES = 8
HEAD_PAD = LANES
QK_PAD = N_HEADS * HEAD_PAD
N_STATE_COLS = N_GROUPS * SSM_STATE
N_SLABS = N_STATE_COLS // LANES
S5_CHUNK = 128
S5_PITCH = S5_CHUNK + SUBLANES
VMEM_LIMIT = 56 * 1024 * 1024


def _rms(x, g):
    return x * lax.rsqrt(jnp.mean(x * x, axis=-1, keepdims=True) + EPS) * g


def _proj_kernel(x_ref, gmix_ref, win_ref, gq_ref, wq_ref, gkv_ref, wkv_ref, vones_ref,
                 hgq_ref, hgk_ref, ctab_ref, stab_ref, *rest):
    n_cast = (len(rest) - 4) // 2
    u_ref, q_ref, k_ref, v_ref = rest[n_cast:n_cast + 4]
    for src_ref, dst_ref in zip(rest[:n_cast], rest[n_cast + 4:]):
        dst_ref[...] = src_ref[...].astype(BF16)

    x = x_ref[...]
    xn = _rms(x, gmix_ref[...]).astype(BF16)
    p = jnp.dot(xn, win_ref[...], preferred_element_type=F32)
    u_ref[...] = p[:, :D_SSM]

    cq = p[:, D_SSM:D_SSM + Q_LORA]
    cqn = _rms(cq, gq_ref[...]).astype(BF16)
    q12 = jnp.dot(cqn, wq_ref[...], preferred_element_type=F32)
    ctab, stab = ctab_ref[...], stab_ref[...]
    t1q, t2q = ctab * hgq_ref[0:1, :], stab * hgq_ref[1:2, :]
    scale = QK_DIM ** -0.5 * LOG2_E
    for h in range(N_HEADS):
        q1 = q12[:, h * HEAD_PAD:(h + 1) * HEAD_PAD]
        q2 = q12[:, QK_PAD + h * HEAD_PAD:QK_PAD + (h + 1) * HEAD_PAD]
        r = lax.rsqrt(jnp.sum(q1 * q1, axis=-1, keepdims=True) * (1.0 / QK_DIM) + EPS)
        qh = (q1 * t1q + q2 * t2q) * (r * scale)
        q_ref[:, h * HEAD_PAD:(h + 1) * HEAD_PAD] = qh.astype(BF16)

    c0 = D_SSM + Q_LORA
    ckv = p[:, c0:c0 + KV_LORA]
    ckvn = _rms(ckv, gkv_ref[...]).astype(BF16)
    kv = jnp.dot(ckvn, wkv_ref[...], preferred_element_type=F32)
    v_ref[...] = (kv[:, QK_PAD:] + vones_ref[...]).astype(BF16)
    kr = p[:, c0 + KV_LORA:c0 + KV_LORA + HEAD_PAD]
    kr_rot = p[:, c0 + KV_LORA + HEAD_PAD:c0 + KV_LORA + 2 * HEAD_PAD]
    ss_r = jnp.sum(kr * kr, axis=-1, keepdims=True)
    t1k = ctab * hgk_ref[0:1, :]
    kr_part = kr_rot * (stab * hgk_ref[1:2, :])
    for h in range(N_HEADS):
        kn = kv[:, h * HEAD_PAD:(h + 1) * HEAD_PAD]
        ss = jnp.sum(kn * kn, axis=-1, keepdims=True) + ss_r
        r = lax.rsqrt(ss * (1.0 / QK_DIM) + EPS)
        kh = ((kn + kr) * t1k + kr_part) * r
        k_ref[:, h * HEAD_PAD:(h + 1) * HEAD_PAD] = kh.astype(BF16)


def _const_spec(shape):
    nd = len(shape)
    return pl.BlockSpec(shape, lambda *_: (0,) * nd)


def _proj_call(x2d, tm, tabs, n_tab_blocks, consts, to_bf16=()):
    n_rows = x2d.shape[0]
    n_steps = n_rows // tm
    row = lambda i: (i, 0)
    tab = lambda i: (i % n_tab_blocks, 0)
    cast_in, cast_out = [], []
    for w in to_bf16:
        _, w_rows, w_cols = w.shape
        rep = next(r for r in (1, 2, 4, 8) if (w_rows * r) % (16 * n_steps) == 0)
        blk_rows = w_rows * rep // n_steps
        cast_in.append(pl.BlockSpec((None, blk_rows, w_cols),
                                    functools.partial(lambda rep, i: (0, i // rep, 0), rep)))
        cast_out.append(pl.BlockSpec((blk_rows, w_cols),
                                     functools.partial(lambda rep, i: (i // rep, 0), rep)))
    in_specs = ([pl.BlockSpec((tm, D_MODEL), row)] + [_const_spec(c.shape) for c in consts]
                + [pl.BlockSpec((tm, HEAD_PAD), tab)] * len(tabs) + cast_in)
    out_shape = (
        jax.ShapeDtypeStruct((n_rows, D_SSM), F32),
        jax.ShapeDtypeStruct((n_rows, QK_PAD), BF16),
        jax.ShapeDtypeStruct((n_rows, QK_PAD), BF16),
        jax.ShapeDtypeStruct((n_rows, QK_PAD), BF16),
    ) + tuple(jax.ShapeDtypeStruct(w.shape[1:], BF16) for w in to_bf16)
    out_specs = (
        pl.BlockSpec((tm, D_SSM), row),
        pl.BlockSpec((tm, QK_PAD), row),
        pl.BlockSpec((tm, QK_PAD), row),
        pl.BlockSpec((tm, QK_PAD), row),
    ) + tuple(cast_out)
    return pl.pallas_call(
        _proj_kernel, out_shape=out_shape, grid=(n_steps,), in_specs=in_specs,
        out_specs=out_specs,
        compiler_params=pltpu.CompilerParams(dimension_semantics=("arbitrary",),
                                             vmem_limit_bytes=VMEM_LIMIT),
        name="proj_mla",
    )(x2d, *consts, *tabs, *to_bf16)


def _s5_kernel(u_ref, um_ref, bin_ref, cout_ref, tin_ref, tout_ref, ar_ref, ai_ref,
               d_ref, wglu_ref, bglu_ref, g_ref, o_ref, xs_ref, h_ref, bmap_ref, cmap_ref, *, batch):
    j = pl.program_id(0)
    slabs_per_half = N_SLABS // 2
    half_ch, half_st = D_SSM // 2, N_STATE_COLS // 2

    def build_maps():
        def diag(shape, row_block, col_block):
            return (lax.broadcasted_iota(jnp.int32, shape, 0) // row_block
                    == lax.broadcasted_iota(jnp.int32, shape, 1) // col_block)
        diag_in = diag((half_ch, half_st), SSM_GROUP, SSM_STATE)
        diag_out = diag((half_st, half_ch), SSM_STATE, SSM_GROUP)
        for part in range(2):
            for half in range(2):
                rows_in = bin_ref[part, half * half_ch:(half + 1) * half_ch, :].astype(BF16)
                tiled = jnp.dot(rows_in, tin_ref[...], preferred_element_type=F32)
                bmap_ref[part, half] = jnp.where(diag_in, tiled, 0.0).astype(BF16)
                rows_out = cout_ref[part, half * half_st:(half + 1) * half_st, :].astype(BF16)
                tiled = jnp.dot(rows_out, tout_ref[...], preferred_element_type=F32)
                cmap_ref[part, half] = jnp.where(diag_out, tiled, 0.0).astype(BF16)

    def project_in(ub, rows):
        for kh in range(2):
            lhs = ub[:, kh * 256:(kh + 1) * 256]
            xre = jnp.dot(lhs, bmap_ref[0, kh], preferred_element_type=F32)
            xim = jnp.dot(lhs, bmap_ref[1, kh], preferred_element_type=F32)
            for cl in range(slabs_per_half):
                c = kh * slabs_per_half + cl
                for b in range(batch):
                    xs_ref[c, pl.ds(b * S5_PITCH, rows), :] = (
                        xre[b * rows:(b + 1) * rows, cl * LANES:(cl + 1) * LANES])
                    xs_ref[c, pl.ds((batch + b) * S5_PITCH, rows), :] = (
                        xim[b * rows:(b + 1) * rows, cl * LANES:(cl + 1) * LANES])

    def scan(n_steps):
        def body(t, hs):
            new = []
            for c in range(N_SLABS):
                rows = pl.ds(t, 2 * batch, stride=S5_PITCH)
                x8 = xs_ref[c, rows, :]
                h = hs[c]
                hn = ar_ref[c] * h + ai_ref[c] * pltpu.roll(h, batch, 0) + x8
                xs_ref[c, rows, :] = hn
                new.append(hn)
            return tuple(new)

        hs = tuple(h_ref[c] for c in range(N_SLABS))
        hs = lax.fori_loop(0, n_steps, body, hs, unroll=4)
        for c in range(N_SLABS):
            h_ref[c] = hs[c]

    @pl.when(j == 0)
    def _():
        build_maps()
        h_ref[...] = jnp.zeros_like(h_ref)
        um = um_ref[...].astype(BF16)
        project_in(jnp.concatenate([um] * batch, axis=0), N_META)
        scan(N_META)

    uf = u_ref[...].reshape(batch * S5_CHUNK, D_SSM)
    project_in(uf.astype(BF16), S5_CHUNK)
    scan(S5_CHUNK)

    ys = []
    for nh in range(2):
        def gather(plane0):
            return jnp.concatenate(
                [jnp.concatenate(
                    [xs_ref[nh * slabs_per_half + cl, pl.ds((plane0 + b) * S5_PITCH, S5_CHUNK), :]
                     for cl in range(slabs_per_half)], axis=1)
                 for b in range(batch)], axis=0).astype(BF16)
        yre = jnp.dot(gather(0), cmap_ref[0, nh], preferred_element_type=F32)
        yim = jnp.dot(gather(batch), cmap_ref[1, nh], preferred_element_type=F32)
        ys.append(yre - yim)
    y = jnp.concatenate(ys, axis=1) + d_ref[...] * uf
    z = 0.5 * y * (1.0 + jnp.tanh(math.sqrt(2.0 / math.pi) * (y + 0.044715 * (y * y * y))))
    gate = jnp.dot(z.astype(BF16), wglu_ref[...], preferred_element_type=F32) + bglu_ref[...]
    out = z * (1.0 / (1.0 + jnp.exp(-gate)))
    o_ref[...] = _rms(out, g_ref[...]).astype(BF16).reshape(batch, S5_CHUNK, D_SSM)


def _s5_call(u3, u_meta, consts):
    batch, seq, _ = u3.shape
    assert 2 * batch == SUBLANES and seq % S5_CHUNK == 0
    grid = (seq // S5_CHUNK,)
    in_specs = [pl.BlockSpec((batch, S5_CHUNK, D_SSM), lambda j: (0, j, 0)),
                _const_spec(u_meta.shape)] + [_const_spec(c.shape) for c in consts]
    return pl.pallas_call(
        functools.partial(_s5_kernel, batch=batch),
        out_shape=jax.ShapeDtypeStruct((batch, seq, D_SSM), BF16),
        grid=grid, in_specs=in_specs,
        out_specs=pl.BlockSpec((batch, S5_CHUNK, D_SSM), lambda j: (0, j, 0)),
        scratch_shapes=[pltpu.VMEM((N_SLABS, 2 * batch * S5_PITCH, LANES), F32),
                        pltpu.VMEM((N_SLABS, 2 * batch, LANES), F32),
                        pltpu.VMEM((2, 2, D_SSM // 2, N_STATE_COLS // 2), BF16),
                        pltpu.VMEM((2, 2, N_STATE_COLS // 2, D_SSM // 2), BF16)],
        compiler_params=pltpu.CompilerParams(dimension_semantics=("arbitrary",),
                                             vmem_limit_bytes=VMEM_LIMIT),
        name="s5_mixer",
    )(u3, u_meta, *consts)


ATT_TQ = 1024
ATT_TK = 1024
ATT_SUB = 256
ATT_HEADS = 4


def _attn_kernel(q_ref, k_ref, v_ref, km_ref, vm_ref, mask_ref, *rest, online):
    o_ref, acc_ref, m_ref = rest[-3:]
    nt = (((1,), (1,)), ((), ()))
    n_q = q_ref.shape[0] // ATT_TQ
    head_lanes = [slice(h * HEAD_PAD, (h + 1) * HEAD_PAD) for h in range(ATT_HEADS)]

    def step(h, sub, q, kblk, vblk, mask):
        s = lax.dot_general(q, kblk, nt, preferred_element_type=F32)
        if online:
            if mask is not None:
                s = jnp.where(mask > 0, s, -jnp.inf)
            m = m_ref[h, sub]
            m_new = jnp.maximum(m, jnp.max(s, axis=-1, keepdims=True))
            p = jnp.exp2(s - m_new).astype(BF16)
            acc_ref[h, sub] = (jnp.exp2(m - m_new) * acc_ref[h, sub]
                               + jnp.dot(p, vblk, preferred_element_type=F32))
            m_ref[h, sub] = m_new
        else:
            p = jnp.exp2(s).astype(BF16)
            if mask is not None:
                p = p * mask
            acc_ref[h, sub] += jnp.dot(p, vblk, preferred_element_type=F32)

    def q_tile(qi, _):
        q0 = pl.multiple_of(qi * ATT_TQ, ATT_TQ)
        rows = pl.ds(q0, ATT_TQ)
        qs = [q_ref[rows, hl] for hl in head_lanes]
        acc_ref[...] = jnp.zeros_like(acc_ref)
        if online:
            m_ref[...] = jnp.full(m_ref.shape, -1e30, F32)

        def body(kb, _):
            krows = pl.ds(pl.multiple_of(kb * ATT_TK, ATT_TK), ATT_TK)
            for h, hl in enumerate(head_lanes):
                step(h, slice(None), qs[h], k_ref[krows, hl], v_ref[krows, hl], None)
            return 0

        lax.fori_loop(0, qi * (ATT_TQ // ATT_TK), body, 0)

        for i in range(ATT_TQ // ATT_SUB):
            sub = slice(i * ATT_SUB, ATT_TQ)
            krows = pl.ds(q0 + i * ATT_SUB, ATT_SUB)
            for h, hl in enumerate(head_lanes):
                kblk, vblk = k_ref[krows, hl], v_ref[krows, hl]
                mask = mask_ref[sub, LANES + i * ATT_SUB:LANES + (i + 1) * ATT_SUB]
                if i == 0:
                    kblk = jnp.concatenate([km_ref[:, hl], kblk], axis=0)
                    vblk = jnp.concatenate([vm_ref[:, hl], vblk], axis=0)
                    mask = mask_ref[sub, :LANES + ATT_SUB]
                step(h, sub, qs[h][sub], kblk, vblk, mask)
        lane = lax.broadcasted_iota(jnp.int32, (ATT_TQ, HEAD_PAD), 1)
        for hp in range(ATT_HEADS // 2):
            even, odd = acc_ref[2 * hp], acc_ref[2 * hp + 1]
            o_even = even * (1.0 / even[:, V_DIM:V_DIM + 1])
            o_odd = odd * (1.0 / odd[:, 0:1])
            o_ref[rows, hp * HEAD_PAD:(hp + 1) * HEAD_PAD] = (
                jnp.where(lane < V_DIM, o_even, o_odd).astype(BF16))
        return 0

    lax.fori_loop(0, n_q, q_tile, 0)


def _attn_call(q3, k3, v3, k_meta, v_meta, mask, *, online, active, prev=None):
    batch, seq, _ = q3.shape
    grid = (batch * active, N_HEADS // ATT_HEADS)
    seq_blk = pl.BlockSpec((None, seq, ATT_HEADS * HEAD_PAD), lambda b, hg: (b, 0, hg))
    meta_blk = pl.BlockSpec((LANES, ATT_HEADS * HEAD_PAD), lambda b, hg: (0, hg))
    operands = [q3, k3, v3, k_meta, v_meta, mask]
    in_specs = [seq_blk, seq_blk, seq_blk, meta_blk, meta_blk, _const_spec(mask.shape)]
    aliases = {}
    if prev is not None:
        aliases = {len(operands): 0}
        operands.append(prev)
        in_specs.append(pl.BlockSpec(memory_space=pl.ANY))
    return pl.pallas_call(
        functools.partial(_attn_kernel, online=online),
        out_shape=jax.ShapeDtypeStruct((batch, seq, N_HEADS * V_DIM), BF16),
        grid=grid, in_specs=in_specs,
        out_specs=pl.BlockSpec((None, seq, ATT_HEADS * V_DIM), lambda b, hg: (b, 0, hg)),
        scratch_shapes=[pltpu.VMEM((ATT_HEADS, ATT_TQ, HEAD_PAD), F32),
                        pltpu.VMEM((ATT_HEADS, ATT_TQ, 1), F32)],
        input_output_aliases=aliases,
        compiler_params=pltpu.CompilerParams(dimension_semantics=("arbitrary", "arbitrary"),
                                             vmem_limit_bytes=VMEM_LIMIT),
        name="mla_attention_online" if online else "mla_attention",
    )(*operands)


FFN_TM = 512


def _ffn_kernel(x_ref, ms_ref, oa_ref, gatt_ref, wout_ref, gffn_ref, wg_ref, wu_ref, wd_ref,
                out_ref):
    ya = _rms(oa_ref[...].astype(F32), gatt_ref[...]).astype(BF16)
    mixed = jnp.concatenate([ms_ref[...], ya], axis=1)
    h1 = x_ref[...] + jnp.dot(mixed, wout_ref[...], preferred_element_type=F32)
    hn = _rms(h1, gffn_ref[...]).astype(BF16)
    g = jnp.dot(hn, wg_ref[...], preferred_element_type=F32)
    u = jnp.dot(hn, wu_ref[...], preferred_element_type=F32)
    a = (g * (1.0 / (1.0 + jnp.exp(-g))) * u).astype(BF16)
    out_ref[...] = h1 + jnp.dot(a, wd_ref[...], preferred_element_type=F32)


def _ffn_call(x2d, ms2d, oa2d, consts):
    n_rows = x2d.shape[0]
    tm = FFN_TM
    row = lambda i: (i, 0)
    once = pl.Buffered(1)
    in_specs = [pl.BlockSpec((tm, D_MODEL), row),
                pl.BlockSpec((tm, D_SSM), row),
                pl.BlockSpec((tm, N_HEADS * V_DIM), row)]
    in_specs += [pl.BlockSpec(c.shape, lambda i: (0, 0), pipeline_mode=once) for c in consts]
    return pl.pallas_call(
        _ffn_kernel,
        out_shape=jax.ShapeDtypeStruct((n_rows, D_MODEL), F32),
        grid=(n_rows // tm,), in_specs=in_specs,
        out_specs=pl.BlockSpec((tm, D_MODEL), row),
        compiler_params=pltpu.CompilerParams(dimension_semantics=("parallel",),
                                             vmem_limit_bytes=VMEM_LIMIT),
        name="outproj_ffn",
    )(x2d, ms2d, oa2d, *consts)


def _rope_tables(first, count):
    pos = np.arange(first, first + count, dtype=np.float64)
    inv_freq = 1.0 / (ROPE_BASE ** (np.arange(0, ROPE, 2, dtype=np.float64) / ROPE))
    ang = pos[:, None] * inv_freq[None, :]
    cos, sin = np.cos(ang), np.sin(ang)
    ctab = np.zeros((count, HEAD_PAD), np.float32)
    stab = np.zeros((count, HEAD_PAD), np.float32)
    ctab[:, :NOPE] = 1.0
    ctab[:, NOPE:QK_DIM] = np.concatenate([cos, cos], axis=1)
    stab[:, NOPE:QK_DIM] = np.concatenate([sin, sin], axis=1)
    return jnp.asarray(ctab), jnp.asarray(stab)


def _head_gains(gain):
    g_r = gain[NOPE:]
    g_r_swapped = jnp.concatenate([g_r[HALF_ROPE:], g_r[:HALF_ROPE]])
    pad = jnp.zeros((HEAD_PAD - QK_DIM,), F32)
    return jnp.stack([jnp.concatenate([gain, pad]),
                      jnp.concatenate([jnp.zeros((NOPE,), F32), g_r_swapped, pad])])


def _rot_half_cols(w):
    return jnp.concatenate([-w[..., HALF_ROPE:], w[..., :HALF_ROPE]], axis=-1)


def _pad_cols(w, left, total):
    return jnp.pad(w, ((0, 0), (left, total - left - w.shape[1])))


def kernel(x, meta_tokens, mix_norm_g, w_in, ssm_a_re, ssm_a_im, ssm_log_dt, ssm_b_re, ssm_b_im,
           ssm_c_re, ssm_c_im, ssm_d, ssm_w_glu, ssm_b_glu, q_lora_norm_g, w_uq, kv_lora_norm_g,
           w_uk, w_uv, q_head_norm_g, k_head_norm_g, ssm_out_norm_g, att_out_norm_g, w_out,
           ffn_norm_g, w_gate, w_up, w_down):
    batch, seq, _ = x.shape
    depth = w_in.shape[0]
    assert depth == 1
    l = 0

    wi = w_in[l]
    o_r = D_SSM + Q_LORA + KV_LORA
    w_r = wi[:, o_r:]
    win = jnp.concatenate([wi[:, :o_r], _pad_cols(w_r, NOPE, HEAD_PAD),
                           _pad_cols(_rot_half_cols(w_r), NOPE, HEAD_PAD)], axis=1).astype(BF16)
    wq3 = w_uq[l].reshape(Q_LORA, N_HEADS, QK_DIM)
    q1 = jnp.pad(wq3, ((0, 0), (0, 0), (0, HEAD_PAD - QK_DIM)))
    q2 = jnp.pad(_rot_half_cols(wq3[..., NOPE:]), ((0, 0), (0, 0), (NOPE, HEAD_PAD - QK_DIM)))
    wq = jnp.concatenate([q1.reshape(Q_LORA, QK_PAD), q2.reshape(Q_LORA, QK_PAD)], axis=1).astype(BF16)
    wk3 = jnp.pad(w_uk[l].reshape(KV_LORA, N_HEADS, NOPE), ((0, 0), (0, 0), (0, HEAD_PAD - NOPE)))
    wv4 = w_uv[l].reshape(KV_LORA, N_HEADS // 2, 2, V_DIM)
    zv = jnp.zeros_like(wv4[:, :, 0])
    wv = jnp.stack([jnp.concatenate([wv4[:, :, 0], zv], axis=-1),
                    jnp.concatenate([zv, wv4[:, :, 1]], axis=-1)], axis=2).reshape(KV_LORA, QK_PAD)
    ones_col = jnp.zeros((2, HEAD_PAD), F32).at[0, V_DIM].set(1.0).at[1, 0].set(1.0)
    vones = jnp.tile(ones_col.reshape(1, 2 * HEAD_PAD), (1, N_HEADS // 2))
    wkv = jnp.concatenate([wk3.reshape(KV_LORA, QK_PAD), wv], axis=1).astype(BF16)
    tabs_m = _rope_tables(0, N_META)
    tabs_f = _rope_tables(N_META, seq)
    proj_consts = (mix_norm_g[l][None], win, q_lora_norm_g[l][None], wq,
                   kv_lora_norm_g[l][None], wkv, vones,
                   _head_gains(q_head_norm_g[l]), _head_gains(k_head_norm_g[l]))

    tm = 512
    ffn_f32 = (w_out, w_gate, w_up, w_down)
    u2, q2d, k2d, v2d, wout_b, wg_b, wu_b, wd_b = _proj_call(
        x.reshape(batch * seq, D_MODEL), tm, tabs_f, seq // tm, proj_consts, ffn_f32)
    u_meta, _, k_meta, v_meta = _proj_call(meta_tokens, N_META, tabs_m, 1, proj_consts)

    dt = jnp.exp(ssm_log_dt[l])[:, None]
    lr, li = ssm_a_re[l], ssm_a_im[l]
    mag = jnp.exp(lr * dt)
    ar = mag * jnp.cos(li * dt)
    ai = mag * jnp.sin(li * dt)
    den = lr * lr + li * li
    fr = ((ar - 1.0) * lr + ai * li) / den
    fi = (ai * lr - (ar - 1.0) * li) / den
    br, bi = ssm_b_re[l], ssm_b_im[l]
    bbr = fr[..., None] * br - fi[..., None] * bi
    bbi = fr[..., None] * bi + fi[..., None] * br
    b_in = jnp.swapaxes(jnp.stack([bbr, bbi]), 2, 3).reshape(2, D_SSM, SSM_STATE)
    c_out = jnp.swapaxes(jnp.stack([ssm_c_re[l], ssm_c_im[l]]), 2, 3)
    c_out = jnp.pad(c_out.reshape(2, N_STATE_COLS, SSM_GROUP),
                    ((0, 0), (0, 0), (0, LANES - SSM_GROUP)))
    half_groups = N_GROUPS // 2
    tile_in = np.tile(np.eye(SSM_STATE, dtype=np.float32), (1, half_groups))
    tile_out = np.zeros((LANES, half_groups * SSM_GROUP), np.float32)
    tile_out[:SSM_GROUP] = np.tile(np.eye(SSM_GROUP, dtype=np.float32), (1, half_groups))

    ar_rows = jnp.broadcast_to(ar.reshape(N_SLABS, 1, LANES), (N_SLABS, 2 * batch, LANES))
    ai_flat = ai.reshape(N_SLABS, 1, LANES)
    ai_rows = jnp.concatenate([jnp.broadcast_to(-ai_flat, (N_SLABS, batch, LANES)),
                               jnp.broadcast_to(ai_flat, (N_SLABS, batch, LANES))], axis=1)
    s5_consts = (b_in, c_out, jnp.asarray(tile_in, BF16), jnp.asarray(tile_out, BF16),
                 ar_rows, ai_rows, ssm_d[l][None], ssm_w_glu[l].astype(BF16),
                 ssm_b_glu[l][None], ssm_out_norm_g[l][None])
    mixed_ssm = _s5_call(u2.reshape(batch, seq, D_SSM), u_meta, s5_consts)

    pad_rows = ((0, LANES - N_META), (0, 0))
    chunk_of = jnp.arange(ATT_TQ) // CHUNK
    causal = (chunk_of[None, :] <= chunk_of[:, None]).astype(BF16)
    mask = jnp.concatenate([jnp.ones((ATT_TQ, LANES), BF16), causal], axis=1)
    score_bound = (LOG2_E * math.sqrt(QK_DIM) * jnp.max(jnp.abs(q_head_norm_g[l]))
                   * jnp.max(jnp.abs(k_head_norm_g[l])))
    attn_args = (q2d.reshape(batch, seq, QK_PAD), k2d.reshape(batch, seq, QK_PAD),
                 v2d.reshape(batch, seq, QK_PAD),
                 jnp.pad(k_meta, pad_rows), jnp.pad(v_meta, pad_rows), mask)
    unshifted = (score_bound <= MAX_UNSHIFTED_LOG2_SCORE).astype(jnp.int32)
    y_att = _attn_call(*attn_args, online=False, active=unshifted)
    y_att = _attn_call(*attn_args, online=True, active=1 - unshifted, prev=y_att)

    ffn_consts = (att_out_norm_g[l][None], wout_b, ffn_norm_g[l][None],
                  wg_b, wu_b, wd_b)
    out = _ffn_call(x.reshape(batch * seq, D_MODEL), mixed_ssm.reshape(batch * seq, D_SSM),
                    y_att.reshape(batch * seq, N_HEADS * V_DIM), ffn_consts)
    return out.reshape(batch, seq, D_MODEL)
```

```python
import functools
import math

import jax
import jax.numpy as jnp
import numpy as np
from jax import lax
from jax.experimental import pallas as pl
from jax.experimental.pallas import tpu as pltpu

F32 = jnp.float32
BF16 = jnp.bfloat16

D_MODEL = 1024
N_META = 16
CHUNK = 64
D_SSM = 512
SSM_GROUP = 16
N_GROUPS = D_SSM // SSM_GROUP
SSM_STATE = 64
N_HEADS = 8
V_DIM = 64
NOPE = 64
ROPE = 32
HALF_ROPE = ROPE // 2
QK_DIM = NOPE + ROPE
Q_LORA = 256
KV_LORA = 128
D_FF = 2816
ROPE_BASE = 10000.0
EPS = 1e-6
LOG2_E = math.log2(math.e)
MAX_UNSHIFTED_LOG2_SCORE = 40.0

LANES = 128
SUBLANES = 8
HEAD_PAD = LANES
QK_PAD = N_HEADS * HEAD_PAD
N_STATE_COLS = N_GROUPS * SSM_STATE
N_SLABS = N_STATE_COLS // LANES
S5_CHUNK = 128
S5_PITCH = S5_CHUNK + SUBLANES
PROJ_TM = 512
VMEM_LIMIT = 56 * 1024 * 1024


def _rms(x, g):
    return x * lax.rsqrt(jnp.mean(x * x, axis=-1, keepdims=True) + EPS) * g


def _proj_kernel(x_ref, gmix_ref, win_ref, gq_ref, wq_ref, gkv_ref, wkv_ref, vones_ref,
                 hgq_ref, hgk_ref, ctab_ref, stab_ref, *rest):
    n_cast = (len(rest) - 4) // 2
    u_ref, q_ref, k_ref, v_ref = rest[n_cast:n_cast + 4]
    for src_ref, dst_ref in zip(rest[:n_cast], rest[n_cast + 4:]):
        dst_ref[...] = src_ref[...].astype(BF16)

    x = x_ref[...]
    xn = _rms(x, gmix_ref[...]).astype(BF16)
    p = jnp.dot(xn, win_ref[...], preferred_element_type=F32)
    u_ref[...] = p[:, :D_SSM]

    cq = p[:, D_SSM:D_SSM + Q_LORA]
    cqn = _rms(cq, gq_ref[...]).astype(BF16)
    q12 = jnp.dot(cqn, wq_ref[...], preferred_element_type=F32)
    ctab, stab = ctab_ref[...], stab_ref[...]
    t1q, t2q = ctab * hgq_ref[0:1, :], stab * hgq_ref[1:2, :]
    scale = QK_DIM ** -0.5 * LOG2_E
    for h in range(N_HEADS):
        q1 = q12[:, h * HEAD_PAD:(h + 1) * HEAD_PAD]
        q2 = q12[:, QK_PAD + h * HEAD_PAD:QK_PAD + (h + 1) * HEAD_PAD]
        r = lax.rsqrt(jnp.sum(q1 * q1, axis=-1, keepdims=True) * (1.0 / QK_DIM) + EPS)
        qh = (q1 * t1q + q2 * t2q) * (r * scale)
        q_ref[:, h * HEAD_PAD:(h + 1) * HEAD_PAD] = qh.astype(BF16)

    c0 = D_SSM + Q_LORA
    ckv = p[:, c0:c0 + KV_LORA]
    ckvn = _rms(ckv, gkv_ref[...]).astype(BF16)
    kv = jnp.dot(ckvn, wkv_ref[...], preferred_element_type=F32)
    v_ref[...] = (kv[:, QK_PAD:] + vones_ref[...]).astype(BF16)
    kr = p[:, c0 + KV_LORA:c0 + KV_LORA + HEAD_PAD]
    kr_rot = p[:, c0 + KV_LORA + HEAD_PAD:c0 + KV_LORA + 2 * HEAD_PAD]
    ss_r = jnp.sum(kr * kr, axis=-1, keepdims=True)
    t1k = ctab * hgk_ref[0:1, :]
    kr_part = kr_rot * (stab * hgk_ref[1:2, :])
    for h in range(N_HEADS):
        kn = kv[:, h * HEAD_PAD:(h + 1) * HEAD_PAD]
        ss = jnp.sum(kn * kn, axis=-1, keepdims=True) + ss_r
        r = lax.rsqrt(ss * (1.0 / QK_DIM) + EPS)
        kh = ((kn + kr) * t1k + kr_part) * r
        k_ref[:, h * HEAD_PAD:(h + 1) * HEAD_PAD] = kh.astype(BF16)


def _const_spec(shape):
    nd = len(shape)
    return pl.BlockSpec(shape, lambda *_: (0,) * nd)


def _proj_call(x2d, tm, tabs, n_tab_blocks, consts, to_bf16=()):
    n_rows = x2d.shape[0]
    n_steps = n_rows // tm
    row = lambda i: (i, 0)
    tab = lambda i: (i % n_tab_blocks, 0)
    cast_in, cast_out = [], []
    for w in to_bf16:
        _, w_rows, w_cols = w.shape
        rep = next(r for r in (1, 2, 4, 8) if (w_rows * r) % (16 * n_steps) == 0)
        blk_rows = w_rows * rep // n_steps
        cast_in.append(pl.BlockSpec((None, blk_rows, w_cols),
                                    functools.partial(lambda rep, i: (0, i // rep, 0), rep)))
        cast_out.append(pl.BlockSpec((blk_rows, w_cols),
                                     functools.partial(lambda rep, i: (i // rep, 0), rep)))
    in_specs = ([pl.BlockSpec((tm, D_MODEL), row)] + [_const_spec(c.shape) for c in consts]
                + [pl.BlockSpec((tm, HEAD_PAD), tab)] * len(tabs) + cast_in)
    out_shape = (
        jax.ShapeDtypeStruct((n_rows, D_SSM), F32),
        jax.ShapeDtypeStruct((n_rows, QK_PAD), BF16),
        jax.ShapeDtypeStruct((n_rows, QK_PAD), BF16),
        jax.ShapeDtypeStruct((n_rows, QK_PAD), BF16),
    ) + tuple(jax.ShapeDtypeStruct(w.shape[1:], BF16) for w in to_bf16)
    out_specs = (
        pl.BlockSpec((tm, D_SSM), row),
        pl.BlockSpec((tm, QK_PAD), row),
        pl.BlockSpec((tm, QK_PAD), row),
        pl.BlockSpec((tm, QK_PAD), row),
    ) + tuple(cast_out)
    return pl.pallas_call(
        _proj_kernel, out_shape=out_shape, grid=(n_steps,), in_specs=in_specs,
        out_specs=out_specs,
        compiler_params=pltpu.CompilerParams(dimension_semantics=("arbitrary",),
                                             vmem_limit_bytes=VMEM_LIMIT),
        name="proj_mla",
    )(x2d, *consts, *tabs, *to_bf16)


def _s5_kernel(u_ref, um_ref, bin_ref, cout_ref, tin_ref, tout_ref, ar_ref, ai_ref,
               d_ref, wglu_ref, bglu_ref, g_ref, o_ref, xs_ref, h_ref, bmap_ref, cmap_ref, *, batch):
    j = pl.program_id(0)
    slabs_per_half = N_SLABS // 2
    half_ch, half_st = D_SSM // 2, N_STATE_COLS // 2

    def build_maps():
        def diag(shape, row_block, col_block):
            return (lax.broadcasted_iota(jnp.int32, shape, 0) // row_block
                    == lax.broadcasted_iota(jnp.int32, shape, 1) // col_block)
        diag_in = diag((half_ch, half_st), SSM_GROUP, SSM_STATE)
        diag_out = diag((half_st, half_ch), SSM_STATE, SSM_GROUP)
        for part in range(2):
            for half in range(2):
                rows_in = bin_ref[part, half * half_ch:(half + 1) * half_ch, :].astype(BF16)
                tiled = jnp.dot(rows_in, tin_ref[...], preferred_element_type=F32)
                bmap_ref[part, half] = jnp.where(diag_in, tiled, 0.0).astype(BF16)
                rows_out = cout_ref[part, half * half_st:(half + 1) * half_st, :].astype(BF16)
                tiled = jnp.dot(rows_out, tout_ref[...], preferred_element_type=F32)
                cmap_ref[part, half] = jnp.where(diag_out, tiled, 0.0).astype(BF16)

    def project_in(ub, rows):
        for kh in range(2):
            lhs = ub[:, kh * half_ch:(kh + 1) * half_ch]
            xre = jnp.dot(lhs, bmap_ref[0, kh], preferred_element_type=F32)
            xim = jnp.dot(lhs, bmap_ref[1, kh], preferred_element_type=F32)
            for cl in range(slabs_per_half):
                c = kh * slabs_per_half + cl
                for b in range(batch):
                    xs_ref[c, pl.ds(b * S5_PITCH, rows), :] = (
                        xre[b * rows:(b + 1) * rows, cl * LANES:(cl + 1) * LANES])
                    xs_ref[c, pl.ds((batch + b) * S5_PITCH, rows), :] = (
                        xim[b * rows:(b + 1) * rows, cl * LANES:(cl + 1) * LANES])

    def scan(n_steps):
        def body(t, hs):
            new = []
            for c in range(N_SLABS):
                rows = pl.ds(t, 2 * batch, stride=S5_PITCH)
                x8 = xs_ref[c, rows, :]
                h = hs[c]
                hn = ar_ref[c] * h + ai_ref[c] * pltpu.roll(h, batch, 0) + x8
                xs_ref[c, rows, :] = hn
                new.append(hn)
            return tuple(new)

        hs = tuple(h_ref[c] for c in range(N_SLABS))
        hs = lax.fori_loop(0, n_steps, body, hs, unroll=4)
        for c in range(N_SLABS):
            h_ref[c] = hs[c]

    @pl.when(j == 0)
    def _():
        build_maps()
        h_ref[...] = jnp.zeros_like(h_ref)
        um = um_ref[...].astype(BF16)
        project_in(jnp.concatenate([um] * batch, axis=0), N_META)
        scan(N_META)

    uf = u_ref[...].reshape(batch * S5_CHUNK, D_SSM)
    project_in(uf.astype(BF16), S5_CHUNK)
    scan(S5_CHUNK)

    ys = []
    for nh in range(2):
        def gather(plane0):
            return jnp.concatenate(
                [jnp.concatenate(
                    [xs_ref[nh * slabs_per_half + cl, pl.ds((plane0 + b) * S5_PITCH, S5_CHUNK), :]
                     for cl in range(slabs_per_half)], axis=1)
                 for b in range(batch)], axis=0).astype(BF16)
        yre = jnp.dot(gather(0), cmap_ref[0, nh], preferred_element_type=F32)
        yim = jnp.dot(gather(batch), cmap_ref[1, nh], preferred_element_type=F32)
        ys.append(yre - yim)
    y = jnp.concatenate(ys, axis=1) + d_ref[...] * uf
    z = 0.5 * y * (1.0 + jnp.tanh(math.sqrt(2.0 / math.pi) * (y + 0.044715 * (y * y * y))))
    gate = jnp.dot(z.astype(BF16), wglu_ref[...], preferred_element_type=F32) + bglu_ref[...]
    out = z * (1.0 / (1.0 + jnp.exp(-gate)))
    o_ref[...] = _rms(out, g_ref[...]).astype(BF16).reshape(batch, S5_CHUNK, D_SSM)


def _s5_call(u3, u_meta, consts):
    batch, seq, _ = u3.shape
    assert 2 * batch == SUBLANES and seq % S5_CHUNK == 0
    grid = (seq // S5_CHUNK,)
    in_specs = [pl.BlockSpec((batch, S5_CHUNK, D_SSM), lambda j: (0, j, 0)),
                _const_spec(u_meta.shape)] + [_const_spec(c.shape) for c in consts]
    return pl.pallas_call(
        functools.partial(_s5_kernel, batch=batch),
        out_shape=jax.ShapeDtypeStruct((batch, seq, D_SSM), BF16),
        grid=grid, in_specs=in_specs,
        out_specs=pl.BlockSpec((batch, S5_CHUNK, D_SSM), lambda j: (0, j, 0)),
        scratch_shapes=[pltpu.VMEM((N_SLABS, 2 * batch * S5_PITCH, LANES), F32),
                        pltpu.VMEM((N_SLABS, 2 * batch, LANES), F32),
                        pltpu.VMEM((2, 2, D_SSM // 2, N_STATE_COLS // 2), BF16),
                        pltpu.VMEM((2, 2, N_STATE_COLS // 2, D_SSM // 2), BF16)],
        compiler_params=pltpu.CompilerParams(dimension_semantics=("arbitrary",),
                                             vmem_limit_bytes=VMEM_LIMIT),
        name="s5_mixer",
    )(u3, u_meta, *consts)


ATT_TQ = 1024
ATT_TK = 1024
ATT_SUB = 256
ATT_HEADS = 4


def _attn_kernel(q_ref, k_ref, v_ref, km_ref, vm_ref, mask_ref, o_ref, acc_ref, m_ref, *, online):
    nt = (((1,), (1,)), ((), ()))
    n_q = q_ref.shape[0] // ATT_TQ
    head_lanes = [slice(h * HEAD_PAD, (h + 1) * HEAD_PAD) for h in range(ATT_HEADS)]

    def step(h, sub, q, kblk, vblk, mask):
        s = lax.dot_general(q, kblk, nt, preferred_element_type=F32)
        if online:
            if mask is not None:
                s = jnp.where(mask > 0, s, -jnp.inf)
            m = m_ref[h, sub]
            m_new = jnp.maximum(m, jnp.max(s, axis=-1, keepdims=True))
            p = jnp.exp2(s - m_new).astype(BF16)
            acc_ref[h, sub] = (jnp.exp2(m - m_new) * acc_ref[h, sub]
                               + jnp.dot(p, vblk, preferred_element_type=F32))
            m_ref[h, sub] = m_new
        else:
            p = jnp.exp2(s).astype(BF16)
            if mask is not None:
                p = p * mask
            acc_ref[h, sub] += jnp.dot(p, vblk, preferred_element_type=F32)

    def q_tile(qi, _):
        q0 = pl.multiple_of(qi * ATT_TQ, ATT_TQ)
        rows = pl.ds(q0, ATT_TQ)
        qs = [q_ref[rows, hl] for hl in head_lanes]
        acc_ref[...] = jnp.zeros_like(acc_ref)
        if online:
            m_ref[...] = jnp.full(m_ref.shape, -1e30, F32)

        def body(kb, _):
            krows = pl.ds(pl.multiple_of(kb * ATT_TK, ATT_TK), ATT_TK)
            for h, hl in enumerate(head_lanes):
                step(h, slice(None), qs[h], k_ref[krows, hl], v_ref[krows, hl], None)
            return 0

        lax.fori_loop(0, qi * (ATT_TQ // ATT_TK), body, 0)

        for i in range(ATT_TQ // ATT_SUB):
            sub = slice(i * ATT_SUB, ATT_TQ)
            krows = pl.ds(q0 + i * ATT_SUB, ATT_SUB)
            for h, hl in enumerate(head_lanes):
                kblk, vblk = k_ref[krows, hl], v_ref[krows, hl]
                mask = mask_ref[sub, LANES + i * ATT_SUB:LANES + (i + 1) * ATT_SUB]
                if i == 0:
                    kblk = jnp.concatenate([km_ref[:, hl], kblk], axis=0)
                    vblk = jnp.concatenate([vm_ref[:, hl], vblk], axis=0)
                    mask = mask_ref[sub, :LANES + ATT_SUB]
                step(h, sub, qs[h][sub], kblk, vblk, mask)
        lane = lax.broadcasted_iota(jnp.int32, (ATT_TQ, HEAD_PAD), 1)
        for hp in range(ATT_HEADS // 2):
            even, odd = acc_ref[2 * hp], acc_ref[2 * hp + 1]
            o_even = even * (1.0 / even[:, V_DIM:V_DIM + 1])
            o_odd = odd * (1.0 / odd[:, 0:1])
            o_ref[rows, hp * HEAD_PAD:(hp + 1) * HEAD_PAD] = (
                jnp.where(lane < V_DIM, o_even, o_odd).astype(BF16))
        return 0

    lax.fori_loop(0, n_q, q_tile, 0)


def _attn_call(q3, k3, v3, k_meta, v_meta, mask, *, online):
    batch, seq, _ = q3.shape
    grid = (batch, N_HEADS // ATT_HEADS)
    seq_blk = pl.BlockSpec((None, seq, ATT_HEADS * HEAD_PAD), lambda b, hg: (b, 0, hg))
    meta_blk = pl.BlockSpec((LANES, ATT_HEADS * HEAD_PAD), lambda b, hg: (0, hg))
    return pl.pallas_call(
        functools.partial(_attn_kernel, online=online),
        out_shape=jax.ShapeDtypeStruct((batch, seq, N_HEADS * V_DIM), BF16),
        grid=grid,
        in_specs=[seq_blk, seq_blk, seq_blk, meta_blk, meta_blk, _const_spec(mask.shape)],
        out_specs=pl.BlockSpec((None, seq, ATT_HEADS * V_DIM), lambda b, hg: (b, 0, hg)),
        scratch_shapes=[pltpu.VMEM((ATT_HEADS, ATT_TQ, HEAD_PAD), F32),
                        pltpu.VMEM((ATT_HEADS, ATT_TQ, 1), F32)],
        compiler_params=pltpu.CompilerParams(dimension_semantics=("parallel", "parallel"),
                                             vmem_limit_bytes=VMEM_LIMIT),
        name="mla_attention_online" if online else "mla_attention",
    )(q3, k3, v3, k_meta, v_meta, mask)


FFN_TM = 512


def _ffn_kernel(x_ref, ms_ref, oa_ref, gatt_ref, wout_ref, gffn_ref, wg_ref, wu_ref, wd_ref,
                out_ref):
    ya = _rms(oa_ref[...].astype(F32), gatt_ref[...]).astype(BF16)
    mixed = jnp.concatenate([ms_ref[...], ya], axis=1)
    h1 = x_ref[...] + jnp.dot(mixed, wout_ref[...], preferred_element_type=F32)
    hn = _rms(h1, gffn_ref[...]).astype(BF16)
    g = jnp.dot(hn, wg_ref[...], preferred_element_type=F32)
    u = jnp.dot(hn, wu_ref[...], preferred_element_type=F32)
    a = (g * (1.0 / (1.0 + jnp.exp(-g))) * u).astype(BF16)
    out_ref[...] = h1 + jnp.dot(a, wd_ref[...], preferred_element_type=F32)


def _ffn_call(x2d, ms2d, oa2d, consts):
    n_rows = x2d.shape[0]
    tm = FFN_TM
    row = lambda i: (i, 0)
    once = pl.Buffered(1)
    in_specs = [pl.BlockSpec((tm, D_MODEL), row),
                pl.BlockSpec((tm, D_SSM), row),
                pl.BlockSpec((tm, N_HEADS * V_DIM), row)]
    in_specs += [pl.BlockSpec(c.shape, lambda i: (0, 0), pipeline_mode=once) for c in consts]
    return pl.pallas_call(
        _ffn_kernel,
        out_shape=jax.ShapeDtypeStruct((n_rows, D_MODEL), F32),
        grid=(n_rows // tm,), in_specs=in_specs,
        out_specs=pl.BlockSpec((tm, D_MODEL), row),
        compiler_params=pltpu.CompilerParams(dimension_semantics=("parallel",),
                                             vmem_limit_bytes=VMEM_LIMIT),
        name="outproj_ffn",
    )(x2d, ms2d, oa2d, *consts)


def _rope_tables(first, count):
    pos = np.arange(first, first + count, dtype=np.float64)
    inv_freq = 1.0 / (ROPE_BASE ** (np.arange(0, ROPE, 2, dtype=np.float64) / ROPE))
    ang = pos[:, None] * inv_freq[None, :]
    cos, sin = np.cos(ang), np.sin(ang)
    ctab = np.zeros((count, HEAD_PAD), np.float32)
    stab = np.zeros((count, HEAD_PAD), np.float32)
    ctab[:, :NOPE] = 1.0
    ctab[:, NOPE:QK_DIM] = np.concatenate([cos, cos], axis=1)
    stab[:, NOPE:QK_DIM] = np.concatenate([sin, sin], axis=1)
    return jnp.asarray(ctab), jnp.asarray(stab)


def _head_gains(gain):
    g_r = gain[NOPE:]
    g_r_swapped = jnp.concatenate([g_r[HALF_ROPE:], g_r[:HALF_ROPE]])
    pad = jnp.zeros((HEAD_PAD - QK_DIM,), F32)
    return jnp.stack([jnp.concatenate([gain, pad]),
                      jnp.concatenate([jnp.zeros((NOPE,), F32), g_r_swapped, pad])])


def _rot_half_cols(w):
    return jnp.concatenate([-w[..., HALF_ROPE:], w[..., :HALF_ROPE]], axis=-1)


def _pad_cols(w, left, total):
    return jnp.pad(w, ((0, 0), (left, total - left - w.shape[1])))


def kernel(x, meta_tokens, mix_norm_g, w_in, ssm_a_re, ssm_a_im, ssm_log_dt, ssm_b_re, ssm_b_im,
           ssm_c_re, ssm_c_im, ssm_d, ssm_w_glu, ssm_b_glu, q_lora_norm_g, w_uq, kv_lora_norm_g,
           w_uk, w_uv, q_head_norm_g, k_head_norm_g, ssm_out_norm_g, att_out_norm_g, w_out,
           ffn_norm_g, w_gate, w_up, w_down):
    batch, seq, _ = x.shape
    depth = w_in.shape[0]
    assert depth == 1
    l = 0

    wi = w_in[l]
    o_r = D_SSM + Q_LORA + KV_LORA
    w_r = wi[:, o_r:]
    win = jnp.concatenate([wi[:, :o_r], _pad_cols(w_r, NOPE, HEAD_PAD),
                           _pad_cols(_rot_half_cols(w_r), NOPE, HEAD_PAD)], axis=1).astype(BF16)
    wq3 = w_uq[l].reshape(Q_LORA, N_HEADS, QK_DIM)
    q1 = jnp.pad(wq3, ((0, 0), (0, 0), (0, HEAD_PAD - QK_DIM)))
    q2 = jnp.pad(_rot_half_cols(wq3[..., NOPE:]), ((0, 0), (0, 0), (NOPE, HEAD_PAD - QK_DIM)))
    wq = jnp.concatenate([q1.reshape(Q_LORA, QK_PAD), q2.reshape(Q_LORA, QK_PAD)], axis=1).astype(BF16)
    wk3 = jnp.pad(w_uk[l].reshape(KV_LORA, N_HEADS, NOPE), ((0, 0), (0, 0), (0, HEAD_PAD - NOPE)))
    wv4 = w_uv[l].reshape(KV_LORA, N_HEADS // 2, 2, V_DIM)
    zv = jnp.zeros_like(wv4[:, :, 0])
    wv = jnp.stack([jnp.concatenate([wv4[:, :, 0], zv], axis=-1),
                    jnp.concatenate([zv, wv4[:, :, 1]], axis=-1)], axis=2).reshape(KV_LORA, QK_PAD)
    ones_col = jnp.zeros((2, HEAD_PAD), F32).at[0, V_DIM].set(1.0).at[1, 0].set(1.0)
    vones = jnp.tile(ones_col.reshape(1, 2 * HEAD_PAD), (1, N_HEADS // 2))
    wkv = jnp.concatenate([wk3.reshape(KV_LORA, QK_PAD), wv], axis=1).astype(BF16)
    tabs_m = _rope_tables(0, N_META)
    tabs_f = _rope_tables(N_META, seq)
    proj_consts = (mix_norm_g[l][None], win, q_lora_norm_g[l][None], wq,
                   kv_lora_norm_g[l][None], wkv, vones,
                   _head_gains(q_head_norm_g[l]), _head_gains(k_head_norm_g[l]))

    tm = PROJ_TM
    ffn_f32 = (w_out, w_gate, w_up, w_down)
    u2, q2d, k2d, v2d, wout_b, wg_b, wu_b, wd_b = _proj_call(
        x.reshape(batch * seq, D_MODEL), tm, tabs_f, seq // tm, proj_consts, ffn_f32)
    u_meta, _, k_meta, v_meta = _proj_call(meta_tokens, N_META, tabs_m, 1, proj_consts)

    dt = jnp.exp(ssm_log_dt[l])[:, None]
    lr, li = ssm_a_re[l], ssm_a_im[l]
    mag = jnp.exp(lr * dt)
    ar = mag * jnp.cos(li * dt)
    ai = mag * jnp.sin(li * dt)
    den = lr * lr + li * li
    fr = ((ar - 1.0) * lr + ai * li) / den
    fi = (ai * lr - (ar - 1.0) * li) / den
    br, bi = ssm_b_re[l], ssm_b_im[l]
    bbr = fr[..., None] * br - fi[..., None] * bi
    bbi = fr[..., None] * bi + fi[..., None] * br
    b_in = jnp.swapaxes(jnp.stack([bbr, bbi]), 2, 3).reshape(2, D_SSM, SSM_STATE)
    c_out = jnp.swapaxes(jnp.stack([ssm_c_re[l], ssm_c_im[l]]), 2, 3)
    c_out = jnp.pad(c_out.reshape(2, N_STATE_COLS, SSM_GROUP),
                    ((0, 0), (0, 0), (0, LANES - SSM_GROUP)))
    half_groups = N_GROUPS // 2
    tile_in = np.tile(np.eye(SSM_STATE, dtype=np.float32), (1, half_groups))
    tile_out = np.zeros((LANES, half_groups * SSM_GROUP), np.float32)
    tile_out[:SSM_GROUP] = np.tile(np.eye(SSM_GROUP, dtype=np.float32), (1, half_groups))

    ar_rows = jnp.broadcast_to(ar.reshape(N_SLABS, 1, LANES), (N_SLABS, 2 * batch, LANES))
    ai_flat = ai.reshape(N_SLABS, 1, LANES)
    ai_rows = jnp.concatenate([jnp.broadcast_to(-ai_flat, (N_SLABS, batch, LANES)),
                               jnp.broadcast_to(ai_flat, (N_SLABS, batch, LANES))], axis=1)
    s5_consts = (b_in, c_out, jnp.asarray(tile_in, BF16), jnp.asarray(tile_out, BF16),
                 ar_rows, ai_rows, ssm_d[l][None], ssm_w_glu[l].astype(BF16),
                 ssm_b_glu[l][None], ssm_out_norm_g[l][None])
    mixed_ssm = _s5_call(u2.reshape(batch, seq, D_SSM), u_meta, s5_consts)

    pad_rows = ((0, LANES - N_META), (0, 0))
    chunk_of = np.arange(ATT_TQ) // CHUNK
    causal = (chunk_of[None, :] <= chunk_of[:, None]).astype(np.float32)
    mask = jnp.asarray(np.concatenate([np.ones((ATT_TQ, LANES), np.float32), causal], axis=1),
                       dtype=BF16)
    score_bound = (LOG2_E * math.sqrt(QK_DIM) * jnp.max(jnp.abs(q_head_norm_g[l]))
                   * jnp.max(jnp.abs(k_head_norm_g[l])))
    attn_args = (q2d.reshape(batch, seq, QK_PAD), k2d.reshape(batch, seq, QK_PAD),
                 v2d.reshape(batch, seq, QK_PAD),
                 jnp.pad(k_meta, pad_rows), jnp.pad(v_meta, pad_rows), mask)
    y_att = lax.cond(score_bound <= MAX_UNSHIFTED_LOG2_SCORE,
                     functools.partial(_attn_call, online=False),
                     functools.partial(_attn_call, online=True), *attn_args)

    ffn_consts = (att_out_norm_g[l][None], wout_b, ffn_norm_g[l][None],
                  wg_b, wu_b, wd_b)
    out = _ffn_call(x.reshape(batch * seq, D_MODEL), mixed_ssm.reshape(batch * seq, D_SSM),
                    y_att.reshape(batch * seq, N_HEADS * V_DIM), ffn_consts)
    return out.reshape(batch, seq, D_MODEL)
```

```python
import functools
import math

import jax
import jax.numpy as jnp
import numpy as np
from jax import lax
from jax.experimental import pallas as pl
from jax.experimental.pallas import tpu as pltpu

F32 = jnp.float32
BF16 = jnp.bfloat16

D_MODEL = 1024
N_META = 16
CHUNK = 64
D_SSM = 512
SSM_GROUP = 16
N_GROUPS = D_SSM // SSM_GROUP
SSM_STATE = 64
N_HEADS = 8
V_DIM = 64
NOPE = 64
ROPE = 32
HALF_ROPE = ROPE // 2
QK_DIM = NOPE + ROPE
Q_LORA = 256
KV_LORA = 128
D_FF = 2816
ROPE_BASE = 10000.0
EPS = 1e-6
LOG2_E = math.log2(math.e)
MAX_UNSHIFTED_LOG2_SCORE = 40.0

LANES = 128
SUBLANES = 8
HEAD_PAD = LANES
QK_PAD = N_HEADS * HEAD_PAD
N_STATE_COLS = N_GROUPS * SSM_STATE
N_SLABS = N_STATE_COLS // LANES
S5_CHUNK = 128
S5_PITCH = S5_CHUNK + SUBLANES
PROJ_TM = 512
VMEM_LIMIT = 56 * 1024 * 1024


def _rms(x, g):
    return x * lax.rsqrt(jnp.mean(x * x, axis=-1, keepdims=True) + EPS) * g


def _proj_kernel(x_ref, xm_ref, gmix_ref, win_ref, gq_ref, wq_ref, gkv_ref, wkv_ref, vones_ref,
                 hgq_ref, hgk_ref, ctab_ref, stab_ref, ctabm_ref, stabm_ref, *rest):
    n_cast = (len(rest) - 7) // 2
    u_ref, q_ref, k_ref, v_ref, um_ref, km_ref, vm_ref = rest[n_cast:n_cast + 7]
    for src_ref, dst_ref in zip(rest[:n_cast], rest[n_cast + 7:]):
        dst_ref[...] = src_ref[...].astype(BF16)
    scale = QK_DIM ** -0.5 * LOG2_E
    c0 = D_SSM + Q_LORA

    def project(x, ctab, stab, rows, u_out, q_out, k_out, v_out):
        xn = _rms(x, gmix_ref[...]).astype(BF16)
        p = jnp.dot(xn, win_ref[...], preferred_element_type=F32)
        u_out[...] = p[:, :D_SSM]

        if q_out is not None:
            cqn = _rms(p[:, D_SSM:c0], gq_ref[...]).astype(BF16)
            q12 = jnp.dot(cqn, wq_ref[...], preferred_element_type=F32)
            t1q, t2q = ctab * hgq_ref[0:1, :], stab * hgq_ref[1:2, :]
            for h in range(N_HEADS):
                q1 = q12[:, h * HEAD_PAD:(h + 1) * HEAD_PAD]
                q2 = q12[:, QK_PAD + h * HEAD_PAD:QK_PAD + (h + 1) * HEAD_PAD]
                r = lax.rsqrt(jnp.sum(q1 * q1, axis=-1, keepdims=True) * (1.0 / QK_DIM) + EPS)
                qh = (q1 * t1q + q2 * t2q) * (r * scale)
                q_out[:, h * HEAD_PAD:(h + 1) * HEAD_PAD] = qh.astype(BF16)

        ckvn = _rms(p[:, c0:c0 + KV_LORA], gkv_ref[...]).astype(BF16)
        kv = jnp.dot(ckvn, wkv_ref[...], preferred_element_type=F32)
        v_out[rows, :] = (kv[:, QK_PAD:] + vones_ref[...]).astype(BF16)
        kr = p[:, c0 + KV_LORA:c0 + KV_LORA + HEAD_PAD]
        kr_rot = p[:, c0 + KV_LORA + HEAD_PAD:c0 + KV_LORA + 2 * HEAD_PAD]
        ss_r = jnp.sum(kr * kr, axis=-1, keepdims=True)
        t1k = ctab * hgk_ref[0:1, :]
        kr_part = kr_rot * (stab * hgk_ref[1:2, :])
        for h in range(N_HEADS):
            kn = kv[:, h * HEAD_PAD:(h + 1) * HEAD_PAD]
            ss = jnp.sum(kn * kn, axis=-1, keepdims=True) + ss_r
            r = lax.rsqrt(ss * (1.0 / QK_DIM) + EPS)
            kh = ((kn + kr) * t1k + kr_part) * r
            k_out[rows, h * HEAD_PAD:(h + 1) * HEAD_PAD] = kh.astype(BF16)

    @pl.when(pl.program_id(0) == 0)
    def _():
        km_ref[...] = jnp.zeros_like(km_ref)
        vm_ref[...] = jnp.zeros_like(vm_ref)
        project(xm_ref[...], ctabm_ref[...], stabm_ref[...], slice(0, N_META),
                um_ref, None, km_ref, vm_ref)

    project(x_ref[...], ctab_ref[...], stab_ref[...], slice(None), u_ref, q_ref, k_ref, v_ref)


def _const_spec(shape):
    nd = len(shape)
    return pl.BlockSpec(shape, lambda *_: (0,) * nd)


def _proj_call(x2d, x_meta, tm, tabs, tabs_meta, n_tab_blocks, consts, to_bf16=()):
    n_rows = x2d.shape[0]
    n_steps = n_rows // tm
    row = lambda i: (i, 0)
    tab = lambda i: (i % n_tab_blocks, 0)
    meta_rows = (LANES, QK_PAD)
    cast_in, cast_out = [], []
    for w in to_bf16:
        _, w_rows, w_cols = w.shape
        rep = next(r for r in (1, 2, 4, 8) if (w_rows * r) % (16 * n_steps) == 0)
        blk_rows = w_rows * rep // n_steps
        cast_in.append(pl.BlockSpec((None, blk_rows, w_cols),
                                    functools.partial(lambda rep, i: (0, i // rep, 0), rep)))
        cast_out.append(pl.BlockSpec((blk_rows, w_cols),
                                     functools.partial(lambda rep, i: (i // rep, 0), rep)))
    in_specs = ([pl.BlockSpec((tm, D_MODEL), row), _const_spec(x_meta.shape)]
                + [_const_spec(c.shape) for c in consts]
                + [pl.BlockSpec((tm, HEAD_PAD), tab)] * len(tabs)
                + [_const_spec(t.shape) for t in tabs_meta] + cast_in)
    out_shape = (
        jax.ShapeDtypeStruct((n_rows, D_SSM), F32),
        jax.ShapeDtypeStruct((n_rows, QK_PAD), BF16),
        jax.ShapeDtypeStruct((n_rows, QK_PAD), BF16),
        jax.ShapeDtypeStruct((n_rows, QK_PAD), BF16),
        jax.ShapeDtypeStruct((x_meta.shape[0], D_SSM), F32),
        jax.ShapeDtypeStruct(meta_rows, BF16),
        jax.ShapeDtypeStruct(meta_rows, BF16),
    ) + tuple(jax.ShapeDtypeStruct(w.shape[1:], BF16) for w in to_bf16)
    out_specs = (
        pl.BlockSpec((tm, D_SSM), row),
        pl.BlockSpec((tm, QK_PAD), row),
        pl.BlockSpec((tm, QK_PAD), row),
        pl.BlockSpec((tm, QK_PAD), row),
        _const_spec((x_meta.shape[0], D_SSM)),
        _const_spec(meta_rows),
        _const_spec(meta_rows),
    ) + tuple(cast_out)
    return pl.pallas_call(
        _proj_kernel, out_shape=out_shape, grid=(n_steps,), in_specs=in_specs,
        out_specs=out_specs,
        compiler_params=pltpu.CompilerParams(dimension_semantics=("arbitrary",),
                                             vmem_limit_bytes=VMEM_LIMIT),
        name="proj_mla",
    )(x2d, x_meta, *consts, *tabs, *tabs_meta, *to_bf16)


def _s5_kernel(u_ref, um_ref, bin_ref, cout_ref, tin_ref, tout_ref, ar_ref, ai_ref,
               d_ref, wglu_ref, bglu_ref, g_ref, o_ref, xs_ref, h_ref, bmap_ref, cmap_ref, *, batch):
    j = pl.program_id(0)
    slabs_per_half = N_SLABS // 2
    half_ch, half_st = D_SSM // 2, N_STATE_COLS // 2

    def build_maps():
        def diag(shape, row_block, col_block):
            return (lax.broadcasted_iota(jnp.int32, shape, 0) // row_block
                    == lax.broadcasted_iota(jnp.int32, shape, 1) // col_block)
        diag_in = diag((half_ch, half_st), SSM_GROUP, SSM_STATE)
        diag_out = diag((half_st, half_ch), SSM_STATE, SSM_GROUP)
        for part in range(2):
            for half in range(2):
                rows_in = bin_ref[part, half * half_ch:(half + 1) * half_ch, :].astype(BF16)
                tiled = jnp.dot(rows_in, tin_ref[...], preferred_element_type=F32)
                bmap_ref[part, half] = jnp.where(diag_in, tiled, 0.0).astype(BF16)
                rows_out = cout_ref[part, half * half_st:(half + 1) * half_st, :].astype(BF16)
                tiled = jnp.dot(rows_out, tout_ref[...], preferred_element_type=F32)
                cmap_ref[part, half] = jnp.where(diag_out, tiled, 0.0).astype(BF16)

    def project_in(ub, rows):
        for kh in range(2):
            lhs = ub[:, kh * half_ch:(kh + 1) * half_ch]
            xre = jnp.dot(lhs, bmap_ref[0, kh], preferred_element_type=F32)
            xim = jnp.dot(lhs, bmap_ref[1, kh], preferred_element_type=F32)
            for cl in range(slabs_per_half):
                c = kh * slabs_per_half + cl
                for b in range(batch):
                    xs_ref[c, pl.ds(b * S5_PITCH, rows), :] = (
                        xre[b * rows:(b + 1) * rows, cl * LANES:(cl + 1) * LANES])
                    xs_ref[c, pl.ds((batch + b) * S5_PITCH, rows), :] = (
                        xim[b * rows:(b + 1) * rows, cl * LANES:(cl + 1) * LANES])

    def scan(n_steps):
        def body(t, hs):
            new = []
            for c in range(N_SLABS):
                rows = pl.ds(t, 2 * batch, stride=S5_PITCH)
                x8 = xs_ref[c, rows, :]
                h = hs[c]
                hn = ar_ref[c] * h + ai_ref[c] * pltpu.roll(h, batch, 0) + x8
                xs_ref[c, rows, :] = hn
                new.append(hn)
            return tuple(new)

        hs = tuple(h_ref[c] for c in range(N_SLABS))
        hs = lax.fori_loop(0, n_steps, body, hs, unroll=4)
        for c in range(N_SLABS):
            h_ref[c] = hs[c]

    @pl.when(j == 0)
    def _():
        build_maps()
        h_ref[...] = jnp.zeros_like(h_ref)
        um = um_ref[...].astype(BF16)
        project_in(jnp.concatenate([um] * batch, axis=0), N_META)
        scan(N_META)

    uf = u_ref[...].reshape(batch * S5_CHUNK, D_SSM)
    project_in(uf.astype(BF16), S5_CHUNK)
    scan(S5_CHUNK)

    ys = []
    for nh in range(2):
        def gather(plane0):
            return jnp.concatenate(
                [jnp.concatenate(
                    [xs_ref[nh * slabs_per_half + cl, pl.ds((plane0 + b) * S5_PITCH, S5_CHUNK), :]
                     for cl in range(slabs_per_half)], axis=1)
                 for b in range(batch)], axis=0).astype(BF16)
        yre = jnp.dot(gather(0), cmap_ref[0, nh], preferred_element_type=F32)
        yim = jnp.dot(gather(batch), cmap_ref[1, nh], preferred_element_type=F32)
        ys.append(yre - yim)
    y = jnp.concatenate(ys, axis=1) + d_ref[...] * uf
    z = 0.5 * y * (1.0 + jnp.tanh(math.sqrt(2.0 / math.pi) * (y + 0.044715 * (y * y * y))))
    gate = jnp.dot(z.astype(BF16), wglu_ref[...], preferred_element_type=F32) + bglu_ref[...]
    out = z * (1.0 / (1.0 + jnp.exp(-gate)))
    o_ref[...] = _rms(out, g_ref[...]).astype(BF16).reshape(batch, S5_CHUNK, D_SSM)


def _s5_call(u3, u_meta, consts):
    batch, seq, _ = u3.shape
    assert 2 * batch == SUBLANES and seq % S5_CHUNK == 0
    grid = (seq // S5_CHUNK,)
    in_specs = [pl.BlockSpec((batch, S5_CHUNK, D_SSM), lambda j: (0, j, 0)),
                _const_spec(u_meta.shape)] + [_const_spec(c.shape) for c in consts]
    return pl.pallas_call(
        functools.partial(_s5_kernel, batch=batch),
        out_shape=jax.ShapeDtypeStruct((batch, seq, D_SSM), BF16),
        grid=grid, in_specs=in_specs,
        out_specs=pl.BlockSpec((batch, S5_CHUNK, D_SSM), lambda j: (0, j, 0)),
        scratch_shapes=[pltpu.VMEM((N_SLABS, 2 * batch * S5_PITCH, LANES), F32),
                        pltpu.VMEM((N_SLABS, 2 * batch, LANES), F32),
                        pltpu.VMEM((2, 2, D_SSM // 2, N_STATE_COLS // 2), BF16),
                        pltpu.VMEM((2, 2, N_STATE_COLS // 2, D_SSM // 2), BF16)],
        compiler_params=pltpu.CompilerParams(dimension_semantics=("arbitrary",),
                                             vmem_limit_bytes=VMEM_LIMIT),
        name="s5_mixer",
    )(u3, u_meta, *consts)


ATT_TQ = 1024
ATT_TK = 1024
ATT_SUB = 256
ATT_HEADS = 4


def _attn_kernel(q_ref, k_ref, v_ref, km_ref, vm_ref, mask_ref, o_ref, acc_ref, m_ref, *, online):
    nt = (((1,), (1,)), ((), ()))
    n_q = q_ref.shape[0] // ATT_TQ
    head_lanes = [slice(h * HEAD_PAD, (h + 1) * HEAD_PAD) for h in range(ATT_HEADS)]

    def step(h, sub, q, kblk, vblk, mask):
        s = lax.dot_general(q, kblk, nt, preferred_element_type=F32)
        if online:
            if mask is not None:
                s = jnp.where(mask > 0, s, -jnp.inf)
            m = m_ref[h, sub]
            m_new = jnp.maximum(m, jnp.max(s, axis=-1, keepdims=True))
            p = jnp.exp2(s - m_new).astype(BF16)
            acc_ref[h, sub] = (jnp.exp2(m - m_new) * acc_ref[h, sub]
                               + jnp.dot(p, vblk, preferred_element_type=F32))
            m_ref[h, sub] = m_new
        else:
            p = jnp.exp2(s).astype(BF16)
            if mask is not None:
                p = p * mask
            acc_ref[h, sub] += jnp.dot(p, vblk, preferred_element_type=F32)

    def q_tile(qi, _):
        q0 = pl.multiple_of(qi * ATT_TQ, ATT_TQ)
        rows = pl.ds(q0, ATT_TQ)
        qs = [q_ref[rows, hl] for hl in head_lanes]
        acc_ref[...] = jnp.zeros_like(acc_ref)
        if online:
            m_ref[...] = jnp.full(m_ref.shape, -1e30, F32)

        def body(kb, _):
            krows = pl.ds(pl.multiple_of(kb * ATT_TK, ATT_TK), ATT_TK)
            for h, hl in enumerate(head_lanes):
                step(h, slice(None), qs[h], k_ref[krows, hl], v_ref[krows, hl], None)
            return 0

        lax.fori_loop(0, qi * (ATT_TQ // ATT_TK), body, 0)

        for i in range(ATT_TQ // ATT_SUB):
            sub = slice(i * ATT_SUB, ATT_TQ)
            krows = pl.ds(q0 + i * ATT_SUB, ATT_SUB)
            for h, hl in enumerate(head_lanes):
                kblk, vblk = k_ref[krows, hl], v_ref[krows, hl]
                mask = mask_ref[sub, LANES + i * ATT_SUB:LANES + (i + 1) * ATT_SUB]
                if i == 0:
                    kblk = jnp.concatenate([km_ref[:, hl], kblk], axis=0)
                    vblk = jnp.concatenate([vm_ref[:, hl], vblk], axis=0)
                    mask = mask_ref[sub, :LANES + ATT_SUB]
                step(h, sub, qs[h][sub], kblk, vblk, mask)
        lane = lax.broadcasted_iota(jnp.int32, (ATT_TQ, HEAD_PAD), 1)
        for hp in range(ATT_HEADS // 2):
            even, odd = acc_ref[2 * hp], acc_ref[2 * hp + 1]
            o_even = even * (1.0 / even[:, V_DIM:V_DIM + 1])
            o_odd = odd * (1.0 / odd[:, 0:1])
            o_ref[rows, hp * HEAD_PAD:(hp + 1) * HEAD_PAD] = (
                jnp.where(lane < V_DIM, o_even, o_odd).astype(BF16))
        return 0

    lax.fori_loop(0, n_q, q_tile, 0)


def _attn_call(q3, k3, v3, k_meta, v_meta, mask, *, online):
    batch, seq, _ = q3.shape
    grid = (batch, N_HEADS // ATT_HEADS)
    seq_blk = pl.BlockSpec((None, seq, ATT_HEADS * HEAD_PAD), lambda b, hg: (b, 0, hg))
    meta_blk = pl.BlockSpec((LANES, ATT_HEADS * HEAD_PAD), lambda b, hg: (0, hg))
    return pl.pallas_call(
        functools.partial(_attn_kernel, online=online),
        out_shape=jax.ShapeDtypeStruct((batch, seq, N_HEADS * V_DIM), BF16),
        grid=grid,
        in_specs=[seq_blk, seq_blk, seq_blk, meta_blk, meta_blk, _const_spec(mask.shape)],
        out_specs=pl.BlockSpec((None, seq, ATT_HEADS * V_DIM), lambda b, hg: (b, 0, hg)),
        scratch_shapes=[pltpu.VMEM((ATT_HEADS, ATT_TQ, HEAD_PAD), F32),
                        pltpu.VMEM((ATT_HEADS, ATT_TQ, 1), F32)],
        compiler_params=pltpu.CompilerParams(dimension_semantics=("parallel", "parallel"),
                                             vmem_limit_bytes=VMEM_LIMIT),
        name="mla_attention_online" if online else "mla_attention",
    )(q3, k3, v3, k_meta, v_meta, mask)


FFN_TM = 512


def _ffn_kernel(x_ref, ms_ref, oa_ref, gatt_ref, wout_ref, gffn_ref, wg_ref, wu_ref, wd_ref,
                out_ref):
    ya = _rms(oa_ref[...].astype(F32), gatt_ref[...]).astype(BF16)
    mixed = jnp.concatenate([ms_ref[...], ya], axis=1)
    h1 = x_ref[...] + jnp.dot(mixed, wout_ref[...], preferred_element_type=F32)
    hn = _rms(h1, gffn_ref[...]).astype(BF16)
    g = jnp.dot(hn, wg_ref[...], preferred_element_type=F32)
    u = jnp.dot(hn, wu_ref[...], preferred_element_type=F32)
    a = (g * (1.0 / (1.0 + jnp.exp(-g))) * u).astype(BF16)
    out_ref[...] = h1 + jnp.dot(a, wd_ref[...], preferred_element_type=F32)


def _ffn_call(x2d, ms2d, oa2d, consts):
    n_rows = x2d.shape[0]
    tm = FFN_TM
    row = lambda i: (i, 0)
    once = pl.Buffered(1)
    in_specs = [pl.BlockSpec((tm, D_MODEL), row),
                pl.BlockSpec((tm, D_SSM), row),
                pl.BlockSpec((tm, N_HEADS * V_DIM), row)]
    in_specs += [pl.BlockSpec(c.shape, lambda i: (0, 0), pipeline_mode=once) for c in consts]
    return pl.pallas_call(
        _ffn_kernel,
        out_shape=jax.ShapeDtypeStruct((n_rows, D_MODEL), F32),
        grid=(n_rows // tm,), in_specs=in_specs,
        out_specs=pl.BlockSpec((tm, D_MODEL), row),
        compiler_params=pltpu.CompilerParams(dimension_semantics=("parallel",),
                                             vmem_limit_bytes=VMEM_LIMIT),
        name="outproj_ffn",
    )(x2d, ms2d, oa2d, *consts)


def _rope_tables(first, count):
    pos = np.arange(first, first + count, dtype=np.float64)
    inv_freq = 1.0 / (ROPE_BASE ** (np.arange(0, ROPE, 2, dtype=np.float64) / ROPE))
    ang = pos[:, None] * inv_freq[None, :]
    cos, sin = np.cos(ang), np.sin(ang)
    ctab = np.zeros((count, HEAD_PAD), np.float32)
    stab = np.zeros((count, HEAD_PAD), np.float32)
    ctab[:, :NOPE] = 1.0
    ctab[:, NOPE:QK_DIM] = np.concatenate([cos, cos], axis=1)
    stab[:, NOPE:QK_DIM] = np.concatenate([sin, sin], axis=1)
    return jnp.asarray(ctab), jnp.asarray(stab)


def _head_gains(gain):
    g_r = gain[NOPE:]
    g_r_swapped = jnp.concatenate([g_r[HALF_ROPE:], g_r[:HALF_ROPE]])
    pad = jnp.zeros((HEAD_PAD - QK_DIM,), F32)
    return jnp.stack([jnp.concatenate([gain, pad]),
                      jnp.concatenate([jnp.zeros((NOPE,), F32), g_r_swapped, pad])])


def _rot_half_cols(w):
    return jnp.concatenate([-w[..., HALF_ROPE:], w[..., :HALF_ROPE]], axis=-1)


def _pad_cols(w, left, total):
    return jnp.pad(w, ((0, 0), (left, total - left - w.shape[1])))


def kernel(x, meta_tokens, mix_norm_g, w_in, ssm_a_re, ssm_a_im, ssm_log_dt, ssm_b_re, ssm_b_im,
           ssm_c_re, ssm_c_im, ssm_d, ssm_w_glu, ssm_b_glu, q_lora_norm_g, w_uq, kv_lora_norm_g,
           w_uk, w_uv, q_head_norm_g, k_head_norm_g, ssm_out_norm_g, att_out_norm_g, w_out,
           ffn_norm_g, w_gate, w_up, w_down):
    batch, seq, _ = x.shape
    depth = w_in.shape[0]
    assert depth == 1
    l = 0

    wi = w_in[l]
    o_r = D_SSM + Q_LORA + KV_LORA
    w_r = wi[:, o_r:]
    win = jnp.concatenate([wi[:, :o_r], _pad_cols(w_r, NOPE, HEAD_PAD),
                           _pad_cols(_rot_half_cols(w_r), NOPE, HEAD_PAD)], axis=1).astype(BF16)
    wq3 = w_uq[l].reshape(Q_LORA, N_HEADS, QK_DIM)
    q1 = jnp.pad(wq3, ((0, 0), (0, 0), (0, HEAD_PAD - QK_DIM)))
    q2 = jnp.pad(_rot_half_cols(wq3[..., NOPE:]), ((0, 0), (0, 0), (NOPE, HEAD_PAD - QK_DIM)))
    wq = jnp.concatenate([q1.reshape(Q_LORA, QK_PAD), q2.reshape(Q_LORA, QK_PAD)], axis=1).astype(BF16)
    wk3 = jnp.pad(w_uk[l].reshape(KV_LORA, N_HEADS, NOPE), ((0, 0), (0, 0), (0, HEAD_PAD - NOPE)))
    wv4 = w_uv[l].reshape(KV_LORA, N_HEADS // 2, 2, V_DIM)
    zv = jnp.zeros_like(wv4[:, :, 0])
    wv = jnp.stack([jnp.concatenate([wv4[:, :, 0], zv], axis=-1),
                    jnp.concatenate([zv, wv4[:, :, 1]], axis=-1)], axis=2).reshape(KV_LORA, QK_PAD)
    ones_col = jnp.zeros((2, HEAD_PAD), F32).at[0, V_DIM].set(1.0).at[1, 0].set(1.0)
    vones = jnp.tile(ones_col.reshape(1, 2 * HEAD_PAD), (1, N_HEADS // 2))
    wkv = jnp.concatenate([wk3.reshape(KV_LORA, QK_PAD), wv], axis=1).astype(BF16)
    tabs_m = _rope_tables(0, N_META)
    tabs_f = _rope_tables(N_META, seq)
    proj_consts = (mix_norm_g[l][None], win, q_lora_norm_g[l][None], wq,
                   kv_lora_norm_g[l][None], wkv, vones,
                   _head_gains(q_head_norm_g[l]), _head_gains(k_head_norm_g[l]))

    tm = PROJ_TM
    ffn_f32 = (w_out, w_gate, w_up, w_down)
    u2, q2d, k2d, v2d, u_meta, k_meta, v_meta, wout_b, wg_b, wu_b, wd_b = _proj_call(
        x.reshape(batch * seq, D_MODEL), meta_tokens, tm, tabs_f, tabs_m, seq // tm,
        proj_consts, ffn_f32)

    dt = jnp.exp(ssm_log_dt[l])[:, None]
    lr, li = ssm_a_re[l], ssm_a_im[l]
    mag = jnp.exp(lr * dt)
    ar = mag * jnp.cos(li * dt)
    ai = mag * jnp.sin(li * dt)
    den = lr * lr + li * li
    fr = ((ar - 1.0) * lr + ai * li) / den
    fi = (ai * lr - (ar - 1.0) * li) / den
    br, bi = ssm_b_re[l], ssm_b_im[l]
    bbr = fr[..., None] * br - fi[..., None] * bi
    bbi = fr[..., None] * bi + fi[..., None] * br
    b_in = jnp.swapaxes(jnp.stack([bbr, bbi]), 2, 3).reshape(2, D_SSM, SSM_STATE)
    c_out = jnp.swapaxes(jnp.stack([ssm_c_re[l], ssm_c_im[l]]), 2, 3)
    c_out = jnp.pad(c_out.reshape(2, N_STATE_COLS, SSM_GROUP),
                    ((0, 0), (0, 0), (0, LANES - SSM_GROUP)))
    half_groups = N_GROUPS // 2
    tile_in = np.tile(np.eye(SSM_STATE, dtype=np.float32), (1, half_groups))
    tile_out = np.zeros((LANES, half_groups * SSM_GROUP), np.float32)
    tile_out[:SSM_GROUP] = np.tile(np.eye(SSM_GROUP, dtype=np.float32), (1, half_groups))

    ar_rows = jnp.broadcast_to(ar.reshape(N_SLABS, 1, LANES), (N_SLABS, 2 * batch, LANES))
    ai_flat = ai.reshape(N_SLABS, 1, LANES)
    ai_rows = jnp.concatenate([jnp.broadcast_to(-ai_flat, (N_SLABS, batch, LANES)),
                               jnp.broadcast_to(ai_flat, (N_SLABS, batch, LANES))], axis=1)
    s5_consts = (b_in, c_out, jnp.asarray(tile_in, BF16), jnp.asarray(tile_out, BF16),
                 ar_rows, ai_rows, ssm_d[l][None], ssm_w_glu[l].astype(BF16),
                 ssm_b_glu[l][None], ssm_out_norm_g[l][None])
    mixed_ssm = _s5_call(u2.reshape(batch, seq, D_SSM), u_meta, s5_consts)

    chunk_of = np.arange(ATT_TQ) // CHUNK
    causal = (chunk_of[None, :] <= chunk_of[:, None]).astype(np.float32)
    mask = jnp.asarray(np.concatenate([np.ones((ATT_TQ, LANES), np.float32), causal], axis=1),
                       dtype=BF16)
    score_bound = (LOG2_E * math.sqrt(QK_DIM) * jnp.max(jnp.abs(q_head_norm_g[l]))
                   * jnp.max(jnp.abs(k_head_norm_g[l])))
    attn_args = (q2d.reshape(batch, seq, QK_PAD), k2d.reshape(batch, seq, QK_PAD),
                 v2d.reshape(batch, seq, QK_PAD), k_meta, v_meta, mask)
    y_att = lax.cond(score_bound <= MAX_UNSHIFTED_LOG2_SCORE,
                     functools.partial(_attn_call, online=False),
                     functools.partial(_attn_call, online=True), *attn_args)

    ffn_consts = (att_out_norm_g[l][None], wout_b, ffn_norm_g[l][None],
                  wg_b, wu_b, wd_b)
    out = _ffn_call(x.reshape(batch * seq, D_MODEL), mixed_ssm.reshape(batch * seq, D_SSM),
                    y_att.reshape(batch * seq, N_HEADS * V_DIM), ffn_consts)
    return out.reshape(batch, seq, D_MODEL)
```

```python
import functools
import math

import jax
import jax.numpy as jnp
import numpy as np
from jax import lax
from jax.experimental import pallas as pl
from jax.experimental.pallas import tpu as pltpu

F32 = jnp.float32
BF16 = jnp.bfloat16

D_MODEL = 1024
N_META = 16
CHUNK = 64
D_SSM = 512
SSM_GROUP = 16
N_GROUPS = D_SSM // SSM_GROUP
SSM_STATE = 64
N_HEADS = 8
V_DIM = 64
NOPE = 64
ROPE = 32
HALF_ROPE = ROPE // 2
QK_DIM = NOPE + ROPE
Q_LORA = 256
KV_LORA = 128
D_FF = 2816
ROPE_BASE = 10000.0
EPS = 1e-6
LOG2_E = math.log2(math.e)
MAX_UNSHIFTED_LOG2_SCORE = 40.0

LANES = 128
SUBLANES = 8
HEAD_PAD = LANES
QK_PAD = N_HEADS * HEAD_PAD
N_STATE_COLS = N_GROUPS * SSM_STATE
N_SLABS = N_STATE_COLS // LANES
S5_CHUNK = 128
S5_PITCH = S5_CHUNK + SUBLANES
PROJ_TM = 512
PROJ_COLS = D_SSM + Q_LORA + KV_LORA + 2 * HEAD_PAD
VMEM_LIMIT = 56 * 1024 * 1024


def _rms(x, g):
    return x * lax.rsqrt(jnp.mean(x * x, axis=-1, keepdims=True) + EPS) * g


def _proj_kernel(x_ref, xm_ref, gmix_ref, win_ref, rsel_ref, gq_ref, wq_ref, gkv_ref, wkv_ref,
                 vones_ref, hgq_ref, hgk_ref, ctab_ref, stab_ref, ctabm_ref, stabm_ref, *rest):
    win_sc = rest[-1]
    rest = rest[:-1]
    n_cast = (len(rest) - 7) // 2
    u_ref, q_ref, k_ref, v_ref, um_ref, km_ref, vm_ref = rest[n_cast:n_cast + 7]
    for src_ref, dst_ref in zip(rest[:n_cast], rest[n_cast + 7:]):
        dst_ref[...] = src_ref[...].astype(BF16)
    scale = QK_DIM ** -0.5 * LOG2_E
    c0 = D_SSM + Q_LORA
    c_rope = c0 + KV_LORA

    def project(x, ctab, stab, rows, u_out, q_out, k_out, v_out):
        xn = _rms(x, gmix_ref[...]).astype(BF16)
        p = jnp.dot(xn, win_sc[...], preferred_element_type=F32)
        u_out[...] = p[:, :D_SSM]

        if q_out is not None:
            cqn = _rms(p[:, D_SSM:c0], gq_ref[...]).astype(BF16)
            q12 = jnp.dot(cqn, wq_ref[...], preferred_element_type=F32)
            t1q, t2q = ctab * hgq_ref[0:1, :], stab * hgq_ref[1:2, :]
            for h in range(N_HEADS):
                q1 = q12[:, h * HEAD_PAD:(h + 1) * HEAD_PAD]
                q2 = q12[:, QK_PAD + h * HEAD_PAD:QK_PAD + (h + 1) * HEAD_PAD]
                r = lax.rsqrt(jnp.sum(q1 * q1, axis=-1, keepdims=True) * (1.0 / QK_DIM) + EPS)
                qh = (q1 * t1q + q2 * t2q) * (r * scale)
                q_out[:, h * HEAD_PAD:(h + 1) * HEAD_PAD] = qh.astype(BF16)

        ckvn = _rms(p[:, c0:c0 + KV_LORA], gkv_ref[...]).astype(BF16)
        kv = jnp.dot(ckvn, wkv_ref[...], preferred_element_type=F32)
        v_out[rows, :] = (kv[:, QK_PAD:] + vones_ref[...]).astype(BF16)
        kr = p[:, c0 + KV_LORA:c0 + KV_LORA + HEAD_PAD]
        kr_rot = p[:, c0 + KV_LORA + HEAD_PAD:c0 + KV_LORA + 2 * HEAD_PAD]
        ss_r = jnp.sum(kr * kr, axis=-1, keepdims=True)
        t1k = ctab * hgk_ref[0:1, :]
        kr_part = kr_rot * (stab * hgk_ref[1:2, :])
        for h in range(N_HEADS):
            kn = kv[:, h * HEAD_PAD:(h + 1) * HEAD_PAD]
            ss = jnp.sum(kn * kn, axis=-1, keepdims=True) + ss_r
            r = lax.rsqrt(ss * (1.0 / QK_DIM) + EPS)
            kh = ((kn + kr) * t1k + kr_part) * r
            k_out[rows, h * HEAD_PAD:(h + 1) * HEAD_PAD] = kh.astype(BF16)

    @pl.when(pl.program_id(0) == 0)
    def _():
        win_sc[:, :c_rope] = win_ref[0, :, :c_rope].astype(BF16)
        win_sc[:, c_rope:] = jnp.dot(win_ref[0, :, c_rope:].astype(BF16), rsel_ref[...],
                                     preferred_element_type=F32).astype(BF16)
        km_ref[...] = jnp.zeros_like(km_ref)
        vm_ref[...] = jnp.zeros_like(vm_ref)
        project(xm_ref[...], ctabm_ref[...], stabm_ref[...], slice(0, N_META),
                um_ref, None, km_ref, vm_ref)

    project(x_ref[...], ctab_ref[...], stab_ref[...], slice(None), u_ref, q_ref, k_ref, v_ref)


def _const_spec(shape):
    nd = len(shape)
    return pl.BlockSpec(shape, lambda *_: (0,) * nd)


def _proj_call(x2d, x_meta, tm, tabs, tabs_meta, n_tab_blocks, consts, to_bf16=()):
    n_rows = x2d.shape[0]
    n_steps = n_rows // tm
    row = lambda i: (i, 0)
    tab = lambda i: (i % n_tab_blocks, 0)
    meta_rows = (LANES, QK_PAD)
    cast_in, cast_out = [], []
    for w in to_bf16:
        _, w_rows, w_cols = w.shape
        rep = next(r for r in (1, 2, 4, 8) if (w_rows * r) % (16 * n_steps) == 0)
        blk_rows = w_rows * rep // n_steps
        cast_in.append(pl.BlockSpec((None, blk_rows, w_cols),
                                    functools.partial(lambda rep, i: (0, i // rep, 0), rep)))
        cast_out.append(pl.BlockSpec((blk_rows, w_cols),
                                     functools.partial(lambda rep, i: (i // rep, 0), rep)))
    in_specs = ([pl.BlockSpec((tm, D_MODEL), row), _const_spec(x_meta.shape)]
                + [_const_spec(c.shape) for c in consts]
                + [pl.BlockSpec((tm, HEAD_PAD), tab)] * len(tabs)
                + [_const_spec(t.shape) for t in tabs_meta] + cast_in)
    out_shape = (
        jax.ShapeDtypeStruct((n_rows, D_SSM), F32),
        jax.ShapeDtypeStruct((n_rows, QK_PAD), BF16),
        jax.ShapeDtypeStruct((n_rows, QK_PAD), BF16),
        jax.ShapeDtypeStruct((n_rows, QK_PAD), BF16),
        jax.ShapeDtypeStruct((x_meta.shape[0], D_SSM), F32),
        jax.ShapeDtypeStruct(meta_rows, BF16),
        jax.ShapeDtypeStruct(meta_rows, BF16),
    ) + tuple(jax.ShapeDtypeStruct(w.shape[1:], BF16) for w in to_bf16)
    out_specs = (
        pl.BlockSpec((tm, D_SSM), row),
        pl.BlockSpec((tm, QK_PAD), row),
        pl.BlockSpec((tm, QK_PAD), row),
        pl.BlockSpec((tm, QK_PAD), row),
        _const_spec((x_meta.shape[0], D_SSM)),
        _const_spec(meta_rows),
        _const_spec(meta_rows),
    ) + tuple(cast_out)
    return pl.pallas_call(
        _proj_kernel, out_shape=out_shape, grid=(n_steps,), in_specs=in_specs,
        out_specs=out_specs,
        scratch_shapes=[pltpu.VMEM((D_MODEL, PROJ_COLS), BF16)],
        compiler_params=pltpu.CompilerParams(dimension_semantics=("arbitrary",),
                                             vmem_limit_bytes=VMEM_LIMIT),
        name="proj_mla",
    )(x2d, x_meta, *consts, *tabs, *tabs_meta, *to_bf16)


def _s5_kernel(u_ref, um_ref, bin_ref, cout_ref, tin_ref, tout_ref, ar_ref, ai_ref,
               d_ref, wglu_ref, bglu_ref, g_ref, o_ref, xs_ref, h_ref, bmap_ref, cmap_ref, *, batch):
    j = pl.program_id(0)
    slabs_per_half = N_SLABS // 2
    half_ch, half_st = D_SSM // 2, N_STATE_COLS // 2

    def build_maps():
        def diag(shape, row_block, col_block):
            return (lax.broadcasted_iota(jnp.int32, shape, 0) // row_block
                    == lax.broadcasted_iota(jnp.int32, shape, 1) // col_block)
        diag_in = diag((half_ch, half_st), SSM_GROUP, SSM_STATE)
        diag_out = diag((half_st, half_ch), SSM_STATE, SSM_GROUP)
        for part in range(2):
            for half in range(2):
                rows_in = bin_ref[part, half * half_ch:(half + 1) * half_ch, :].astype(BF16)
                tiled = jnp.dot(rows_in, tin_ref[...], preferred_element_type=F32)
                bmap_ref[part, half] = jnp.where(diag_in, tiled, 0.0).astype(BF16)
                rows_out = cout_ref[part, half * half_st:(half + 1) * half_st, :].astype(BF16)
                tiled = jnp.dot(rows_out, tout_ref[...], preferred_element_type=F32)
                cmap_ref[part, half] = jnp.where(diag_out, tiled, 0.0).astype(BF16)

    def project_in(ub, rows):
        for kh in range(2):
            lhs = ub[:, kh * half_ch:(kh + 1) * half_ch]
            xre = jnp.dot(lhs, bmap_ref[0, kh], preferred_element_type=F32)
            xim = jnp.dot(lhs, bmap_ref[1, kh], preferred_element_type=F32)
            for cl in range(slabs_per_half):
                c = kh * slabs_per_half + cl
                for b in range(batch):
                    xs_ref[c, pl.ds(b * S5_PITCH, rows), :] = (
                        xre[b * rows:(b + 1) * rows, cl * LANES:(cl + 1) * LANES])
                    xs_ref[c, pl.ds((batch + b) * S5_PITCH, rows), :] = (
                        xim[b * rows:(b + 1) * rows, cl * LANES:(cl + 1) * LANES])

    def scan(n_steps):
        def body(t, hs):
            new = []
            for c in range(N_SLABS):
                rows = pl.ds(t, 2 * batch, stride=S5_PITCH)
                x8 = xs_ref[c, rows, :]
                h = hs[c]
                hn = ar_ref[c] * h + ai_ref[c] * pltpu.roll(h, batch, 0) + x8
                xs_ref[c, rows, :] = hn
                new.append(hn)
            return tuple(new)

        hs = tuple(h_ref[c] for c in range(N_SLABS))
        hs = lax.fori_loop(0, n_steps, body, hs, unroll=4)
        for c in range(N_SLABS):
            h_ref[c] = hs[c]

    @pl.when(j == 0)
    def _():
        build_maps()
        h_ref[...] = jnp.zeros_like(h_ref)
        um = um_ref[...].astype(BF16)
        project_in(jnp.concatenate([um] * batch, axis=0), N_META)
        scan(N_META)

    uf = u_ref[...].reshape(batch * S5_CHUNK, D_SSM)
    project_in(uf.astype(BF16), S5_CHUNK)
    scan(S5_CHUNK)

    ys = []
    for nh in range(2):
        def gather(plane0):
            return jnp.concatenate(
                [jnp.concatenate(
                    [xs_ref[nh * slabs_per_half + cl, pl.ds((plane0 + b) * S5_PITCH, S5_CHUNK), :]
                     for cl in range(slabs_per_half)], axis=1)
                 for b in range(batch)], axis=0).astype(BF16)
        yre = jnp.dot(gather(0), cmap_ref[0, nh], preferred_element_type=F32)
        yim = jnp.dot(gather(batch), cmap_ref[1, nh], preferred_element_type=F32)
        ys.append(yre - yim)
    y = jnp.concatenate(ys, axis=1) + d_ref[...] * uf
    z = 0.5 * y * (1.0 + jnp.tanh(math.sqrt(2.0 / math.pi) * (y + 0.044715 * (y * y * y))))
    gate = jnp.dot(z.astype(BF16), wglu_ref[...], preferred_element_type=F32) + bglu_ref[...]
    out = z * (1.0 / (1.0 + jnp.exp(-gate)))
    o_ref[...] = _rms(out, g_ref[...]).astype(BF16).reshape(batch, S5_CHUNK, D_SSM)


def _s5_call(u3, u_meta, consts):
    batch, seq, _ = u3.shape
    assert 2 * batch == SUBLANES and seq % S5_CHUNK == 0
    grid = (seq // S5_CHUNK,)
    in_specs = [pl.BlockSpec((batch, S5_CHUNK, D_SSM), lambda j: (0, j, 0)),
                _const_spec(u_meta.shape)] + [_const_spec(c.shape) for c in consts]
    return pl.pallas_call(
        functools.partial(_s5_kernel, batch=batch),
        out_shape=jax.ShapeDtypeStruct((batch, seq, D_SSM), BF16),
        grid=grid, in_specs=in_specs,
        out_specs=pl.BlockSpec((batch, S5_CHUNK, D_SSM), lambda j: (0, j, 0)),
        scratch_shapes=[pltpu.VMEM((N_SLABS, 2 * batch * S5_PITCH, LANES), F32),
                        pltpu.VMEM((N_SLABS, 2 * batch, LANES), F32),
                        pltpu.VMEM((2, 2, D_SSM // 2, N_STATE_COLS // 2), BF16),
                        pltpu.VMEM((2, 2, N_STATE_COLS // 2, D_SSM // 2), BF16)],
        compiler_params=pltpu.CompilerParams(dimension_semantics=("arbitrary",),
                                             vmem_limit_bytes=VMEM_LIMIT),
        name="s5_mixer",
    )(u3, u_meta, *consts)


ATT_TQ = 1024
ATT_TK = 1024
ATT_SUB = 256
ATT_HEADS = 4


def _attn_kernel(q_ref, k_ref, v_ref, km_ref, vm_ref, mask_ref, o_ref, acc_ref, m_ref, *, online):
    nt = (((1,), (1,)), ((), ()))
    n_q = q_ref.shape[0] // ATT_TQ
    head_lanes = [slice(h * HEAD_PAD, (h + 1) * HEAD_PAD) for h in range(ATT_HEADS)]

    def step(h, sub, q, kblk, vblk, mask):
        s = lax.dot_general(q, kblk, nt, preferred_element_type=F32)
        if online:
            if mask is not None:
                s = jnp.where(mask > 0, s, -jnp.inf)
            m = m_ref[h, sub]
            m_new = jnp.maximum(m, jnp.max(s, axis=-1, keepdims=True))
            p = jnp.exp2(s - m_new).astype(BF16)
            acc_ref[h, sub] = (jnp.exp2(m - m_new) * acc_ref[h, sub]
                               + jnp.dot(p, vblk, preferred_element_type=F32))
            m_ref[h, sub] = m_new
        else:
            p = jnp.exp2(s).astype(BF16)
            if mask is not None:
                p = p * mask
            acc_ref[h, sub] += jnp.dot(p, vblk, preferred_element_type=F32)

    def q_tile(qi, _):
        q0 = pl.multiple_of(qi * ATT_TQ, ATT_TQ)
        rows = pl.ds(q0, ATT_TQ)
        qs = [q_ref[rows, hl] for hl in head_lanes]
        acc_ref[...] = jnp.zeros_like(acc_ref)
        if online:
            m_ref[...] = jnp.full(m_ref.shape, -1e30, F32)

        def body(kb, _):
            krows = pl.ds(pl.multiple_of(kb * ATT_TK, ATT_TK), ATT_TK)
            for h, hl in enumerate(head_lanes):
                step(h, slice(None), qs[h], k_ref[krows, hl], v_ref[krows, hl], None)
            return 0

        lax.fori_loop(0, qi * (ATT_TQ // ATT_TK), body, 0)

        for i in range(ATT_TQ // ATT_SUB):
            sub = slice(i * ATT_SUB, ATT_TQ)
            krows = pl.ds(q0 + i * ATT_SUB, ATT_SUB)
            for h, hl in enumerate(head_lanes):
                kblk, vblk = k_ref[krows, hl], v_ref[krows, hl]
                mask = mask_ref[sub, LANES + i * ATT_SUB:LANES + (i + 1) * ATT_SUB]
                if i == 0:
                    kblk = jnp.concatenate([km_ref[:, hl], kblk], axis=0)
                    vblk = jnp.concatenate([vm_ref[:, hl], vblk], axis=0)
                    mask = mask_ref[sub, :LANES + ATT_SUB]
                step(h, sub, qs[h][sub], kblk, vblk, mask)
        lane = lax.broadcasted_iota(jnp.int32, (ATT_TQ, HEAD_PAD), 1)
        for hp in range(ATT_HEADS // 2):
            even, odd = acc_ref[2 * hp], acc_ref[2 * hp + 1]
            o_even = even * (1.0 / even[:, V_DIM:V_DIM + 1])
            o_odd = odd * (1.0 / odd[:, 0:1])
            o_ref[rows, hp * HEAD_PAD:(hp + 1) * HEAD_PAD] = (
                jnp.where(lane < V_DIM, o_even, o_odd).astype(BF16))
        return 0

    lax.fori_loop(0, n_q, q_tile, 0)


def _attn_call(q3, k3, v3, k_meta, v_meta, mask, *, online):
    batch, seq, _ = q3.shape
    grid = (batch, N_HEADS // ATT_HEADS)
    seq_blk = pl.BlockSpec((None, seq, ATT_HEADS * HEAD_PAD), lambda b, hg: (b, 0, hg))
    meta_blk = pl.BlockSpec((LANES, ATT_HEADS * HEAD_PAD), lambda b, hg: (0, hg))
    return pl.pallas_call(
        functools.partial(_attn_kernel, online=online),
        out_shape=jax.ShapeDtypeStruct((batch, seq, N_HEADS * V_DIM), BF16),
        grid=grid,
        in_specs=[seq_blk, seq_blk, seq_blk, meta_blk, meta_blk, _const_spec(mask.shape)],
        out_specs=pl.BlockSpec((None, seq, ATT_HEADS * V_DIM), lambda b, hg: (b, 0, hg)),
        scratch_shapes=[pltpu.VMEM((ATT_HEADS, ATT_TQ, HEAD_PAD), F32),
                        pltpu.VMEM((ATT_HEADS, ATT_TQ, 1), F32)],
        compiler_params=pltpu.CompilerParams(dimension_semantics=("parallel", "parallel"),
                                             vmem_limit_bytes=VMEM_LIMIT),
        name="mla_attention_online" if online else "mla_attention",
    )(q3, k3, v3, k_meta, v_meta, mask)


FFN_TM = 512


def _ffn_kernel(x_ref, ms_ref, oa_ref, gatt_ref, wout_ref, gffn_ref, wg_ref, wu_ref, wd_ref,
                out_ref):
    ya = _rms(oa_ref[...].astype(F32), gatt_ref[...]).astype(BF16)
    mixed = jnp.concatenate([ms_ref[...], ya], axis=1)
    h1 = x_ref[...] + jnp.dot(mixed, wout_ref[...], preferred_element_type=F32)
    hn = _rms(h1, gffn_ref[...]).astype(BF16)
    g = jnp.dot(hn, wg_ref[...], preferred_element_type=F32)
    u = jnp.dot(hn, wu_ref[...], preferred_element_type=F32)
    a = (g * (1.0 / (1.0 + jnp.exp(-g))) * u).astype(BF16)
    out_ref[...] = h1 + jnp.dot(a, wd_ref[...], preferred_element_type=F32)


def _ffn_call(x2d, ms2d, oa2d, consts):
    n_rows = x2d.shape[0]
    tm = FFN_TM
    row = lambda i: (i, 0)
    once = pl.Buffered(1)
    in_specs = [pl.BlockSpec((tm, D_MODEL), row),
                pl.BlockSpec((tm, D_SSM), row),
                pl.BlockSpec((tm, N_HEADS * V_DIM), row)]
    in_specs += [pl.BlockSpec(c.shape, lambda i: (0, 0), pipeline_mode=once) for c in consts]
    return pl.pallas_call(
        _ffn_kernel,
        out_shape=jax.ShapeDtypeStruct((n_rows, D_MODEL), F32),
        grid=(n_rows // tm,), in_specs=in_specs,
        out_specs=pl.BlockSpec((tm, D_MODEL), row),
        compiler_params=pltpu.CompilerParams(dimension_semantics=("parallel",),
                                             vmem_limit_bytes=VMEM_LIMIT),
        name="outproj_ffn",
    )(x2d, ms2d, oa2d, *consts)


def _rope_tables(first, count):
    pos = np.arange(first, first + count, dtype=np.float64)
    inv_freq = 1.0 / (ROPE_BASE ** (np.arange(0, ROPE, 2, dtype=np.float64) / ROPE))
    ang = pos[:, None] * inv_freq[None, :]
    cos, sin = np.cos(ang), np.sin(ang)
    ctab = np.zeros((count, HEAD_PAD), np.float32)
    stab = np.zeros((count, HEAD_PAD), np.float32)
    ctab[:, :NOPE] = 1.0
    ctab[:, NOPE:QK_DIM] = np.concatenate([cos, cos], axis=1)
    stab[:, NOPE:QK_DIM] = np.concatenate([sin, sin], axis=1)
    return jnp.asarray(ctab), jnp.asarray(stab)


def _head_gains(gain):
    g_r = gain[NOPE:]
    g_r_swapped = jnp.concatenate([g_r[HALF_ROPE:], g_r[:HALF_ROPE]])
    pad = jnp.zeros((HEAD_PAD - QK_DIM,), F32)
    return jnp.stack([jnp.concatenate([gain, pad]),
                      jnp.concatenate([jnp.zeros((NOPE,), F32), g_r_swapped, pad])])


def _rot_half_cols(w):
    return jnp.concatenate([-w[..., HALF_ROPE:], w[..., :HALF_ROPE]], axis=-1)


def kernel(x, meta_tokens, mix_norm_g, w_in, ssm_a_re, ssm_a_im, ssm_log_dt, ssm_b_re, ssm_b_im,
           ssm_c_re, ssm_c_im, ssm_d, ssm_w_glu, ssm_b_glu, q_lora_norm_g, w_uq, kv_lora_norm_g,
           w_uk, w_uv, q_head_norm_g, k_head_norm_g, ssm_out_norm_g, att_out_norm_g, w_out,
           ffn_norm_g, w_gate, w_up, w_down):
    batch, seq, _ = x.shape
    depth = w_in.shape[0]
    assert depth == 1
    l = 0

    rope_sel = np.zeros((ROPE, 2 * HEAD_PAD), np.float32)
    for j in range(ROPE):
        rope_sel[j, NOPE + j] = 1.0
    for i in range(HALF_ROPE):
        rope_sel[HALF_ROPE + i, HEAD_PAD + NOPE + i] = -1.0
        rope_sel[i, HEAD_PAD + NOPE + HALF_ROPE + i] = 1.0
    wq3 = w_uq[l].reshape(Q_LORA, N_HEADS, QK_DIM)
    q1 = jnp.pad(wq3, ((0, 0), (0, 0), (0, HEAD_PAD - QK_DIM)))
    q2 = jnp.pad(_rot_half_cols(wq3[..., NOPE:]), ((0, 0), (0, 0), (NOPE, HEAD_PAD - QK_DIM)))
    wq = jnp.concatenate([q1.reshape(Q_LORA, QK_PAD), q2.reshape(Q_LORA, QK_PAD)], axis=1).astype(BF16)
    wk3 = jnp.pad(w_uk[l].reshape(KV_LORA, N_HEADS, NOPE), ((0, 0), (0, 0), (0, HEAD_PAD - NOPE)))
    wv4 = w_uv[l].reshape(KV_LORA, N_HEADS // 2, 2, V_DIM)
    zv = jnp.zeros_like(wv4[:, :, 0])
    wv = jnp.stack([jnp.concatenate([wv4[:, :, 0], zv], axis=-1),
                    jnp.concatenate([zv, wv4[:, :, 1]], axis=-1)], axis=2).reshape(KV_LORA, QK_PAD)
    ones_col = jnp.zeros((2, HEAD_PAD), F32).at[0, V_DIM].set(1.0).at[1, 0].set(1.0)
    vones = jnp.tile(ones_col.reshape(1, 2 * HEAD_PAD), (1, N_HEADS // 2))
    wkv = jnp.concatenate([wk3.reshape(KV_LORA, QK_PAD), wv], axis=1).astype(BF16)
    tabs_m = _rope_tables(0, N_META)
    tabs_f = _rope_tables(N_META, seq)
    proj_consts = (mix_norm_g[l][None], w_in, jnp.asarray(rope_sel, BF16), q_lora_norm_g[l][None], wq,
                   kv_lora_norm_g[l][None], wkv, vones,
                   _head_gains(q_head_norm_g[l]), _head_gains(k_head_norm_g[l]))

    tm = PROJ_TM
    ffn_f32 = (w_out, w_gate, w_up, w_down)
    u2, q2d, k2d, v2d, u_meta, k_meta, v_meta, wout_b, wg_b, wu_b, wd_b = _proj_call(
        x.reshape(batch * seq, D_MODEL), meta_tokens, tm, tabs_f, tabs_m, seq // tm,
        proj_consts, ffn_f32)

    dt = jnp.exp(ssm_log_dt[l])[:, None]
    lr, li = ssm_a_re[l], ssm_a_im[l]
    mag = jnp.exp(lr * dt)
    ar = mag * jnp.cos(li * dt)
    ai = mag * jnp.sin(li * dt)
    den = lr * lr + li * li
    fr = ((ar - 1.0) * lr + ai * li) / den
    fi = (ai * lr - (ar - 1.0) * li) / den
    br, bi = ssm_b_re[l], ssm_b_im[l]
    bbr = fr[..., None] * br - fi[..., None] * bi
    bbi = fr[..., None] * bi + fi[..., None] * br
    b_in = jnp.swapaxes(jnp.stack([bbr, bbi]), 2, 3).reshape(2, D_SSM, SSM_STATE)
    c_out = jnp.swapaxes(jnp.stack([ssm_c_re[l], ssm_c_im[l]]), 2, 3)
    c_out = jnp.pad(c_out.reshape(2, N_STATE_COLS, SSM_GROUP),
                    ((0, 0), (0, 0), (0, LANES - SSM_GROUP)))
    half_groups = N_GROUPS // 2
    tile_in = np.tile(np.eye(SSM_STATE, dtype=np.float32), (1, half_groups))
    tile_out = np.zeros((LANES, half_groups * SSM_GROUP), np.float32)
    tile_out[:SSM_GROUP] = np.tile(np.eye(SSM_GROUP, dtype=np.float32), (1, half_groups))

    ar_rows = jnp.broadcast_to(ar.reshape(N_SLABS, 1, LANES), (N_SLABS, 2 * batch, LANES))
    ai_flat = ai.reshape(N_SLABS, 1, LANES)
    ai_rows = jnp.concatenate([jnp.broadcast_to(-ai_flat, (N_SLABS, batch, LANES)),
                               jnp.broadcast_to(ai_flat, (N_SLABS, batch, LANES))], axis=1)
    s5_consts = (b_in, c_out, jnp.asarray(tile_in, BF16), jnp.asarray(tile_out, BF16),
                 ar_rows, ai_rows, ssm_d[l][None], ssm_w_glu[l].astype(BF16),
                 ssm_b_glu[l][None], ssm_out_norm_g[l][None])
    mixed_ssm = _s5_call(u2.reshape(batch, seq, D_SSM), u_meta, s5_consts)

    chunk_of = np.arange(ATT_TQ) // CHUNK
    causal = (chunk_of[None, :] <= chunk_of[:, None]).astype(np.float32)
    mask = jnp.asarray(np.concatenate([np.ones((ATT_TQ, LANES), np.float32), causal], axis=1),
                       dtype=BF16)
    score_bound = (LOG2_E * math.sqrt(QK_DIM) * jnp.max(jnp.abs(q_head_norm_g[l]))
                   * jnp.max(jnp.abs(k_head_norm_g[l])))
    attn_args = (q2d.reshape(batch, seq, QK_PAD), k2d.reshape(batch, seq, QK_PAD),
                 v2d.reshape(batch, seq, QK_PAD), k_meta, v_meta, mask)
    y_att = lax.cond(score_bound <= MAX_UNSHIFTED_LOG2_SCORE,
                     functools.partial(_attn_call, online=False),
                     functools.partial(_attn_call, online=True), *attn_args)

    ffn_consts = (att_out_norm_g[l][None], wout_b, ffn_norm_g[l][None],
                  wg_b, wu_b, wd_b)
    out = _ffn_call(x.reshape(batch * seq, D_MODEL), mixed_ssm.reshape(batch * seq, D_SSM),
                    y_att.reshape(batch * seq, N_HEADS * V_DIM), ffn_consts)
    return out.reshape(batch, seq, D_MODEL)
```

```python
import functools
import math

import jax
import jax.numpy as jnp
import numpy as np
from jax import lax
from jax.experimental import pallas as pl
from jax.experimental.pallas import tpu as pltpu

F32 = jnp.float32
BF16 = jnp.bfloat16

D_MODEL = 1024
N_META = 16
CHUNK = 64
D_SSM = 512
SSM_GROUP = 16
N_GROUPS = D_SSM // SSM_GROUP
SSM_STATE = 64
N_HEADS = 8
V_DIM = 64
NOPE = 64
ROPE = 32
HALF_ROPE = ROPE // 2
QK_DIM = NOPE + ROPE
Q_LORA = 256
KV_LORA = 128
D_FF = 2816
ROPE_BASE = 10000.0
EPS = 1e-6
LOG2_E = math.log2(math.e)
MAX_UNSHIFTED_LOG2_SCORE = 40.0

LANES = 128
SUBLANES = 8
HEAD_PAD = LANES
QK_PAD = N_HEADS * HEAD_PAD
N_STATE_COLS = N_GROUPS * SSM_STATE
N_SLABS = N_STATE_COLS // LANES
S5_CHUNK = 128
S5_PITCH = S5_CHUNK + SUBLANES
PROJ_TM = 512
VMEM_LIMIT = 56 * 1024 * 1024


def _rms(x, g):
    return x * lax.rsqrt(jnp.mean(x * x, axis=-1, keepdims=True) + EPS) * g


def _proj_kernel(x_ref, xm_ref, gmix_ref, win_ref, gq_ref, wq_ref, gkv_ref, wkv_ref, vones_ref,
                 hgq_ref, hgk_ref, ctab_ref, stab_ref, ctabm_ref, stabm_ref, *rest):
    n_cast = (len(rest) - 7) // 2
    u_ref, q_ref, k_ref, v_ref, um_ref, km_ref, vm_ref = rest[n_cast:n_cast + 7]
    for src_ref, dst_ref in zip(rest[:n_cast], rest[n_cast + 7:]):
        dst_ref[...] = src_ref[...].astype(BF16)
    scale = QK_DIM ** -0.5 * LOG2_E
    c0 = D_SSM + Q_LORA

    def project(x, ctab, stab, rows, u_out, q_out, k_out, v_out):
        xn = _rms(x, gmix_ref[...]).astype(BF16)
        p = jnp.dot(xn, win_ref[...], preferred_element_type=F32)
        u_out[...] = p[:, :D_SSM]

        if q_out is not None:
            cqn = _rms(p[:, D_SSM:c0], gq_ref[...]).astype(BF16)
            q12 = jnp.dot(cqn, wq_ref[...], preferred_element_type=F32)
            t1q, t2q = ctab * hgq_ref[0:1, :], stab * hgq_ref[1:2, :]
            for h in range(N_HEADS):
                q1 = q12[:, h * HEAD_PAD:(h + 1) * HEAD_PAD]
                q2 = q12[:, QK_PAD + h * HEAD_PAD:QK_PAD + (h + 1) * HEAD_PAD]
                r = lax.rsqrt(jnp.sum(q1 * q1, axis=-1, keepdims=True) * (1.0 / QK_DIM) + EPS)
                qh = (q1 * t1q + q2 * t2q) * (r * scale)
                q_out[:, h * HEAD_PAD:(h + 1) * HEAD_PAD] = qh.astype(BF16)

        ckvn = _rms(p[:, c0:c0 + KV_LORA], gkv_ref[...]).astype(BF16)
        kv = jnp.dot(ckvn, wkv_ref[...], preferred_element_type=F32)
        v_out[rows, :] = (kv[:, QK_PAD:] + vones_ref[...]).astype(BF16)
        kr = p[:, c0 + KV_LORA:c0 + KV_LORA + HEAD_PAD]
        kr_rot = p[:, c0 + KV_LORA + HEAD_PAD:c0 + KV_LORA + 2 * HEAD_PAD]
        ss_r = jnp.sum(kr * kr, axis=-1, keepdims=True)
        t1k = ctab * hgk_ref[0:1, :]
        kr_part = kr_rot * (stab * hgk_ref[1:2, :])
        for h in range(N_HEADS):
            kn = kv[:, h * HEAD_PAD:(h + 1) * HEAD_PAD]
            ss = jnp.sum(kn * kn, axis=-1, keepdims=True) + ss_r
            r = lax.rsqrt(ss * (1.0 / QK_DIM) + EPS)
            kh = ((kn + kr) * t1k + kr_part) * r
            k_out[rows, h * HEAD_PAD:(h + 1) * HEAD_PAD] = kh.astype(BF16)

    @pl.when(pl.program_id(0) == 0)
    def _():
        km_ref[...] = jnp.zeros_like(km_ref)
        vm_ref[...] = jnp.zeros_like(vm_ref)
        project(xm_ref[...], ctabm_ref[...], stabm_ref[...], slice(0, N_META),
                um_ref, None, km_ref, vm_ref)

    project(x_ref[...], ctab_ref[...], stab_ref[...], slice(None), u_ref, q_ref, k_ref, v_ref)


def _const_spec(shape):
    nd = len(shape)
    return pl.BlockSpec(shape, lambda *_: (0,) * nd)


def _proj_call(x2d, x_meta, tm, tabs, tabs_meta, n_tab_blocks, consts, to_bf16=()):
    n_rows = x2d.shape[0]
    n_steps = n_rows // tm
    row = lambda i: (i, 0)
    tab = lambda i: (i % n_tab_blocks, 0)
    meta_rows = (LANES, QK_PAD)
    cast_in, cast_out = [], []
    for w in to_bf16:
        _, w_rows, w_cols = w.shape
        rep = next(r for r in (1, 2, 4, 8) if (w_rows * r) % (16 * n_steps) == 0)
        blk_rows = w_rows * rep // n_steps
        cast_in.append(pl.BlockSpec((None, blk_rows, w_cols),
                                    functools.partial(lambda rep, i: (0, i // rep, 0), rep)))
        cast_out.append(pl.BlockSpec((blk_rows, w_cols),
                                     functools.partial(lambda rep, i: (i // rep, 0), rep)))
    in_specs = ([pl.BlockSpec((tm, D_MODEL), row), _const_spec(x_meta.shape)]
                + [_const_spec(c.shape) for c in consts]
                + [pl.BlockSpec((tm, HEAD_PAD), tab)] * len(tabs)
                + [_const_spec(t.shape) for t in tabs_meta] + cast_in)
    out_shape = (
        jax.ShapeDtypeStruct((n_rows, D_SSM), F32),
        jax.ShapeDtypeStruct((n_rows, QK_PAD), BF16),
        jax.ShapeDtypeStruct((n_rows, QK_PAD), BF16),
        jax.ShapeDtypeStruct((n_rows, QK_PAD), BF16),
        jax.ShapeDtypeStruct((x_meta.shape[0], D_SSM), F32),
        jax.ShapeDtypeStruct(meta_rows, BF16),
        jax.ShapeDtypeStruct(meta_rows, BF16),
    ) + tuple(jax.ShapeDtypeStruct(w.shape[1:], BF16) for w in to_bf16)
    out_specs = (
        pl.BlockSpec((tm, D_SSM), row),
        pl.BlockSpec((tm, QK_PAD), row),
        pl.BlockSpec((tm, QK_PAD), row),
        pl.BlockSpec((tm, QK_PAD), row),
        _const_spec((x_meta.shape[0], D_SSM)),
        _const_spec(meta_rows),
        _const_spec(meta_rows),
    ) + tuple(cast_out)
    return pl.pallas_call(
        _proj_kernel, out_shape=out_shape, grid=(n_steps,), in_specs=in_specs,
        out_specs=out_specs,
        compiler_params=pltpu.CompilerParams(dimension_semantics=("arbitrary",),
                                             vmem_limit_bytes=VMEM_LIMIT),
        name="proj_mla",
    )(x2d, x_meta, *consts, *tabs, *tabs_meta, *to_bf16)


def _s5_kernel(u_ref, un_ref, um_ref, bin_ref, cout_ref, tin_ref, tout_ref, ar_ref, ai_ref,
               d_ref, wglu_ref, bglu_ref, g_ref, o_ref, xa_ref, xb_ref, h_ref, bmap_ref, cmap_ref,
               *, batch):
    j = pl.program_id(0)
    slabs_per_half = N_SLABS // 2
    half_ch, half_st = D_SSM // 2, N_STATE_COLS // 2

    def build_maps():
        def diag(shape, row_block, col_block):
            return (lax.broadcasted_iota(jnp.int32, shape, 0) // row_block
                    == lax.broadcasted_iota(jnp.int32, shape, 1) // col_block)
        diag_in = diag((half_ch, half_st), SSM_GROUP, SSM_STATE)
        diag_out = diag((half_st, half_ch), SSM_STATE, SSM_GROUP)
        for part in range(2):
            for half in range(2):
                rows_in = bin_ref[part, half * half_ch:(half + 1) * half_ch, :].astype(BF16)
                tiled = jnp.dot(rows_in, tin_ref[...], preferred_element_type=F32)
                bmap_ref[part, half] = jnp.where(diag_in, tiled, 0.0).astype(BF16)
                rows_out = cout_ref[part, half * half_st:(half + 1) * half_st, :].astype(BF16)
                tiled = jnp.dot(rows_out, tout_ref[...], preferred_element_type=F32)
                cmap_ref[part, half] = jnp.where(diag_out, tiled, 0.0).astype(BF16)

    def project_in(ub, rows, xs_ref):
        for kh in range(2):
            lhs = ub[:, kh * half_ch:(kh + 1) * half_ch]
            xre = jnp.dot(lhs, bmap_ref[0, kh], preferred_element_type=F32)
            xim = jnp.dot(lhs, bmap_ref[1, kh], preferred_element_type=F32)
            for cl in range(slabs_per_half):
                c = kh * slabs_per_half + cl
                for b in range(batch):
                    xs_ref[c, pl.ds(b * S5_PITCH, rows), :] = (
                        xre[b * rows:(b + 1) * rows, cl * LANES:(cl + 1) * LANES])
                    xs_ref[c, pl.ds((batch + b) * S5_PITCH, rows), :] = (
                        xim[b * rows:(b + 1) * rows, cl * LANES:(cl + 1) * LANES])

    def scan(n_steps, xs_ref):
        def body(t, hs):
            new = []
            for c in range(N_SLABS):
                rows = pl.ds(t, 2 * batch, stride=S5_PITCH)
                x8 = xs_ref[c, rows, :]
                h = hs[c]
                hn = ar_ref[c] * h + ai_ref[c] * pltpu.roll(h, batch, 0) + x8
                xs_ref[c, rows, :] = hn
                new.append(hn)
            return tuple(new)

        hs = tuple(h_ref[c] for c in range(N_SLABS))
        hs = lax.fori_loop(0, n_steps, body, hs, unroll=4)
        for c in range(N_SLABS):
            h_ref[c] = hs[c]

    def output_map(xs_ref):
        ys = []
        for nh in range(2):
            def gather(plane0):
                return jnp.concatenate(
                    [jnp.concatenate(
                        [xs_ref[nh * slabs_per_half + cl,
                                pl.ds((plane0 + b) * S5_PITCH, S5_CHUNK), :]
                         for cl in range(slabs_per_half)], axis=1)
                     for b in range(batch)], axis=0).astype(BF16)
            yre = jnp.dot(gather(0), cmap_ref[0, nh], preferred_element_type=F32)
            yim = jnp.dot(gather(batch), cmap_ref[1, nh], preferred_element_type=F32)
            ys.append(yre - yim)
        uf = u_ref[...].reshape(batch * S5_CHUNK, D_SSM)
        y = jnp.concatenate(ys, axis=1) + d_ref[...] * uf
        z = 0.5 * y * (1.0 + jnp.tanh(math.sqrt(2.0 / math.pi) * (y + 0.044715 * (y * y * y))))
        gate = jnp.dot(z.astype(BF16), wglu_ref[...], preferred_element_type=F32) + bglu_ref[...]
        out = z * (1.0 / (1.0 + jnp.exp(-gate)))
        o_ref[...] = _rms(out, g_ref[...]).astype(BF16).reshape(batch, S5_CHUNK, D_SSM)

    @pl.when(j == 0)
    def _():
        build_maps()
        h_ref[...] = jnp.zeros_like(h_ref)
        um = um_ref[...].astype(BF16)
        project_in(jnp.concatenate([um] * batch, axis=0), N_META, xb_ref)
        scan(N_META, xb_ref)
        project_in(u_ref[...].reshape(batch * S5_CHUNK, D_SSM).astype(BF16), S5_CHUNK, xa_ref)

    def step(x_cur, x_next):
        scan(S5_CHUNK, x_cur)
        output_map(x_cur)
        project_in(un_ref[...].reshape(batch * S5_CHUNK, D_SSM).astype(BF16), S5_CHUNK, x_next)

    @pl.when(lax.rem(j, 2) == 0)
    def _():
        step(xa_ref, xb_ref)

    @pl.when(lax.rem(j, 2) == 1)
    def _():
        step(xb_ref, xa_ref)


def _s5_call(u3, u_meta, consts):
    batch, seq, _ = u3.shape
    assert 2 * batch == SUBLANES and seq % S5_CHUNK == 0
    n_chunks = seq // S5_CHUNK
    grid = (n_chunks,)
    chunk_blk = (batch, S5_CHUNK, D_SSM)
    in_specs = [pl.BlockSpec(chunk_blk, lambda j: (0, j, 0)),
                pl.BlockSpec(chunk_blk, lambda j: (0, jnp.minimum(j + 1, n_chunks - 1), 0)),
                _const_spec(u_meta.shape)] + [_const_spec(c.shape) for c in consts]
    planes = pltpu.VMEM((N_SLABS, 2 * batch * S5_PITCH, LANES), F32)
    return pl.pallas_call(
        functools.partial(_s5_kernel, batch=batch),
        out_shape=jax.ShapeDtypeStruct((batch, seq, D_SSM), BF16),
        grid=grid, in_specs=in_specs,
        out_specs=pl.BlockSpec(chunk_blk, lambda j: (0, j, 0)),
        scratch_shapes=[planes, planes,
                        pltpu.VMEM((N_SLABS, 2 * batch, LANES), F32),
                        pltpu.VMEM((2, 2, D_SSM // 2, N_STATE_COLS // 2), BF16),
                        pltpu.VMEM((2, 2, N_STATE_COLS // 2, D_SSM // 2), BF16)],
        compiler_params=pltpu.CompilerParams(dimension_semantics=("arbitrary",),
                                             vmem_limit_bytes=VMEM_LIMIT),
        name="s5_mixer",
    )(u3, u3, u_meta, *consts)


ATT_TQ = 1024
ATT_TK = 1024
ATT_SUB = 256
ATT_HEADS = 4


def _attn_kernel(q_ref, k_ref, v_ref, km_ref, vm_ref, mask_ref, o_ref, acc_ref, m_ref, *, online):
    nt = (((1,), (1,)), ((), ()))
    n_q = q_ref.shape[0] // ATT_TQ
    head_lanes = [slice(h * HEAD_PAD, (h + 1) * HEAD_PAD) for h in range(ATT_HEADS)]

    def step(h, sub, q, kblk, vblk, mask):
        s = lax.dot_general(q, kblk, nt, preferred_element_type=F32)
        if online:
            if mask is not None:
                s = jnp.where(mask > 0, s, -jnp.inf)
            m = m_ref[h, sub]
            m_new = jnp.maximum(m, jnp.max(s, axis=-1, keepdims=True))
            p = jnp.exp2(s - m_new).astype(BF16)
            acc_ref[h, sub] = (jnp.exp2(m - m_new) * acc_ref[h, sub]
                               + jnp.dot(p, vblk, preferred_element_type=F32))
            m_ref[h, sub] = m_new
        else:
            p = jnp.exp2(s).astype(BF16)
            if mask is not None:
                p = p * mask
            acc_ref[h, sub] += jnp.dot(p, vblk, preferred_element_type=F32)

    def q_tile(qi, _):
        q0 = pl.multiple_of(qi * ATT_TQ, ATT_TQ)
        rows = pl.ds(q0, ATT_TQ)
        qs = [q_ref[rows, hl] for hl in head_lanes]
        acc_ref[...] = jnp.zeros_like(acc_ref)
        if online:
            m_ref[...] = jnp.full(m_ref.shape, -1e30, F32)

        def body(kb, _):
            krows = pl.ds(pl.multiple_of(kb * ATT_TK, ATT_TK), ATT_TK)
            for h, hl in enumerate(head_lanes):
                step(h, slice(None), qs[h], k_ref[krows, hl], v_ref[krows, hl], None)
            return 0

        lax.fori_loop(0, qi * (ATT_TQ // ATT_TK), body, 0)

        for i in range(ATT_TQ // ATT_SUB):
            sub = slice(i * ATT_SUB, ATT_TQ)
            krows = pl.ds(q0 + i * ATT_SUB, ATT_SUB)
            for h, hl in enumerate(head_lanes):
                kblk, vblk = k_ref[krows, hl], v_ref[krows, hl]
                mask = mask_ref[sub, LANES + i * ATT_SUB:LANES + (i + 1) * ATT_SUB]
                if i == 0:
                    kblk = jnp.concatenate([km_ref[:, hl], kblk], axis=0)
                    vblk = jnp.concatenate([vm_ref[:, hl], vblk], axis=0)
                    mask = mask_ref[sub, :LANES + ATT_SUB]
                step(h, sub, qs[h][sub], kblk, vblk, mask)
        lane = lax.broadcasted_iota(jnp.int32, (ATT_TQ, HEAD_PAD), 1)
        for hp in range(ATT_HEADS // 2):
            even, odd = acc_ref[2 * hp], acc_ref[2 * hp + 1]
            o_even = even * (1.0 / even[:, V_DIM:V_DIM + 1])
            o_odd = odd * (1.0 / odd[:, 0:1])
            o_ref[rows, hp * HEAD_PAD:(hp + 1) * HEAD_PAD] = (
                jnp.where(lane < V_DIM, o_even, o_odd).astype(BF16))
        return 0

    lax.fori_loop(0, n_q, q_tile, 0)


def _attn_call(q3, k3, v3, k_meta, v_meta, mask, *, online):
    batch, seq, _ = q3.shape
    grid = (batch, N_HEADS // ATT_HEADS)
    seq_blk = pl.BlockSpec((None, seq, ATT_HEADS * HEAD_PAD), lambda b, hg: (b, 0, hg))
    meta_blk = pl.BlockSpec((LANES, ATT_HEADS * HEAD_PAD), lambda b, hg: (0, hg))
    return pl.pallas_call(
        functools.partial(_attn_kernel, online=online),
        out_shape=jax.ShapeDtypeStruct((batch, seq, N_HEADS * V_DIM), BF16),
        grid=grid,
        in_specs=[seq_blk, seq_blk, seq_blk, meta_blk, meta_blk, _const_spec(mask.shape)],
        out_specs=pl.BlockSpec((None, seq, ATT_HEADS * V_DIM), lambda b, hg: (b, 0, hg)),
        scratch_shapes=[pltpu.VMEM((ATT_HEADS, ATT_TQ, HEAD_PAD), F32),
                        pltpu.VMEM((ATT_HEADS, ATT_TQ, 1), F32)],
        compiler_params=pltpu.CompilerParams(dimension_semantics=("parallel", "parallel"),
                                             vmem_limit_bytes=VMEM_LIMIT),
        name="mla_attention_online" if online else "mla_attention",
    )(q3, k3, v3, k_meta, v_meta, mask)


FFN_TM = 512


def _ffn_kernel(x_ref, ms_ref, oa_ref, gatt_ref, wout_ref, gffn_ref, wg_ref, wu_ref, wd_ref,
                out_ref):
    ya = _rms(oa_ref[...].astype(F32), gatt_ref[...]).astype(BF16)
    mixed = jnp.concatenate([ms_ref[...], ya], axis=1)
    h1 = x_ref[...] + jnp.dot(mixed, wout_ref[...], preferred_element_type=F32)
    hn = _rms(h1, gffn_ref[...]).astype(BF16)
    g = jnp.dot(hn, wg_ref[...], preferred_element_type=F32)
    u = jnp.dot(hn, wu_ref[...], preferred_element_type=F32)
    a = (g * (1.0 / (1.0 + jnp.exp(-g))) * u).astype(BF16)
    out_ref[...] = h1 + jnp.dot(a, wd_ref[...], preferred_element_type=F32)


def _ffn_call(x2d, ms2d, oa2d, consts):
    n_rows = x2d.shape[0]
    tm = FFN_TM
    row = lambda i: (i, 0)
    once = pl.Buffered(1)
    in_specs = [pl.BlockSpec((tm, D_MODEL), row),
                pl.BlockSpec((tm, D_SSM), row),
                pl.BlockSpec((tm, N_HEADS * V_DIM), row)]
    in_specs += [pl.BlockSpec(c.shape, lambda i: (0, 0), pipeline_mode=once) for c in consts]
    return pl.pallas_call(
        _ffn_kernel,
        out_shape=jax.ShapeDtypeStruct((n_rows, D_MODEL), F32),
        grid=(n_rows // tm,), in_specs=in_specs,
        out_specs=pl.BlockSpec((tm, D_MODEL), row),
        compiler_params=pltpu.CompilerParams(dimension_semantics=("parallel",),
                                             vmem_limit_bytes=VMEM_LIMIT),
        name="outproj_ffn",
    )(x2d, ms2d, oa2d, *consts)


def _rope_tables(first, count):
    pos = np.arange(first, first + count, dtype=np.float64)
    inv_freq = 1.0 / (ROPE_BASE ** (np.arange(0, ROPE, 2, dtype=np.float64) / ROPE))
    ang = pos[:, None] * inv_freq[None, :]
    cos, sin = np.cos(ang), np.sin(ang)
    ctab = np.zeros((count, HEAD_PAD), np.float32)
    stab = np.zeros((count, HEAD_PAD), np.float32)
    ctab[:, :NOPE] = 1.0
    ctab[:, NOPE:QK_DIM] = np.concatenate([cos, cos], axis=1)
    stab[:, NOPE:QK_DIM] = np.concatenate([sin, sin], axis=1)
    return jnp.asarray(ctab), jnp.asarray(stab)


def _head_gains(gain):
    g_r = gain[NOPE:]
    g_r_swapped = jnp.concatenate([g_r[HALF_ROPE:], g_r[:HALF_ROPE]])
    pad = jnp.zeros((HEAD_PAD - QK_DIM,), F32)
    return jnp.stack([jnp.concatenate([gain, pad]),
                      jnp.concatenate([jnp.zeros((NOPE,), F32), g_r_swapped, pad])])


def _rot_half_cols(w):
    return jnp.concatenate([-w[..., HALF_ROPE:], w[..., :HALF_ROPE]], axis=-1)


def _pad_cols(w, left, total):
    return jnp.pad(w, ((0, 0), (left, total - left - w.shape[1])))


def kernel(x, meta_tokens, mix_norm_g, w_in, ssm_a_re, ssm_a_im, ssm_log_dt, ssm_b_re, ssm_b_im,
           ssm_c_re, ssm_c_im, ssm_d, ssm_w_glu, ssm_b_glu, q_lora_norm_g, w_uq, kv_lora_norm_g,
           w_uk, w_uv, q_head_norm_g, k_head_norm_g, ssm_out_norm_g, att_out_norm_g, w_out,
           ffn_norm_g, w_gate, w_up, w_down):
    batch, seq, _ = x.shape
    depth = w_in.shape[0]
    assert depth == 1
    l = 0

    wi = w_in[l]
    o_r = D_SSM + Q_LORA + KV_LORA
    w_r = wi[:, o_r:]
    win = jnp.concatenate([wi[:, :o_r], _pad_cols(w_r, NOPE, HEAD_PAD),
                           _pad_cols(_rot_half_cols(w_r), NOPE, HEAD_PAD)], axis=1).astype(BF16)
    wq3 = w_uq[l].reshape(Q_LORA, N_HEADS, QK_DIM)
    q1 = jnp.pad(wq3, ((0, 0), (0, 0), (0, HEAD_PAD - QK_DIM)))
    q2 = jnp.pad(_rot_half_cols(wq3[..., NOPE:]), ((0, 0), (0, 0), (NOPE, HEAD_PAD - QK_DIM)))
    wq = jnp.concatenate([q1.reshape(Q_LORA, QK_PAD), q2.reshape(Q_LORA, QK_PAD)], axis=1).astype(BF16)
    wk3 = jnp.pad(w_uk[l].reshape(KV_LORA, N_HEADS, NOPE), ((0, 0), (0, 0), (0, HEAD_PAD - NOPE)))
    wv4 = w_uv[l].reshape(KV_LORA, N_HEADS // 2, 2, V_DIM)
    zv = jnp.zeros_like(wv4[:, :, 0])
    wv = jnp.stack([jnp.concatenate([wv4[:, :, 0], zv], axis=-1),
                    jnp.concatenate([zv, wv4[:, :, 1]], axis=-1)], axis=2).reshape(KV_LORA, QK_PAD)
    ones_col = jnp.zeros((2, HEAD_PAD), F32).at[0, V_DIM].set(1.0).at[1, 0].set(1.0)
    vones = jnp.tile(ones_col.reshape(1, 2 * HEAD_PAD), (1, N_HEADS // 2))
    wkv = jnp.concatenate([wk3.reshape(KV_LORA, QK_PAD), wv], axis=1).astype(BF16)
    tabs_m = _rope_tables(0, N_META)
    tabs_f = _rope_tables(N_META, seq)
    proj_consts = (mix_norm_g[l][None], win, q_lora_norm_g[l][None], wq,
                   kv_lora_norm_g[l][None], wkv, vones,
                   _head_gains(q_head_norm_g[l]), _head_gains(k_head_norm_g[l]))

    tm = PROJ_TM
    ffn_f32 = (w_out, w_gate, w_up, w_down)
    u2, q2d, k2d, v2d, u_meta, k_meta, v_meta, wout_b, wg_b, wu_b, wd_b = _proj_call(
        x.reshape(batch * seq, D_MODEL), meta_tokens, tm, tabs_f, tabs_m, seq // tm,
        proj_consts, ffn_f32)

    dt = jnp.exp(ssm_log_dt[l])[:, None]
    lr, li = ssm_a_re[l], ssm_a_im[l]
    mag = jnp.exp(lr * dt)
    ar = mag * jnp.cos(li * dt)
    ai = mag * jnp.sin(li * dt)
    den = lr * lr + li * li
    fr = ((ar - 1.0) * lr + ai * li) / den
    fi = (ai * lr - (ar - 1.0) * li) / den
    br, bi = ssm_b_re[l], ssm_b_im[l]
    bbr = fr[..., None] * br - fi[..., None] * bi
    bbi = fr[..., None] * bi + fi[..., None] * br
    b_in = jnp.swapaxes(jnp.stack([bbr, bbi]), 2, 3).reshape(2, D_SSM, SSM_STATE)
    c_out = jnp.swapaxes(jnp.stack([ssm_c_re[l], ssm_c_im[l]]), 2, 3)
    c_out = jnp.pad(c_out.reshape(2, N_STATE_COLS, SSM_GROUP),
                    ((0, 0), (0, 0), (0, LANES - SSM_GROUP)))
    half_groups = N_GROUPS // 2
    tile_in = np.tile(np.eye(SSM_STATE, dtype=np.float32), (1, half_groups))
    tile_out = np.zeros((LANES, half_groups * SSM_GROUP), np.float32)
    tile_out[:SSM_GROUP] = np.tile(np.eye(SSM_GROUP, dtype=np.float32), (1, half_groups))

    ar_rows = jnp.broadcast_to(ar.reshape(N_SLABS, 1, LANES), (N_SLABS, 2 * batch, LANES))
    ai_flat = ai.reshape(N_SLABS, 1, LANES)
    ai_rows = jnp.concatenate([jnp.broadcast_to(-ai_flat, (N_SLABS, batch, LANES)),
                               jnp.broadcast_to(ai_flat, (N_SLABS, batch, LANES))], axis=1)
    s5_consts = (b_in, c_out, jnp.asarray(tile_in, BF16), jnp.asarray(tile_out, BF16),
                 ar_rows, ai_rows, ssm_d[l][None], ssm_w_glu[l].astype(BF16),
                 ssm_b_glu[l][None], ssm_out_norm_g[l][None])
    mixed_ssm = _s5_call(u2.reshape(batch, seq, D_SSM), u_meta, s5_consts)

    chunk_of = np.arange(ATT_TQ) // CHUNK
    causal = (chunk_of[None, :] <= chunk_of[:, None]).astype(np.float32)
    mask = jnp.asarray(np.concatenate([np.ones((ATT_TQ, LANES), np.float32), causal], axis=1),
                       dtype=BF16)
    score_bound = (LOG2_E * math.sqrt(QK_DIM) * jnp.max(jnp.abs(q_head_norm_g[l]))
                   * jnp.max(jnp.abs(k_head_norm_g[l])))
    attn_args = (q2d.reshape(batch, seq, QK_PAD), k2d.reshape(batch, seq, QK_PAD),
                 v2d.reshape(batch, seq, QK_PAD), k_meta, v_meta, mask)
    y_att = lax.cond(score_bound <= MAX_UNSHIFTED_LOG2_SCORE,
                     functools.partial(_attn_call, online=False),
                     functools.partial(_attn_call, online=True), *attn_args)

    ffn_consts = (att_out_norm_g[l][None], wout_b, ffn_norm_g[l][None],
                  wg_b, wu_b, wd_b)
    out = _ffn_call(x.reshape(batch * seq, D_MODEL), mixed_ssm.reshape(batch * seq, D_SSM),
                    y_att.reshape(batch * seq, N_HEADS * V_DIM), ffn_consts)
    return out.reshape(batch, seq, D_MODEL)
```

```python
import functools
import math

import jax
import jax.numpy as jnp
import numpy as np
from jax import lax
from jax.experimental import pallas as pl
from jax.experimental.pallas import tpu as pltpu

F32 = jnp.float32
BF16 = jnp.bfloat16

D_MODEL = 1024
N_META = 16
CHUNK = 64
D_SSM = 512
SSM_GROUP = 16
N_GROUPS = D_SSM // SSM_GROUP
SSM_STATE = 64
N_HEADS = 8
V_DIM = 64
NOPE = 64
ROPE = 32
HALF_ROPE = ROPE // 2
QK_DIM = NOPE + ROPE
Q_LORA = 256
KV_LORA = 128
D_FF = 2816
ROPE_BASE = 10000.0
EPS = 1e-6
LOG2_E = math.log2(math.e)
MAX_UNSHIFTED_LOG2_SCORE = 40.0

LANES = 128
SUBLANES = 8
HEAD_PAD = LANES
QK_PAD = N_HEADS * HEAD_PAD
N_STATE_COLS = N_GROUPS * SSM_STATE
N_SLABS = N_STATE_COLS // LANES
S5_CHUNK = 128
S5_PITCH = S5_CHUNK + SUBLANES
PROJ_TM = 512
VMEM_LIMIT = 56 * 1024 * 1024


def _rms(x, g):
    return x * lax.rsqrt(jnp.mean(x * x, axis=-1, keepdims=True) + EPS) * g


def _proj_kernel(x_ref, xm_ref, gmix_ref, win_ref, gq_ref, wq_ref, gkv_ref, wkv_ref, vones_ref,
                 hgq_ref, hgk_ref, ctab_ref, stab_ref, ctabm_ref, stabm_ref, *rest):
    p_ref = rest[-1]
    rest = rest[:-1]
    n_cast = (len(rest) - 7) // 2
    u_ref, q_ref, k_ref, v_ref, um_ref, km_ref, vm_ref = rest[n_cast:n_cast + 7]
    for src_ref, dst_ref in zip(rest[:n_cast], rest[n_cast + 7:]):
        dst_ref[...] = src_ref[...].astype(BF16)
    scale = QK_DIM ** -0.5 * LOG2_E
    c0 = D_SSM + Q_LORA

    def in_proj(x):
        xn = _rms(x, gmix_ref[...]).astype(BF16)
        return jnp.dot(xn, win_ref[...], preferred_element_type=F32)

    def heads(p, ctab, stab, rows, u_out, q_out, k_out, v_out):
        u_out[...] = p[:, :D_SSM]

        if q_out is not None:
            cqn = _rms(p[:, D_SSM:c0], gq_ref[...]).astype(BF16)
            q12 = jnp.dot(cqn, wq_ref[...], preferred_element_type=F32)
            t1q, t2q = ctab * hgq_ref[0:1, :], stab * hgq_ref[1:2, :]
            for h in range(N_HEADS):
                q1 = q12[:, h * HEAD_PAD:(h + 1) * HEAD_PAD]
                q2 = q12[:, QK_PAD + h * HEAD_PAD:QK_PAD + (h + 1) * HEAD_PAD]
                r = lax.rsqrt(jnp.sum(q1 * q1, axis=-1, keepdims=True) * (1.0 / QK_DIM) + EPS)
                qh = (q1 * t1q + q2 * t2q) * (r * scale)
                q_out[:, h * HEAD_PAD:(h + 1) * HEAD_PAD] = qh.astype(BF16)

        ckvn = _rms(p[:, c0:c0 + KV_LORA], gkv_ref[...]).astype(BF16)
        kv = jnp.dot(ckvn, wkv_ref[...], preferred_element_type=F32)
        v_out[rows, :] = (kv[:, QK_PAD:] + vones_ref[...]).astype(BF16)
        kr = p[:, c0 + KV_LORA:c0 + KV_LORA + HEAD_PAD]
        kr_rot = p[:, c0 + KV_LORA + HEAD_PAD:c0 + KV_LORA + 2 * HEAD_PAD]
        ss_r = jnp.sum(kr * kr, axis=-1, keepdims=True)
        t1k = ctab * hgk_ref[0:1, :]
        kr_part = kr_rot * (stab * hgk_ref[1:2, :])
        for h in range(N_HEADS):
            kn = kv[:, h * HEAD_PAD:(h + 1) * HEAD_PAD]
            ss = jnp.sum(kn * kn, axis=-1, keepdims=True) + ss_r
            r = lax.rsqrt(ss * (1.0 / QK_DIM) + EPS)
            kh = ((kn + kr) * t1k + kr_part) * r
            k_out[rows, h * HEAD_PAD:(h + 1) * HEAD_PAD] = kh.astype(BF16)

    @pl.when(pl.program_id(0) == 0)
    def _():
        km_ref[...] = jnp.zeros_like(km_ref)
        vm_ref[...] = jnp.zeros_like(vm_ref)
        heads(in_proj(xm_ref[...]), ctabm_ref[...], stabm_ref[...], slice(0, N_META),
              um_ref, None, km_ref, vm_ref)
        p_ref[...] = jnp.zeros_like(p_ref)

    heads(p_ref[...], ctab_ref[...], stab_ref[...], slice(None), u_ref, q_ref, k_ref, v_ref)
    p_ref[...] = in_proj(x_ref[...])


def _const_spec(shape):
    nd = len(shape)
    return pl.BlockSpec(shape, lambda *_: (0,) * nd)


def _proj_call(x2d, x_meta, tm, tabs, tabs_meta, n_tab_blocks, consts, to_bf16=()):
    n_rows = x2d.shape[0]
    n_steps = n_rows // tm
    row_in = lambda s: (jnp.minimum(s, n_steps - 1), 0)
    row = lambda s: (jnp.maximum(s - 1, 0), 0)
    tab = lambda s: (jnp.maximum(s - 1, 0) % n_tab_blocks, 0)
    meta_rows = (LANES, QK_PAD)
    cast_in, cast_out = [], []
    for w in to_bf16:
        _, w_rows, w_cols = w.shape
        rep = next(r for r in (1, 2, 4, 8) if (w_rows * r) % (16 * n_steps) == 0)
        blk_rows = w_rows * rep // n_steps
        cast_in.append(pl.BlockSpec(
            (None, blk_rows, w_cols),
            functools.partial(lambda rep, s: (0, jnp.minimum(s, n_steps - 1) // rep, 0), rep)))
        cast_out.append(pl.BlockSpec(
            (blk_rows, w_cols),
            functools.partial(lambda rep, s: (jnp.minimum(s, n_steps - 1) // rep, 0), rep)))
    in_specs = ([pl.BlockSpec((tm, D_MODEL), row_in), _const_spec(x_meta.shape)]
                + [_const_spec(c.shape) for c in consts]
                + [pl.BlockSpec((tm, HEAD_PAD), tab)] * len(tabs)
                + [_const_spec(t.shape) for t in tabs_meta] + cast_in)
    out_shape = (
        jax.ShapeDtypeStruct((n_rows, D_SSM), F32),
        jax.ShapeDtypeStruct((n_rows, QK_PAD), BF16),
        jax.ShapeDtypeStruct((n_rows, QK_PAD), BF16),
        jax.ShapeDtypeStruct((n_rows, QK_PAD), BF16),
        jax.ShapeDtypeStruct((x_meta.shape[0], D_SSM), F32),
        jax.ShapeDtypeStruct(meta_rows, BF16),
        jax.ShapeDtypeStruct(meta_rows, BF16),
    ) + tuple(jax.ShapeDtypeStruct(w.shape[1:], BF16) for w in to_bf16)
    out_specs = (
        pl.BlockSpec((tm, D_SSM), row),
        pl.BlockSpec((tm, QK_PAD), row),
        pl.BlockSpec((tm, QK_PAD), row),
        pl.BlockSpec((tm, QK_PAD), row),
        _const_spec((x_meta.shape[0], D_SSM)),
        _const_spec(meta_rows),
        _const_spec(meta_rows),
    ) + tuple(cast_out)
    return pl.pallas_call(
        _proj_kernel, out_shape=out_shape, grid=(n_steps + 1,), in_specs=in_specs,
        out_specs=out_specs,
        scratch_shapes=[pltpu.VMEM((tm, D_SSM + Q_LORA + KV_LORA + 2 * HEAD_PAD), F32)],
        compiler_params=pltpu.CompilerParams(dimension_semantics=("arbitrary",),
                                             vmem_limit_bytes=VMEM_LIMIT),
        name="proj_mla",
    )(x2d, x_meta, *consts, *tabs, *tabs_meta, *to_bf16)


def _s5_kernel(u_ref, un_ref, um_ref, bin_ref, cout_ref, tin_ref, tout_ref, ar_ref, ai_ref,
               d_ref, wglu_ref, bglu_ref, g_ref, o_ref, xa_ref, xb_ref, h_ref, bmap_ref, cmap_ref,
               *, batch):
    j = pl.program_id(0)
    slabs_per_half = N_SLABS // 2
    half_ch, half_st = D_SSM // 2, N_STATE_COLS // 2

    def build_maps():
        def diag(shape, row_block, col_block):
            return (lax.broadcasted_iota(jnp.int32, shape, 0) // row_block
                    == lax.broadcasted_iota(jnp.int32, shape, 1) // col_block)
        diag_in = diag((half_ch, half_st), SSM_GROUP, SSM_STATE)
        diag_out = diag((half_st, half_ch), SSM_STATE, SSM_GROUP)
        for part in range(2):
            for half in range(2):
                rows_in = bin_ref[part, half * half_ch:(half + 1) * half_ch, :].astype(BF16)
                tiled = jnp.dot(rows_in, tin_ref[...], preferred_element_type=F32)
                bmap_ref[part, half] = jnp.where(diag_in, tiled, 0.0).astype(BF16)
                rows_out = cout_ref[part, half * half_st:(half + 1) * half_st, :].astype(BF16)
                tiled = jnp.dot(rows_out, tout_ref[...], preferred_element_type=F32)
                cmap_ref[part, half] = jnp.where(diag_out, tiled, 0.0).astype(BF16)

    def project_in(ub, rows, xs_ref):
        for kh in range(2):
            lhs = ub[:, kh * half_ch:(kh + 1) * half_ch]
            xre = jnp.dot(lhs, bmap_ref[0, kh], preferred_element_type=F32)
            xim = jnp.dot(lhs, bmap_ref[1, kh], preferred_element_type=F32)
            for cl in range(slabs_per_half):
                c = kh * slabs_per_half + cl
                for b in range(batch):
                    xs_ref[c, pl.ds(b * S5_PITCH, rows), :] = (
                        xre[b * rows:(b + 1) * rows, cl * LANES:(cl + 1) * LANES])
                    xs_ref[c, pl.ds((batch + b) * S5_PITCH, rows), :] = (
                        xim[b * rows:(b + 1) * rows, cl * LANES:(cl + 1) * LANES])

    def scan(n_steps, xs_ref):
        def body(t, hs):
            new = []
            for c in range(N_SLABS):
                rows = pl.ds(t, 2 * batch, stride=S5_PITCH)
                x8 = xs_ref[c, rows, :]
                h = hs[c]
                hn = ar_ref[c] * h + ai_ref[c] * pltpu.roll(h, batch, 0) + x8
                xs_ref[c, rows, :] = hn
                new.append(hn)
            return tuple(new)

        hs = tuple(h_ref[c] for c in range(N_SLABS))
        hs = lax.fori_loop(0, n_steps, body, hs, unroll=4)
        for c in range(N_SLABS):
            h_ref[c] = hs[c]

    def output_map(xs_ref):
        ys = []
        for nh in range(2):
            def gather(plane0):
                return jnp.concatenate(
                    [jnp.concatenate(
                        [xs_ref[nh * slabs_per_half + cl,
                                pl.ds((plane0 + b) * S5_PITCH, S5_CHUNK), :]
                         for cl in range(slabs_per_half)], axis=1)
                     for b in range(batch)], axis=0).astype(BF16)
            yre = jnp.dot(gather(0), cmap_ref[0, nh], preferred_element_type=F32)
            yim = jnp.dot(gather(batch), cmap_ref[1, nh], preferred_element_type=F32)
            ys.append(yre - yim)
        uf = u_ref[...].reshape(batch * S5_CHUNK, D_SSM)
        y = jnp.concatenate(ys, axis=1) + d_ref[...] * uf
        z = 0.5 * y * (1.0 + jnp.tanh(math.sqrt(2.0 / math.pi) * (y + 0.044715 * (y * y * y))))
        gate = jnp.dot(z.astype(BF16), wglu_ref[...], preferred_element_type=F32) + bglu_ref[...]
        out = z * (1.0 / (1.0 + jnp.exp(-gate)))
        o_ref[...] = _rms(out, g_ref[...]).astype(BF16).reshape(batch, S5_CHUNK, D_SSM)

    @pl.when(j == 0)
    def _():
        build_maps()
        h_ref[...] = jnp.zeros_like(h_ref)
        um = um_ref[...].astype(BF16)
        project_in(jnp.concatenate([um] * batch, axis=0), N_META, xb_ref)
        scan(N_META, xb_ref)
        project_in(u_ref[...].reshape(batch * S5_CHUNK, D_SSM).astype(BF16), S5_CHUNK, xa_ref)

    def step(x_cur, x_next):
        scan(S5_CHUNK, x_cur)
        output_map(x_cur)
        project_in(un_ref[...].reshape(batch * S5_CHUNK, D_SSM).astype(BF16), S5_CHUNK, x_next)

    @pl.when(lax.rem(j, 2) == 0)
    def _():
        step(xa_ref, xb_ref)

    @pl.when(lax.rem(j, 2) == 1)
    def _():
        step(xb_ref, xa_ref)


def _s5_call(u3, u_meta, consts):
    batch, seq, _ = u3.shape
    assert 2 * batch == SUBLANES and seq % S5_CHUNK == 0
    n_chunks = seq // S5_CHUNK
    grid = (n_chunks,)
    chunk_blk = (batch, S5_CHUNK, D_SSM)
    in_specs = [pl.BlockSpec(chunk_blk, lambda j: (0, j, 0)),
                pl.BlockSpec(chunk_blk, lambda j: (0, jnp.minimum(j + 1, n_chunks - 1), 0)),
                _const_spec(u_meta.shape)] + [_const_spec(c.shape) for c in consts]
    planes = pltpu.VMEM((N_SLABS, 2 * batch * S5_PITCH, LANES), F32)
    return pl.pallas_call(
        functools.partial(_s5_kernel, batch=batch),
        out_shape=jax.ShapeDtypeStruct((batch, seq, D_SSM), BF16),
        grid=grid, in_specs=in_specs,
        out_specs=pl.BlockSpec(chunk_blk, lambda j: (0, j, 0)),
        scratch_shapes=[planes, planes,
                        pltpu.VMEM((N_SLABS, 2 * batch, LANES), F32),
                        pltpu.VMEM((2, 2, D_SSM // 2, N_STATE_COLS // 2), BF16),
                        pltpu.VMEM((2, 2, N_STATE_COLS // 2, D_SSM // 2), BF16)],
        compiler_params=pltpu.CompilerParams(dimension_semantics=("arbitrary",),
                                             vmem_limit_bytes=VMEM_LIMIT),
        name="s5_mixer",
    )(u3, u3, u_meta, *consts)


ATT_TQ = 1024
ATT_TK = 1024
ATT_SUB = 256
ATT_HEADS = 4


def _attn_kernel(q_ref, k_ref, v_ref, km_ref, vm_ref, mask_ref, o_ref, acc_ref, m_ref, *, online):
    nt = (((1,), (1,)), ((), ()))
    n_q = q_ref.shape[0] // ATT_TQ
    head_lanes = [slice(h * HEAD_PAD, (h + 1) * HEAD_PAD) for h in range(ATT_HEADS)]

    def step(h, sub, q, kblk, vblk, mask):
        s = lax.dot_general(q, kblk, nt, preferred_element_type=F32)
        if online:
            if mask is not None:
                s = jnp.where(mask > 0, s, -jnp.inf)
            m = m_ref[h, sub]
            m_new = jnp.maximum(m, jnp.max(s, axis=-1, keepdims=True))
            p = jnp.exp2(s - m_new).astype(BF16)
            acc_ref[h, sub] = (jnp.exp2(m - m_new) * acc_ref[h, sub]
                               + jnp.dot(p, vblk, preferred_element_type=F32))
            m_ref[h, sub] = m_new
        else:
            p = jnp.exp2(s).astype(BF16)
            if mask is not None:
                p = p * mask
            acc_ref[h, sub] += jnp.dot(p, vblk, preferred_element_type=F32)

    def q_tile(qi, _):
        q0 = pl.multiple_of(qi * ATT_TQ, ATT_TQ)
        rows = pl.ds(q0, ATT_TQ)
        qs = [q_ref[rows, hl] for hl in head_lanes]
        acc_ref[...] = jnp.zeros_like(acc_ref)
        if online:
            m_ref[...] = jnp.full(m_ref.shape, -1e30, F32)

        def body(kb, _):
            krows = pl.ds(pl.multiple_of(kb * ATT_TK, ATT_TK), ATT_TK)
            for h, hl in enumerate(head_lanes):
                step(h, slice(None), qs[h], k_ref[krows, hl], v_ref[krows, hl], None)
            return 0

        lax.fori_loop(0, qi * (ATT_TQ // ATT_TK), body, 0)

        for i in range(ATT_TQ // ATT_SUB):
            sub = slice(i * ATT_SUB, ATT_TQ)
            krows = pl.ds(q0 + i * ATT_SUB, ATT_SUB)
            for h, hl in enumerate(head_lanes):
                kblk, vblk = k_ref[krows, hl], v_ref[krows, hl]
                mask = mask_ref[sub, LANES + i * ATT_SUB:LANES + (i + 1) * ATT_SUB]
                if i == 0:
                    kblk = jnp.concatenate([km_ref[:, hl], kblk], axis=0)
                    vblk = jnp.concatenate([vm_ref[:, hl], vblk], axis=0)
                    mask = mask_ref[sub, :LANES + ATT_SUB]
                step(h, sub, qs[h][sub], kblk, vblk, mask)
        lane = lax.broadcasted_iota(jnp.int32, (ATT_TQ, HEAD_PAD), 1)
        for hp in range(ATT_HEADS // 2):
            even, odd = acc_ref[2 * hp], acc_ref[2 * hp + 1]
            o_even = even * (1.0 / even[:, V_DIM:V_DIM + 1])
            o_odd = odd * (1.0 / odd[:, 0:1])
            o_ref[rows, hp * HEAD_PAD:(hp + 1) * HEAD_PAD] = (
                jnp.where(lane < V_DIM, o_even, o_odd).astype(BF16))
        return 0

    lax.fori_loop(0, n_q, q_tile, 0)


def _attn_call(q3, k3, v3, k_meta, v_meta, mask, *, online):
    batch, seq, _ = q3.shape
    grid = (batch, N_HEADS // ATT_HEADS)
    seq_blk = pl.BlockSpec((None, seq, ATT_HEADS * HEAD_PAD), lambda b, hg: (b, 0, hg))
    meta_blk = pl.BlockSpec((LANES, ATT_HEADS * HEAD_PAD), lambda b, hg: (0, hg))
    return pl.pallas_call(
        functools.partial(_attn_kernel, online=online),
        out_shape=jax.ShapeDtypeStruct((batch, seq, N_HEADS * V_DIM), BF16),
        grid=grid,
        in_specs=[seq_blk, seq_blk, seq_blk, meta_blk, meta_blk, _const_spec(mask.shape)],
        out_specs=pl.BlockSpec((None, seq, ATT_HEADS * V_DIM), lambda b, hg: (b, 0, hg)),
        scratch_shapes=[pltpu.VMEM((ATT_HEADS, ATT_TQ, HEAD_PAD), F32),
                        pltpu.VMEM((ATT_HEADS, ATT_TQ, 1), F32)],
        compiler_params=pltpu.CompilerParams(dimension_semantics=("parallel", "parallel"),
                                             vmem_limit_bytes=VMEM_LIMIT),
        name="mla_attention_online" if online else "mla_attention",
    )(q3, k3, v3, k_meta, v_meta, mask)


FFN_TM = 512


def _ffn_kernel(x_ref, ms_ref, oa_ref, gatt_ref, wout_ref, gffn_ref, wg_ref, wu_ref, wd_ref,
                out_ref):
    ya = _rms(oa_ref[...].astype(F32), gatt_ref[...]).astype(BF16)
    mixed = jnp.concatenate([ms_ref[...], ya], axis=1)
    h1 = x_ref[...] + jnp.dot(mixed, wout_ref[...], preferred_element_type=F32)
    hn = _rms(h1, gffn_ref[...]).astype(BF16)
    g = jnp.dot(hn, wg_ref[...], preferred_element_type=F32)
    u = jnp.dot(hn, wu_ref[...], preferred_element_type=F32)
    a = (g * (1.0 / (1.0 + jnp.exp(-g))) * u).astype(BF16)
    out_ref[...] = h1 + jnp.dot(a, wd_ref[...], preferred_element_type=F32)


def _ffn_call(x2d, ms2d, oa2d, consts):
    n_rows = x2d.shape[0]
    tm = FFN_TM
    row = lambda i: (i, 0)
    once = pl.Buffered(1)
    in_specs = [pl.BlockSpec((tm, D_MODEL), row),
                pl.BlockSpec((tm, D_SSM), row),
                pl.BlockSpec((tm, N_HEADS * V_DIM), row)]
    in_specs += [pl.BlockSpec(c.shape, lambda i: (0, 0), pipeline_mode=once) for c in consts]
    return pl.pallas_call(
        _ffn_kernel,
        out_shape=jax.ShapeDtypeStruct((n_rows, D_MODEL), F32),
        grid=(n_rows // tm,), in_specs=in_specs,
        out_specs=pl.BlockSpec((tm, D_MODEL), row),
        compiler_params=pltpu.CompilerParams(dimension_semantics=("parallel",),
                                             vmem_limit_bytes=VMEM_LIMIT),
        name="outproj_ffn",
    )(x2d, ms2d, oa2d, *consts)


def _rope_tables(first, count):
    pos = np.arange(first, first + count, dtype=np.float64)
    inv_freq = 1.0 / (ROPE_BASE ** (np.arange(0, ROPE, 2, dtype=np.float64) / ROPE))
    ang = pos[:, None] * inv_freq[None, :]
    cos, sin = np.cos(ang), np.sin(ang)
    ctab = np.zeros((count, HEAD_PAD), np.float32)
    stab = np.zeros((count, HEAD_PAD), np.float32)
    ctab[:, :NOPE] = 1.0
    ctab[:, NOPE:QK_DIM] = np.concatenate([cos, cos], axis=1)
    stab[:, NOPE:QK_DIM] = np.concatenate([sin, sin], axis=1)
    return jnp.asarray(ctab), jnp.asarray(stab)


def _head_gains(gain):
    g_r = gain[NOPE:]
    g_r_swapped = jnp.concatenate([g_r[HALF_ROPE:], g_r[:HALF_ROPE]])
    pad = jnp.zeros((HEAD_PAD - QK_DIM,), F32)
    return jnp.stack([jnp.concatenate([gain, pad]),
                      jnp.concatenate([jnp.zeros((NOPE,), F32), g_r_swapped, pad])])


def _rot_half_cols(w):
    return jnp.concatenate([-w[..., HALF_ROPE:], w[..., :HALF_ROPE]], axis=-1)


def _pad_cols(w, left, total):
    return jnp.pad(w, ((0, 0), (left, total - left - w.shape[1])))


def kernel(x, meta_tokens, mix_norm_g, w_in, ssm_a_re, ssm_a_im, ssm_log_dt, ssm_b_re, ssm_b_im,
           ssm_c_re, ssm_c_im, ssm_d, ssm_w_glu, ssm_b_glu, q_lora_norm_g, w_uq, kv_lora_norm_g,
           w_uk, w_uv, q_head_norm_g, k_head_norm_g, ssm_out_norm_g, att_out_norm_g, w_out,
           ffn_norm_g, w_gate, w_up, w_down):
    batch, seq, _ = x.shape
    depth = w_in.shape[0]
    assert depth == 1
    l = 0

    wi = w_in[l]
    o_r = D_SSM + Q_LORA + KV_LORA
    w_r = wi[:, o_r:]
    win = jnp.concatenate([wi[:, :o_r], _pad_cols(w_r, NOPE, HEAD_PAD),
                           _pad_cols(_rot_half_cols(w_r), NOPE, HEAD_PAD)], axis=1).astype(BF16)
    wq3 = w_uq[l].reshape(Q_LORA, N_HEADS, QK_DIM)
    q1 = jnp.pad(wq3, ((0, 0), (0, 0), (0, HEAD_PAD - QK_DIM)))
    q2 = jnp.pad(_rot_half_cols(wq3[..., NOPE:]), ((0, 0), (0, 0), (NOPE, HEAD_PAD - QK_DIM)))
    wq = jnp.concatenate([q1.reshape(Q_LORA, QK_PAD), q2.reshape(Q_LORA, QK_PAD)], axis=1).astype(BF16)
    wk3 = jnp.pad(w_uk[l].reshape(KV_LORA, N_HEADS, NOPE), ((0, 0), (0, 0), (0, HEAD_PAD - NOPE)))
    wv4 = w_uv[l].reshape(KV_LORA, N_HEADS // 2, 2, V_DIM)
    zv = jnp.zeros_like(wv4[:, :, 0])
    wv = jnp.stack([jnp.concatenate([wv4[:, :, 0], zv], axis=-1),
                    jnp.concatenate([zv, wv4[:, :, 1]], axis=-1)], axis=2).reshape(KV_LORA, QK_PAD)
    ones_col = jnp.zeros((2, HEAD_PAD), F32).at[0, V_DIM].set(1.0).at[1, 0].set(1.0)
    vones = jnp.tile(ones_col.reshape(1, 2 * HEAD_PAD), (1, N_HEADS // 2))
    wkv = jnp.concatenate([wk3.reshape(KV_LORA, QK_PAD), wv], axis=1).astype(BF16)
    tabs_m = _rope_tables(0, N_META)
    tabs_f = _rope_tables(N_META, seq)
    proj_consts = (mix_norm_g[l][None], win, q_lora_norm_g[l][None], wq,
                   kv_lora_norm_g[l][None], wkv, vones,
                   _head_gains(q_head_norm_g[l]), _head_gains(k_head_norm_g[l]))

    tm = PROJ_TM
    ffn_f32 = (w_out, w_gate, w_up, w_down)
    u2, q2d, k2d, v2d, u_meta, k_meta, v_meta, wout_b, wg_b, wu_b, wd_b = _proj_call(
        x.reshape(batch * seq, D_MODEL), meta_tokens, tm, tabs_f, tabs_m, seq // tm,
        proj_consts, ffn_f32)

    dt = jnp.exp(ssm_log_dt[l])[:, None]
    lr, li = ssm_a_re[l], ssm_a_im[l]
    mag = jnp.exp(lr * dt)
    ar = mag * jnp.cos(li * dt)
    ai = mag * jnp.sin(li * dt)
    den = lr * lr + li * li
    fr = ((ar - 1.0) * lr + ai * li) / den
    fi = (ai * lr - (ar - 1.0) * li) / den
    br, bi = ssm_b_re[l], ssm_b_im[l]
    bbr = fr[..., None] * br - fi[..., None] * bi
    bbi = fr[..., None] * bi + fi[..., None] * br
    b_in = jnp.swapaxes(jnp.stack([bbr, bbi]), 2, 3).reshape(2, D_SSM, SSM_STATE)
    c_out = jnp.swapaxes(jnp.stack([ssm_c_re[l], ssm_c_im[l]]), 2, 3)
    c_out = jnp.pad(c_out.reshape(2, N_STATE_COLS, SSM_GROUP),
                    ((0, 0), (0, 0), (0, LANES - SSM_GROUP)))
    half_groups = N_GROUPS // 2
    tile_in = np.tile(np.eye(SSM_STATE, dtype=np.float32), (1, half_groups))
    tile_out = np.zeros((LANES, half_groups * SSM_GROUP), np.float32)
    tile_out[:SSM_GROUP] = np.tile(np.eye(SSM_GROUP, dtype=np.float32), (1, half_groups))

    ar_rows = jnp.broadcast_to(ar.reshape(N_SLABS, 1, LANES), (N_SLABS, 2 * batch, LANES))
    ai_flat = ai.reshape(N_SLABS, 1, LANES)
    ai_rows = jnp.concatenate([jnp.broadcast_to(-ai_flat, (N_SLABS, batch, LANES)),
                               jnp.broadcast_to(ai_flat, (N_SLABS, batch, LANES))], axis=1)
    s5_consts = (b_in, c_out, jnp.asarray(tile_in, BF16), jnp.asarray(tile_out, BF16),
                 ar_rows, ai_rows, ssm_d[l][None], ssm_w_glu[l].astype(BF16),
                 ssm_b_glu[l][None], ssm_out_norm_g[l][None])
    mixed_ssm = _s5_call(u2.reshape(batch, seq, D_SSM), u_meta, s5_consts)

    chunk_of = np.arange(ATT_TQ) // CHUNK
    causal = (chunk_of[None, :] <= chunk_of[:, None]).astype(np.float32)
    mask = jnp.asarray(np.concatenate([np.ones((ATT_TQ, LANES), np.float32), causal], axis=1),
                       dtype=BF16)
    score_bound = (LOG2_E * math.sqrt(QK_DIM) * jnp.max(jnp.abs(q_head_norm_g[l]))
                   * jnp.max(jnp.abs(k_head_norm_g[l])))
    attn_args = (q2d.reshape(batch, seq, QK_PAD), k2d.reshape(batch, seq, QK_PAD),
                 v2d.reshape(batch, seq, QK_PAD), k_meta, v_meta, mask)
    y_att = lax.cond(score_bound <= MAX_UNSHIFTED_LOG2_SCORE,
                     functools.partial(_attn_call, online=False),
                     functools.partial(_attn_call, online=True), *attn_args)

    ffn_consts = (att_out_norm_g[l][None], wout_b, ffn_norm_g[l][None],
                  wg_b, wu_b, wd_b)
    out = _ffn_call(x.reshape(batch * seq, D_MODEL), mixed_ssm.reshape(batch * seq, D_SSM),
                    y_att.reshape(batch * seq, N_HEADS * V_DIM), ffn_consts)
    return out.reshape(batch, seq, D_MODEL)
```

```python
import functools
import math

import jax
import jax.numpy as jnp
import numpy as np
from jax import lax
from jax.experimental import pallas as pl
from jax.experimental.pallas import tpu as pltpu

F32 = jnp.float32
BF16 = jnp.bfloat16

D_MODEL = 1024
N_META = 16
CHUNK = 64
D_SSM = 512
SSM_GROUP = 16
N_GROUPS = D_SSM // SSM_GROUP
SSM_STATE = 64
N_HEADS = 8
V_DIM = 64
NOPE = 64
ROPE = 32
HALF_ROPE = ROPE // 2
QK_DIM = NOPE + ROPE
Q_LORA = 256
KV_LORA = 128
D_FF = 2816
ROPE_BASE = 10000.0
EPS = 1e-6
LOG2_E = math.log2(math.e)
MAX_UNSHIFTED_LOG2_SCORE = 40.0

LANES = 128
SUBLANES = 8
HEAD_PAD = LANES
QK_PAD = N_HEADS * HEAD_PAD
N_STATE_COLS = N_GROUPS * SSM_STATE
N_SLABS = N_STATE_COLS // LANES
S5_CHUNK = 128
S5_PITCH = S5_CHUNK + SUBLANES
PROJ_TM = 512
VMEM_LIMIT = 56 * 1024 * 1024


def _rms(x, g):
    return x * lax.rsqrt(jnp.mean(x * x, axis=-1, keepdims=True) + EPS) * g


def _proj_kernel(x_ref, xm_ref, gmix_ref, win_ref, gq_ref, wq_ref, gkv_ref, wkv_ref, vones_ref,
                 hgq_ref, hgk_ref, ctab_ref, stab_ref, ctabm_ref, stabm_ref, *rest):
    n_cast = (len(rest) - 7) // 2
    u_ref, q_ref, k_ref, v_ref, um_ref, km_ref, vm_ref = rest[n_cast:n_cast + 7]
    for src_ref, dst_ref in zip(rest[:n_cast], rest[n_cast + 7:]):
        dst_ref[...] = src_ref[...].astype(BF16)
    scale = QK_DIM ** -0.5 * LOG2_E
    c0 = D_SSM + Q_LORA

    def project(x, ctab, stab, rows, u_out, q_out, k_out, v_out):
        xn = _rms(x, gmix_ref[...]).astype(BF16)
        p = jnp.dot(xn, win_ref[...], preferred_element_type=F32)
        u_out[...] = p[:, :D_SSM]

        if q_out is not None:
            cqn = _rms(p[:, D_SSM:c0], gq_ref[...]).astype(BF16)
            q12 = jnp.dot(cqn, wq_ref[...], preferred_element_type=F32)
            t1q, t2q = ctab * hgq_ref[0:1, :], stab * hgq_ref[1:2, :]
            for h in range(N_HEADS):
                q1 = q12[:, h * HEAD_PAD:(h + 1) * HEAD_PAD]
                q2 = q12[:, QK_PAD + h * HEAD_PAD:QK_PAD + (h + 1) * HEAD_PAD]
                r = lax.rsqrt(jnp.sum(q1 * q1, axis=-1, keepdims=True) * (1.0 / QK_DIM) + EPS)
                qh = (q1 * t1q + q2 * t2q) * (r * scale)
                q_out[:, h * HEAD_PAD:(h + 1) * HEAD_PAD] = qh.astype(BF16)

        ckvn = _rms(p[:, c0:c0 + KV_LORA], gkv_ref[...]).astype(BF16)
        kv = jnp.dot(ckvn, wkv_ref[...], preferred_element_type=F32)
        v_out[rows, :] = (kv[:, QK_PAD:] + vones_ref[...]).astype(BF16)
        kr = p[:, c0 + KV_LORA:c0 + KV_LORA + HEAD_PAD]
        kr_rot = p[:, c0 + KV_LORA + HEAD_PAD:c0 + KV_LORA + 2 * HEAD_PAD]
        ss_r = jnp.sum(kr * kr, axis=-1, keepdims=True)
        t1k = ctab * hgk_ref[0:1, :]
        kr_part = kr_rot * (stab * hgk_ref[1:2, :])
        for h in range(N_HEADS):
            kn = kv[:, h * HEAD_PAD:(h + 1) * HEAD_PAD]
            ss = jnp.sum(kn * kn, axis=-1, keepdims=True) + ss_r
            r = lax.rsqrt(ss * (1.0 / QK_DIM) + EPS)
            kh = ((kn + kr) * t1k + kr_part) * r
            k_out[rows, h * HEAD_PAD:(h + 1) * HEAD_PAD] = kh.astype(BF16)

    @pl.when(pl.program_id(0) == 0)
    def _():
        km_ref[...] = jnp.zeros_like(km_ref)
        vm_ref[...] = jnp.zeros_like(vm_ref)
        project(xm_ref[...], ctabm_ref[...], stabm_ref[...], slice(0, N_META),
                um_ref, None, km_ref, vm_ref)

    project(x_ref[...], ctab_ref[...], stab_ref[...], slice(None), u_ref, q_ref, k_ref, v_ref)


def _const_spec(shape):
    nd = len(shape)
    return pl.BlockSpec(shape, lambda *_: (0,) * nd)


def _cast_specs(to_bf16, n_steps):
    cast_in, cast_out = [], []
    for w in to_bf16:
        _, w_rows, w_cols = w.shape
        rep = next(r for r in (1, 2, 4, 8) if (w_rows * r) % (16 * n_steps) == 0)
        blk_rows = w_rows * rep // n_steps
        cast_in.append(pl.BlockSpec((None, blk_rows, w_cols),
                                    functools.partial(lambda rep, i: (0, i // rep, 0), rep)))
        cast_out.append(pl.BlockSpec((blk_rows, w_cols),
                                     functools.partial(lambda rep, i: (i // rep, 0), rep)))
    return cast_in, cast_out, tuple(jax.ShapeDtypeStruct(w.shape[1:], BF16) for w in to_bf16)


def _proj_call(x2d, x_meta, tm, tabs, tabs_meta, n_tab_blocks, consts, to_bf16=()):
    n_rows = x2d.shape[0]
    n_steps = n_rows // tm
    row = lambda i: (i, 0)
    tab = lambda i: (i % n_tab_blocks, 0)
    meta_rows = (LANES, QK_PAD)
    cast_in, cast_out, cast_shapes = _cast_specs(to_bf16, n_steps)
    in_specs = ([pl.BlockSpec((tm, D_MODEL), row), _const_spec(x_meta.shape)]
                + [_const_spec(c.shape) for c in consts]
                + [pl.BlockSpec((tm, HEAD_PAD), tab)] * len(tabs)
                + [_const_spec(t.shape) for t in tabs_meta] + cast_in)
    out_shape = (
        jax.ShapeDtypeStruct((n_rows, D_SSM), F32),
        jax.ShapeDtypeStruct((n_rows, QK_PAD), BF16),
        jax.ShapeDtypeStruct((n_rows, QK_PAD), BF16),
        jax.ShapeDtypeStruct((n_rows, QK_PAD), BF16),
        jax.ShapeDtypeStruct((x_meta.shape[0], D_SSM), F32),
        jax.ShapeDtypeStruct(meta_rows, BF16),
        jax.ShapeDtypeStruct(meta_rows, BF16),
    ) + cast_shapes
    out_specs = (
        pl.BlockSpec((tm, D_SSM), row),
        pl.BlockSpec((tm, QK_PAD), row),
        pl.BlockSpec((tm, QK_PAD), row),
        pl.BlockSpec((tm, QK_PAD), row),
        _const_spec((x_meta.shape[0], D_SSM)),
        _const_spec(meta_rows),
        _const_spec(meta_rows),
    ) + tuple(cast_out)
    return pl.pallas_call(
        _proj_kernel, out_shape=out_shape, grid=(n_steps,), in_specs=in_specs,
        out_specs=out_specs,
        compiler_params=pltpu.CompilerParams(dimension_semantics=("arbitrary",),
                                             vmem_limit_bytes=VMEM_LIMIT),
        name="proj_mla",
    )(x2d, x_meta, *consts, *tabs, *tabs_meta, *to_bf16)


def _s5_kernel(u_ref, un_ref, um_ref, bin_ref, cout_ref, tin_ref, tout_ref, ar_ref, ai_ref,
               d_ref, wglu_ref, bglu_ref, g_ref, *rest, batch):
    xa_ref, xb_ref, h_ref, bmap_ref, cmap_ref = rest[-5:]
    n_cast = (len(rest) - 6) // 2
    o_ref = rest[n_cast]
    for src_ref, dst_ref in zip(rest[:n_cast], rest[n_cast + 1:2 * n_cast + 1]):
        dst_ref[...] = src_ref[...].astype(BF16)
    j = pl.program_id(0)
    slabs_per_half = N_SLABS // 2
    half_ch, half_st = D_SSM // 2, N_STATE_COLS // 2

    def build_maps():
        def diag(shape, row_block, col_block):
            return (lax.broadcasted_iota(jnp.int32, shape, 0) // row_block
                    == lax.broadcasted_iota(jnp.int32, shape, 1) // col_block)
        diag_in = diag((half_ch, half_st), SSM_GROUP, SSM_STATE)
        diag_out = diag((half_st, half_ch), SSM_STATE, SSM_GROUP)
        for part in range(2):
            for half in range(2):
                rows_in = bin_ref[part, half * half_ch:(half + 1) * half_ch, :].astype(BF16)
                tiled = jnp.dot(rows_in, tin_ref[...], preferred_element_type=F32)
                bmap_ref[part, half] = jnp.where(diag_in, tiled, 0.0).astype(BF16)
                rows_out = cout_ref[part, half * half_st:(half + 1) * half_st, :].astype(BF16)
                tiled = jnp.dot(rows_out, tout_ref[...], preferred_element_type=F32)
                cmap_ref[part, half] = jnp.where(diag_out, tiled, 0.0).astype(BF16)

    def project_in(ub, rows, xs_ref):
        for kh in range(2):
            lhs = ub[:, kh * half_ch:(kh + 1) * half_ch]
            xre = jnp.dot(lhs, bmap_ref[0, kh], preferred_element_type=F32)
            xim = jnp.dot(lhs, bmap_ref[1, kh], preferred_element_type=F32)
            for cl in range(slabs_per_half):
                c = kh * slabs_per_half + cl
                for b in range(batch):
                    xs_ref[c, pl.ds(b * S5_PITCH, rows), :] = (
                        xre[b * rows:(b + 1) * rows, cl * LANES:(cl + 1) * LANES])
                    xs_ref[c, pl.ds((batch + b) * S5_PITCH, rows), :] = (
                        xim[b * rows:(b + 1) * rows, cl * LANES:(cl + 1) * LANES])

    def scan(n_steps, xs_ref):
        def body(t, hs):
            new = []
            for c in range(N_SLABS):
                rows = pl.ds(t, 2 * batch, stride=S5_PITCH)
                x8 = xs_ref[c, rows, :]
                h = hs[c]
                hn = ar_ref[c] * h + ai_ref[c] * pltpu.roll(h, batch, 0) + x8
                xs_ref[c, rows, :] = hn
                new.append(hn)
            return tuple(new)

        hs = tuple(h_ref[c] for c in range(N_SLABS))
        hs = lax.fori_loop(0, n_steps, body, hs, unroll=4)
        for c in range(N_SLABS):
            h_ref[c] = hs[c]

    def output_map(xs_ref):
        ys = []
        for nh in range(2):
            def gather(plane0):
                return jnp.concatenate(
                    [jnp.concatenate(
                        [xs_ref[nh * slabs_per_half + cl,
                                pl.ds((plane0 + b) * S5_PITCH, S5_CHUNK), :]
                         for cl in range(slabs_per_half)], axis=1)
                     for b in range(batch)], axis=0).astype(BF16)
            yre = jnp.dot(gather(0), cmap_ref[0, nh], preferred_element_type=F32)
            yim = jnp.dot(gather(batch), cmap_ref[1, nh], preferred_element_type=F32)
            ys.append(yre - yim)
        uf = u_ref[...].reshape(batch * S5_CHUNK, D_SSM)
        y = jnp.concatenate(ys, axis=1) + d_ref[...] * uf
        z = 0.5 * y * (1.0 + jnp.tanh(math.sqrt(2.0 / math.pi) * (y + 0.044715 * (y * y * y))))
        gate = jnp.dot(z.astype(BF16), wglu_ref[...], preferred_element_type=F32) + bglu_ref[...]
        out = z * (1.0 / (1.0 + jnp.exp(-gate)))
        o_ref[...] = _rms(out, g_ref[...]).astype(BF16).reshape(batch, S5_CHUNK, D_SSM)

    @pl.when(j == 0)
    def _():
        build_maps()
        h_ref[...] = jnp.zeros_like(h_ref)
        um = um_ref[...].astype(BF16)
        project_in(jnp.concatenate([um] * batch, axis=0), N_META, xb_ref)
        scan(N_META, xb_ref)
        project_in(u_ref[...].reshape(batch * S5_CHUNK, D_SSM).astype(BF16), S5_CHUNK, xa_ref)

    def step(x_cur, x_next):
        scan(S5_CHUNK, x_cur)
        output_map(x_cur)
        project_in(un_ref[...].reshape(batch * S5_CHUNK, D_SSM).astype(BF16), S5_CHUNK, x_next)

    @pl.when(lax.rem(j, 2) == 0)
    def _():
        step(xa_ref, xb_ref)

    @pl.when(lax.rem(j, 2) == 1)
    def _():
        step(xb_ref, xa_ref)


def _s5_call(u3, u_meta, consts, to_bf16=()):
    batch, seq, _ = u3.shape
    assert 2 * batch == SUBLANES and seq % S5_CHUNK == 0
    n_chunks = seq // S5_CHUNK
    grid = (n_chunks,)
    chunk_blk = (batch, S5_CHUNK, D_SSM)
    cast_in, cast_out, cast_shapes = _cast_specs(to_bf16, n_chunks)
    in_specs = [pl.BlockSpec(chunk_blk, lambda j: (0, j, 0)),
                pl.BlockSpec(chunk_blk, lambda j: (0, jnp.minimum(j + 1, n_chunks - 1), 0)),
                _const_spec(u_meta.shape)] + [_const_spec(c.shape) for c in consts] + cast_in
    planes = pltpu.VMEM((N_SLABS, 2 * batch * S5_PITCH, LANES), F32)
    return pl.pallas_call(
        functools.partial(_s5_kernel, batch=batch),
        out_shape=(jax.ShapeDtypeStruct((batch, seq, D_SSM), BF16),) + cast_shapes,
        grid=grid, in_specs=in_specs,
        out_specs=(pl.BlockSpec(chunk_blk, lambda j: (0, j, 0)),) + tuple(cast_out),
        scratch_shapes=[planes, planes,
                        pltpu.VMEM((N_SLABS, 2 * batch, LANES), F32),
                        pltpu.VMEM((2, 2, D_SSM // 2, N_STATE_COLS // 2), BF16),
                        pltpu.VMEM((2, 2, N_STATE_COLS // 2, D_SSM // 2), BF16)],
        compiler_params=pltpu.CompilerParams(dimension_semantics=("arbitrary",),
                                             vmem_limit_bytes=VMEM_LIMIT),
        name="s5_mixer",
    )(u3, u3, u_meta, *consts, *to_bf16)


ATT_TQ = 1024
ATT_TK = 1024
ATT_SUB = 256
ATT_HEADS = 4


def _attn_kernel(q_ref, k_ref, v_ref, km_ref, vm_ref, mask_ref, o_ref, acc_ref, m_ref, *, online):
    nt = (((1,), (1,)), ((), ()))
    n_q = q_ref.shape[0] // ATT_TQ
    head_lanes = [slice(h * HEAD_PAD, (h + 1) * HEAD_PAD) for h in range(ATT_HEADS)]

    def step(h, sub, q, kblk, vblk, mask):
        s = lax.dot_general(q, kblk, nt, preferred_element_type=F32)
        if online:
            if mask is not None:
                s = jnp.where(mask > 0, s, -jnp.inf)
            m = m_ref[h, sub]
            m_new = jnp.maximum(m, jnp.max(s, axis=-1, keepdims=True))
            p = jnp.exp2(s - m_new).astype(BF16)
            acc_ref[h, sub] = (jnp.exp2(m - m_new) * acc_ref[h, sub]
                               + jnp.dot(p, vblk, preferred_element_type=F32))
            m_ref[h, sub] = m_new
        else:
            p = jnp.exp2(s).astype(BF16)
            if mask is not None:
                p = p * mask
            acc_ref[h, sub] += jnp.dot(p, vblk, preferred_element_type=F32)

    def q_tile(qi, _):
        q0 = pl.multiple_of(qi * ATT_TQ, ATT_TQ)
        rows = pl.ds(q0, ATT_TQ)
        qs = [q_ref[rows, hl] for hl in head_lanes]
        acc_ref[...] = jnp.zeros_like(acc_ref)
        if online:
            m_ref[...] = jnp.full(m_ref.shape, -1e30, F32)

        def body(kb, _):
            krows = pl.ds(pl.multiple_of(kb * ATT_TK, ATT_TK), ATT_TK)
            for h, hl in enumerate(head_lanes):
                step(h, slice(None), qs[h], k_ref[krows, hl], v_ref[krows, hl], None)
            return 0

        lax.fori_loop(0, qi * (ATT_TQ // ATT_TK), body, 0)

        for i in range(ATT_TQ // ATT_SUB):
            sub = slice(i * ATT_SUB, ATT_TQ)
            krows = pl.ds(q0 + i * ATT_SUB, ATT_SUB)
            for h, hl in enumerate(head_lanes):
                kblk, vblk = k_ref[krows, hl], v_ref[krows, hl]
                mask = mask_ref[sub, LANES + i * ATT_SUB:LANES + (i + 1) * ATT_SUB]
                if i == 0:
                    kblk = jnp.concatenate([km_ref[:, hl], kblk], axis=0)
                    vblk = jnp.concatenate([vm_ref[:, hl], vblk], axis=0)
                    mask = mask_ref[sub, :LANES + ATT_SUB]
                step(h, sub, qs[h][sub], kblk, vblk, mask)
        lane = lax.broadcasted_iota(jnp.int32, (ATT_TQ, HEAD_PAD), 1)
        for hp in range(ATT_HEADS // 2):
            even, odd = acc_ref[2 * hp], acc_ref[2 * hp + 1]
            o_even = even * (1.0 / even[:, V_DIM:V_DIM + 1])
            o_odd = odd * (1.0 / odd[:, 0:1])
            o_ref[rows, hp * HEAD_PAD:(hp + 1) * HEAD_PAD] = (
                jnp.where(lane < V_DIM, o_even, o_odd).astype(BF16))
        return 0

    lax.fori_loop(0, n_q, q_tile, 0)


def _attn_call(q3, k3, v3, k_meta, v_meta, mask, *, online):
    batch, seq, _ = q3.shape
    grid = (batch, N_HEADS // ATT_HEADS)
    seq_blk = pl.BlockSpec((None, seq, ATT_HEADS * HEAD_PAD), lambda b, hg: (b, 0, hg))
    meta_blk = pl.BlockSpec((LANES, ATT_HEADS * HEAD_PAD), lambda b, hg: (0, hg))
    return pl.pallas_call(
        functools.partial(_attn_kernel, online=online),
        out_shape=jax.ShapeDtypeStruct((batch, seq, N_HEADS * V_DIM), BF16),
        grid=grid,
        in_specs=[seq_blk, seq_blk, seq_blk, meta_blk, meta_blk, _const_spec(mask.shape)],
        out_specs=pl.BlockSpec((None, seq, ATT_HEADS * V_DIM), lambda b, hg: (b, 0, hg)),
        scratch_shapes=[pltpu.VMEM((ATT_HEADS, ATT_TQ, HEAD_PAD), F32),
                        pltpu.VMEM((ATT_HEADS, ATT_TQ, 1), F32)],
        compiler_params=pltpu.CompilerParams(dimension_semantics=("parallel", "parallel"),
                                             vmem_limit_bytes=VMEM_LIMIT),
        name="mla_attention_online" if online else "mla_attention",
    )(q3, k3, v3, k_meta, v_meta, mask)


FFN_TM = 512


def _ffn_kernel(x_ref, ms_ref, oa_ref, gatt_ref, wout_ref, gffn_ref, wg_ref, wu_ref, wd_ref,
                out_ref):
    ya = _rms(oa_ref[...].astype(F32), gatt_ref[...]).astype(BF16)
    mixed = jnp.concatenate([ms_ref[...], ya], axis=1)
    h1 = x_ref[...] + jnp.dot(mixed, wout_ref[...], preferred_element_type=F32)
    hn = _rms(h1, gffn_ref[...]).astype(BF16)
    g = jnp.dot(hn, wg_ref[...], preferred_element_type=F32)
    u = jnp.dot(hn, wu_ref[...], preferred_element_type=F32)
    a = (g * (1.0 / (1.0 + jnp.exp(-g))) * u).astype(BF16)
    out_ref[...] = h1 + jnp.dot(a, wd_ref[...], preferred_element_type=F32)


def _ffn_call(x2d, ms2d, oa2d, consts):
    n_rows = x2d.shape[0]
    tm = FFN_TM
    row = lambda i: (i, 0)
    once = pl.Buffered(1)
    in_specs = [pl.BlockSpec((tm, D_MODEL), row),
                pl.BlockSpec((tm, D_SSM), row),
                pl.BlockSpec((tm, N_HEADS * V_DIM), row)]
    in_specs += [pl.BlockSpec(c.shape, lambda i: (0, 0), pipeline_mode=once) for c in consts]
    return pl.pallas_call(
        _ffn_kernel,
        out_shape=jax.ShapeDtypeStruct((n_rows, D_MODEL), F32),
        grid=(n_rows // tm,), in_specs=in_specs,
        out_specs=pl.BlockSpec((tm, D_MODEL), row),
        compiler_params=pltpu.CompilerParams(dimension_semantics=("parallel",),
                                             vmem_limit_bytes=VMEM_LIMIT),
        name="outproj_ffn",
    )(x2d, ms2d, oa2d, *consts)


def _rope_tables(first, count):
    pos = np.arange(first, first + count, dtype=np.float64)
    inv_freq = 1.0 / (ROPE_BASE ** (np.arange(0, ROPE, 2, dtype=np.float64) / ROPE))
    ang = pos[:, None] * inv_freq[None, :]
    cos, sin = np.cos(ang), np.sin(ang)
    ctab = np.zeros((count, HEAD_PAD), np.float32)
    stab = np.zeros((count, HEAD_PAD), np.float32)
    ctab[:, :NOPE] = 1.0
    ctab[:, NOPE:QK_DIM] = np.concatenate([cos, cos], axis=1)
    stab[:, NOPE:QK_DIM] = np.concatenate([sin, sin], axis=1)
    return jnp.asarray(ctab), jnp.asarray(stab)


def _head_gains(gain):
    g_r = gain[NOPE:]
    g_r_swapped = jnp.concatenate([g_r[HALF_ROPE:], g_r[:HALF_ROPE]])
    pad = jnp.zeros((HEAD_PAD - QK_DIM,), F32)
    return jnp.stack([jnp.concatenate([gain, pad]),
                      jnp.concatenate([jnp.zeros((NOPE,), F32), g_r_swapped, pad])])


def _rot_half_cols(w):
    return jnp.concatenate([-w[..., HALF_ROPE:], w[..., :HALF_ROPE]], axis=-1)


def _pad_cols(w, left, total):
    return jnp.pad(w, ((0, 0), (left, total - left - w.shape[1])))


def kernel(x, meta_tokens, mix_norm_g, w_in, ssm_a_re, ssm_a_im, ssm_log_dt, ssm_b_re, ssm_b_im,
           ssm_c_re, ssm_c_im, ssm_d, ssm_w_glu, ssm_b_glu, q_lora_norm_g, w_uq, kv_lora_norm_g,
           w_uk, w_uv, q_head_norm_g, k_head_norm_g, ssm_out_norm_g, att_out_norm_g, w_out,
           ffn_norm_g, w_gate, w_up, w_down):
    batch, seq, _ = x.shape
    depth = w_in.shape[0]
    assert depth == 1
    l = 0

    wi = w_in[l]
    o_r = D_SSM + Q_LORA + KV_LORA
    w_r = wi[:, o_r:]
    win = jnp.concatenate([wi[:, :o_r], _pad_cols(w_r, NOPE, HEAD_PAD),
                           _pad_cols(_rot_half_cols(w_r), NOPE, HEAD_PAD)], axis=1).astype(BF16)
    wq3 = w_uq[l].reshape(Q_LORA, N_HEADS, QK_DIM)
    q1 = jnp.pad(wq3, ((0, 0), (0, 0), (0, HEAD_PAD - QK_DIM)))
    q2 = jnp.pad(_rot_half_cols(wq3[..., NOPE:]), ((0, 0), (0, 0), (NOPE, HEAD_PAD - QK_DIM)))
    wq = jnp.concatenate([q1.reshape(Q_LORA, QK_PAD), q2.reshape(Q_LORA, QK_PAD)], axis=1).astype(BF16)
    wk3 = jnp.pad(w_uk[l].reshape(KV_LORA, N_HEADS, NOPE), ((0, 0), (0, 0), (0, HEAD_PAD - NOPE)))
    wv4 = w_uv[l].reshape(KV_LORA, N_HEADS // 2, 2, V_DIM)
    zv = jnp.zeros_like(wv4[:, :, 0])
    wv = jnp.stack([jnp.concatenate([wv4[:, :, 0], zv], axis=-1),
                    jnp.concatenate([zv, wv4[:, :, 1]], axis=-1)], axis=2).reshape(KV_LORA, QK_PAD)
    ones_col = jnp.zeros((2, HEAD_PAD), F32).at[0, V_DIM].set(1.0).at[1, 0].set(1.0)
    vones = jnp.tile(ones_col.reshape(1, 2 * HEAD_PAD), (1, N_HEADS // 2))
    wkv = jnp.concatenate([wk3.reshape(KV_LORA, QK_PAD), wv], axis=1).astype(BF16)
    tabs_m = _rope_tables(0, N_META)
    tabs_f = _rope_tables(N_META, seq)
    proj_consts = (mix_norm_g[l][None], win, q_lora_norm_g[l][None], wq,
                   kv_lora_norm_g[l][None], wkv, vones,
                   _head_gains(q_head_norm_g[l]), _head_gains(k_head_norm_g[l]))

    tm = PROJ_TM
    u2, q2d, k2d, v2d, u_meta, k_meta, v_meta = _proj_call(
        x.reshape(batch * seq, D_MODEL), meta_tokens, tm, tabs_f, tabs_m, seq // tm, proj_consts)

    dt = jnp.exp(ssm_log_dt[l])[:, None]
    lr, li = ssm_a_re[l], ssm_a_im[l]
    mag = jnp.exp(lr * dt)
    ar = mag * jnp.cos(li * dt)
    ai = mag * jnp.sin(li * dt)
    den = lr * lr + li * li
    fr = ((ar - 1.0) * lr + ai * li) / den
    fi = (ai * lr - (ar - 1.0) * li) / den
    br, bi = ssm_b_re[l], ssm_b_im[l]
    bbr = fr[..., None] * br - fi[..., None] * bi
    bbi = fr[..., None] * bi + fi[..., None] * br
    b_in = jnp.swapaxes(jnp.stack([bbr, bbi]), 2, 3).reshape(2, D_SSM, SSM_STATE)
    c_out = jnp.swapaxes(jnp.stack([ssm_c_re[l], ssm_c_im[l]]), 2, 3)
    c_out = jnp.pad(c_out.reshape(2, N_STATE_COLS, SSM_GROUP),
                    ((0, 0), (0, 0), (0, LANES - SSM_GROUP)))
    half_groups = N_GROUPS // 2
    tile_in = np.tile(np.eye(SSM_STATE, dtype=np.float32), (1, half_groups))
    tile_out = np.zeros((LANES, half_groups * SSM_GROUP), np.float32)
    tile_out[:SSM_GROUP] = np.tile(np.eye(SSM_GROUP, dtype=np.float32), (1, half_groups))

    ar_rows = jnp.broadcast_to(ar.reshape(N_SLABS, 1, LANES), (N_SLABS, 2 * batch, LANES))
    ai_flat = ai.reshape(N_SLABS, 1, LANES)
    ai_rows = jnp.concatenate([jnp.broadcast_to(-ai_flat, (N_SLABS, batch, LANES)),
                               jnp.broadcast_to(ai_flat, (N_SLABS, batch, LANES))], axis=1)
    s5_consts = (b_in, c_out, jnp.asarray(tile_in, BF16), jnp.asarray(tile_out, BF16),
                 ar_rows, ai_rows, ssm_d[l][None], ssm_w_glu[l].astype(BF16),
                 ssm_b_glu[l][None], ssm_out_norm_g[l][None])
    mixed_ssm, wout_b, wg_b, wu_b, wd_b = _s5_call(
        u2.reshape(batch, seq, D_SSM), u_meta, s5_consts, (w_out, w_gate, w_up, w_down))

    chunk_of = np.arange(ATT_TQ) // CHUNK
    causal = (chunk_of[None, :] <= chunk_of[:, None]).astype(np.float32)
    mask = jnp.asarray(np.concatenate([np.ones((ATT_TQ, LANES), np.float32), causal], axis=1),
                       dtype=BF16)
    score_bound = (LOG2_E * math.sqrt(QK_DIM) * jnp.max(jnp.abs(q_head_norm_g[l]))
                   * jnp.max(jnp.abs(k_head_norm_g[l])))
    attn_args = (q2d.reshape(batch, seq, QK_PAD), k2d.reshape(batch, seq, QK_PAD),
                 v2d.reshape(batch, seq, QK_PAD), k_meta, v_meta, mask)
    y_att = lax.cond(score_bound <= MAX_UNSHIFTED_LOG2_SCORE,
                     functools.partial(_attn_call, online=False),
                     functools.partial(_attn_call, online=True), *attn_args)

    ffn_consts = (att_out_norm_g[l][None], wout_b, ffn_norm_g[l][None],
                  wg_b, wu_b, wd_b)
    out = _ffn_call(x.reshape(batch * seq, D_MODEL), mixed_ssm.reshape(batch * seq, D_SSM),
                    y_att.reshape(batch * seq, N_HEADS * V_DIM), ffn_consts)
    return out.reshape(batch, seq, D_MODEL)
```

```python
import functools
import math

import jax
import jax.numpy as jnp
import numpy as np
from jax import lax
from jax.experimental import pallas as pl
from jax.experimental.pallas import tpu as pltpu

F32 = jnp.float32
BF16 = jnp.bfloat16

D_MODEL = 1024
N_META = 16
CHUNK = 64
D_SSM = 512
SSM_GROUP = 16
N_GROUPS = D_SSM // SSM_GROUP
SSM_STATE = 64
N_HEADS = 8
V_DIM = 64
NOPE = 64
ROPE = 32
HALF_ROPE = ROPE // 2
QK_DIM = NOPE + ROPE
Q_LORA = 256
KV_LORA = 128
D_FF = 2816
ROPE_BASE = 10000.0
EPS = 1e-6
LOG2_E = math.log2(math.e)
MAX_UNSHIFTED_LOG2_SCORE = 40.0

LANES = 128
SUBLANES = 8
HEAD_PAD = LANES
QK_PAD = N_HEADS * HEAD_PAD
N_STATE_COLS = N_GROUPS * SSM_STATE
N_SLABS = N_STATE_COLS // LANES
S5_CHUNK = 128
S5_PITCH = S5_CHUNK + SUBLANES
PROJ_TM = 512
VMEM_LIMIT = 56 * 1024 * 1024


def _rms(x, g):
    return x * lax.rsqrt(jnp.mean(x * x, axis=-1, keepdims=True) + EPS) * g


def _proj_kernel(x_ref, xm_ref, gmix_ref, win_ref, gq_ref, wq_ref, gkv_ref, wkv_ref, vones_ref,
                 hgq_ref, hgk_ref, ctab_ref, stab_ref, ctabm_ref, stabm_ref, *rest):
    n_cast = (len(rest) - 7) // 2
    u_ref, q_ref, k_ref, v_ref, um_ref, km_ref, vm_ref = rest[n_cast:n_cast + 7]
    for src_ref, dst_ref in zip(rest[:n_cast], rest[n_cast + 7:]):
        dst_ref[...] = src_ref[...].astype(BF16)
    scale = QK_DIM ** -0.5 * LOG2_E
    c0 = D_SSM + Q_LORA

    def project(x, ctab, stab, rows, u_out, q_out, k_out, v_out):
        xn = _rms(x, gmix_ref[...]).astype(BF16)
        p = jnp.dot(xn, win_ref[...], preferred_element_type=F32)
        u_out[...] = p[:, :D_SSM]

        if q_out is not None:
            cqn = _rms(p[:, D_SSM:c0], gq_ref[...]).astype(BF16)
            q12 = jnp.dot(cqn, wq_ref[...], preferred_element_type=F32)
            t1q, t2q = ctab * hgq_ref[0:1, :], stab * hgq_ref[1:2, :]
            for h in range(N_HEADS):
                q1 = q12[:, h * HEAD_PAD:(h + 1) * HEAD_PAD]
                q2 = q12[:, QK_PAD + h * HEAD_PAD:QK_PAD + (h + 1) * HEAD_PAD]
                r = lax.rsqrt(jnp.sum(q1 * q1, axis=-1, keepdims=True) * (1.0 / QK_DIM) + EPS)
                qh = (q1 * t1q + q2 * t2q) * (r * scale)
                q_out[:, h * HEAD_PAD:(h + 1) * HEAD_PAD] = qh.astype(BF16)

        ckvn = _rms(p[:, c0:c0 + KV_LORA], gkv_ref[...]).astype(BF16)
        kv = jnp.dot(ckvn, wkv_ref[...], preferred_element_type=F32)
        v_out[rows, :] = (kv[:, QK_PAD:] + vones_ref[...]).astype(BF16)
        kr = p[:, c0 + KV_LORA:c0 + KV_LORA + HEAD_PAD]
        kr_rot = p[:, c0 + KV_LORA + HEAD_PAD:c0 + KV_LORA + 2 * HEAD_PAD]
        ss_r = jnp.sum(kr * kr, axis=-1, keepdims=True)
        t1k = ctab * hgk_ref[0:1, :]
        kr_part = kr_rot * (stab * hgk_ref[1:2, :])
        for h in range(N_HEADS):
            kn = kv[:, h * HEAD_PAD:(h + 1) * HEAD_PAD]
            ss = jnp.sum(kn * kn, axis=-1, keepdims=True) + ss_r
            r = lax.rsqrt(ss * (1.0 / QK_DIM) + EPS)
            kh = ((kn + kr) * t1k + kr_part) * r
            k_out[rows, h * HEAD_PAD:(h + 1) * HEAD_PAD] = kh.astype(BF16)

    @pl.when(pl.program_id(0) == 0)
    def _():
        km_ref[...] = jnp.zeros_like(km_ref)
        vm_ref[...] = jnp.zeros_like(vm_ref)
        project(xm_ref[...], ctabm_ref[...], stabm_ref[...], slice(0, N_META),
                um_ref, None, km_ref, vm_ref)

    project(x_ref[...], ctab_ref[...], stab_ref[...], slice(None), u_ref, q_ref, k_ref, v_ref)


def _const_spec(shape):
    nd = len(shape)
    return pl.BlockSpec(shape, lambda *_: (0,) * nd)


def _proj_call(x2d, x_meta, tm, tabs, tabs_meta, n_tab_blocks, consts, to_bf16=()):
    n_rows = x2d.shape[0]
    n_steps = n_rows // tm
    row = lambda i: (i, 0)
    tab = lambda i: (i % n_tab_blocks, 0)
    meta_rows = (LANES, QK_PAD)
    cast_in, cast_out = [], []
    for w in to_bf16:
        _, w_rows, w_cols = w.shape
        rep = next(r for r in (1, 2, 4, 8) if (w_rows * r) % (16 * n_steps) == 0)
        blk_rows = w_rows * rep // n_steps
        cast_in.append(pl.BlockSpec((None, blk_rows, w_cols),
                                    functools.partial(lambda rep, i: (0, i // rep, 0), rep)))
        cast_out.append(pl.BlockSpec((blk_rows, w_cols),
                                     functools.partial(lambda rep, i: (i // rep, 0), rep)))
    in_specs = ([pl.BlockSpec((tm, D_MODEL), row), _const_spec(x_meta.shape)]
                + [_const_spec(c.shape) for c in consts]
                + [pl.BlockSpec((tm, HEAD_PAD), tab)] * len(tabs)
                + [_const_spec(t.shape) for t in tabs_meta] + cast_in)
    out_shape = (
        jax.ShapeDtypeStruct((n_rows, D_SSM), F32),
        jax.ShapeDtypeStruct((n_rows, QK_PAD), BF16),
        jax.ShapeDtypeStruct((n_rows, QK_PAD), BF16),
        jax.ShapeDtypeStruct((n_rows, QK_PAD), BF16),
        jax.ShapeDtypeStruct((x_meta.shape[0], D_SSM), F32),
        jax.ShapeDtypeStruct(meta_rows, BF16),
        jax.ShapeDtypeStruct(meta_rows, BF16),
    ) + tuple(jax.ShapeDtypeStruct(w.shape[1:], BF16) for w in to_bf16)
    out_specs = (
        pl.BlockSpec((tm, D_SSM), row),
        pl.BlockSpec((tm, QK_PAD), row),
        pl.BlockSpec((tm, QK_PAD), row),
        pl.BlockSpec((tm, QK_PAD), row),
        _const_spec((x_meta.shape[0], D_SSM)),
        _const_spec(meta_rows),
        _const_spec(meta_rows),
    ) + tuple(cast_out)
    return pl.pallas_call(
        _proj_kernel, out_shape=out_shape, grid=(n_steps,), in_specs=in_specs,
        out_specs=out_specs,
        compiler_params=pltpu.CompilerParams(dimension_semantics=("arbitrary",),
                                             vmem_limit_bytes=VMEM_LIMIT),
        name="proj_mla",
    )(x2d, x_meta, *consts, *tabs, *tabs_meta, *to_bf16)


def _s5_kernel(u_ref, un_ref, um_ref, bin_ref, cout_ref, tin_ref, tout_ref, ar_ref, ai_ref,
               d_ref, wglu_ref, bglu_ref, g_ref, o_ref, xa_ref, xb_ref, h_ref, bmap_ref, cmap_ref,
               *, batch):
    j = pl.program_id(0)
    slabs_per_half = N_SLABS // 2
    half_ch, half_st = D_SSM // 2, N_STATE_COLS // 2

    def build_maps():
        def diag(shape, row_block, col_block):
            return (lax.broadcasted_iota(jnp.int32, shape, 0) // row_block
                    == lax.broadcasted_iota(jnp.int32, shape, 1) // col_block)
        diag_in = diag((half_ch, half_st), SSM_GROUP, SSM_STATE)
        diag_out = diag((half_st, half_ch), SSM_STATE, SSM_GROUP)
        for part in range(2):
            for half in range(2):
                rows_in = bin_ref[part, half * half_ch:(half + 1) * half_ch, :].astype(BF16)
                tiled = jnp.dot(rows_in, tin_ref[...], preferred_element_type=F32)
                bmap_ref[part, half] = jnp.where(diag_in, tiled, 0.0).astype(BF16)
                rows_out = cout_ref[part, half * half_st:(half + 1) * half_st, :].astype(BF16)
                tiled = jnp.dot(rows_out, tout_ref[...], preferred_element_type=F32)
                cmap_ref[part, half] = jnp.where(diag_out, tiled, 0.0).astype(BF16)

    def project_in(ub, rows, xs_ref):
        for kh in range(2):
            lhs = ub[:, kh * half_ch:(kh + 1) * half_ch]
            xre = jnp.dot(lhs, bmap_ref[0, kh], preferred_element_type=F32)
            xim = jnp.dot(lhs, bmap_ref[1, kh], preferred_element_type=F32)
            for cl in range(slabs_per_half):
                c = kh * slabs_per_half + cl
                for b in range(batch):
                    xs_ref[c, pl.ds(b * S5_PITCH, rows), :] = (
                        xre[b * rows:(b + 1) * rows, cl * LANES:(cl + 1) * LANES])
                    xs_ref[c, pl.ds((batch + b) * S5_PITCH, rows), :] = (
                        xim[b * rows:(b + 1) * rows, cl * LANES:(cl + 1) * LANES])

    def scan(n_steps, xs_ref):
        def body(t, hs):
            new = []
            for c in range(N_SLABS):
                rows = pl.ds(t, 2 * batch, stride=S5_PITCH)
                x8 = xs_ref[c, rows, :]
                h = hs[c]
                hn = ar_ref[c] * h + ai_ref[c] * pltpu.roll(h, batch, 0) + x8
                xs_ref[c, rows, :] = hn
                new.append(hn)
            return tuple(new)

        hs = tuple(h_ref[c] for c in range(N_SLABS))
        hs = lax.fori_loop(0, n_steps, body, hs, unroll=8)
        for c in range(N_SLABS):
            h_ref[c] = hs[c]

    def output_map(xs_ref):
        ys = []
        for nh in range(2):
            def gather(plane0):
                return jnp.concatenate(
                    [jnp.concatenate(
                        [xs_ref[nh * slabs_per_half + cl,
                                pl.ds((plane0 + b) * S5_PITCH, S5_CHUNK), :]
                         for cl in range(slabs_per_half)], axis=1)
                     for b in range(batch)], axis=0).astype(BF16)
            yre = jnp.dot(gather(0), cmap_ref[0, nh], preferred_element_type=F32)
            yim = jnp.dot(gather(batch), cmap_ref[1, nh], preferred_element_type=F32)
            ys.append(yre - yim)
        uf = u_ref[...].reshape(batch * S5_CHUNK, D_SSM)
        y = jnp.concatenate(ys, axis=1) + d_ref[...] * uf
        z = 0.5 * y * (1.0 + jnp.tanh(math.sqrt(2.0 / math.pi) * (y + 0.044715 * (y * y * y))))
        gate = jnp.dot(z.astype(BF16), wglu_ref[...], preferred_element_type=F32) + bglu_ref[...]
        out = z * (1.0 / (1.0 + jnp.exp(-gate)))
        o_ref[...] = _rms(out, g_ref[...]).astype(BF16).reshape(batch, S5_CHUNK, D_SSM)

    @pl.when(j == 0)
    def _():
        build_maps()
        h_ref[...] = jnp.zeros_like(h_ref)
        um = um_ref[...].astype(BF16)
        project_in(jnp.concatenate([um] * batch, axis=0), N_META, xb_ref)
        scan(N_META, xb_ref)
        project_in(u_ref[...].reshape(batch * S5_CHUNK, D_SSM).astype(BF16), S5_CHUNK, xa_ref)

    def step(x_cur, x_next):
        scan(S5_CHUNK, x_cur)
        output_map(x_cur)
        project_in(un_ref[...].reshape(batch * S5_CHUNK, D_SSM).astype(BF16), S5_CHUNK, x_next)

    @pl.when(lax.rem(j, 2) == 0)
    def _():
        step(xa_ref, xb_ref)

    @pl.when(lax.rem(j, 2) == 1)
    def _():
        step(xb_ref, xa_ref)


def _s5_call(u3, u_meta, consts):
    batch, seq, _ = u3.shape
    assert 2 * batch == SUBLANES and seq % S5_CHUNK == 0
    n_chunks = seq // S5_CHUNK
    grid = (n_chunks,)
    chunk_blk = (batch, S5_CHUNK, D_SSM)
    in_specs = [pl.BlockSpec(chunk_blk, lambda j: (0, j, 0)),
                pl.BlockSpec(chunk_blk, lambda j: (0, jnp.minimum(j + 1, n_chunks - 1), 0)),
                _const_spec(u_meta.shape)] + [_const_spec(c.shape) for c in consts]
    planes = pltpu.VMEM((N_SLABS, 2 * batch * S5_PITCH, LANES), F32)
    return pl.pallas_call(
        functools.partial(_s5_kernel, batch=batch),
        out_shape=jax.ShapeDtypeStruct((batch, seq, D_SSM), BF16),
        grid=grid, in_specs=in_specs,
        out_specs=pl.BlockSpec(chunk_blk, lambda j: (0, j, 0)),
        scratch_shapes=[planes, planes,
                        pltpu.VMEM((N_SLABS, 2 * batch, LANES), F32),
                        pltpu.VMEM((2, 2, D_SSM // 2, N_STATE_COLS // 2), BF16),
                        pltpu.VMEM((2, 2, N_STATE_COLS // 2, D_SSM // 2), BF16)],
        compiler_params=pltpu.CompilerParams(dimension_semantics=("arbitrary",),
                                             vmem_limit_bytes=VMEM_LIMIT),
        name="s5_mixer",
    )(u3, u3, u_meta, *consts)


ATT_TQ = 1024
ATT_TK = 1024
ATT_SUB = 256
ATT_HEADS = 4


def _attn_kernel(q_ref, k_ref, v_ref, km_ref, vm_ref, mask_ref, o_ref, acc_ref, m_ref, *, online):
    nt = (((1,), (1,)), ((), ()))
    n_q = q_ref.shape[0] // ATT_TQ
    head_lanes = [slice(h * HEAD_PAD, (h + 1) * HEAD_PAD) for h in range(ATT_HEADS)]

    def step(h, sub, q, kblk, vblk, mask):
        s = lax.dot_general(q, kblk, nt, preferred_element_type=F32)
        if online:
            if mask is not None:
                s = jnp.where(mask > 0, s, -jnp.inf)
            m = m_ref[h, sub]
            m_new = jnp.maximum(m, jnp.max(s, axis=-1, keepdims=True))
            p = jnp.exp2(s - m_new).astype(BF16)
            acc_ref[h, sub] = (jnp.exp2(m - m_new) * acc_ref[h, sub]
                               + jnp.dot(p, vblk, preferred_element_type=F32))
            m_ref[h, sub] = m_new
        else:
            p = jnp.exp2(s).astype(BF16)
            if mask is not None:
                p = p * mask
            acc_ref[h, sub] += jnp.dot(p, vblk, preferred_element_type=F32)

    def q_tile(qi, _):
        q0 = pl.multiple_of(qi * ATT_TQ, ATT_TQ)
        rows = pl.ds(q0, ATT_TQ)
        qs = [q_ref[rows, hl] for hl in head_lanes]
        acc_ref[...] = jnp.zeros_like(acc_ref)
        if online:
            m_ref[...] = jnp.full(m_ref.shape, -1e30, F32)

        def body(kb, _):
            krows = pl.ds(pl.multiple_of(kb * ATT_TK, ATT_TK), ATT_TK)
            for h, hl in enumerate(head_lanes):
                step(h, slice(None), qs[h], k_ref[krows, hl], v_ref[krows, hl], None)
            return 0

        lax.fori_loop(0, qi * (ATT_TQ // ATT_TK), body, 0)

        for i in range(ATT_TQ // ATT_SUB):
            sub = slice(i * ATT_SUB, ATT_TQ)
            krows = pl.ds(q0 + i * ATT_SUB, ATT_SUB)
            for h, hl in enumerate(head_lanes):
                kblk, vblk = k_ref[krows, hl], v_ref[krows, hl]
                mask = mask_ref[sub, LANES + i * ATT_SUB:LANES + (i + 1) * ATT_SUB]
                if i == 0:
                    kblk = jnp.concatenate([km_ref[:, hl], kblk], axis=0)
                    vblk = jnp.concatenate([vm_ref[:, hl], vblk], axis=0)
                    mask = mask_ref[sub, :LANES + ATT_SUB]
                step(h, sub, qs[h][sub], kblk, vblk, mask)
        lane = lax.broadcasted_iota(jnp.int32, (ATT_TQ, HEAD_PAD), 1)
        for hp in range(ATT_HEADS // 2):
            even, odd = acc_ref[2 * hp], acc_ref[2 * hp + 1]
            o_even = even * (1.0 / even[:, V_DIM:V_DIM + 1])
            o_odd = odd * (1.0 / odd[:, 0:1])
            o_ref[rows, hp * HEAD_PAD:(hp + 1) * HEAD_PAD] = (
                jnp.where(lane < V_DIM, o_even, o_odd).astype(BF16))
        return 0

    lax.fori_loop(0, n_q, q_tile, 0)


def _attn_call(q3, k3, v3, k_meta, v_meta, mask, *, online):
    batch, seq, _ = q3.shape
    grid = (batch, N_HEADS // ATT_HEADS)
    seq_blk = pl.BlockSpec((None, seq, ATT_HEADS * HEAD_PAD), lambda b, hg: (b, 0, hg))
    meta_blk = pl.BlockSpec((LANES, ATT_HEADS * HEAD_PAD), lambda b, hg: (0, hg))
    return pl.pallas_call(
        functools.partial(_attn_kernel, online=online),
        out_shape=jax.ShapeDtypeStruct((batch, seq, N_HEADS * V_DIM), BF16),
        grid=grid,
        in_specs=[seq_blk, seq_blk, seq_blk, meta_blk, meta_blk, _const_spec(mask.shape)],
        out_specs=pl.BlockSpec((None, seq, ATT_HEADS * V_DIM), lambda b, hg: (b, 0, hg)),
        scratch_shapes=[pltpu.VMEM((ATT_HEADS, ATT_TQ, HEAD_PAD), F32),
                        pltpu.VMEM((ATT_HEADS, ATT_TQ, 1), F32)],
        compiler_params=pltpu.CompilerParams(dimension_semantics=("parallel", "parallel"),
                                             vmem_limit_bytes=VMEM_LIMIT),
        name="mla_attention_online" if online else "mla_attention",
    )(q3, k3, v3, k_meta, v_meta, mask)


FFN_TM = 512


def _ffn_kernel(x_ref, ms_ref, oa_ref, gatt_ref, wout_ref, gffn_ref, wg_ref, wu_ref, wd_ref,
                out_ref):
    ya = _rms(oa_ref[...].astype(F32), gatt_ref[...]).astype(BF16)
    mixed = jnp.concatenate([ms_ref[...], ya], axis=1)
    h1 = x_ref[...] + jnp.dot(mixed, wout_ref[...], preferred_element_type=F32)
    hn = _rms(h1, gffn_ref[...]).astype(BF16)
    g = jnp.dot(hn, wg_ref[...], preferred_element_type=F32)
    u = jnp.dot(hn, wu_ref[...], preferred_element_type=F32)
    a = (g * (1.0 / (1.0 + jnp.exp(-g))) * u).astype(BF16)
    out_ref[...] = h1 + jnp.dot(a, wd_ref[...], preferred_element_type=F32)


def _ffn_call(x2d, ms2d, oa2d, consts):
    n_rows = x2d.shape[0]
    tm = FFN_TM
    row = lambda i: (i, 0)
    once = pl.Buffered(1)
    in_specs = [pl.BlockSpec((tm, D_MODEL), row),
                pl.BlockSpec((tm, D_SSM), row),
                pl.BlockSpec((tm, N_HEADS * V_DIM), row)]
    in_specs += [pl.BlockSpec(c.shape, lambda i: (0, 0), pipeline_mode=once) for c in consts]
    return pl.pallas_call(
        _ffn_kernel,
        out_shape=jax.ShapeDtypeStruct((n_rows, D_MODEL), F32),
        grid=(n_rows // tm,), in_specs=in_specs,
        out_specs=pl.BlockSpec((tm, D_MODEL), row),
        compiler_params=pltpu.CompilerParams(dimension_semantics=("parallel",),
                                             vmem_limit_bytes=VMEM_LIMIT),
        name="outproj_ffn",
    )(x2d, ms2d, oa2d, *consts)


def _rope_tables(first, count):
    pos = np.arange(first, first + count, dtype=np.float64)
    inv_freq = 1.0 / (ROPE_BASE ** (np.arange(0, ROPE, 2, dtype=np.float64) / ROPE))
    ang = pos[:, None] * inv_freq[None, :]
    cos, sin = np.cos(ang), np.sin(ang)
    ctab = np.zeros((count, HEAD_PAD), np.float32)
    stab = np.zeros((count, HEAD_PAD), np.float32)
    ctab[:, :NOPE] = 1.0
    ctab[:, NOPE:QK_DIM] = np.concatenate([cos, cos], axis=1)
    stab[:, NOPE:QK_DIM] = np.concatenate([sin, sin], axis=1)
    return jnp.asarray(ctab), jnp.asarray(stab)


def _head_gains(gain):
    g_r = gain[NOPE:]
    g_r_swapped = jnp.concatenate([g_r[HALF_ROPE:], g_r[:HALF_ROPE]])
    pad = jnp.zeros((HEAD_PAD - QK_DIM,), F32)
    return jnp.stack([jnp.concatenate([gain, pad]),
                      jnp.concatenate([jnp.zeros((NOPE,), F32), g_r_swapped, pad])])


def _rot_half_cols(w):
    return jnp.concatenate([-w[..., HALF_ROPE:], w[..., :HALF_ROPE]], axis=-1)


def _pad_cols(w, left, total):
    return jnp.pad(w, ((0, 0), (left, total - left - w.shape[1])))


def kernel(x, meta_tokens, mix_norm_g, w_in, ssm_a_re, ssm_a_im, ssm_log_dt, ssm_b_re, ssm_b_im,
           ssm_c_re, ssm_c_im, ssm_d, ssm_w_glu, ssm_b_glu, q_lora_norm_g, w_uq, kv_lora_norm_g,
           w_uk, w_uv, q_head_norm_g, k_head_norm_g, ssm_out_norm_g, att_out_norm_g, w_out,
           ffn_norm_g, w_gate, w_up, w_down):
    batch, seq, _ = x.shape
    depth = w_in.shape[0]
    assert depth == 1
    l = 0

    wi = w_in[l]
    o_r = D_SSM + Q_LORA + KV_LORA
    w_r = wi[:, o_r:]
    win = jnp.concatenate([wi[:, :o_r], _pad_cols(w_r, NOPE, HEAD_PAD),
                           _pad_cols(_rot_half_cols(w_r), NOPE, HEAD_PAD)], axis=1).astype(BF16)
    wq3 = w_uq[l].reshape(Q_LORA, N_HEADS, QK_DIM)
    q1 = jnp.pad(wq3, ((0, 0), (0, 0), (0, HEAD_PAD - QK_DIM)))
    q2 = jnp.pad(_rot_half_cols(wq3[..., NOPE:]), ((0, 0), (0, 0), (NOPE, HEAD_PAD - QK_DIM)))
    wq = jnp.concatenate([q1.reshape(Q_LORA, QK_PAD), q2.reshape(Q_LORA, QK_PAD)], axis=1).astype(BF16)
    wk3 = jnp.pad(w_uk[l].reshape(KV_LORA, N_HEADS, NOPE), ((0, 0), (0, 0), (0, HEAD_PAD - NOPE)))
    wv4 = w_uv[l].reshape(KV_LORA, N_HEADS // 2, 2, V_DIM)
    zv = jnp.zeros_like(wv4[:, :, 0])
    wv = jnp.stack([jnp.concatenate([wv4[:, :, 0], zv], axis=-1),
                    jnp.concatenate([zv, wv4[:, :, 1]], axis=-1)], axis=2).reshape(KV_LORA, QK_PAD)
    ones_col = jnp.zeros((2, HEAD_PAD), F32).at[0, V_DIM].set(1.0).at[1, 0].set(1.0)
    vones = jnp.tile(ones_col.reshape(1, 2 * HEAD_PAD), (1, N_HEADS // 2))
    wkv = jnp.concatenate([wk3.reshape(KV_LORA, QK_PAD), wv], axis=1).astype(BF16)
    tabs_m = _rope_tables(0, N_META)
    tabs_f = _rope_tables(N_META, seq)
    proj_consts = (mix_norm_g[l][None], win, q_lora_norm_g[l][None], wq,
                   kv_lora_norm_g[l][None], wkv, vones,
                   _head_gains(q_head_norm_g[l]), _head_gains(k_head_norm_g[l]))

    tm = PROJ_TM
    ffn_f32 = (w_out, w_gate, w_up, w_down)
    u2, q2d, k2d, v2d, u_meta, k_meta, v_meta, wout_b, wg_b, wu_b, wd_b = _proj_call(
        x.reshape(batch * seq, D_MODEL), meta_tokens, tm, tabs_f, tabs_m, seq // tm,
        proj_consts, ffn_f32)

    dt = jnp.exp(ssm_log_dt[l])[:, None]
    lr, li = ssm_a_re[l], ssm_a_im[l]
    mag = jnp.exp(lr * dt)
    ar = mag * jnp.cos(li * dt)
    ai = mag * jnp.sin(li * dt)
    den = lr * lr + li * li
    fr = ((ar - 1.0) * lr + ai * li) / den
    fi = (ai * lr - (ar - 1.0) * li) / den
    br, bi = ssm_b_re[l], ssm_b_im[l]
    bbr = fr[..., None] * br - fi[..., None] * bi
    bbi = fr[..., None] * bi + fi[..., None] * br
    b_in = jnp.swapaxes(jnp.stack([bbr, bbi]), 2, 3).reshape(2, D_SSM, SSM_STATE)
    c_out = jnp.swapaxes(jnp.stack([ssm_c_re[l], ssm_c_im[l]]), 2, 3)
    c_out = jnp.pad(c_out.reshape(2, N_STATE_COLS, SSM_GROUP),
                    ((0, 0), (0, 0), (0, LANES - SSM_GROUP)))
    half_groups = N_GROUPS // 2
    tile_in = np.tile(np.eye(SSM_STATE, dtype=np.float32), (1, half_groups))
    tile_out = np.zeros((LANES, half_groups * SSM_GROUP), np.float32)
    tile_out[:SSM_GROUP] = np.tile(np.eye(SSM_GROUP, dtype=np.float32), (1, half_groups))

    ar_rows = jnp.broadcast_to(ar.reshape(N_SLABS, 1, LANES), (N_SLABS, 2 * batch, LANES))
    ai_flat = ai.reshape(N_SLABS, 1, LANES)
    ai_rows = jnp.concatenate([jnp.broadcast_to(-ai_flat, (N_SLABS, batch, LANES)),
                               jnp.broadcast_to(ai_flat, (N_SLABS, batch, LANES))], axis=1)
    s5_consts = (b_in, c_out, jnp.asarray(tile_in, BF16), jnp.asarray(tile_out, BF16),
                 ar_rows, ai_rows, ssm_d[l][None], ssm_w_glu[l].astype(BF16),
                 ssm_b_glu[l][None], ssm_out_norm_g[l][None])
    mixed_ssm = _s5_call(u2.reshape(batch, seq, D_SSM), u_meta, s5_consts)

    chunk_of = np.arange(ATT_TQ) // CHUNK
    causal = (chunk_of[None, :] <= chunk_of[:, None]).astype(np.float32)
    mask = jnp.asarray(np.concatenate([np.ones((ATT_TQ, LANES), np.float32), causal], axis=1),
                       dtype=BF16)
    score_bound = (LOG2_E * math.sqrt(QK_DIM) * jnp.max(jnp.abs(q_head_norm_g[l]))
                   * jnp.max(jnp.abs(k_head_norm_g[l])))
    attn_args = (q2d.reshape(batch, seq, QK_PAD), k2d.reshape(batch, seq, QK_PAD),
                 v2d.reshape(batch, seq, QK_PAD), k_meta, v_meta, mask)
    y_att = lax.cond(score_bound <= MAX_UNSHIFTED_LOG2_SCORE,
                     functools.partial(_attn_call, online=False),
                     functools.partial(_attn_call, online=True), *attn_args)

    ffn_consts = (att_out_norm_g[l][None], wout_b, ffn_norm_g[l][None],
                  wg_b, wu_b, wd_b)
    out = _ffn_call(x.reshape(batch * seq, D_MODEL), mixed_ssm.reshape(batch * seq, D_SSM),
                    y_att.reshape(batch * seq, N_HEADS * V_DIM), ffn_consts)
    return out.reshape(batch, seq, D_MODEL)
```

```python
import functools
import math

import jax
import jax.numpy as jnp
import numpy as np
from jax import lax
from jax.experimental import pallas as pl
from jax.experimental.pallas import tpu as pltpu

F32 = jnp.float32
BF16 = jnp.bfloat16

D_MODEL = 1024
N_META = 16
CHUNK = 64
D_SSM = 512
SSM_GROUP = 16
N_GROUPS = D_SSM // SSM_GROUP
SSM_STATE = 64
N_HEADS = 8
V_DIM = 64
NOPE = 64
ROPE = 32
HALF_ROPE = ROPE // 2
QK_DIM = NOPE + ROPE
Q_LORA = 256
KV_LORA = 128
D_FF = 2816
ROPE_BASE = 10000.0
EPS = 1e-6
LOG2_E = math.log2(math.e)
MAX_UNSHIFTED_LOG2_SCORE = 40.0

LANES = 128
SUBLANES = 8
HEAD_PAD = LANES
QK_PAD = N_HEADS * HEAD_PAD
N_STATE_COLS = N_GROUPS * SSM_STATE
N_SLABS = N_STATE_COLS // LANES
S5_CHUNK = 128
S5_PITCH = S5_CHUNK + SUBLANES
PROJ_TM = 512
VMEM_LIMIT = 56 * 1024 * 1024


def _rms(x, g):
    return x * lax.rsqrt(jnp.mean(x * x, axis=-1, keepdims=True) + EPS) * g


def _proj_kernel(x_ref, xm_ref, gmix_ref, win_ref, gq_ref, wq_ref, gkv_ref, wkv_ref, vones_ref,
                 hgq_ref, hgk_ref, ctab_ref, stab_ref, ctabm_ref, stabm_ref, *rest):
    n_cast = (len(rest) - 7) // 2
    u_ref, q_ref, k_ref, v_ref, um_ref, km_ref, vm_ref = rest[n_cast:n_cast + 7]
    for src_ref, dst_ref in zip(rest[:n_cast], rest[n_cast + 7:]):
        dst_ref[...] = src_ref[...].astype(BF16)
    scale = QK_DIM ** -0.5 * LOG2_E
    c0 = D_SSM + Q_LORA

    def project(x, ctab, stab, rows, u_out, q_out, k_out, v_out):
        xn = _rms(x, gmix_ref[...]).astype(BF16)
        p = jnp.dot(xn, win_ref[...], preferred_element_type=F32)
        u_out[...] = p[:, :D_SSM]

        if q_out is not None:
            cqn = _rms(p[:, D_SSM:c0], gq_ref[...]).astype(BF16)
            q12 = jnp.dot(cqn, wq_ref[...], preferred_element_type=F32)
            t1q, t2q = ctab * hgq_ref[0:1, :], stab * hgq_ref[1:2, :]
            for h in range(N_HEADS):
                q1 = q12[:, h * HEAD_PAD:(h + 1) * HEAD_PAD]
                q2 = q12[:, QK_PAD + h * HEAD_PAD:QK_PAD + (h + 1) * HEAD_PAD]
                r = lax.rsqrt(jnp.sum(q1 * q1, axis=-1, keepdims=True) * (1.0 / QK_DIM) + EPS)
                qh = (q1 * t1q + q2 * t2q) * (r * scale)
                q_out[:, h * HEAD_PAD:(h + 1) * HEAD_PAD] = qh.astype(BF16)

        ckvn = _rms(p[:, c0:c0 + KV_LORA], gkv_ref[...]).astype(BF16)
        kv = jnp.dot(ckvn, wkv_ref[...], preferred_element_type=F32)
        v_out[rows, :] = (kv[:, QK_PAD:] + vones_ref[...]).astype(BF16)
        kr = p[:, c0 + KV_LORA:c0 + KV_LORA + HEAD_PAD]
        kr_rot = p[:, c0 + KV_LORA + HEAD_PAD:c0 + KV_LORA + 2 * HEAD_PAD]
        ss_r = jnp.sum(kr * kr, axis=-1, keepdims=True)
        t1k = ctab * hgk_ref[0:1, :]
        kr_part = kr_rot * (stab * hgk_ref[1:2, :])
        for h in range(N_HEADS):
            kn = kv[:, h * HEAD_PAD:(h + 1) * HEAD_PAD]
            ss = jnp.sum(kn * kn, axis=-1, keepdims=True) + ss_r
            r = lax.rsqrt(ss * (1.0 / QK_DIM) + EPS)
            kh = ((kn + kr) * t1k + kr_part) * r
            k_out[rows, h * HEAD_PAD:(h + 1) * HEAD_PAD] = kh.astype(BF16)

    @pl.when(pl.program_id(0) == 0)
    def _():
        km_ref[...] = jnp.zeros_like(km_ref)
        vm_ref[...] = jnp.zeros_like(vm_ref)
        project(xm_ref[...], ctabm_ref[...], stabm_ref[...], slice(0, N_META),
                um_ref, None, km_ref, vm_ref)

    project(x_ref[...], ctab_ref[...], stab_ref[...], slice(None), u_ref, q_ref, k_ref, v_ref)


def _const_spec(shape):
    nd = len(shape)
    return pl.BlockSpec(shape, lambda *_: (0,) * nd)


def _proj_call(x2d, x_meta, tm, tabs, tabs_meta, n_tab_blocks, consts, to_bf16=()):
    n_rows = x2d.shape[0]
    n_steps = n_rows // tm
    row = lambda i: (i, 0)
    tab = lambda i: (i % n_tab_blocks, 0)
    meta_rows = (LANES, QK_PAD)
    cast_in, cast_out = [], []
    for w in to_bf16:
        _, w_rows, w_cols = w.shape
        rep = next(r for r in (1, 2, 4, 8) if (w_rows * r) % (16 * n_steps) == 0)
        blk_rows = w_rows * rep // n_steps
        cast_in.append(pl.BlockSpec((None, blk_rows, w_cols),
                                    functools.partial(lambda rep, i: (0, i // rep, 0), rep)))
        cast_out.append(pl.BlockSpec((blk_rows, w_cols),
                                     functools.partial(lambda rep, i: (i // rep, 0), rep)))
    in_specs = ([pl.BlockSpec((tm, D_MODEL), row), _const_spec(x_meta.shape)]
                + [_const_spec(c.shape) for c in consts]
                + [pl.BlockSpec((tm, HEAD_PAD), tab)] * len(tabs)
                + [_const_spec(t.shape) for t in tabs_meta] + cast_in)
    out_shape = (
        jax.ShapeDtypeStruct((n_rows, D_SSM), F32),
        jax.ShapeDtypeStruct((n_rows, QK_PAD), BF16),
        jax.ShapeDtypeStruct((n_rows, QK_PAD), BF16),
        jax.ShapeDtypeStruct((n_rows, QK_PAD), BF16),
        jax.ShapeDtypeStruct((x_meta.shape[0], D_SSM), F32),
        jax.ShapeDtypeStruct(meta_rows, BF16),
        jax.ShapeDtypeStruct(meta_rows, BF16),
    ) + tuple(jax.ShapeDtypeStruct(w.shape[1:], BF16) for w in to_bf16)
    out_specs = (
        pl.BlockSpec((tm, D_SSM), row),
        pl.BlockSpec((tm, QK_PAD), row),
        pl.BlockSpec((tm, QK_PAD), row),
        pl.BlockSpec((tm, QK_PAD), row),
        _const_spec((x_meta.shape[0], D_SSM)),
        _const_spec(meta_rows),
        _const_spec(meta_rows),
    ) + tuple(cast_out)
    return pl.pallas_call(
        _proj_kernel, out_shape=out_shape, grid=(n_steps,), in_specs=in_specs,
        out_specs=out_specs,
        compiler_params=pltpu.CompilerParams(dimension_semantics=("arbitrary",),
                                             vmem_limit_bytes=VMEM_LIMIT),
        name="proj_mla",
    )(x2d, x_meta, *consts, *tabs, *tabs_meta, *to_bf16)


def _s5_kernel(u_ref, un_ref, um_ref, bin_ref, cout_ref, tin_ref, tout_ref, ar_ref, ai_ref,
               d_ref, wglu_ref, bglu_ref, g_ref, o_ref, xa_ref, xb_ref, h_ref, bmap_ref, cmap_ref,
               *, batch):
    j = pl.program_id(0)
    slabs_per_half = N_SLABS // 2
    half_ch, half_st = D_SSM // 2, N_STATE_COLS // 2

    def build_maps():
        def diag(shape, row_block, col_block):
            return (lax.broadcasted_iota(jnp.int32, shape, 0) // row_block
                    == lax.broadcasted_iota(jnp.int32, shape, 1) // col_block)
        diag_in = diag((half_ch, half_st), SSM_GROUP, SSM_STATE)
        diag_out = diag((half_st, half_ch), SSM_STATE, SSM_GROUP)
        for part in range(2):
            for half in range(2):
                rows_in = bin_ref[part, half * half_ch:(half + 1) * half_ch, :].astype(BF16)
                tiled = jnp.dot(rows_in, tin_ref[...], preferred_element_type=F32)
                bmap_ref[part, half] = jnp.where(diag_in, tiled, 0.0).astype(BF16)
                rows_out = cout_ref[part, half * half_st:(half + 1) * half_st, :].astype(BF16)
                tiled = jnp.dot(rows_out, tout_ref[...], preferred_element_type=F32)
                cmap_ref[part, half] = jnp.where(diag_out, tiled, 0.0).astype(BF16)

    def project_in(ub, rows, xs_ref):
        for kh in range(2):
            lhs = ub[:, kh * half_ch:(kh + 1) * half_ch]
            xre = jnp.dot(lhs, bmap_ref[0, kh], preferred_element_type=F32)
            xim = jnp.dot(lhs, bmap_ref[1, kh], preferred_element_type=F32)
            for cl in range(slabs_per_half):
                c = kh * slabs_per_half + cl
                for b in range(batch):
                    xs_ref[c, pl.ds(b * S5_PITCH, rows), :] = (
                        xre[b * rows:(b + 1) * rows, cl * LANES:(cl + 1) * LANES])
                    xs_ref[c, pl.ds((batch + b) * S5_PITCH, rows), :] = (
                        xim[b * rows:(b + 1) * rows, cl * LANES:(cl + 1) * LANES])

    def scan(n_steps, xs_ref):
        def body(t, hs):
            new = []
            for c in range(N_SLABS):
                rows = pl.ds(t, 2 * batch, stride=S5_PITCH)
                x8 = xs_ref[c, rows, :]
                h = hs[c]
                hn = ar_ref[c] * h + ai_ref[c] * pltpu.roll(h, batch, 0) + x8
                xs_ref[c, rows, :] = hn
                new.append(hn)
            return tuple(new)

        hs = tuple(h_ref[c] for c in range(N_SLABS))
        hs = lax.fori_loop(0, n_steps, body, hs, unroll=16)
        for c in range(N_SLABS):
            h_ref[c] = hs[c]

    def output_map(xs_ref):
        ys = []
        for nh in range(2):
            def gather(plane0):
                return jnp.concatenate(
                    [jnp.concatenate(
                        [xs_ref[nh * slabs_per_half + cl,
                                pl.ds((plane0 + b) * S5_PITCH, S5_CHUNK), :]
                         for cl in range(slabs_per_half)], axis=1)
                     for b in range(batch)], axis=0).astype(BF16)
            yre = jnp.dot(gather(0), cmap_ref[0, nh], preferred_element_type=F32)
            yim = jnp.dot(gather(batch), cmap_ref[1, nh], preferred_element_type=F32)
            ys.append(yre - yim)
        uf = u_ref[...].reshape(batch * S5_CHUNK, D_SSM)
        y = jnp.concatenate(ys, axis=1) + d_ref[...] * uf
        z = 0.5 * y * (1.0 + jnp.tanh(math.sqrt(2.0 / math.pi) * (y + 0.044715 * (y * y * y))))
        gate = jnp.dot(z.astype(BF16), wglu_ref[...], preferred_element_type=F32) + bglu_ref[...]
        out = z * (1.0 / (1.0 + jnp.exp(-gate)))
        o_ref[...] = _rms(out, g_ref[...]).astype(BF16).reshape(batch, S5_CHUNK, D_SSM)

    @pl.when(j == 0)
    def _():
        build_maps()
        h_ref[...] = jnp.zeros_like(h_ref)
        um = um_ref[...].astype(BF16)
        project_in(jnp.concatenate([um] * batch, axis=0), N_META, xb_ref)
        scan(N_META, xb_ref)
        project_in(u_ref[...].reshape(batch * S5_CHUNK, D_SSM).astype(BF16), S5_CHUNK, xa_ref)

    def step(x_cur, x_next):
        scan(S5_CHUNK, x_cur)
        output_map(x_cur)
        project_in(un_ref[...].reshape(batch * S5_CHUNK, D_SSM).astype(BF16), S5_CHUNK, x_next)

    @pl.when(lax.rem(j, 2) == 0)
    def _():
        step(xa_ref, xb_ref)

    @pl.when(lax.rem(j, 2) == 1)
    def _():
        step(xb_ref, xa_ref)


def _s5_call(u3, u_meta, consts):
    batch, seq, _ = u3.shape
    assert 2 * batch == SUBLANES and seq % S5_CHUNK == 0
    n_chunks = seq // S5_CHUNK
    grid = (n_chunks,)
    chunk_blk = (batch, S5_CHUNK, D_SSM)
    in_specs = [pl.BlockSpec(chunk_blk, lambda j: (0, j, 0)),
                pl.BlockSpec(chunk_blk, lambda j: (0, jnp.minimum(j + 1, n_chunks - 1), 0)),
                _const_spec(u_meta.shape)] + [_const_spec(c.shape) for c in consts]
    planes = pltpu.VMEM((N_SLABS, 2 * batch * S5_PITCH, LANES), F32)
    return pl.pallas_call(
        functools.partial(_s5_kernel, batch=batch),
        out_shape=jax.ShapeDtypeStruct((batch, seq, D_SSM), BF16),
        grid=grid, in_specs=in_specs,
        out_specs=pl.BlockSpec(chunk_blk, lambda j: (0, j, 0)),
        scratch_shapes=[planes, planes,
                        pltpu.VMEM((N_SLABS, 2 * batch, LANES), F32),
                        pltpu.VMEM((2, 2, D_SSM // 2, N_STATE_COLS // 2), BF16),
                        pltpu.VMEM((2, 2, N_STATE_COLS // 2, D_SSM // 2), BF16)],
        compiler_params=pltpu.CompilerParams(dimension_semantics=("arbitrary",),
                                             vmem_limit_bytes=VMEM_LIMIT),
        name="s5_mixer",
    )(u3, u3, u_meta, *consts)


ATT_TQ = 1024
ATT_TK = 1024
ATT_SUB = 256
ATT_HEADS = 4


def _attn_kernel(q_ref, k_ref, v_ref, km_ref, vm_ref, mask_ref, o_ref, acc_ref, m_ref, *, online):
    nt = (((1,), (1,)), ((), ()))
    n_q = q_ref.shape[0] // ATT_TQ
    head_lanes = [slice(h * HEAD_PAD, (h + 1) * HEAD_PAD) for h in range(ATT_HEADS)]

    def step(h, sub, q, kblk, vblk, mask):
        s = lax.dot_general(q, kblk, nt, preferred_element_type=F32)
        if online:
            if mask is not None:
                s = jnp.where(mask > 0, s, -jnp.inf)
            m = m_ref[h, sub]
            m_new = jnp.maximum(m, jnp.max(s, axis=-1, keepdims=True))
            p = jnp.exp2(s - m_new).astype(BF16)
            acc_ref[h, sub] = (jnp.exp2(m - m_new) * acc_ref[h, sub]
                               + jnp.dot(p, vblk, preferred_element_type=F32))
            m_ref[h, sub] = m_new
        else:
            p = jnp.exp2(s).astype(BF16)
            if mask is not None:
                p = p * mask
            acc_ref[h, sub] += jnp.dot(p, vblk, preferred_element_type=F32)

    def q_tile(qi, _):
        q0 = pl.multiple_of(qi * ATT_TQ, ATT_TQ)
        rows = pl.ds(q0, ATT_TQ)
        qs = [q_ref[rows, hl] for hl in head_lanes]
        acc_ref[...] = jnp.zeros_like(acc_ref)
        if online:
            m_ref[...] = jnp.full(m_ref.shape, -1e30, F32)

        def body(kb, _):
            krows = pl.ds(pl.multiple_of(kb * ATT_TK, ATT_TK), ATT_TK)
            for h, hl in enumerate(head_lanes):
                step(h, slice(None), qs[h], k_ref[krows, hl], v_ref[krows, hl], None)
            return 0

        lax.fori_loop(0, qi * (ATT_TQ // ATT_TK), body, 0)

        for i in range(ATT_TQ // ATT_SUB):
            sub = slice(i * ATT_SUB, ATT_TQ)
            krows = pl.ds(q0 + i * ATT_SUB, ATT_SUB)
            for h, hl in enumerate(head_lanes):
                kblk, vblk = k_ref[krows, hl], v_ref[krows, hl]
                mask = mask_ref[sub, LANES + i * ATT_SUB:LANES + (i + 1) * ATT_SUB]
                if i == 0:
                    kblk = jnp.concatenate([km_ref[:, hl], kblk], axis=0)
                    vblk = jnp.concatenate([vm_ref[:, hl], vblk], axis=0)
                    mask = mask_ref[sub, :LANES + ATT_SUB]
                step(h, sub, qs[h][sub], kblk, vblk, mask)
        lane = lax.broadcasted_iota(jnp.int32, (ATT_TQ, HEAD_PAD), 1)
        for hp in range(ATT_HEADS // 2):
            even, odd = acc_ref[2 * hp], acc_ref[2 * hp + 1]
            o_even = even * (1.0 / even[:, V_DIM:V_DIM + 1])
            o_odd = odd * (1.0 / odd[:, 0:1])
            o_ref[rows, hp * HEAD_PAD:(hp + 1) * HEAD_PAD] = (
                jnp.where(lane < V_DIM, o_even, o_odd).astype(BF16))
        return 0

    lax.fori_loop(0, n_q, q_tile, 0)


def _attn_call(q3, k3, v3, k_meta, v_meta, mask, *, online):
    batch, seq, _ = q3.shape
    grid = (batch, N_HEADS // ATT_HEADS)
    seq_blk = pl.BlockSpec((None, seq, ATT_HEADS * HEAD_PAD), lambda b, hg: (b, 0, hg))
    meta_blk = pl.BlockSpec((LANES, ATT_HEADS * HEAD_PAD), lambda b, hg: (0, hg))
    return pl.pallas_call(
        functools.partial(_attn_kernel, online=online),
        out_shape=jax.ShapeDtypeStruct((batch, seq, N_HEADS * V_DIM), BF16),
        grid=grid,
        in_specs=[seq_blk, seq_blk, seq_blk, meta_blk, meta_blk, _const_spec(mask.shape)],
        out_specs=pl.BlockSpec((None, seq, ATT_HEADS * V_DIM), lambda b, hg: (b, 0, hg)),
        scratch_shapes=[pltpu.VMEM((ATT_HEADS, ATT_TQ, HEAD_PAD), F32),
                        pltpu.VMEM((ATT_HEADS, ATT_TQ, 1), F32)],
        compiler_params=pltpu.CompilerParams(dimension_semantics=("parallel", "parallel"),
                                             vmem_limit_bytes=VMEM_LIMIT),
        name="mla_attention_online" if online else "mla_attention",
    )(q3, k3, v3, k_meta, v_meta, mask)


FFN_TM = 512


def _ffn_kernel(x_ref, ms_ref, oa_ref, gatt_ref, wout_ref, gffn_ref, wg_ref, wu_ref, wd_ref,
                out_ref):
    ya = _rms(oa_ref[...].astype(F32), gatt_ref[...]).astype(BF16)
    mixed = jnp.concatenate([ms_ref[...], ya], axis=1)
    h1 = x_ref[...] + jnp.dot(mixed, wout_ref[...], preferred_element_type=F32)
    hn = _rms(h1, gffn_ref[...]).astype(BF16)
    g = jnp.dot(hn, wg_ref[...], preferred_element_type=F32)
    u = jnp.dot(hn, wu_ref[...], preferred_element_type=F32)
    a = (g * (1.0 / (1.0 + jnp.exp(-g))) * u).astype(BF16)
    out_ref[...] = h1 + jnp.dot(a, wd_ref[...], preferred_element_type=F32)


def _ffn_call(x2d, ms2d, oa2d, consts):
    n_rows = x2d.shape[0]
    tm = FFN_TM
    row = lambda i: (i, 0)
    once = pl.Buffered(1)
    in_specs = [pl.BlockSpec((tm, D_MODEL), row),
                pl.BlockSpec((tm, D_SSM), row),
                pl.BlockSpec((tm, N_HEADS * V_DIM), row)]
    in_specs += [pl.BlockSpec(c.shape, lambda i: (0, 0), pipeline_mode=once) for c in consts]
    return pl.pallas_call(
        _ffn_kernel,
        out_shape=jax.ShapeDtypeStruct((n_rows, D_MODEL), F32),
        grid=(n_rows // tm,), in_specs=in_specs,
        out_specs=pl.BlockSpec((tm, D_MODEL), row),
        compiler_params=pltpu.CompilerParams(dimension_semantics=("parallel",),
                                             vmem_limit_bytes=VMEM_LIMIT),
        name="outproj_ffn",
    )(x2d, ms2d, oa2d, *consts)


def _rope_tables(first, count):
    pos = np.arange(first, first + count, dtype=np.float64)
    inv_freq = 1.0 / (ROPE_BASE ** (np.arange(0, ROPE, 2, dtype=np.float64) / ROPE))
    ang = pos[:, None] * inv_freq[None, :]
    cos, sin = np.cos(ang), np.sin(ang)
    ctab = np.zeros((count, HEAD_PAD), np.float32)
    stab = np.zeros((count, HEAD_PAD), np.float32)
    ctab[:, :NOPE] = 1.0
    ctab[:, NOPE:QK_DIM] = np.concatenate([cos, cos], axis=1)
    stab[:, NOPE:QK_DIM] = np.concatenate([sin, sin], axis=1)
    return jnp.asarray(ctab), jnp.asarray(stab)


def _head_gains(gain):
    g_r = gain[NOPE:]
    g_r_swapped = jnp.concatenate([g_r[HALF_ROPE:], g_r[:HALF_ROPE]])
    pad = jnp.zeros((HEAD_PAD - QK_DIM,), F32)
    return jnp.stack([jnp.concatenate([gain, pad]),
                      jnp.concatenate([jnp.zeros((NOPE,), F32), g_r_swapped, pad])])


def _rot_half_cols(w):
    return jnp.concatenate([-w[..., HALF_ROPE:], w[..., :HALF_ROPE]], axis=-1)


def _pad_cols(w, left, total):
    return jnp.pad(w, ((0, 0), (left, total - left - w.shape[1])))


def kernel(x, meta_tokens, mix_norm_g, w_in, ssm_a_re, ssm_a_im, ssm_log_dt, ssm_b_re, ssm_b_im,
           ssm_c_re, ssm_c_im, ssm_d, ssm_w_glu, ssm_b_glu, q_lora_norm_g, w_uq, kv_lora_norm_g,
           w_uk, w_uv, q_head_norm_g, k_head_norm_g, ssm_out_norm_g, att_out_norm_g, w_out,
           ffn_norm_g, w_gate, w_up, w_down):
    batch, seq, _ = x.shape
    depth = w_in.shape[0]
    assert depth == 1
    l = 0

    wi = w_in[l]
    o_r = D_SSM + Q_LORA + KV_LORA
    w_r = wi[:, o_r:]
    win = jnp.concatenate([wi[:, :o_r], _pad_cols(w_r, NOPE, HEAD_PAD),
                           _pad_cols(_rot_half_cols(w_r), NOPE, HEAD_PAD)], axis=1).astype(BF16)
    wq3 = w_uq[l].reshape(Q_LORA, N_HEADS, QK_DIM)
    q1 = jnp.pad(wq3, ((0, 0), (0, 0), (0, HEAD_PAD - QK_DIM)))
    q2 = jnp.pad(_rot_half_cols(wq3[..., NOPE:]), ((0, 0), (0, 0), (NOPE, HEAD_PAD - QK_DIM)))
    wq = jnp.concatenate([q1.reshape(Q_LORA, QK_PAD), q2.reshape(Q_LORA, QK_PAD)], axis=1).astype(BF16)
    wk3 = jnp.pad(w_uk[l].reshape(KV_LORA, N_HEADS, NOPE), ((0, 0), (0, 0), (0, HEAD_PAD - NOPE)))
    wv4 = w_uv[l].reshape(KV_LORA, N_HEADS // 2, 2, V_DIM)
    zv = jnp.zeros_like(wv4[:, :, 0])
    wv = jnp.stack([jnp.concatenate([wv4[:, :, 0], zv], axis=-1),
                    jnp.concatenate([zv, wv4[:, :, 1]], axis=-1)], axis=2).reshape(KV_LORA, QK_PAD)
    ones_col = jnp.zeros((2, HEAD_PAD), F32).at[0, V_DIM].set(1.0).at[1, 0].set(1.0)
    vones = jnp.tile(ones_col.reshape(1, 2 * HEAD_PAD), (1, N_HEADS // 2))
    wkv = jnp.concatenate([wk3.reshape(KV_LORA, QK_PAD), wv], axis=1).astype(BF16)
    tabs_m = _rope_tables(0, N_META)
    tabs_f = _rope_tables(N_META, seq)
    proj_consts = (mix_norm_g[l][None], win, q_lora_norm_g[l][None], wq,
                   kv_lora_norm_g[l][None], wkv, vones,
                   _head_gains(q_head_norm_g[l]), _head_gains(k_head_norm_g[l]))

    tm = PROJ_TM
    ffn_f32 = (w_out, w_gate, w_up, w_down)
    u2, q2d, k2d, v2d, u_meta, k_meta, v_meta, wout_b, wg_b, wu_b, wd_b = _proj_call(
        x.reshape(batch * seq, D_MODEL), meta_tokens, tm, tabs_f, tabs_m, seq // tm,
        proj_consts, ffn_f32)

    dt = jnp.exp(ssm_log_dt[l])[:, None]
    lr, li = ssm_a_re[l], ssm_a_im[l]
    mag = jnp.exp(lr * dt)
    ar = mag * jnp.cos(li * dt)
    ai = mag * jnp.sin(li * dt)
    den = lr * lr + li * li
    fr = ((ar - 1.0) * lr + ai * li) / den
    fi = (ai * lr - (ar - 1.0) * li) / den
    br, bi = ssm_b_re[l], ssm_b_im[l]
    bbr = fr[..., None] * br - fi[..., None] * bi
    bbi = fr[..., None] * bi + fi[..., None] * br
    b_in = jnp.swapaxes(jnp.stack([bbr, bbi]), 2, 3).reshape(2, D_SSM, SSM_STATE)
    c_out = jnp.swapaxes(jnp.stack([ssm_c_re[l], ssm_c_im[l]]), 2, 3)
    c_out = jnp.pad(c_out.reshape(2, N_STATE_COLS, SSM_GROUP),
                    ((0, 0), (0, 0), (0, LANES - SSM_GROUP)))
    half_groups = N_GROUPS // 2
    tile_in = np.tile(np.eye(SSM_STATE, dtype=np.float32), (1, half_groups))
    tile_out = np.zeros((LANES, half_groups * SSM_GROUP), np.float32)
    tile_out[:SSM_GROUP] = np.tile(np.eye(SSM_GROUP, dtype=np.float32), (1, half_groups))

    ar_rows = jnp.broadcast_to(ar.reshape(N_SLABS, 1, LANES), (N_SLABS, 2 * batch, LANES))
    ai_flat = ai.reshape(N_SLABS, 1, LANES)
    ai_rows = jnp.concatenate([jnp.broadcast_to(-ai_flat, (N_SLABS, batch, LANES)),
                               jnp.broadcast_to(ai_flat, (N_SLABS, batch, LANES))], axis=1)
    s5_consts = (b_in, c_out, jnp.asarray(tile_in, BF16), jnp.asarray(tile_out, BF16),
                 ar_rows, ai_rows, ssm_d[l][None], ssm_w_glu[l].astype(BF16),
                 ssm_b_glu[l][None], ssm_out_norm_g[l][None])
    mixed_ssm = _s5_call(u2.reshape(batch, seq, D_SSM), u_meta, s5_consts)

    chunk_of = np.arange(ATT_TQ) // CHUNK
    causal = (chunk_of[None, :] <= chunk_of[:, None]).astype(np.float32)
    mask = jnp.asarray(np.concatenate([np.ones((ATT_TQ, LANES), np.float32), causal], axis=1),
                       dtype=BF16)
    score_bound = (LOG2_E * math.sqrt(QK_DIM) * jnp.max(jnp.abs(q_head_norm_g[l]))
                   * jnp.max(jnp.abs(k_head_norm_g[l])))
    attn_args = (q2d.reshape(batch, seq, QK_PAD), k2d.reshape(batch, seq, QK_PAD),
                 v2d.reshape(batch, seq, QK_PAD), k_meta, v_meta, mask)
    y_att = lax.cond(score_bound <= MAX_UNSHIFTED_LOG2_SCORE,
                     functools.partial(_attn_call, online=False),
                     functools.partial(_attn_call, online=True), *attn_args)

    ffn_consts = (att_out_norm_g[l][None], wout_b, ffn_norm_g[l][None],
                  wg_b, wu_b, wd_b)
    out = _ffn_call(x.reshape(batch * seq, D_MODEL), mixed_ssm.reshape(batch * seq, D_SSM),
                    y_att.reshape(batch * seq, N_HEADS * V_DIM), ffn_consts)
    return out.reshape(batch, seq, D_MODEL)
```
